```python
import math
import jax
import jax.numpy as jnp
from jax import lax
import numpy as np

D_MODEL = 1024
BATCH = 32
SEQ = 256
DEPTH = 4
DEC_BATCH = 2
DEC_SEQ = 1024
PAST_LEN = 512

GRID_W = 64
EPS = 1e-6
NEG_INF = -1e30
ROPE_BASE = 10000.0
BLOCK = 128
WINDOW = 128
HEAD_DIM = 64
RET_HEADS = 4
RET_WIDTH = RET_HEADS * HEAD_DIM
HY_WIDTH = 256
HY_ORDER = 2
HY_BANDS = 8
HY_POS_DIM = 1 + 2 * HY_BANDS
HY_FILTER_HIDDEN = 64
HY_DECAY_SLOW = 3.07
HY_DECAY_FAST = 15.35
GQA_Q_HEADS = 4
GQA_KV_HEADS = 2
GQA_GROUPS = GQA_Q_HEADS // GQA_KV_HEADS
MLA_HEADS = 4
MLA_Q_RANK = 256
MLA_KV_RANK = 128
MLA_NOPE = 64
MLA_ROPE = 32
MLA_V = 64
D_FF = 2816
RET_IN = 4 * RET_WIDTH
HY_IN = (HY_ORDER + 1) * HY_WIDTH
GQA_IN = (GQA_Q_HEADS + 2 * GQA_KV_HEADS) * HEAD_DIM
MLA_IN = MLA_Q_RANK + MLA_KV_RANK + MLA_ROPE
IN_WIDTH = RET_IN + HY_IN + GQA_IN + MLA_IN
MIX_WIDTH = RET_WIDTH + HY_WIDTH + GQA_Q_HEADS * HEAD_DIM + MLA_HEADS * MLA_V

kernel_name = 'hybrid_flow_trunk_ctx_prefix_step'


def rms_norm(x, g):
    xf = x.astype(jnp.float32)
    y = xf * lax.rsqrt(jnp.mean(xf * xf, axis=-1, keepdims=True) + EPS)
    return (y * g.astype(jnp.float32)).astype(x.dtype)


def modulation(cond, w, b):
    m = jax.nn.silu(cond) @ w + b
    return jnp.split(m[..., None, :], 6, axis=-1)


def dwconv3(x, w, b):
    xp = jnp.pad(x, ((0, 0), (1, 1), (0, 0)))
    return xp[:, :-2] * w[0] + xp[:, 1:-1] * w[1] + xp[:, 2:] * w[2] + b


def axial_rope_tables(n_tokens, dim, dtype):
    n_rows = n_tokens // GRID_W
    rows = jnp.repeat(jnp.arange(n_rows, dtype=jnp.float32), GRID_W)
    cols = jnp.tile(jnp.arange(GRID_W, dtype=jnp.float32), n_rows)
    quarter = dim // 4
    inv = ROPE_BASE ** (-jnp.arange(quarter, dtype=jnp.float32) / quarter)
    ang = jnp.concatenate([rows[:, None] * inv, cols[:, None] * inv], axis=-1)
    return jnp.cos(ang).astype(dtype), jnp.sin(ang).astype(dtype)


def apply_rope(x, cos, sin):
    x1, x2 = jnp.split(x, 2, axis=-1)
    return jnp.concatenate([x1 * cos - x2 * sin, x1 * sin + x2 * cos], axis=-1)


def retention_scan(q, k, v, log_gamma, s0):
    b, nh, seq_len, dk = q.shape
    dv = v.shape[-1]
    n = seq_len // BLOCK
    qc = q.reshape(b, nh, n, BLOCK, dk)
    kc = k.reshape(b, nh, n, BLOCK, dk)
    vc = v.reshape(b, nh, n, BLOCK, dv)
    pos = jnp.arange(BLOCK, dtype=jnp.float32)
    lag = pos[:, None] - pos[None, :]
    decay = jnp.where(lag >= 0, jnp.exp(jnp.maximum(lag, 0.0) * log_gamma[:, None, None]), 0.0)
    scores = jnp.einsum('bhnid,bhnjd->bhnij', qc, kc) * decay[None, :, None]
    o_intra = jnp.einsum('bhnij,bhnje->bhnie', scores, vc)
    q_dec = jnp.exp((pos + 1.0) * log_gamma[:, None])
    k_dec = jnp.exp((BLOCK - 1.0 - pos) * log_gamma[:, None])
    chunk_dec = jnp.exp(BLOCK * log_gamma)[None, :, None, None]
    inc = jnp.einsum('bhnjd,bhnje->nbhde', kc * k_dec[None, :, None, :, None], vc)

    def step(s, inc_n):
        return s * chunk_dec + inc_n, s

    s_final, s_before = lax.scan(step, s0, inc)
    o_cross = jnp.einsum('bhnid,nbhde->bhnie', qc * q_dec[None, :, None, :, None], s_before)
    return (o_intra + o_cross).reshape(b, nh, seq_len, dv), s_final


def retention_mixer(x_in, decay_logit, gn_g, s0):
    b, seq_len, _ = x_in.shape
    q, k, v, gate = jnp.split(x_in, 4, axis=-1)

    def heads(t):
        return t.astype(jnp.float32).reshape(b, seq_len, RET_HEADS, HEAD_DIM).transpose(0, 2, 1, 3)

    log_gamma = jax.nn.log_sigmoid(decay_logit.astype(jnp.float32))
    qh, kh, vh = heads(q), heads(k) * HEAD_DIM ** -0.5, heads(v)
    o_f, s_f = retention_scan(qh, kh, vh, log_gamma[0], s0[:, 0])
    o_b, s_b = retention_scan(jnp.flip(qh, 2), jnp.flip(kh, 2), jnp.flip(vh, 2), log_gamma[1], s0[:, 1])
    o = o_f + jnp.flip(o_b, 2)
    mu = jnp.mean(o, axis=-1, keepdims=True)
    var = jnp.var(o, axis=-1, keepdims=True)
    o = ((o - mu) * lax.rsqrt(var + EPS)).transpose(0, 2, 1, 3).reshape(b, seq_len, RET_WIDTH)
    y = jax.nn.silu(gate.astype(jnp.float32)) * (o * gn_g.astype(jnp.float32))
    return y.astype(x_in.dtype), jnp.stack([s_f, s_b], axis=1)


def hyena_filters(seq_len, w1, b1, w2, b2, w3, decay):
    t = jnp.arange(seq_len, dtype=jnp.float32) / seq_len
    bands = jnp.arange(1, HY_BANDS + 1, dtype=jnp.float32)
    ang = (2.0 * math.pi) * t[:, None] * bands
    z = jnp.concatenate([t[:, None], jnp.cos(ang), jnp.sin(ang)], axis=-1)
    h = jnp.sin(z @ w1 + b1)
    h = jnp.sin(h @ w2 + b2)
    h = (h @ w3).astype(jnp.float32).reshape(seq_len, HY_ORDER, 2, HY_WIDTH)
    window = jnp.exp(-t[:, None, None, None] * jnp.abs(decay.astype(jnp.float32))[None, None])
    return h * window


def bidir_fftconv(u, h_fwd, h_bwd, bias):
    seq_len, ch = h_fwd.shape
    g = jnp.concatenate([h_fwd, jnp.zeros((1, ch), jnp.float32), h_bwd[:seq_len - 1][::-1]], axis=0)
    spec = jnp.fft.rfft(u, n=2 * seq_len, axis=1) * jnp.fft.rfft(g, axis=0)[None]
    y = jnp.fft.irfft(spec, n=2 * seq_len, axis=1)[:, :seq_len]
    return y + u * bias.astype(jnp.float32)


def hyena_mixer(x_in, p):
    seq_len = x_in.shape[1]
    u = dwconv3(x_in, p['hy_short_w'], p['hy_short_b']).astype(jnp.float32)
    x1, x2, v = jnp.split(u, 3, axis=-1)
    filt = hyena_filters(seq_len, p['hy_w1'], p['hy_b1'], p['hy_w2'], p['hy_b2'], p['hy_w3'], p['hy_decay'])
    z = x1 * bidir_fftconv(v, filt[:, 0, 0], filt[:, 0, 1], p['hy_bias'][0])
    z = x2 * bidir_fftconv(z, filt[:, 1, 0], filt[:, 1, 1], p['hy_bias'][1])
    return z.astype(x_in.dtype)


def dense_attention(q, k, v, scale, sink):
    b, nk, g, lq, dq = q.shape
    dv = v.shape[-1]
    nq = lq // BLOCK
    qb = jnp.moveaxis(q.reshape(b, nk, g, nq, BLOCK, dq), 3, 0)

    def one_block(qblk):
        s = jnp.einsum('bkgqd,bkjd->bkgqj', qblk, k).astype(jnp.float32) * scale
        if sink is None:
            pr = jax.nn.softmax(s, axis=-1)
        else:
            s_sink = jnp.broadcast_to(sink.astype(jnp.float32)[None, :, :, None, None], s.shape[:-1] + (1,))
            pr = jax.nn.softmax(jnp.concatenate([s, s_sink], axis=-1), axis=-1)[..., :-1]
        return jnp.einsum('bkgqj,bkje->bkgqe', pr.astype(v.dtype), v)

    o = lax.map(one_block, qb)
    return jnp.moveaxis(o, 0, 3).reshape(b, nk, g, lq, dv)


def window_attention(q, k, v, k_ctx, v_ctx, sink):
    b, nkv, g, seq_len, d = q.shape
    nb = seq_len // BLOCK
    qb = q.reshape(b, nkv, g, nb, BLOCK, d)

    def bands(t):
        tp = jnp.pad(t, ((0, 0), (0, 0), (BLOCK, BLOCK), (0, 0))).reshape(b, nkv, nb + 2, BLOCK, d)
        return jnp.concatenate([tp[:, :, :nb], tp[:, :, 1:nb + 1], tp[:, :, 2:]], axis=3)

    kw, vw = bands(k), bands(v)
    qpos = jnp.arange(seq_len).reshape(nb, BLOCK, 1)
    kpos = (jnp.arange(nb)[:, None, None] - 1) * BLOCK + jnp.arange(3 * BLOCK)[None, None, :]
    valid = (jnp.abs(kpos - qpos) <= WINDOW) & (kpos >= 0) & (kpos < seq_len)
    scale = d ** -0.5
    s_win = jnp.einsum('bkgnqd,bknjd->bkgnqj', qb, kw).astype(jnp.float32) * scale
    s_win = jnp.where(valid, s_win, NEG_INF)
    s_ctx = jnp.einsum('bkgnqd,bkcd->bkgnqc', qb, k_ctx).astype(jnp.float32) * scale
    s_sink = jnp.broadcast_to(sink.astype(jnp.float32)[None, :, :, None, None, None], s_win.shape[:-1] + (1,))
    pr = jax.nn.softmax(jnp.concatenate([s_win, s_ctx, s_sink], axis=-1), axis=-1).astype(v.dtype)
    n_win = 3 * BLOCK
    n_ctx = k_ctx.shape[2]
    o = (jnp.einsum('bkgnqj,bknjd->bkgnqd', pr[..., :n_win], vw)
         + jnp.einsum('bkgnqc,bkcd->bkgnqd', pr[..., n_win:n_win + n_ctx], v_ctx))
    return o.reshape(b, nkv, g, seq_len, d)


def mla_keys_values(ckv, k_rope, w_uk, w_uv):
    b, seq_len, _ = ckv.shape
    k_nope = (ckv @ w_uk).reshape(b, seq_len, MLA_HEADS, MLA_NOPE).transpose(0, 2, 1, 3)
    v = (ckv @ w_uv).reshape(b, seq_len, MLA_HEADS, MLA_V).transpose(0, 2, 1, 3)
    k_r = jnp.broadcast_to(k_rope[:, None], (b, MLA_HEADS, seq_len, MLA_ROPE))
    return jnp.concatenate([k_nope, k_r], axis=-1), v


def token_mixer(h, p, cache):
    b, seq_len, _ = h.shape
    latent = cache is not None
    proj = h @ p['w_in']
    ret_in, hy_in, gqa_in, mla_in = jnp.split(proj, [RET_IN, RET_IN + HY_IN, RET_IN + HY_IN + GQA_IN], axis=-1)
    if latent:
        s0 = cache['state_ret'].astype(jnp.float32)
    else:
        s0 = jnp.zeros((b, 2, RET_HEADS, HEAD_DIM, HEAD_DIM), jnp.float32)
    y_ret, ret_state = retention_mixer(ret_in, p['ret_decay_logit'], p['ret_gn_g'], s0)
    y_hy = hyena_mixer(hy_in, p)
    nq = GQA_Q_HEADS * HEAD_DIM
    nkv = GQA_KV_HEADS * HEAD_DIM
    gq = gqa_in[..., :nq].reshape(b, seq_len, GQA_KV_HEADS, GQA_GROUPS, HEAD_DIM).transpose(0, 2, 3, 1, 4)
    gk = gqa_in[..., nq:nq + nkv].reshape(b, seq_len, GQA_KV_HEADS, HEAD_DIM).transpose(0, 2, 1, 3)
    gv = gqa_in[..., nq + nkv:].reshape(b, seq_len, GQA_KV_HEADS, HEAD_DIM).transpose(0, 2, 1, 3)
    sink = p['gqa_sink'].reshape(GQA_KV_HEADS, GQA_GROUPS)
    q_lat, kv_lat, k_rope = jnp.split(mla_in, [MLA_Q_RANK, MLA_Q_RANK + MLA_KV_RANK], axis=-1)
    mq = (rms_norm(q_lat, p['mla_q_norm']) @ p['mla_w_uq']).reshape(
        b, seq_len, MLA_HEADS, MLA_NOPE + MLA_ROPE).transpose(0, 2, 1, 3)
    q_nope, q_rope = mq[..., :MLA_NOPE], mq[..., MLA_NOPE:]
    ckv = rms_norm(kv_lat, p['mla_kv_norm'])
    if latent:
        cos_g, sin_g = axial_rope_tables(seq_len, HEAD_DIM, h.dtype)
        o_gqa = window_attention(apply_rope(gq, cos_g, sin_g), apply_rope(gk, cos_g, sin_g), gv,
                                 cache['gqa_k'], cache['gqa_v'], sink)
        cos_m, sin_m = axial_rope_tables(seq_len, MLA_ROPE, h.dtype)
        q_rope = apply_rope(q_rope, cos_m, sin_m)
        k_lat, v_lat = mla_keys_values(ckv, apply_rope(k_rope, cos_m, sin_m), p['mla_w_uk'], p['mla_w_uv'])
        k_ctx, v_ctx = mla_keys_values(cache['mla_ckv'], cache['mla_krope'], p['mla_w_uk'], p['mla_w_uv'])
        mk = jnp.concatenate([k_ctx, k_lat], axis=2)
        mv = jnp.concatenate([v_ctx, v_lat], axis=2)
        ctx_state = None
    else:
        o_gqa = dense_attention(gq, gk, gv, HEAD_DIM ** -0.5, sink)
        mk, mv = mla_keys_values(ckv, k_rope, p['mla_w_uk'], p['mla_w_uv'])
        ctx_state = (ret_state, gk, gv, ckv, k_rope)
    mq = jnp.concatenate([q_nope, q_rope], axis=-1)[:, :, None]
    o_mla = dense_attention(mq, mk, mv, (MLA_NOPE + MLA_ROPE) ** -0.5, None)[:, :, 0]
    y_gqa = o_gqa.transpose(0, 3, 1, 2, 4).reshape(b, seq_len, nq)
    y_mla = o_mla.transpose(0, 2, 1, 3).reshape(b, seq_len, MLA_HEADS * MLA_V)
    y = jnp.concatenate([y_ret, y_hy, y_gqa, y_mla], axis=-1) @ p['w_out']
    return y, ctx_state


def conv_ffn(h, p):
    gate, up = jnp.split(h @ p['ffn_w_up'], 2, axis=-1)
    gate = dwconv3(gate, p['ffn_conv_w'], p['ffn_conv_b'])
    return (jax.nn.silu(gate) * up) @ p['ffn_w_down']


def trunk_layer(x, mod, p, cache):
    shift1, scale1, gate1, shift2, scale2, gate2 = mod
    h = rms_norm(x, p['norm_g'][0]) * (1.0 + scale1) + shift1
    y, ctx_state = token_mixer(h, p, cache)
    x = x + gate1 * rms_norm(y, p['norm_g'][1])
    h = rms_norm(x, p['norm_g'][2]) * (1.0 + scale2) + shift2
    x = x + gate2 * rms_norm(conv_ffn(h, p), p['norm_g'][3])
    return x, ctx_state


def setup_inputs(seed: int = 0) -> dict:
    key = jax.random.key(seed)
    keys = jax.random.split(key, 48)
    counter = [0]

    def nrm(shape, scale):
        k = keys[counter[0]]
        counter[0] += 1
        return jax.random.normal(k, shape, jnp.float32) * scale

    ret_logit0 = jnp.log(2.0 ** (5.0 + jnp.arange(RET_HEADS, dtype=jnp.float32)) - 1.0)
    hy_decay0 = jnp.linspace(HY_DECAY_SLOW, HY_DECAY_FAST, HY_WIDTH, dtype=jnp.float32)
    return {
        'x_prompt': nrm((BATCH, SEQ, D_MODEL), 1.0),
        'x_sample': nrm((DEC_BATCH, DEC_SEQ, D_MODEL), 1.0),
        'state_ret': nrm((DEC_BATCH, DEPTH, 2, RET_HEADS, HEAD_DIM, HEAD_DIM), 1.0),
        'cache_gqa_k': nrm((DEC_BATCH, DEPTH, GQA_KV_HEADS, PAST_LEN, HEAD_DIM), 1.0),
        'cache_gqa_v': nrm((DEC_BATCH, DEPTH, GQA_KV_HEADS, PAST_LEN, HEAD_DIM), 1.0),
        'cache_mla_ckv': nrm((DEC_BATCH, DEPTH, PAST_LEN, MLA_KV_RANK), 1.0),
        'cache_mla_krope': nrm((DEC_BATCH, DEPTH, PAST_LEN, MLA_ROPE), 1.0),
        'c': nrm((DEC_BATCH, D_MODEL), 1.0),
        'c_ctx': nrm((D_MODEL,), 1.0),
        'ada_w': nrm((DEPTH, D_MODEL, 6 * D_MODEL), 0.5 * D_MODEL ** -0.5),
        'ada_b': nrm((DEPTH, 6 * D_MODEL), 0.02),
        'norm_g': 1.0 + nrm((DEPTH, 4, D_MODEL), 0.02),
        'w_in': nrm((DEPTH, D_MODEL, IN_WIDTH), D_MODEL ** -0.5),
        'ret_decay_logit': ret_logit0[None, None] + nrm((DEPTH, 2, RET_HEADS), 0.1),
        'ret_gn_g': 1.0 + nrm((DEPTH, RET_WIDTH), 0.02),
        'hy_short_w': nrm((DEPTH, 3, HY_IN), 3 ** -0.5),
        'hy_short_b': nrm((DEPTH, HY_IN), 0.02),
        'hy_w1': nrm((DEPTH, HY_POS_DIM, HY_FILTER_HIDDEN), 1.0),
        'hy_b1': nrm((DEPTH, HY_FILTER_HIDDEN), 0.1),
        'hy_w2': nrm((DEPTH, HY_FILTER_HIDDEN, HY_FILTER_HIDDEN), HY_FILTER_HIDDEN ** -0.5),
        'hy_b2': nrm((DEPTH, HY_FILTER_HIDDEN), 0.1),
        'hy_w3': nrm((DEPTH, HY_FILTER_HIDDEN, HY_ORDER * 2 * HY_WIDTH), 0.3 * HY_FILTER_HIDDEN ** -0.5),
        'hy_decay': hy_decay0[None, None] + nrm((DEPTH, 2, HY_WIDTH), 0.1),
        'hy_bias': nrm((DEPTH, HY_ORDER, HY_WIDTH), 0.1),
        'gqa_sink': nrm((DEPTH, GQA_Q_HEADS), 0.5),
        'mla_q_norm': 1.0 + nrm((DEPTH, MLA_Q_RANK), 0.02),
        'mla_kv_norm': 1.0 + nrm((DEPTH, MLA_KV_RANK), 0.02),
        'mla_w_uq': nrm((DEPTH, MLA_Q_RANK, MLA_HEADS * (MLA_NOPE + MLA_ROPE)), MLA_Q_RANK ** -0.5),
        'mla_w_uk': nrm((DEPTH, MLA_KV_RANK, MLA_HEADS * MLA_NOPE), MLA_KV_RANK ** -0.5),
        'mla_w_uv': nrm((DEPTH, MLA_KV_RANK, MLA_HEADS * MLA_V), MLA_KV_RANK ** -0.5),
        'w_out': nrm((DEPTH, MIX_WIDTH, D_MODEL), MIX_WIDTH ** -0.5),
        'ffn_w_up': nrm((DEPTH, D_MODEL, 2 * D_FF), D_MODEL ** -0.5),
        'ffn_conv_w': nrm((DEPTH, 3, D_FF), 3 ** -0.5),
        'ffn_conv_b': nrm((DEPTH, D_FF), 0.02),
        'ffn_w_down': nrm((DEPTH, D_FF, D_MODEL), D_FF ** -0.5),
    }


def reference(x_prompt, x_sample, state_ret, cache_gqa_k, cache_gqa_v, cache_mla_ckv, cache_mla_krope,
              c, c_ctx, ada_w, ada_b, norm_g, w_in, ret_decay_logit, ret_gn_g, hy_short_w, hy_short_b,
              hy_w1, hy_b1, hy_w2, hy_b2, hy_w3, hy_decay, hy_bias, gqa_sink, mla_q_norm, mla_kv_norm,
              mla_w_uq, mla_w_uk, mla_w_uv, w_out, ffn_w_up, ffn_conv_w, ffn_conv_b, ffn_w_down):
    xp = x_prompt
    xs = x_sample
    ret_states, gqa_ks, gqa_vs, mla_ckvs, mla_krs = [], [], [], [], []
    for l in range(DEPTH):
        p = {
            'norm_g': norm_g[l], 'w_in': w_in[l], 'ret_decay_logit': ret_decay_logit[l],
            'ret_gn_g': ret_gn_g[l], 'hy_short_w': hy_short_w[l], 'hy_short_b': hy_short_b[l],
            'hy_w1': hy_w1[l], 'hy_b1': hy_b1[l], 'hy_w2': hy_w2[l], 'hy_b2': hy_b2[l],
            'hy_w3': hy_w3[l], 'hy_decay': hy_decay[l], 'hy_bias': hy_bias[l], 'gqa_sink': gqa_sink[l],
            'mla_q_norm': mla_q_norm[l], 'mla_kv_norm': mla_kv_norm[l], 'mla_w_uq': mla_w_uq[l],
            'mla_w_uk': mla_w_uk[l], 'mla_w_uv': mla_w_uv[l], 'w_out': w_out[l],
            'ffn_w_up': ffn_w_up[l], 'ffn_conv_w': ffn_conv_w[l], 'ffn_conv_b': ffn_conv_b[l],
            'ffn_w_down': ffn_w_down[l],
        }
        xp, ctx_state = trunk_layer(xp, modulation(c_ctx, ada_w[l], ada_b[l]), p, None)
        ret_states.append(ctx_state[0])
        gqa_ks.append(ctx_state[1])
        gqa_vs.append(ctx_state[2])
        mla_ckvs.append(ctx_state[3])
        mla_krs.append(ctx_state[4])
        cache = {
            'state_ret': state_ret[:, l], 'gqa_k': cache_gqa_k[:, l], 'gqa_v': cache_gqa_v[:, l],
            'mla_ckv': cache_mla_ckv[:, l], 'mla_krope': cache_mla_krope[:, l],
        }
        xs, _ = trunk_layer(xs, modulation(c, ada_w[l], ada_b[l]), p, cache)
    new_state_ret = jnp.stack(ret_states, axis=1)
    new_cache_gqa_k = jnp.stack(gqa_ks, axis=1)
    new_cache_gqa_v = jnp.stack(gqa_vs, axis=1)
    new_cache_mla_ckv = jnp.stack(mla_ckvs, axis=1)
    new_cache_mla_krope = jnp.stack(mla_krs, axis=1)
    return (xp, xs, new_state_ret, new_cache_gqa_k, new_cache_gqa_v, new_cache_mla_ckv, new_cache_mla_krope)
```

```python
import functools
import math

import numpy as np
import jax
import jax.numpy as jnp
from jax import lax
from jax.experimental import pallas as pl
from jax.experimental.pallas import tpu as pltpu

F32 = jnp.float32
BF16 = jnp.bfloat16

D_MODEL = 1024
DEPTH = 4
GRID_W = 64
EPS = 1e-6
NEG_INF = -1e30
ROPE_BASE = 10000.0
BLOCK = 128
WINDOW = 128
HEAD_DIM = 64
RET_HEADS = 4
RET_WIDTH = RET_HEADS * HEAD_DIM
HY_WIDTH = 256
HY_ORDER = 2
HY_BANDS = 8
HY_POS_DIM = 1 + 2 * HY_BANDS
HY_POS_PAD = 32
HY_FILTER_HIDDEN = 64
GQA_Q_HEADS = 4
GQA_KV_HEADS = 2
GQA_GROUPS = GQA_Q_HEADS // GQA_KV_HEADS
MLA_HEADS = 4
MLA_Q_RANK = 256
MLA_KV_RANK = 128
MLA_NOPE = 64
MLA_ROPE = 32
MLA_V = 64
D_FF = 2816
RET_IN = 4 * RET_WIDTH
HY_IN = (HY_ORDER + 1) * HY_WIDTH
GQA_IN = (GQA_Q_HEADS + 2 * GQA_KV_HEADS) * HEAD_DIM
MLA_IN = MLA_Q_RANK + MLA_KV_RANK + MLA_ROPE
MLA_IN_PAD = 512
IN_WIDTH = RET_IN + HY_IN + GQA_IN + MLA_IN
IN_WIDTH_PAD = RET_IN + HY_IN + GQA_IN + MLA_IN_PAD
MIX_PART = 256
N_COND = 8
FFN_CHUNK = 256
N_FFN_CHUNKS = D_FF // FFN_CHUNK

VMEM_LIMIT = 56 * 1024 * 1024


def _params(n_axes=1):
    return pltpu.CompilerParams(
        dimension_semantics=("arbitrary",) * n_axes, vmem_limit_bytes=VMEM_LIMIT)


def _dot(a, b):
    return jnp.dot(a, b, preferred_element_type=F32)


def _dot_nt(a, b):
    return lax.dot_general(a, b, (((1,), (1,)), ((), ())), preferred_element_type=F32)


def _dot_tn(a, b):
    return lax.dot_general(a, b, (((0,), (0,)), ((), ())), preferred_element_type=F32)


def _rms(x, g):
    return x * lax.rsqrt(jnp.mean(x * x, axis=-1, keepdims=True) + EPS) * g


def _sigmoid(x):
    return 1.0 / (1.0 + jnp.exp(-x))


def _silu(x):
    return x * _sigmoid(x)


def _shift_rows(x, seq_len):
    n = x.shape[0]
    assert seq_len & (seq_len - 1) == 0
    pos = lax.broadcasted_iota(jnp.int32, (n, 1), 0) & (seq_len - 1)
    prev = jnp.where(pos != 0, pltpu.roll(x, 1, axis=0), 0.0)
    nxt = jnp.where(pos != seq_len - 1, pltpu.roll(x, n - 1, axis=0), 0.0)
    return prev, nxt


def _rot_half(x, half):
    lane = lax.broadcasted_iota(jnp.int32, x.shape, 1) & (2 * half - 1)
    return jnp.where(lane < half, pltpu.roll(x, 128 - half, axis=1), pltpu.roll(x, half, axis=1))


def _mod_kernel(cond_ref, w_ref, b_ref, out_ref):
    s = _silu(cond_ref[...]).astype(BF16)
    out_ref[0] = _dot(s, w_ref[0].astype(BF16)) + b_ref[0]


def _modulation(cond, ada_w, ada_b):
    tn = 1536
    return pl.pallas_call(
        _mod_kernel,
        out_shape=jax.ShapeDtypeStruct((DEPTH, N_COND, 6 * D_MODEL), F32),
        grid=(DEPTH, 6 * D_MODEL // tn),
        in_specs=[
            pl.BlockSpec((N_COND, D_MODEL), lambda l, j: (0, 0)),
            pl.BlockSpec((1, D_MODEL, tn), lambda l, j: (l, 0, j)),
            pl.BlockSpec((1, 1, tn), lambda l, j: (l, 0, j)),
        ],
        out_specs=pl.BlockSpec((1, N_COND, tn), lambda l, j: (l, 0, j)),
        compiler_params=_params(2),
        name="modulation",
    )(cond, ada_w, ada_b.reshape(DEPTH, 1, 6 * D_MODEL))


def _filter_kernel(z_ref, w1_ref, b1_ref, w2_ref, b2_ref, w3_ref, dec_ref, f_ref, out_ref, *, seq_len):
    hi = lax.Precision.HIGHEST
    h = jnp.sin(jnp.dot(z_ref[...], w1_ref[0], precision=hi, preferred_element_type=F32) + b1_ref[0])
    h = jnp.sin(jnp.dot(h, w2_ref[0], precision=hi, preferred_element_type=F32) + b2_ref[0])
    h = jnp.dot(h, w3_ref[0], precision=hi, preferred_element_type=F32)
    row = lax.broadcasted_iota(jnp.int32, (seq_len, 1), 0)
    t = row.astype(F32) / seq_len
    dec = jnp.abs(dec_ref[0])
    win_f = jnp.exp(-t * dec[0:1, :])
    win_b = jnp.exp(-t * dec[1:2, :])
    fm = f_ref[...]
    inv = 1.0 / seq_len
    scale = jnp.where(row == 0, 0.5 * inv, inv)
    for o in range(HY_ORDER):
        base = o * 2 * HY_WIDTH
        hf = h[:, base:base + HY_WIDTH] * win_f
        hb = h[:, base + HY_WIDTH:base + 2 * HY_WIDTH] * win_b
        hbs = jnp.where(row != 0, pltpu.roll(hb, 1, axis=0), 0.0)
        sf = _dot(fm, hf.astype(BF16))
        sb = _dot(fm, hbs.astype(BF16))
        g_re = sf[:seq_len] + sb[:seq_len]
        g_im = sf[seq_len:] - sb[seq_len:]
        g_ny = sf[seq_len:] + sb[seq_len:]
        a = g_re * scale
        out_ref[0, 3 * o] = a
        out_ref[0, 3 * o + 1] = jnp.where(row == 0, g_ny * (0.5 * inv), a)
        out_ref[0, 3 * o + 2] = jnp.where(row == 0, 0.0, g_im * inv)


def _hyena_tables(seq_len, z, w1p, b1, w2, b2, w3, decay, fmat):
    hid = HY_FILTER_HIDDEN
    wide = HY_ORDER * 2 * HY_WIDTH
    return pl.pallas_call(
        functools.partial(_filter_kernel, seq_len=seq_len),
        out_shape=jax.ShapeDtypeStruct((DEPTH, 3 * HY_ORDER, seq_len, HY_WIDTH), F32),
        grid=(DEPTH,),
        in_specs=[
            pl.BlockSpec((seq_len, HY_POS_PAD), lambda l: (0, 0)),
            pl.BlockSpec((1, HY_POS_PAD, hid), lambda l: (l, 0, 0)),
            pl.BlockSpec((1, 1, hid), lambda l: (l, 0, 0)),
            pl.BlockSpec((1, hid, hid), lambda l: (l, 0, 0)),
            pl.BlockSpec((1, 1, hid), lambda l: (l, 0, 0)),
            pl.BlockSpec((1, hid, wide), lambda l: (l, 0, 0)),
            pl.BlockSpec((1, 2, HY_WIDTH), lambda l: (l, 0, 0)),
            pl.BlockSpec((2 * seq_len, seq_len), lambda l: (0, 0)),
        ],
        out_specs=pl.BlockSpec((1, 3 * HY_ORDER, seq_len, HY_WIDTH), lambda l: (l, 0, 0, 0)),
        compiler_params=_params(1),
        name=f"hyena_tables_{seq_len}",
    )(z, w1p, b1, w2, b2, w3, decay, fmat)


def _in_proj_kernel(x_ref, mod_ref, g_ref, w_ref, ret_ref, hy_ref, gqa_ref, mla_ref):
    shift = mod_ref[0, 0, 0:1, :]
    scale = mod_ref[0, 0, 1:2, :]
    h = (_rms(x_ref[...], g_ref[0, 0:1, :]) * (1.0 + scale) + shift).astype(BF16)
    c0 = 0
    for ref in (ret_ref, hy_ref, gqa_ref, mla_ref):
        width = ref.shape[1]
        ref[...] = _dot(h, w_ref[0, :, c0:c0 + width])
        c0 += width


def _in_proj(x, mod, norm_g, w_in, layer, tm, cond_of_tile):
    rows = x.shape[0]
    widths = (RET_IN, HY_IN, GQA_IN, MLA_IN_PAD)
    return pl.pallas_call(
        _in_proj_kernel,
        out_shape=[jax.ShapeDtypeStruct((rows, w), F32) for w in widths],
        grid=(rows // tm,),
        in_specs=[
            pl.BlockSpec((tm, D_MODEL), lambda i: (i, 0)),
            pl.BlockSpec((1, 1, 6, D_MODEL), lambda i: (layer, cond_of_tile(i), 0, 0)),
            pl.BlockSpec((1, 4, D_MODEL), lambda i: (layer, 0, 0)),
            pl.BlockSpec((1, D_MODEL, IN_WIDTH_PAD), lambda i: (layer, 0, 0)),
        ],
        out_specs=[pl.BlockSpec((tm, w), lambda i: (i, 0)) for w in widths],
        compiler_params=_params(1),
        name="in_proj",
    )(x, mod, norm_g, w_in)


def _out_proj_kernel(x_ref, m0_ref, m1_ref, m2_ref, m3_ref, mod_ref, g_ref, w_ref, xm_ref, h2_ref):
    y = None
    for i, m_ref in enumerate((m0_ref, m1_ref, m2_ref, m3_ref)):
        part = _dot(m_ref[...], w_ref[0, i * MIX_PART:(i + 1) * MIX_PART, :])
        y = part if y is None else y + part
    gate1 = mod_ref[0, 0, 2:3, :]
    shift2 = mod_ref[0, 0, 3:4, :]
    scale2 = mod_ref[0, 0, 4:5, :]
    xm = x_ref[...] + gate1 * _rms(y, g_ref[0, 1:2, :])
    xm_ref[...] = xm
    h2_ref[...] = (_rms(xm, g_ref[0, 2:3, :]) * (1.0 + scale2) + shift2).astype(BF16)


def _out_proj(x, mixes, mod, norm_g, w_out, layer, tm, cond_of_tile):
    rows = x.shape[0]
    return pl.pallas_call(
        _out_proj_kernel,
        out_shape=[jax.ShapeDtypeStruct((rows, D_MODEL), F32),
                   jax.ShapeDtypeStruct((rows, D_MODEL), BF16)],
        grid=(rows // tm,),
        in_specs=[pl.BlockSpec((tm, D_MODEL), lambda i: (i, 0))]
        + [pl.BlockSpec((tm, MIX_PART), lambda i: (i, 0))] * 4
        + [
            pl.BlockSpec((1, 1, 6, D_MODEL), lambda i: (layer, cond_of_tile(i), 0, 0)),
            pl.BlockSpec((1, 4, D_MODEL), lambda i: (layer, 0, 0)),
            pl.BlockSpec((1, D_MODEL, D_MODEL), lambda i: (layer, 0, 0)),
        ],
        out_specs=[pl.BlockSpec((tm, D_MODEL), lambda i: (i, 0))] * 2,
        compiler_params=_params(1),
        name="out_proj",
    )(x, *mixes, mod, norm_g, w_out)


def _ffn_kernel(h2_ref, xm_ref, mod_ref, g_ref, wg_ref, wu_ref, cw_ref, cb_ref, wd_ref, out_ref,
                acc_ref, *, seq_len):
    h2 = h2_ref[...]
    for c in range(N_FFN_CHUNKS):
        sl = slice(c * FFN_CHUNK, (c + 1) * FFN_CHUNK)
        gate = _dot(h2, wg_ref[0, c])
        up = _dot(h2, wu_ref[0, c])
        prev, nxt = _shift_rows(gate, seq_len)
        gate = (prev * cw_ref[0, 0:1, sl] + gate * cw_ref[0, 1:2, sl] + nxt * cw_ref[0, 2:3, sl]
                + cb_ref[0, :, sl])
        act = (_silu(gate) * up).astype(BF16)
        part = _dot(act, wd_ref[0, c])
        if c == 0:
            acc_ref[...] = part
        else:
            acc_ref[...] += part
    gate2 = mod_ref[0, 0, 5:6, :]
    out_ref[...] = xm_ref[...] + gate2 * _rms(acc_ref[...], g_ref[0, 3:4, :])


def _ffn(h2, xm, mod, norm_g, w_gate, w_up, conv_w, conv_b, w_down, layer, tm, seq_len, cond_of_tile):
    rows = xm.shape[0]
    resident = dict(pipeline_mode=pl.Buffered(1))
    return pl.pallas_call(
        functools.partial(_ffn_kernel, seq_len=seq_len),
        out_shape=jax.ShapeDtypeStruct((rows, D_MODEL), F32),
        grid=(rows // tm,),
        in_specs=[
            pl.BlockSpec((tm, D_MODEL), lambda i: (i, 0)),
            pl.BlockSpec((tm, D_MODEL), lambda i: (i, 0)),
            pl.BlockSpec((1, 1, 6, D_MODEL), lambda i: (layer, cond_of_tile(i), 0, 0)),
            pl.BlockSpec((1, 4, D_MODEL), lambda i: (layer, 0, 0)),
            pl.BlockSpec((1, N_FFN_CHUNKS, D_MODEL, FFN_CHUNK), lambda i: (layer, 0, 0, 0), **resident),
            pl.BlockSpec((1, N_FFN_CHUNKS, D_MODEL, FFN_CHUNK), lambda i: (layer, 0, 0, 0), **resident),
            pl.BlockSpec((1, 3, D_FF), lambda i: (layer, 0, 0)),
            pl.BlockSpec((1, 1, D_FF), lambda i: (layer, 0, 0)),
            pl.BlockSpec((1, N_FFN_CHUNKS, FFN_CHUNK, D_MODEL), lambda i: (layer, 0, 0, 0), **resident),
        ],
        out_specs=pl.BlockSpec((tm, D_MODEL), lambda i: (i, 0)),
        scratch_shapes=[pltpu.VMEM((tm, D_MODEL), F32)],
        compiler_params=_params(1),
        name="conv_ffn",
    )(h2, xm, mod, norm_g, w_gate, w_up, conv_w, conv_b, w_down)


def _ret_kernel(*refs, seq_len, latent, bq):
    if latent:
        x_ref, dl_ref, gn_ref, s0_ref, y_ref = refs
    else:
        x_ref, dl_ref, gn_ref, y_ref, st_ref = refs
    dl = dl_ref[0]
    lg = jnp.minimum(dl, 0.0) - jnp.log(1.0 + jnp.exp(-jnp.abs(dl)))
    colf = lax.broadcasted_iota(jnp.int32, (1, seq_len), 1).astype(F32)
    posf = lax.broadcasted_iota(jnp.int32, (seq_len, 1), 0).astype(F32)
    for h in range(RET_HEADS):
        lo = h * HEAD_DIM
        lgf = lg[0:1, h:h + 1]
        lgb = lg[1:2, h:h + 1]
        k = x_ref[:, RET_WIDTH + lo:RET_WIDTH + lo + HEAD_DIM] * (HEAD_DIM ** -0.5)
        v = x_ref[:, 2 * RET_WIDTH + lo:2 * RET_WIDTH + lo + HEAD_DIM]
        kb = k.astype(BF16)
        vb = v.astype(BF16)
        if not latent:
            kf = (k * jnp.exp((seq_len - 1.0 - posf) * lgf)).astype(BF16)
            kr = (k * jnp.exp(posf * lgb)).astype(BF16)
            st_ref[0, 0, h] = _dot_tn(kf, vb)
            st_ref[0, 1, h] = _dot_tn(kr, vb)
        for r0 in range(0, seq_len, bq):
            q = x_ref[r0:r0 + bq, lo:lo + HEAD_DIM]
            rowf = (lax.broadcasted_iota(jnp.int32, (bq, 1), 0) + r0).astype(F32)
            lag = rowf - colf
            decay = jnp.exp(jnp.where(lag >= 0.0, lag * lgf, -lag * lgb))
            decay = jnp.where(lag == 0.0, 2.0, decay)
            s = _dot_nt(q.astype(BF16), kb) * decay
            o = _dot(s.astype(BF16), vb)
            if latent:
                qf = (q * jnp.exp((rowf + 1.0) * lgf)).astype(BF16)
                qr = (q * jnp.exp((seq_len - rowf) * lgb)).astype(BF16)
                o = o + _dot(qf, s0_ref[0, 0, 0, h].astype(BF16)) + _dot(qr, s0_ref[0, 0, 1, h].astype(BF16))
            mu = jnp.mean(o, axis=-1, keepdims=True)
            d = o - mu
            var = jnp.mean(d * d, axis=-1, keepdims=True)
            on = d * lax.rsqrt(var + EPS)
            gate = x_ref[r0:r0 + bq, 3 * RET_WIDTH + lo:3 * RET_WIDTH + lo + HEAD_DIM]
            y = _silu(gate) * (on * gn_ref[0, :, lo:lo + HEAD_DIM])
            y_ref[r0:r0 + bq, lo:lo + HEAD_DIM] = y.astype(BF16)


def _retention(ret_in, decay_logit, gn_g, state0, layer, n_batch, seq_len):
    latent = state0 is not None
    kern = functools.partial(_ret_kernel, seq_len=seq_len, latent=latent, bq=min(seq_len, 256))
    in_specs = [
        pl.BlockSpec((seq_len, RET_IN), lambda b: (b, 0)),
        pl.BlockSpec((1, 2, RET_HEADS), lambda b: (layer, 0, 0)),
        pl.BlockSpec((1, 1, RET_WIDTH), lambda b: (layer, 0, 0)),
    ]
    args = [ret_in, decay_logit, gn_g]
    y_shape = jax.ShapeDtypeStruct((n_batch * seq_len, MIX_PART), BF16)
    y_spec = pl.BlockSpec((seq_len, MIX_PART), lambda b: (b, 0))
    if latent:
        in_specs.append(pl.BlockSpec((1, 1, 2, RET_HEADS, HEAD_DIM, HEAD_DIM),
                                     lambda b: (b, layer, 0, 0, 0, 0)))
        args.append(state0)
        out_shape, out_specs = y_shape, y_spec
    else:
        out_shape = [y_shape, jax.ShapeDtypeStruct((n_batch, 2, RET_HEADS, HEAD_DIM, HEAD_DIM), F32)]
        out_specs = [y_spec, pl.BlockSpec((1, 2, RET_HEADS, HEAD_DIM, HEAD_DIM), lambda b: (b, 0, 0, 0, 0))]
    return pl.pallas_call(
        kern, out_shape=out_shape, grid=(n_batch,), in_specs=in_specs, out_specs=out_specs,
        compiler_params=_params(1), name=f"retention_{seq_len}",
    )(*args)


def _hyena_kernel(x_ref, sw_ref, sb_ref, bias_ref, tab_ref, f_ref, ft_ref, y_ref, *, seq_len):
    x = x_ref[...]
    prev, nxt = _shift_rows(x, seq_len)
    u = prev * sw_ref[0, 0:1, :] + x * sw_ref[0, 1:2, :] + nxt * sw_ref[0, 2:3, :] + sb_ref[0]
    x1 = u[:, 0:HY_WIDTH]
    x2 = u[:, HY_WIDTH:2 * HY_WIDTH]
    z = u[:, 2 * HY_WIDTH:3 * HY_WIDTH]
    for o, gate in enumerate((x1, x2)):
        spec = _dot(f_ref[...], z.astype(BF16))
        s_re = spec[:seq_len]
        s_im = spec[seq_len:]
        a = tab_ref[0, 3 * o]
        a_ny = tab_ref[0, 3 * o + 1]
        b = tab_ref[0, 3 * o + 2]
        y_re = (s_re * a - s_im * b).astype(BF16)
        y_im = (s_re * b + s_im * a_ny).astype(BF16)
        conv = _dot(ft_ref[:, 0:seq_len], y_re) + _dot(ft_ref[:, seq_len:2 * seq_len], y_im)
        z = gate * (conv + z * bias_ref[0, o:o + 1, :])
    y_ref[...] = z.astype(BF16)


def _hyena(hy_in, short_w, short_b, hy_bias, tables, fmat, fmat_t, layer, n_batch, seq_len):
    const = dict(pipeline_mode=pl.Buffered(1))
    return pl.pallas_call(
        functools.partial(_hyena_kernel, seq_len=seq_len),
        out_shape=jax.ShapeDtypeStruct((n_batch * seq_len, MIX_PART), BF16),
        grid=(n_batch,),
        in_specs=[
            pl.BlockSpec((seq_len, HY_IN), lambda b: (b, 0)),
            pl.BlockSpec((1, 3, HY_IN), lambda b: (layer, 0, 0)),
            pl.BlockSpec((1, 1, HY_IN), lambda b: (layer, 0, 0)),
            pl.BlockSpec((1, HY_ORDER, HY_WIDTH), lambda b: (layer, 0, 0)),
            pl.BlockSpec((1, 3 * HY_ORDER, seq_len, HY_WIDTH), lambda b: (layer, 0, 0, 0), **const),
            pl.BlockSpec((2 * seq_len, seq_len), lambda b: (0, 0), **const),
            pl.BlockSpec((seq_len, 2 * seq_len), lambda b: (0, 0), **const),
        ],
        out_specs=pl.BlockSpec((seq_len, MIX_PART), lambda b: (b, 0)),
        compiler_params=_params(1),
        name=f"hyena_{seq_len}",
    )(hy_in, short_w, short_b, hy_bias, tables, fmat, fmat_t)


def _softmax_parts(scores, sink):
    m = None
    for s in scores:
        mi = jnp.max(s, axis=-1, keepdims=True)
        m = mi if m is None else jnp.maximum(m, mi)
    if sink is not None:
        m = jnp.maximum(m, sink)
    parts = [jnp.exp(s - m) for s in scores]
    den = None
    for p in parts:
        di = jnp.sum(p, axis=-1, keepdims=True)
        den = di if den is None else den + di
    if sink is not None:
        den = den + jnp.exp(sink - m)
    return parts, den


def _gqa_ctx_kernel(x_ref, sink_ref, y_ref, k_out_ref, v_out_ref, *, seq_len):
    nq = GQA_Q_HEADS * HEAD_DIM
    nkv = GQA_KV_HEADS * HEAD_DIM
    scale = HEAD_DIM ** -0.5
    for kv in range(GQA_KV_HEADS):
        k = x_ref[:, nq + kv * HEAD_DIM:nq + (kv + 1) * HEAD_DIM]
        v = x_ref[:, nq + nkv + kv * HEAD_DIM:nq + nkv + (kv + 1) * HEAD_DIM]
        k_out_ref[0, kv] = k
        v_out_ref[0, kv] = v
        kb = k.astype(BF16)
        vb = v.astype(BF16)
        for g in range(GQA_GROUPS):
            j = kv * GQA_GROUPS + g
            q = x_ref[:, j * HEAD_DIM:(j + 1) * HEAD_DIM]
            s = _dot_nt(q.astype(BF16), kb) * scale
            (p,), den = _softmax_parts([s], sink_ref[0, :, j:j + 1])
            o = _dot(p.astype(BF16), vb) / den
            y_ref[:, j * HEAD_DIM:(j + 1) * HEAD_DIM] = o.astype(BF16)


def _gqa_ctx(gqa_in, sink, layer, n_batch, seq_len):
    return pl.pallas_call(
        functools.partial(_gqa_ctx_kernel, seq_len=seq_len),
        out_shape=[
            jax.ShapeDtypeStruct((n_batch * seq_len, MIX_PART), BF16),
            jax.ShapeDtypeStruct((n_batch, GQA_KV_HEADS, seq_len, HEAD_DIM), F32),
            jax.ShapeDtypeStruct((n_batch, GQA_KV_HEADS, seq_len, HEAD_DIM), F32),
        ],
        grid=(n_batch,),
        in_specs=[
            pl.BlockSpec((seq_len, GQA_IN), lambda b: (b, 0)),
            pl.BlockSpec((1, 1, GQA_Q_HEADS), lambda b: (layer, 0, 0)),
        ],
        out_specs=[
            pl.BlockSpec((seq_len, MIX_PART), lambda b: (b, 0)),
            pl.BlockSpec((1, GQA_KV_HEADS, seq_len, HEAD_DIM), lambda b: (b, 0, 0, 0)),
            pl.BlockSpec((1, GQA_KV_HEADS, seq_len, HEAD_DIM), lambda b: (b, 0, 0, 0)),
        ],
        compiler_params=_params(1),
        name=f"gqa_ctx_{seq_len}",
    )(gqa_in, sink)


def _gqa_win_kernel(x_ref, sink_ref, kc_ref, vc_ref, cos_ref, sin_ref, y_ref, q_scr, k_scr, v_scr, *, seq_len):
    nq = GQA_Q_HEADS * HEAD_DIM
    nkv = GQA_KV_HEADS * HEAD_DIM
    scale = HEAD_DIM ** -0.5
    cos = cos_ref[...]
    sin = sin_ref[...]

    def rope(t):
        return t * cos + _rot_half(t, HEAD_DIM // 2) * sin

    for c in range(nq // 128):
        q_scr[:, c * 128:(c + 1) * 128] = rope(x_ref[:, c * 128:(c + 1) * 128]).astype(BF16)
    zeros = jnp.zeros((BLOCK, nkv), BF16)
    k_scr[0:BLOCK, :] = zeros
    k_scr[BLOCK + seq_len:2 * BLOCK + seq_len, :] = zeros
    v_scr[0:BLOCK, :] = zeros
    v_scr[BLOCK + seq_len:2 * BLOCK + seq_len, :] = zeros
    k_scr[BLOCK:BLOCK + seq_len, :] = rope(x_ref[:, nq:nq + nkv]).astype(BF16)
    v_scr[BLOCK:BLOCK + seq_len, :] = x_ref[:, nq + nkv:nq + 2 * nkv].astype(BF16)

    ii = lax.broadcasted_iota(jnp.int32, (BLOCK, 3 * BLOCK), 0)
    jj = lax.broadcasted_iota(jnp.int32, (BLOCK, 3 * BLOCK), 1)
    band = (jj >= ii) & (jj <= ii + 2 * WINDOW)

    def block(n, carry):
        r0 = pl.multiple_of(n * BLOCK, BLOCK)
        kpos = jj + (n - 1) * BLOCK
        valid = band & (kpos >= 0) & (kpos < seq_len)
        for kv in range(GQA_KV_HEADS):
            cs = slice(kv * HEAD_DIM, (kv + 1) * HEAD_DIM)
            kw = k_scr[pl.ds(r0, 3 * BLOCK), cs]
            vw = v_scr[pl.ds(r0, 3 * BLOCK), cs]
            kc = kc_ref[0, 0, kv].astype(BF16)
            vc = vc_ref[0, 0, kv].astype(BF16)
            for g in range(GQA_GROUPS):
                j = kv * GQA_GROUPS + g
                q = q_scr[pl.ds(r0, BLOCK), j * HEAD_DIM:(j + 1) * HEAD_DIM]
                s_win = jnp.where(valid, _dot_nt(q, kw) * scale, NEG_INF)
                s_ctx = _dot_nt(q, kc) * scale
                (p_win, p_ctx), den = _softmax_parts([s_win, s_ctx], sink_ref[0, :, j:j + 1])
                o = (_dot(p_win.astype(BF16), vw) + _dot(p_ctx.astype(BF16), vc)) / den
                y_ref[pl.ds(r0, BLOCK), j * HEAD_DIM:(j + 1) * HEAD_DIM] = o.astype(BF16)
        return carry

    lax.fori_loop(0, seq_len // BLOCK, block, 0)


def _gqa_win(gqa_in, sink, cache_k, cache_v, cos, sin, layer, n_batch, seq_len):
    past = cache_k.shape[3]
    return pl.pallas_call(
        functools.partial(_gqa_win_kernel, seq_len=seq_len),
        out_shape=jax.ShapeDtypeStruct((n_batch * seq_len, MIX_PART), BF16),
        grid=(n_batch,),
        in_specs=[
            pl.BlockSpec((seq_len, GQA_IN), lambda b: (b, 0)),
            pl.BlockSpec((1, 1, GQA_Q_HEADS), lambda b: (layer, 0, 0)),
            pl.BlockSpec((1, 1, GQA_KV_HEADS, past, HEAD_DIM), lambda b: (b, layer, 0, 0, 0)),
            pl.BlockSpec((1, 1, GQA_KV_HEADS, past, HEAD_DIM), lambda b: (b, layer, 0, 0, 0)),
            pl.BlockSpec((seq_len, 128), lambda b: (0, 0)),
            pl.BlockSpec((seq_len, 128), lambda b: (0, 0)),
        ],
        out_specs=pl.BlockSpec((seq_len, MIX_PART), lambda b: (b, 0)),
        scratch_shapes=[
            pltpu.VMEM((seq_len, GQA_Q_HEADS * HEAD_DIM), BF16),
            pltpu.VMEM((seq_len + 2 * BLOCK, GQA_KV_HEADS * HEAD_DIM), BF16),
            pltpu.VMEM((seq_len + 2 * BLOCK, GQA_KV_HEADS * HEAD_DIM), BF16),
        ],
        compiler_params=_params(1),
        name=f"gqa_win_{seq_len}",
    )(gqa_in, sink, cache_k, cache_v, cos, sin)


def _mla_kernel(*refs, seq_len, latent, bq):
    if latent:
        (x_ref, qn_ref, kvn_ref, wq_ref, wk_ref, wv_ref, ckv_c_ref, kr_c_ref, cos_ref, sin_ref,
         y_ref, q_scr, kn_scr, kr_scr, v_scr) = refs
        past = ckv_c_ref.shape[2]
    else:
        (x_ref, qn_ref, kvn_ref, wq_ref, wk_ref, wv_ref,
         y_ref, ckv_out_ref, kr_out_ref, q_scr, kn_scr, kr_scr, v_scr) = refs
        past = 0
    nope_w = MLA_HEADS * MLA_NOPE
    scale = (MLA_NOPE + MLA_ROPE) ** -0.5
    q_lat = x_ref[:, 0:MLA_Q_RANK]
    kv_lat = x_ref[:, MLA_Q_RANK:MLA_Q_RANK + MLA_KV_RANK]
    k_rope = x_ref[:, MLA_Q_RANK + MLA_KV_RANK:MLA_Q_RANK + MLA_KV_RANK + 128]
    mq = _dot(_rms(q_lat, qn_ref[0]).astype(BF16), wq_ref[0])
    ckv = _rms(kv_lat, kvn_ref[0])
    q_rope = mq[:, nope_w:nope_w + 128]
    if latent:
        cos = cos_ref[...]
        sin = sin_ref[...]
        q_rope = q_rope * cos + _rot_half(q_rope, MLA_ROPE // 2) * sin
        k_rope = k_rope * cos + _rot_half(k_rope, MLA_ROPE // 2) * sin
        ckv_c = ckv_c_ref[0, 0].astype(BF16)
        kn_scr[0:past, :] = _dot(ckv_c, wk_ref[0]).astype(BF16)
        v_scr[0:past, :] = _dot(ckv_c, wv_ref[0]).astype(BF16)
        kr_scr[0:past, :] = kr_c_ref[0, 0].astype(BF16)
    else:
        ckv_out_ref[0] = ckv
        kr_out_ref[0] = k_rope[:, 0:MLA_ROPE]
    q_scr[:, 0:nope_w] = mq[:, 0:nope_w].astype(BF16)
    q_scr[:, nope_w:nope_w + 128] = q_rope.astype(BF16)
    ckv_b = ckv.astype(BF16)
    kn_scr[past:past + seq_len, :] = _dot(ckv_b, wk_ref[0]).astype(BF16)
    v_scr[past:past + seq_len, :] = _dot(ckv_b, wv_ref[0]).astype(BF16)
    kr_scr[past:past + seq_len, :] = k_rope[:, 0:MLA_ROPE].astype(BF16)

    kr_all = kr_scr[...]
    for h in range(MLA_HEADS):
        kn = kn_scr[:, h * MLA_NOPE:(h + 1) * MLA_NOPE]
        vv = v_scr[:, h * MLA_V:(h + 1) * MLA_V]
        for r0 in range(0, seq_len, bq):
            qn = q_scr[r0:r0 + bq, h * MLA_NOPE:(h + 1) * MLA_NOPE]
            qr = q_scr[r0:r0 + bq, nope_w + h * MLA_ROPE:nope_w + (h + 1) * MLA_ROPE]
            s = (_dot_nt(qn, kn) + _dot_nt(qr, kr_all)) * scale
            (p,), den = _softmax_parts([s], None)
            o = _dot(p.astype(BF16), vv) / den
            y_ref[r0:r0 + bq, h * MLA_V:(h + 1) * MLA_V] = o.astype(BF16)


def _mla(mla_in, q_norm, kv_norm, w_uq, w_uk, w_uv, cache, rope, layer, n_batch, seq_len):
    latent = cache is not None
    past = cache[0].shape[2] if latent else 0
    kern = functools.partial(_mla_kernel, seq_len=seq_len, latent=latent, bq=min(seq_len, 256))
    q_w = MLA_HEADS * (MLA_NOPE + MLA_ROPE)
    kv_w = MLA_HEADS * MLA_NOPE
    in_specs = [
        pl.BlockSpec((seq_len, MLA_IN_PAD), lambda b: (b, 0)),
        pl.BlockSpec((1, 1, MLA_Q_RANK), lambda b: (layer, 0, 0)),
        pl.BlockSpec((1, 1, MLA_KV_RANK), lambda b: (layer, 0, 0)),
        pl.BlockSpec((1, MLA_Q_RANK, q_w), lambda b: (layer, 0, 0)),
        pl.BlockSpec((1, MLA_KV_RANK, kv_w), lambda b: (layer, 0, 0)),
        pl.BlockSpec((1, MLA_KV_RANK, kv_w), lambda b: (layer, 0, 0)),
    ]
    args = [mla_in, q_norm, kv_norm, w_uq, w_uk, w_uv]
    y_shape = jax.ShapeDtypeStruct((n_batch * seq_len, MIX_PART), BF16)
    y_spec = pl.BlockSpec((seq_len, MIX_PART), lambda b: (b, 0))
    if latent:
        in_specs += [
            pl.BlockSpec((1, 1, past, MLA_KV_RANK), lambda b: (b, layer, 0, 0)),
            pl.BlockSpec((1, 1, past, MLA_ROPE), lambda b: (b, layer, 0, 0)),
            pl.BlockSpec((seq_len, 128), lambda b: (0, 0)),
            pl.BlockSpec((seq_len, 128), lambda b: (0, 0)),
        ]
        args += [cache[0], cache[1], rope[0], rope[1]]
        out_shape, out_specs = y_shape, y_spec
    else:
        out_shape = [y_shape,
                     jax.ShapeDtypeStruct((n_batch, seq_len, MLA_KV_RANK), F32),
                     jax.ShapeDtypeStruct((n_batch, seq_len, MLA_ROPE), F32)]
        out_specs = [y_spec,
                     pl.BlockSpec((1, seq_len, MLA_KV_RANK), lambda b: (b, 0, 0)),
                     pl.BlockSpec((1, seq_len, MLA_ROPE), lambda b: (b, 0, 0))]
    keys = past + seq_len
    return pl.pallas_call(
        kern, out_shape=out_shape, grid=(n_batch,), in_specs=in_specs, out_specs=out_specs,
        scratch_shapes=[
            pltpu.VMEM((seq_len, q_w), BF16),
            pltpu.VMEM((keys, kv_w), BF16),
            pltpu.VMEM((keys, MLA_ROPE), BF16),
            pltpu.VMEM((keys, kv_w), BF16),
        ],
        compiler_params=_params(1),
        name=f"mla_{seq_len}",
    )(*args)


def _dft_matrix(seq_len):
    k = np.arange(seq_len, dtype=np.float64)[:, None]
    j = np.arange(seq_len, dtype=np.float64)[None, :]
    ang = np.pi * k * j / seq_len
    re = np.cos(ang)
    im = -np.sin(ang)
    im[0, :] = np.where(np.arange(seq_len) % 2 == 0, 1.0, -1.0)
    return np.concatenate([re, im], axis=0).astype(np.float32)


def _hyena_positions(seq_len):
    t = np.arange(seq_len, dtype=np.float64) / seq_len
    bands = np.arange(1, HY_BANDS + 1, dtype=np.float64)
    ang = 2.0 * math.pi * t[:, None] * bands
    z = np.concatenate([t[:, None], np.cos(ang), np.sin(ang)], axis=-1)
    return np.pad(z, ((0, 0), (0, HY_POS_PAD - HY_POS_DIM))).astype(np.float32)


def _rope_tables(n_tokens, dim):
    rows = np.repeat(np.arange(n_tokens // GRID_W, dtype=np.float64), GRID_W)
    cols = np.tile(np.arange(GRID_W, dtype=np.float64), n_tokens // GRID_W)
    quarter = dim // 4
    inv = ROPE_BASE ** (-np.arange(quarter, dtype=np.float64) / quarter)
    ang = np.concatenate([rows[:, None] * inv, cols[:, None] * inv], axis=-1)
    cos = np.concatenate([np.cos(ang), np.cos(ang)], axis=-1)
    sin = np.concatenate([-np.sin(ang), np.sin(ang)], axis=-1)
    reps = 128 // dim
    return (np.tile(cos, (1, reps)).astype(np.float32), np.tile(sin, (1, reps)).astype(np.float32))


def kernel(x_prompt, x_sample, state_ret, cache_gqa_k, cache_gqa_v, cache_mla_ckv, cache_mla_krope, c, c_ctx, ada_w, ada_b, norm_g, w_in, ret_decay_logit, ret_gn_g, hy_short_w, hy_short_b, hy_w1, hy_b1, hy_w2, hy_b2, hy_w3, hy_decay, hy_bias, gqa_sink, mla_q_norm, mla_kv_norm, mla_w_uq, mla_w_uk, mla_w_uv, w_out, ffn_w_up, ffn_conv_w, ffn_conv_b, ffn_w_down):
    n_p, len_p, _ = x_prompt.shape
    n_s, len_s, _ = x_sample.shape

    w_in_b = jnp.pad(w_in, ((0, 0), (0, 0), (0, IN_WIDTH_PAD - IN_WIDTH))).astype(BF16)
    w_out_b = w_out.astype(BF16)
    up = ffn_w_up.reshape(DEPTH, D_MODEL, 2, N_FFN_CHUNKS, FFN_CHUNK).astype(BF16)
    w_gate_b = up[:, :, 0].transpose(0, 2, 1, 3)
    w_up_b = up[:, :, 1].transpose(0, 2, 1, 3)
    w_down_b = ffn_w_down.reshape(DEPTH, N_FFN_CHUNKS, FFN_CHUNK, D_MODEL).astype(BF16)
    uq = mla_w_uq.reshape(DEPTH, MLA_Q_RANK, MLA_HEADS, MLA_NOPE + MLA_ROPE)
    w_uq_b = jnp.concatenate(
        [uq[..., :MLA_NOPE].reshape(DEPTH, MLA_Q_RANK, MLA_HEADS * MLA_NOPE),
         uq[..., MLA_NOPE:].reshape(DEPTH, MLA_Q_RANK, MLA_HEADS * MLA_ROPE)], axis=-1).astype(BF16)
    w_uk_b = mla_w_uk.astype(BF16)
    w_uv_b = mla_w_uv.astype(BF16)

    cond = jnp.concatenate([c_ctx[None], c, jnp.zeros((N_COND - 1 - n_s, D_MODEL), F32)], axis=0)
    mod = _modulation(cond, ada_w, ada_b).reshape(DEPTH, N_COND, 6, D_MODEL)

    w1p = jnp.pad(hy_w1, ((0, 0), (0, HY_POS_PAD - HY_POS_DIM), (0, 0)))
    b1 = hy_b1.reshape(DEPTH, 1, HY_FILTER_HIDDEN)
    b2 = hy_b2.reshape(DEPTH, 1, HY_FILTER_HIDDEN)
    groups = {}
    for seq_len in (len_p, len_s):
        f32mat = _dft_matrix(seq_len)
        fmat = jnp.asarray(f32mat).astype(BF16)
        fmat_t = jnp.asarray(np.ascontiguousarray(f32mat.T)).astype(BF16)
        tables = _hyena_tables(seq_len, jnp.asarray(_hyena_positions(seq_len)), w1p, b1, hy_w2, b2,
                               hy_w3, hy_decay, fmat)
        groups[seq_len] = (tables, fmat, fmat_t)

    rope_g = tuple(jnp.asarray(t) for t in _rope_tables(len_s, HEAD_DIM))
    rope_m = tuple(jnp.asarray(t) for t in _rope_tables(len_s, MLA_ROPE))

    gn_g = ret_gn_g.reshape(DEPTH, 1, RET_WIDTH)
    short_b = hy_short_b.reshape(DEPTH, 1, HY_IN)
    sink = gqa_sink.reshape(DEPTH, 1, GQA_Q_HEADS)
    q_norm = mla_q_norm.reshape(DEPTH, 1, MLA_Q_RANK)
    kv_norm = mla_kv_norm.reshape(DEPTH, 1, MLA_KV_RANK)
    conv_b = ffn_conv_b.reshape(DEPTH, 1, D_FF)

    xp = x_prompt.reshape(n_p * len_p, D_MODEL)
    xs = x_sample.reshape(n_s * len_s, D_MODEL)
    tm_p = 512
    tm_s = 512
    cond_p = lambda i: 0
    cond_s = lambda i: 1 + (i * tm_s) // len_s
    tm_ffn = 1024
    cond_s_ffn = lambda i: 1 + (i * tm_ffn) // len_s

    ret_states, gqa_ks, gqa_vs, mla_ckvs, mla_krs = [], [], [], [], []
    for l in range(DEPTH):
        tables, fmat, fmat_t = groups[len_p]
        ret_in, hy_in, gqa_in, mla_in = _in_proj(xp, mod, norm_g, w_in_b, l, tm_p, cond_p)
        y_ret, st = _retention(ret_in, ret_decay_logit, gn_g, None, l, n_p, len_p)
        y_hy = _hyena(hy_in, hy_short_w, short_b, hy_bias, tables, fmat, fmat_t, l, n_p, len_p)
        y_gqa, gk, gv = _gqa_ctx(gqa_in, sink, l, n_p, len_p)
        y_mla, ckv, kr = _mla(mla_in, q_norm, kv_norm, w_uq_b, w_uk_b, w_uv_b, None, None, l, n_p, len_p)
        xm, h2 = _out_proj(xp, (y_ret, y_hy, y_gqa, y_mla), mod, norm_g, w_out_b, l, tm_p, cond_p)
        xp = _ffn(h2, xm, mod, norm_g, w_gate_b, w_up_b, ffn_conv_w, conv_b, w_down_b, l, tm_ffn, len_p, cond_p)
        ret_states.append(st)
        gqa_ks.append(gk)
        gqa_vs.append(gv)
        mla_ckvs.append(ckv)
        mla_krs.append(kr)
        tables, fmat, fmat_t = groups[len_s]
        ret_in, hy_in, gqa_in, mla_in = _in_proj(xs, mod, norm_g, w_in_b, l, tm_s, cond_s)
        y_ret = _retention(ret_in, ret_decay_logit, gn_g, state_ret, l, n_s, len_s)
        y_hy = _hyena(hy_in, hy_short_w, short_b, hy_bias, tables, fmat, fmat_t, l, n_s, len_s)
        y_gqa = _gqa_win(gqa_in, sink, cache_gqa_k, cache_gqa_v, rope_g[0], rope_g[1], l, n_s, len_s)
        y_mla = _mla(mla_in, q_norm, kv_norm, w_uq_b, w_uk_b, w_uv_b, (cache_mla_ckv, cache_mla_krope),
                     rope_m, l, n_s, len_s)
        xm, h2 = _out_proj(xs, (y_ret, y_hy, y_gqa, y_mla), mod, norm_g, w_out_b, l, tm_s, cond_s)
        xs = _ffn(h2, xm, mod, norm_g, w_gate_b, w_up_b, ffn_conv_w, conv_b, w_down_b, l, tm_ffn, len_s, cond_s_ffn)

    return (xp.reshape(n_p, len_p, D_MODEL), xs.reshape(n_s, len_s, D_MODEL),
            jnp.stack(ret_states, axis=1), jnp.stack(gqa_ks, axis=1), jnp.stack(gqa_vs, axis=1),
            jnp.stack(mla_ckvs, axis=1), jnp.stack(mla_krs, axis=1))
```

```python
import functools
import math

import numpy as np
import jax
import jax.numpy as jnp
from jax import lax
from jax.experimental import pallas as pl
from jax.experimental.pallas import tpu as pltpu

F32 = jnp.float32
BF16 = jnp.bfloat16

D_MODEL = 1024
DEPTH = 4
GRID_W = 64
EPS = 1e-6
NEG_INF = -1e30
ROPE_BASE = 10000.0
BLOCK = 128
WINDOW = 128
HEAD_DIM = 64
RET_HEADS = 4
RET_WIDTH = RET_HEADS * HEAD_DIM
HY_WIDTH = 256
HY_ORDER = 2
HY_BANDS = 8
HY_POS_DIM = 1 + 2 * HY_BANDS
HY_POS_PAD = 32
HY_FILTER_HIDDEN = 64
GQA_Q_HEADS = 4
GQA_KV_HEADS = 2
GQA_GROUPS = GQA_Q_HEADS // GQA_KV_HEADS
MLA_HEADS = 4
MLA_Q_RANK = 256
MLA_KV_RANK = 128
MLA_NOPE = 64
MLA_ROPE = 32
MLA_V = 64
D_FF = 2816
RET_IN = 4 * RET_WIDTH
HY_IN = (HY_ORDER + 1) * HY_WIDTH
GQA_IN = (GQA_Q_HEADS + 2 * GQA_KV_HEADS) * HEAD_DIM
MLA_IN = MLA_Q_RANK + MLA_KV_RANK + MLA_ROPE
MLA_IN_PAD = 512
IN_WIDTH = RET_IN + HY_IN + GQA_IN + MLA_IN
IN_WIDTH_PAD = RET_IN + HY_IN + GQA_IN + MLA_IN_PAD
MIX_PART = 256
N_COND = 8
FFN_CHUNK = 256
N_FFN_CHUNKS = D_FF // FFN_CHUNK

VMEM_LIMIT = 56 * 1024 * 1024


def _params(n_axes=1):
    return pltpu.CompilerParams(
        dimension_semantics=("arbitrary",) * n_axes, vmem_limit_bytes=VMEM_LIMIT)


def _dot(a, b):
    return jnp.dot(a, b, preferred_element_type=F32)


def _dot_nt(a, b):
    return lax.dot_general(a, b, (((1,), (1,)), ((), ())), preferred_element_type=F32)


def _dot_tn(a, b):
    return lax.dot_general(a, b, (((0,), (0,)), ((), ())), preferred_element_type=F32)


def _rms(x, g):
    return x * lax.rsqrt(jnp.mean(x * x, axis=-1, keepdims=True) + EPS) * g


def _sigmoid(x):
    return 1.0 / (1.0 + jnp.exp(-x))


def _silu(x):
    return x * _sigmoid(x)


def _shift_rows(x, seq_len):
    n = x.shape[0]
    assert seq_len & (seq_len - 1) == 0
    pos = lax.broadcasted_iota(jnp.int32, (n, 1), 0) & (seq_len - 1)
    prev = jnp.where(pos != 0, pltpu.roll(x, 1, axis=0), 0.0)
    nxt = jnp.where(pos != seq_len - 1, pltpu.roll(x, n - 1, axis=0), 0.0)
    return prev, nxt


def _rot_half(x, half):
    lane = lax.broadcasted_iota(jnp.int32, x.shape, 1) & (2 * half - 1)
    return jnp.where(lane < half, pltpu.roll(x, 128 - half, axis=1), pltpu.roll(x, half, axis=1))


def _mod_kernel(cond_ref, w_ref, b_ref, out_ref):
    s = _silu(cond_ref[...]).astype(BF16)
    out_ref[0] = _dot(s, w_ref[0].astype(BF16)) + b_ref[0]


def _modulation(cond, ada_w, ada_b):
    tn = 1536
    return pl.pallas_call(
        _mod_kernel,
        out_shape=jax.ShapeDtypeStruct((DEPTH, N_COND, 6 * D_MODEL), F32),
        grid=(DEPTH, 6 * D_MODEL // tn),
        in_specs=[
            pl.BlockSpec((N_COND, D_MODEL), lambda l, j: (0, 0)),
            pl.BlockSpec((1, D_MODEL, tn), lambda l, j: (l, 0, j)),
            pl.BlockSpec((1, 1, tn), lambda l, j: (l, 0, j)),
        ],
        out_specs=pl.BlockSpec((1, N_COND, tn), lambda l, j: (l, 0, j)),
        compiler_params=_params(2),
        name="modulation",
    )(cond, ada_w, ada_b.reshape(DEPTH, 1, 6 * D_MODEL))


def _filter_kernel(z_ref, w1_ref, b1_ref, w2_ref, b2_ref, w3_ref, dec_ref, f_ref, out_ref, *, seq_len):
    hi = lax.Precision.HIGHEST
    h = jnp.sin(jnp.dot(z_ref[...], w1_ref[0], precision=hi, preferred_element_type=F32) + b1_ref[0])
    h = jnp.sin(jnp.dot(h, w2_ref[0], precision=hi, preferred_element_type=F32) + b2_ref[0])
    h = jnp.dot(h, w3_ref[0], precision=hi, preferred_element_type=F32)
    row = lax.broadcasted_iota(jnp.int32, (seq_len, 1), 0)
    t = row.astype(F32) / seq_len
    dec = jnp.abs(dec_ref[0])
    win_f = jnp.exp(-t * dec[0:1, :])
    win_b = jnp.exp(-t * dec[1:2, :])
    fm = f_ref[...]
    inv = 1.0 / seq_len
    scale = jnp.where(row == 0, 0.5 * inv, inv)
    for o in range(HY_ORDER):
        base = o * 2 * HY_WIDTH
        hf = h[:, base:base + HY_WIDTH] * win_f
        hb = h[:, base + HY_WIDTH:base + 2 * HY_WIDTH] * win_b
        hbs = jnp.where(row != 0, pltpu.roll(hb, 1, axis=0), 0.0)
        sf = _dot(fm, hf.astype(BF16))
        sb = _dot(fm, hbs.astype(BF16))
        g_re = sf[:seq_len] + sb[:seq_len]
        g_im = sf[seq_len:] - sb[seq_len:]
        g_ny = sf[seq_len:] + sb[seq_len:]
        a = g_re * scale
        out_ref[0, 3 * o] = a
        out_ref[0, 3 * o + 1] = jnp.where(row == 0, g_ny * (0.5 * inv), a)
        out_ref[0, 3 * o + 2] = jnp.where(row == 0, 0.0, g_im * inv)


def _hyena_tables(seq_len, z, w1p, b1, w2, b2, w3, decay, fmat):
    hid = HY_FILTER_HIDDEN
    wide = HY_ORDER * 2 * HY_WIDTH
    return pl.pallas_call(
        functools.partial(_filter_kernel, seq_len=seq_len),
        out_shape=jax.ShapeDtypeStruct((DEPTH, 3 * HY_ORDER, seq_len, HY_WIDTH), F32),
        grid=(DEPTH,),
        in_specs=[
            pl.BlockSpec((seq_len, HY_POS_PAD), lambda l: (0, 0)),
            pl.BlockSpec((1, HY_POS_PAD, hid), lambda l: (l, 0, 0)),
            pl.BlockSpec((1, 1, hid), lambda l: (l, 0, 0)),
            pl.BlockSpec((1, hid, hid), lambda l: (l, 0, 0)),
            pl.BlockSpec((1, 1, hid), lambda l: (l, 0, 0)),
            pl.BlockSpec((1, hid, wide), lambda l: (l, 0, 0)),
            pl.BlockSpec((1, 2, HY_WIDTH), lambda l: (l, 0, 0)),
            pl.BlockSpec((2 * seq_len, seq_len), lambda l: (0, 0)),
        ],
        out_specs=pl.BlockSpec((1, 3 * HY_ORDER, seq_len, HY_WIDTH), lambda l: (l, 0, 0, 0)),
        compiler_params=_params(1),
        name=f"hyena_tables_{seq_len}",
    )(z, w1p, b1, w2, b2, w3, decay, fmat)


def _layer_stacked(tail, n_batch, layer):
    zeros = (0,) * len(tail)
    return (jax.ShapeDtypeStruct((n_batch, DEPTH) + tail, F32),
            pl.BlockSpec((1, 1) + tail, lambda b: (b, layer) + zeros))


def _carry_specs(in_specs, args, carried, first_out):
    aliases = {}
    if carried is not None:
        for k, arr in enumerate(carried):
            aliases[len(args)] = first_out + k
            in_specs.append(pl.BlockSpec(memory_space=pl.ANY))
            args.append(arr)
    return aliases


def _in_proj_kernel(x_ref, mod_ref, g_ref, w_ref, ret_ref, hy_ref, gqa_ref, mla_ref):
    shift = mod_ref[0, 0, 0:1, :]
    scale = mod_ref[0, 0, 1:2, :]
    h = (_rms(x_ref[...], g_ref[0, 0:1, :]) * (1.0 + scale) + shift).astype(BF16)
    c0 = 0
    for ref in (ret_ref, hy_ref, gqa_ref, mla_ref):
        width = ref.shape[1]
        ref[...] = _dot(h, w_ref[0, :, c0:c0 + width])
        c0 += width


def _in_proj(x, mod, norm_g, w_in, layer, tm, cond_of_tile):
    rows = x.shape[0]
    widths = (RET_IN, HY_IN, GQA_IN, MLA_IN_PAD)
    return pl.pallas_call(
        _in_proj_kernel,
        out_shape=[jax.ShapeDtypeStruct((rows, w), F32) for w in widths],
        grid=(rows // tm,),
        in_specs=[
            pl.BlockSpec((tm, D_MODEL), lambda i: (i, 0)),
            pl.BlockSpec((1, 1, 6, D_MODEL), lambda i: (layer, cond_of_tile(i), 0, 0)),
            pl.BlockSpec((1, 4, D_MODEL), lambda i: (layer, 0, 0)),
            pl.BlockSpec((1, D_MODEL, IN_WIDTH_PAD), lambda i: (layer, 0, 0)),
        ],
        out_specs=[pl.BlockSpec((tm, w), lambda i: (i, 0)) for w in widths],
        compiler_params=_params(1),
        name="in_proj",
    )(x, mod, norm_g, w_in)


def _out_proj_kernel(x_ref, m0_ref, m1_ref, m2_ref, m3_ref, mod_ref, g_ref, w_ref, xm_ref, h2_ref):
    y = None
    for i, m_ref in enumerate((m0_ref, m1_ref, m2_ref, m3_ref)):
        part = _dot(m_ref[...], w_ref[0, i * MIX_PART:(i + 1) * MIX_PART, :])
        y = part if y is None else y + part
    gate1 = mod_ref[0, 0, 2:3, :]
    shift2 = mod_ref[0, 0, 3:4, :]
    scale2 = mod_ref[0, 0, 4:5, :]
    xm = x_ref[...] + gate1 * _rms(y, g_ref[0, 1:2, :])
    xm_ref[...] = xm
    h2_ref[...] = (_rms(xm, g_ref[0, 2:3, :]) * (1.0 + scale2) + shift2).astype(BF16)


def _out_proj(x, mixes, mod, norm_g, w_out, layer, tm, cond_of_tile):
    rows = x.shape[0]
    return pl.pallas_call(
        _out_proj_kernel,
        out_shape=[jax.ShapeDtypeStruct((rows, D_MODEL), F32),
                   jax.ShapeDtypeStruct((rows, D_MODEL), BF16)],
        grid=(rows // tm,),
        in_specs=[pl.BlockSpec((tm, D_MODEL), lambda i: (i, 0))]
        + [pl.BlockSpec((tm, MIX_PART), lambda i: (i, 0))] * 4
        + [
            pl.BlockSpec((1, 1, 6, D_MODEL), lambda i: (layer, cond_of_tile(i), 0, 0)),
            pl.BlockSpec((1, 4, D_MODEL), lambda i: (layer, 0, 0)),
            pl.BlockSpec((1, D_MODEL, D_MODEL), lambda i: (layer, 0, 0)),
        ],
        out_specs=[pl.BlockSpec((tm, D_MODEL), lambda i: (i, 0))] * 2,
        compiler_params=_params(1),
        name="out_proj",
    )(x, *mixes, mod, norm_g, w_out)


def _ffn_kernel(h2_ref, xm_ref, mod_ref, g_ref, wu_ref, cw_ref, cb_ref, wd_ref, out_ref,
                acc_ref, *, seq_len):
    h2 = h2_ref[...]
    for c in range(N_FFN_CHUNKS):
        sl = slice(c * FFN_CHUNK, (c + 1) * FFN_CHUNK)
        gate = _dot(h2, wu_ref[0, :, sl])
        up = _dot(h2, wu_ref[0, :, D_FF + c * FFN_CHUNK:D_FF + (c + 1) * FFN_CHUNK])
        prev, nxt = _shift_rows(gate, seq_len)
        gate = (prev * cw_ref[0, 0:1, sl] + gate * cw_ref[0, 1:2, sl] + nxt * cw_ref[0, 2:3, sl]
                + cb_ref[0, :, sl])
        act = (_silu(gate) * up).astype(BF16)
        part = _dot(act, wd_ref[0, sl, :])
        if c == 0:
            acc_ref[...] = part
        else:
            acc_ref[...] += part
    gate2 = mod_ref[0, 0, 5:6, :]
    out_ref[...] = xm_ref[...] + gate2 * _rms(acc_ref[...], g_ref[0, 3:4, :])


def _ffn(h2, xm, mod, norm_g, w_up, conv_w, conv_b, w_down, layer, tm, seq_len, cond_of_tile):
    rows = xm.shape[0]
    resident = dict(pipeline_mode=pl.Buffered(1))
    return pl.pallas_call(
        functools.partial(_ffn_kernel, seq_len=seq_len),
        out_shape=jax.ShapeDtypeStruct((rows, D_MODEL), F32),
        grid=(rows // tm,),
        in_specs=[
            pl.BlockSpec((tm, D_MODEL), lambda i: (i, 0)),
            pl.BlockSpec((tm, D_MODEL), lambda i: (i, 0)),
            pl.BlockSpec((1, 1, 6, D_MODEL), lambda i: (layer, cond_of_tile(i), 0, 0)),
            pl.BlockSpec((1, 4, D_MODEL), lambda i: (layer, 0, 0)),
            pl.BlockSpec((1, D_MODEL, 2 * D_FF), lambda i: (layer, 0, 0), **resident),
            pl.BlockSpec((1, 3, D_FF), lambda i: (layer, 0, 0)),
            pl.BlockSpec((1, 1, D_FF), lambda i: (layer, 0, 0)),
            pl.BlockSpec((1, D_FF, D_MODEL), lambda i: (layer, 0, 0), **resident),
        ],
        out_specs=pl.BlockSpec((tm, D_MODEL), lambda i: (i, 0)),
        scratch_shapes=[pltpu.VMEM((tm, D_MODEL), F32)],
        compiler_params=_params(1),
        name="conv_ffn",
    )(h2, xm, mod, norm_g, w_up, conv_w, conv_b, w_down)


def _ret_kernel(*refs, seq_len, latent, bq):
    if latent:
        x_ref, dl_ref, gn_ref, s0_ref, y_ref = refs
    else:
        x_ref, dl_ref, gn_ref, *_, y_ref, st_ref = refs
    dl = dl_ref[0]
    lg = jnp.minimum(dl, 0.0) - jnp.log(1.0 + jnp.exp(-jnp.abs(dl)))
    colf = lax.broadcasted_iota(jnp.int32, (1, seq_len), 1).astype(F32)
    posf = lax.broadcasted_iota(jnp.int32, (seq_len, 1), 0).astype(F32)
    for h in range(RET_HEADS):
        lo = h * HEAD_DIM
        lgf = lg[0:1, h:h + 1]
        lgb = lg[1:2, h:h + 1]
        k = x_ref[:, RET_WIDTH + lo:RET_WIDTH + lo + HEAD_DIM] * (HEAD_DIM ** -0.5)
        v = x_ref[:, 2 * RET_WIDTH + lo:2 * RET_WIDTH + lo + HEAD_DIM]
        kb = k.astype(BF16)
        vb = v.astype(BF16)
        if not latent:
            kf = (k * jnp.exp((seq_len - 1.0 - posf) * lgf)).astype(BF16)
            kr = (k * jnp.exp(posf * lgb)).astype(BF16)
            st_ref[0, 0, 0, h] = _dot_tn(kf, vb)
            st_ref[0, 0, 1, h] = _dot_tn(kr, vb)
        for r0 in range(0, seq_len, bq):
            q = x_ref[r0:r0 + bq, lo:lo + HEAD_DIM]
            rowf = (lax.broadcasted_iota(jnp.int32, (bq, 1), 0) + r0).astype(F32)
            lag = rowf - colf
            decay = jnp.exp(jnp.where(lag >= 0.0, lag * lgf, -lag * lgb))
            decay = jnp.where(lag == 0.0, 2.0, decay)
            s = _dot_nt(q.astype(BF16), kb) * decay
            o = _dot(s.astype(BF16), vb)
            if latent:
                qf = (q * jnp.exp((rowf + 1.0) * lgf)).astype(BF16)
                qr = (q * jnp.exp((seq_len - rowf) * lgb)).astype(BF16)
                o = o + _dot(qf, s0_ref[0, 0, 0, h].astype(BF16)) + _dot(qr, s0_ref[0, 0, 1, h].astype(BF16))
            mu = jnp.mean(o, axis=-1, keepdims=True)
            d = o - mu
            var = jnp.mean(d * d, axis=-1, keepdims=True)
            on = d * lax.rsqrt(var + EPS)
            gate = x_ref[r0:r0 + bq, 3 * RET_WIDTH + lo:3 * RET_WIDTH + lo + HEAD_DIM]
            y = _silu(gate) * (on * gn_ref[0, :, lo:lo + HEAD_DIM])
            y_ref[r0:r0 + bq, lo:lo + HEAD_DIM] = y.astype(BF16)


def _retention(ret_in, decay_logit, gn_g, state0, carried, layer, n_batch, seq_len):
    latent = state0 is not None
    aliases = {}
    kern = functools.partial(_ret_kernel, seq_len=seq_len, latent=latent, bq=min(seq_len, 256))
    in_specs = [
        pl.BlockSpec((seq_len, RET_IN), lambda b: (b, 0)),
        pl.BlockSpec((1, 2, RET_HEADS), lambda b: (layer, 0, 0)),
        pl.BlockSpec((1, 1, RET_WIDTH), lambda b: (layer, 0, 0)),
    ]
    args = [ret_in, decay_logit, gn_g]
    y_shape = jax.ShapeDtypeStruct((n_batch * seq_len, MIX_PART), BF16)
    y_spec = pl.BlockSpec((seq_len, MIX_PART), lambda b: (b, 0))
    if latent:
        in_specs.append(pl.BlockSpec((1, 1, 2, RET_HEADS, HEAD_DIM, HEAD_DIM),
                                     lambda b: (b, layer, 0, 0, 0, 0)))
        args.append(state0)
        out_shape, out_specs = y_shape, y_spec
    else:
        st_shape, st_spec = _layer_stacked((2, RET_HEADS, HEAD_DIM, HEAD_DIM), n_batch, layer)
        out_shape = [y_shape, st_shape]
        out_specs = [y_spec, st_spec]
        aliases = _carry_specs(in_specs, args, carried, 1)
    return pl.pallas_call(
        kern, out_shape=out_shape, grid=(n_batch,), in_specs=in_specs, out_specs=out_specs,
        input_output_aliases=aliases,
        compiler_params=_params(1), name=f"retention_{seq_len}",
    )(*args)


def _hyena_kernel(x_ref, sw_ref, sb_ref, bias_ref, tab_ref, f_ref, ft_ref, y_ref, *, seq_len):
    x = x_ref[...]
    prev, nxt = _shift_rows(x, seq_len)
    u = prev * sw_ref[0, 0:1, :] + x * sw_ref[0, 1:2, :] + nxt * sw_ref[0, 2:3, :] + sb_ref[0]
    x1 = u[:, 0:HY_WIDTH]
    x2 = u[:, HY_WIDTH:2 * HY_WIDTH]
    z = u[:, 2 * HY_WIDTH:3 * HY_WIDTH]
    for o, gate in enumerate((x1, x2)):
        spec = _dot(f_ref[...], z.astype(BF16))
        s_re = spec[:seq_len]
        s_im = spec[seq_len:]
        a = tab_ref[0, 3 * o]
        a_ny = tab_ref[0, 3 * o + 1]
        b = tab_ref[0, 3 * o + 2]
        y_re = (s_re * a - s_im * b).astype(BF16)
        y_im = (s_re * b + s_im * a_ny).astype(BF16)
        conv = _dot(ft_ref[:, 0:seq_len], y_re) + _dot(ft_ref[:, seq_len:2 * seq_len], y_im)
        z = gate * (conv + z * bias_ref[0, o:o + 1, :])
    y_ref[...] = z.astype(BF16)


def _hyena(hy_in, short_w, short_b, hy_bias, tables, fmat, fmat_t, layer, n_batch, seq_len):
    const = dict(pipeline_mode=pl.Buffered(1))
    return pl.pallas_call(
        functools.partial(_hyena_kernel, seq_len=seq_len),
        out_shape=jax.ShapeDtypeStruct((n_batch * seq_len, MIX_PART), BF16),
        grid=(n_batch,),
        in_specs=[
            pl.BlockSpec((seq_len, HY_IN), lambda b: (b, 0)),
            pl.BlockSpec((1, 3, HY_IN), lambda b: (layer, 0, 0)),
            pl.BlockSpec((1, 1, HY_IN), lambda b: (layer, 0, 0)),
            pl.BlockSpec((1, HY_ORDER, HY_WIDTH), lambda b: (layer, 0, 0)),
            pl.BlockSpec((1, 3 * HY_ORDER, seq_len, HY_WIDTH), lambda b: (layer, 0, 0, 0), **const),
            pl.BlockSpec((2 * seq_len, seq_len), lambda b: (0, 0), **const),
            pl.BlockSpec((seq_len, 2 * seq_len), lambda b: (0, 0), **const),
        ],
        out_specs=pl.BlockSpec((seq_len, MIX_PART), lambda b: (b, 0)),
        compiler_params=_params(1),
        name=f"hyena_{seq_len}",
    )(hy_in, short_w, short_b, hy_bias, tables, fmat, fmat_t)


def _softmax_parts(scores, sink):
    m = None
    for s in scores:
        mi = jnp.max(s, axis=-1, keepdims=True)
        m = mi if m is None else jnp.maximum(m, mi)
    if sink is not None:
        m = jnp.maximum(m, sink)
    parts = [jnp.exp(s - m) for s in scores]
    den = None
    for p in parts:
        di = jnp.sum(p, axis=-1, keepdims=True)
        den = di if den is None else den + di
    if sink is not None:
        den = den + jnp.exp(sink - m)
    return parts, den


def _gqa_ctx_kernel(*refs, seq_len):
    x_ref, sink_ref, *_, y_ref, k_out_ref, v_out_ref = refs
    nq = GQA_Q_HEADS * HEAD_DIM
    nkv = GQA_KV_HEADS * HEAD_DIM
    scale = HEAD_DIM ** -0.5
    for kv in range(GQA_KV_HEADS):
        k = x_ref[:, nq + kv * HEAD_DIM:nq + (kv + 1) * HEAD_DIM]
        v = x_ref[:, nq + nkv + kv * HEAD_DIM:nq + nkv + (kv + 1) * HEAD_DIM]
        k_out_ref[0, 0, kv] = k
        v_out_ref[0, 0, kv] = v
        kb = k.astype(BF16)
        vb = v.astype(BF16)
        for g in range(GQA_GROUPS):
            j = kv * GQA_GROUPS + g
            q = x_ref[:, j * HEAD_DIM:(j + 1) * HEAD_DIM]
            s = _dot_nt(q.astype(BF16), kb) * scale
            (p,), den = _softmax_parts([s], sink_ref[0, :, j:j + 1])
            o = _dot(p.astype(BF16), vb) / den
            y_ref[:, j * HEAD_DIM:(j + 1) * HEAD_DIM] = o.astype(BF16)


def _gqa_ctx(gqa_in, sink, carried, layer, n_batch, seq_len):
    kv_shape, kv_spec = _layer_stacked((GQA_KV_HEADS, seq_len, HEAD_DIM), n_batch, layer)
    in_specs = [
        pl.BlockSpec((seq_len, GQA_IN), lambda b: (b, 0)),
        pl.BlockSpec((1, 1, GQA_Q_HEADS), lambda b: (layer, 0, 0)),
    ]
    args = [gqa_in, sink]
    aliases = _carry_specs(in_specs, args, carried, 1)
    return pl.pallas_call(
        functools.partial(_gqa_ctx_kernel, seq_len=seq_len),
        out_shape=[jax.ShapeDtypeStruct((n_batch * seq_len, MIX_PART), BF16), kv_shape, kv_shape],
        grid=(n_batch,),
        in_specs=in_specs,
        out_specs=[pl.BlockSpec((seq_len, MIX_PART), lambda b: (b, 0)), kv_spec, kv_spec],
        input_output_aliases=aliases,
        compiler_params=_params(1),
        name=f"gqa_ctx_{seq_len}",
    )(*args)


def _gqa_win_kernel(x_ref, sink_ref, kc_ref, vc_ref, cos_ref, sin_ref, y_ref, q_scr, k_scr, v_scr, *, seq_len):
    nq = GQA_Q_HEADS * HEAD_DIM
    nkv = GQA_KV_HEADS * HEAD_DIM
    scale = HEAD_DIM ** -0.5
    cos = cos_ref[...]
    sin = sin_ref[...]

    def rope(t):
        return t * cos + _rot_half(t, HEAD_DIM // 2) * sin

    for c in range(nq // 128):
        q_scr[:, c * 128:(c + 1) * 128] = rope(x_ref[:, c * 128:(c + 1) * 128]).astype(BF16)
    zeros = jnp.zeros((BLOCK, nkv), BF16)
    k_scr[0:BLOCK, :] = zeros
    k_scr[BLOCK + seq_len:2 * BLOCK + seq_len, :] = zeros
    v_scr[0:BLOCK, :] = zeros
    v_scr[BLOCK + seq_len:2 * BLOCK + seq_len, :] = zeros
    k_scr[BLOCK:BLOCK + seq_len, :] = rope(x_ref[:, nq:nq + nkv]).astype(BF16)
    v_scr[BLOCK:BLOCK + seq_len, :] = x_ref[:, nq + nkv:nq + 2 * nkv].astype(BF16)

    ii = lax.broadcasted_iota(jnp.int32, (BLOCK, 3 * BLOCK), 0)
    jj = lax.broadcasted_iota(jnp.int32, (BLOCK, 3 * BLOCK), 1)
    band = (jj >= ii) & (jj <= ii + 2 * WINDOW)

    def block(n, carry):
        r0 = pl.multiple_of(n * BLOCK, BLOCK)
        kpos = jj + (n - 1) * BLOCK
        valid = band & (kpos >= 0) & (kpos < seq_len)
        for kv in range(GQA_KV_HEADS):
            cs = slice(kv * HEAD_DIM, (kv + 1) * HEAD_DIM)
            kw = k_scr[pl.ds(r0, 3 * BLOCK), cs]
            vw = v_scr[pl.ds(r0, 3 * BLOCK), cs]
            kc = kc_ref[0, 0, kv].astype(BF16)
            vc = vc_ref[0, 0, kv].astype(BF16)
            for g in range(GQA_GROUPS):
                j = kv * GQA_GROUPS + g
                q = q_scr[pl.ds(r0, BLOCK), j * HEAD_DIM:(j + 1) * HEAD_DIM]
                s_win = jnp.where(valid, _dot_nt(q, kw) * scale, NEG_INF)
                s_ctx = _dot_nt(q, kc) * scale
                (p_win, p_ctx), den = _softmax_parts([s_win, s_ctx], sink_ref[0, :, j:j + 1])
                o = (_dot(p_win.astype(BF16), vw) + _dot(p_ctx.astype(BF16), vc)) / den
                y_ref[pl.ds(r0, BLOCK), j * HEAD_DIM:(j + 1) * HEAD_DIM] = o.astype(BF16)
        return carry

    lax.fori_loop(0, seq_len // BLOCK, block, 0)


def _gqa_win(gqa_in, sink, cache_k, cache_v, cos, sin, layer, n_batch, seq_len):
    past = cache_k.shape[3]
    return pl.pallas_call(
        functools.partial(_gqa_win_kernel, seq_len=seq_len),
        out_shape=jax.ShapeDtypeStruct((n_batch * seq_len, MIX_PART), BF16),
        grid=(n_batch,),
        in_specs=[
            pl.BlockSpec((seq_len, GQA_IN), lambda b: (b, 0)),
            pl.BlockSpec((1, 1, GQA_Q_HEADS), lambda b: (layer, 0, 0)),
            pl.BlockSpec((1, 1, GQA_KV_HEADS, past, HEAD_DIM), lambda b: (b, layer, 0, 0, 0)),
            pl.BlockSpec((1, 1, GQA_KV_HEADS, past, HEAD_DIM), lambda b: (b, layer, 0, 0, 0)),
            pl.BlockSpec((seq_len, 128), lambda b: (0, 0)),
            pl.BlockSpec((seq_len, 128), lambda b: (0, 0)),
        ],
        out_specs=pl.BlockSpec((seq_len, MIX_PART), lambda b: (b, 0)),
        scratch_shapes=[
            pltpu.VMEM((seq_len, GQA_Q_HEADS * HEAD_DIM), BF16),
            pltpu.VMEM((seq_len + 2 * BLOCK, GQA_KV_HEADS * HEAD_DIM), BF16),
            pltpu.VMEM((seq_len + 2 * BLOCK, GQA_KV_HEADS * HEAD_DIM), BF16),
        ],
        compiler_params=_params(1),
        name=f"gqa_win_{seq_len}",
    )(gqa_in, sink, cache_k, cache_v, cos, sin)


def _mla_kernel(*refs, seq_len, latent, bq):
    if latent:
        (x_ref, qn_ref, kvn_ref, wq_ref, wk_ref, wv_ref, ckv_c_ref, kr_c_ref, cos_ref, sin_ref,
         y_ref, q_scr, kn_scr, kr_scr, v_scr) = refs
        past = ckv_c_ref.shape[2]
    else:
        (x_ref, qn_ref, kvn_ref, wq_ref, wk_ref, wv_ref, *_,
         y_ref, ckv_out_ref, kr_out_ref, q_scr, kn_scr, kr_scr, v_scr) = refs
        past = 0
    nope_w = MLA_HEADS * MLA_NOPE
    scale = (MLA_NOPE + MLA_ROPE) ** -0.5
    q_lat = x_ref[:, 0:MLA_Q_RANK]
    kv_lat = x_ref[:, MLA_Q_RANK:MLA_Q_RANK + MLA_KV_RANK]
    k_rope = x_ref[:, MLA_Q_RANK + MLA_KV_RANK:MLA_Q_RANK + MLA_KV_RANK + 128]
    mq = _dot(_rms(q_lat, qn_ref[0]).astype(BF16), wq_ref[0])
    ckv = _rms(kv_lat, kvn_ref[0])
    q_rope = mq[:, nope_w:nope_w + 128]
    if latent:
        cos = cos_ref[...]
        sin = sin_ref[...]
        q_rope = q_rope * cos + _rot_half(q_rope, MLA_ROPE // 2) * sin
        k_rope = k_rope * cos + _rot_half(k_rope, MLA_ROPE // 2) * sin
        ckv_c = ckv_c_ref[0, 0].astype(BF16)
        kn_scr[0:past, :] = _dot(ckv_c, wk_ref[0]).astype(BF16)
        v_scr[0:past, :] = _dot(ckv_c, wv_ref[0]).astype(BF16)
        kr_scr[0:past, :] = kr_c_ref[0, 0].astype(BF16)
    else:
        ckv_out_ref[0, 0] = ckv
        kr_out_ref[0, 0] = k_rope[:, 0:MLA_ROPE]
    q_scr[:, 0:nope_w] = mq[:, 0:nope_w].astype(BF16)
    q_scr[:, nope_w:nope_w + 128] = q_rope.astype(BF16)
    ckv_b = ckv.astype(BF16)
    kn_scr[past:past + seq_len, :] = _dot(ckv_b, wk_ref[0]).astype(BF16)
    v_scr[past:past + seq_len, :] = _dot(ckv_b, wv_ref[0]).astype(BF16)
    kr_scr[past:past + seq_len, :] = k_rope[:, 0:MLA_ROPE].astype(BF16)

    kr_all = kr_scr[...]
    for h in range(MLA_HEADS):
        kn = kn_scr[:, h * MLA_NOPE:(h + 1) * MLA_NOPE]
        vv = v_scr[:, h * MLA_V:(h + 1) * MLA_V]
        for r0 in range(0, seq_len, bq):
            qn = q_scr[r0:r0 + bq, h * MLA_NOPE:(h + 1) * MLA_NOPE]
            qr = q_scr[r0:r0 + bq, nope_w + h * MLA_ROPE:nope_w + (h + 1) * MLA_ROPE]
            s = (_dot_nt(qn, kn) + _dot_nt(qr, kr_all)) * scale
            (p,), den = _softmax_parts([s], None)
            o = _dot(p.astype(BF16), vv) / den
            y_ref[r0:r0 + bq, h * MLA_V:(h + 1) * MLA_V] = o.astype(BF16)


def _mla(mla_in, q_norm, kv_norm, w_uq, w_uk, w_uv, cache, rope, carried, layer, n_batch, seq_len):
    latent = cache is not None
    aliases = {}
    past = cache[0].shape[2] if latent else 0
    kern = functools.partial(_mla_kernel, seq_len=seq_len, latent=latent, bq=min(seq_len, 256))
    q_w = MLA_HEADS * (MLA_NOPE + MLA_ROPE)
    kv_w = MLA_HEADS * MLA_NOPE
    in_specs = [
        pl.BlockSpec((seq_len, MLA_IN_PAD), lambda b: (b, 0)),
        pl.BlockSpec((1, 1, MLA_Q_RANK), lambda b: (layer, 0, 0)),
        pl.BlockSpec((1, 1, MLA_KV_RANK), lambda b: (layer, 0, 0)),
        pl.BlockSpec((1, MLA_Q_RANK, q_w), lambda b: (layer, 0, 0)),
        pl.BlockSpec((1, MLA_KV_RANK, kv_w), lambda b: (layer, 0, 0)),
        pl.BlockSpec((1, MLA_KV_RANK, kv_w), lambda b: (layer, 0, 0)),
    ]
    args = [mla_in, q_norm, kv_norm, w_uq, w_uk, w_uv]
    y_shape = jax.ShapeDtypeStruct((n_batch * seq_len, MIX_PART), BF16)
    y_spec = pl.BlockSpec((seq_len, MIX_PART), lambda b: (b, 0))
    if latent:
        in_specs += [
            pl.BlockSpec((1, 1, past, MLA_KV_RANK), lambda b: (b, layer, 0, 0)),
            pl.BlockSpec((1, 1, past, MLA_ROPE), lambda b: (b, layer, 0, 0)),
            pl.BlockSpec((seq_len, 128), lambda b: (0, 0)),
            pl.BlockSpec((seq_len, 128), lambda b: (0, 0)),
        ]
        args += [cache[0], cache[1], rope[0], rope[1]]
        out_shape, out_specs = y_shape, y_spec
    else:
        ckv_shape, ckv_spec = _layer_stacked((seq_len, MLA_KV_RANK), n_batch, layer)
        kr_shape, kr_spec = _layer_stacked((seq_len, MLA_ROPE), n_batch, layer)
        out_shape = [y_shape, ckv_shape, kr_shape]
        out_specs = [y_spec, ckv_spec, kr_spec]
        aliases = _carry_specs(in_specs, args, carried, 1)
    keys = past + seq_len
    return pl.pallas_call(
        kern, out_shape=out_shape, grid=(n_batch,), in_specs=in_specs, out_specs=out_specs,
        input_output_aliases=aliases,
        scratch_shapes=[
            pltpu.VMEM((seq_len, q_w), BF16),
            pltpu.VMEM((keys, kv_w), BF16),
            pltpu.VMEM((keys, MLA_ROPE), BF16),
            pltpu.VMEM((keys, kv_w), BF16),
        ],
        compiler_params=_params(1),
        name=f"mla_{seq_len}",
    )(*args)


def _dft_matrix(seq_len):
    k = np.arange(seq_len, dtype=np.float64)[:, None]
    j = np.arange(seq_len, dtype=np.float64)[None, :]
    ang = np.pi * k * j / seq_len
    re = np.cos(ang)
    im = -np.sin(ang)
    im[0, :] = np.where(np.arange(seq_len) % 2 == 0, 1.0, -1.0)
    return np.concatenate([re, im], axis=0).astype(np.float32)


def _hyena_positions(seq_len):
    t = np.arange(seq_len, dtype=np.float64) / seq_len
    bands = np.arange(1, HY_BANDS + 1, dtype=np.float64)
    ang = 2.0 * math.pi * t[:, None] * bands
    z = np.concatenate([t[:, None], np.cos(ang), np.sin(ang)], axis=-1)
    return np.pad(z, ((0, 0), (0, HY_POS_PAD - HY_POS_DIM))).astype(np.float32)


def _rope_tables(n_tokens, dim):
    rows = np.repeat(np.arange(n_tokens // GRID_W, dtype=np.float64), GRID_W)
    cols = np.tile(np.arange(GRID_W, dtype=np.float64), n_tokens // GRID_W)
    quarter = dim // 4
    inv = ROPE_BASE ** (-np.arange(quarter, dtype=np.float64) / quarter)
    ang = np.concatenate([rows[:, None] * inv, cols[:, None] * inv], axis=-1)
    cos = np.concatenate([np.cos(ang), np.cos(ang)], axis=-1)
    sin = np.concatenate([-np.sin(ang), np.sin(ang)], axis=-1)
    reps = 128 // dim
    return (np.tile(cos, (1, reps)).astype(np.float32), np.tile(sin, (1, reps)).astype(np.float32))


def kernel(x_prompt, x_sample, state_ret, cache_gqa_k, cache_gqa_v, cache_mla_ckv, cache_mla_krope, c, c_ctx, ada_w, ada_b, norm_g, w_in, ret_decay_logit, ret_gn_g, hy_short_w, hy_short_b, hy_w1, hy_b1, hy_w2, hy_b2, hy_w3, hy_decay, hy_bias, gqa_sink, mla_q_norm, mla_kv_norm, mla_w_uq, mla_w_uk, mla_w_uv, w_out, ffn_w_up, ffn_conv_w, ffn_conv_b, ffn_w_down):
    n_p, len_p, _ = x_prompt.shape
    n_s, len_s, _ = x_sample.shape

    w_in_b = jnp.pad(w_in, ((0, 0), (0, 0), (0, IN_WIDTH_PAD - IN_WIDTH))).astype(BF16)
    w_out_b = w_out.astype(BF16)
    w_up_b = ffn_w_up.astype(BF16)
    w_down_b = ffn_w_down.astype(BF16)
    uq = mla_w_uq.reshape(DEPTH, MLA_Q_RANK, MLA_HEADS, MLA_NOPE + MLA_ROPE)
    w_uq_b = jnp.concatenate(
        [uq[..., :MLA_NOPE].reshape(DEPTH, MLA_Q_RANK, MLA_HEADS * MLA_NOPE),
         uq[..., MLA_NOPE:].reshape(DEPTH, MLA_Q_RANK, MLA_HEADS * MLA_ROPE)], axis=-1).astype(BF16)
    w_uk_b = mla_w_uk.astype(BF16)
    w_uv_b = mla_w_uv.astype(BF16)

    cond = jnp.concatenate([c_ctx[None], c, jnp.zeros((N_COND - 1 - n_s, D_MODEL), F32)], axis=0)
    mod = _modulation(cond, ada_w, ada_b).reshape(DEPTH, N_COND, 6, D_MODEL)

    w1p = jnp.pad(hy_w1, ((0, 0), (0, HY_POS_PAD - HY_POS_DIM), (0, 0)))
    b1 = hy_b1.reshape(DEPTH, 1, HY_FILTER_HIDDEN)
    b2 = hy_b2.reshape(DEPTH, 1, HY_FILTER_HIDDEN)
    groups = {}
    for seq_len in (len_p, len_s):
        f32mat = _dft_matrix(seq_len)
        fmat = jnp.asarray(f32mat).astype(BF16)
        fmat_t = jnp.asarray(np.ascontiguousarray(f32mat.T)).astype(BF16)
        tables = _hyena_tables(seq_len, jnp.asarray(_hyena_positions(seq_len)), w1p, b1, hy_w2, b2,
                               hy_w3, hy_decay, fmat)
        groups[seq_len] = (tables, fmat, fmat_t)

    rope_g = tuple(jnp.asarray(t) for t in _rope_tables(len_s, HEAD_DIM))
    rope_m = tuple(jnp.asarray(t) for t in _rope_tables(len_s, MLA_ROPE))

    gn_g = ret_gn_g.reshape(DEPTH, 1, RET_WIDTH)
    short_b = hy_short_b.reshape(DEPTH, 1, HY_IN)
    sink = gqa_sink.reshape(DEPTH, 1, GQA_Q_HEADS)
    q_norm = mla_q_norm.reshape(DEPTH, 1, MLA_Q_RANK)
    kv_norm = mla_kv_norm.reshape(DEPTH, 1, MLA_KV_RANK)
    conv_b = ffn_conv_b.reshape(DEPTH, 1, D_FF)

    xp = x_prompt.reshape(n_p * len_p, D_MODEL)
    xs = x_sample.reshape(n_s * len_s, D_MODEL)
    tm_p = 512
    tm_s = 512
    cond_p = lambda i: 0
    cond_s = lambda i: 1 + (i * tm_s) // len_s
    tm_ffn = 1024
    cond_s_ffn = lambda i: 1 + (i * tm_ffn) // len_s

    st = gkv = ckr = None
    for l in range(DEPTH):
        tables, fmat, fmat_t = groups[len_p]
        ret_in, hy_in, gqa_in, mla_in = _in_proj(xp, mod, norm_g, w_in_b, l, tm_p, cond_p)
        y_ret, *st = _retention(ret_in, ret_decay_logit, gn_g, None, st, l, n_p, len_p)
        y_hy = _hyena(hy_in, hy_short_w, short_b, hy_bias, tables, fmat, fmat_t, l, n_p, len_p)
        y_gqa, *gkv = _gqa_ctx(gqa_in, sink, gkv, l, n_p, len_p)
        y_mla, *ckr = _mla(mla_in, q_norm, kv_norm, w_uq_b, w_uk_b, w_uv_b, None, None, ckr, l, n_p, len_p)
        xm, h2 = _out_proj(xp, (y_ret, y_hy, y_gqa, y_mla), mod, norm_g, w_out_b, l, tm_p, cond_p)
        xp = _ffn(h2, xm, mod, norm_g, w_up_b, ffn_conv_w, conv_b, w_down_b, l, tm_ffn, len_p, cond_p)
        tables, fmat, fmat_t = groups[len_s]
        ret_in, hy_in, gqa_in, mla_in = _in_proj(xs, mod, norm_g, w_in_b, l, tm_s, cond_s)
        y_ret = _retention(ret_in, ret_decay_logit, gn_g, state_ret, None, l, n_s, len_s)
        y_hy = _hyena(hy_in, hy_short_w, short_b, hy_bias, tables, fmat, fmat_t, l, n_s, len_s)
        y_gqa = _gqa_win(gqa_in, sink, cache_gqa_k, cache_gqa_v, rope_g[0], rope_g[1], l, n_s, len_s)
        y_mla = _mla(mla_in, q_norm, kv_norm, w_uq_b, w_uk_b, w_uv_b, (cache_mla_ckv, cache_mla_krope),
                     rope_m, None, l, n_s, len_s)
        xm, h2 = _out_proj(xs, (y_ret, y_hy, y_gqa, y_mla), mod, norm_g, w_out_b, l, tm_s, cond_s)
        xs = _ffn(h2, xm, mod, norm_g, w_up_b, ffn_conv_w, conv_b, w_down_b, l, tm_ffn, len_s, cond_s_ffn)

    return (xp.reshape(n_p, len_p, D_MODEL), xs.reshape(n_s, len_s, D_MODEL),
            st[0], gkv[0], gkv[1], ckr[0], ckr[1])
```

```python
import functools
import math

import numpy as np
import jax
import jax.numpy as jnp
from jax import lax
from jax.experimental import pallas as pl
from jax.experimental.pallas import tpu as pltpu

F32 = jnp.float32
BF16 = jnp.bfloat16

D_MODEL = 1024
DEPTH = 4
GRID_W = 64
EPS = 1e-6
NEG_INF = -1e30
ROPE_BASE = 10000.0
BLOCK = 128
WINDOW = 128
HEAD_DIM = 64
LANES = 128
RET_HEADS = 4
RET_WIDTH = RET_HEADS * HEAD_DIM
HY_WIDTH = 256
HY_ORDER = 2
HY_BANDS = 8
HY_POS_DIM = 1 + 2 * HY_BANDS
HY_POS_PAD = 32
HY_FILTER_HIDDEN = 64
GQA_Q_HEADS = 4
GQA_KV_HEADS = 2
GQA_GROUPS = GQA_Q_HEADS // GQA_KV_HEADS
MLA_HEADS = 4
MLA_Q_RANK = 256
MLA_KV_RANK = 128
MLA_NOPE = 64
MLA_ROPE = 32
MLA_V = 64
D_FF = 2816
RET_IN = 4 * RET_WIDTH
HY_IN = (HY_ORDER + 1) * HY_WIDTH
GQA_IN = (GQA_Q_HEADS + 2 * GQA_KV_HEADS) * HEAD_DIM
MLA_IN = MLA_Q_RANK + MLA_KV_RANK + MLA_ROPE
MLA_IN_TILED = MLA_Q_RANK + MLA_KV_RANK + MLA_HEADS * MLA_ROPE
IN_WIDTH = RET_IN + HY_IN + GQA_IN + MLA_IN
IN_WIDTH_TILED = RET_IN + HY_IN + GQA_IN + MLA_IN_TILED
GQA_HEAD_ORDER = (0, 2, 1, 3)
MIX_PART = 256
N_COND = 8
FFN_CHUNK = 256
N_FFN_CHUNKS = D_FF // FFN_CHUNK

VMEM_LIMIT = 56 * 1024 * 1024


def _params(n_axes=1):
    return pltpu.CompilerParams(
        dimension_semantics=("arbitrary",) * n_axes, vmem_limit_bytes=VMEM_LIMIT)


def _dot(a, b):
    return jnp.dot(a, b, preferred_element_type=F32)


def _dot_nt(a, b):
    return lax.dot_general(a, b, (((1,), (1,)), ((), ())), preferred_element_type=F32)


def _dot_tn(a, b):
    return lax.dot_general(a, b, (((0,), (0,)), ((), ())), preferred_element_type=F32)


def _rms(x, g):
    return x * lax.rsqrt(jnp.mean(x * x, axis=-1, keepdims=True) + EPS) * g


def _sigmoid(x):
    return 1.0 / (1.0 + jnp.exp(-x))


def _silu(x):
    return x * _sigmoid(x)


def _shift_rows(x, seq_len):
    n = x.shape[0]
    assert seq_len & (seq_len - 1) == 0
    pos = lax.broadcasted_iota(jnp.int32, (n, 1), 0) & (seq_len - 1)
    prev = jnp.where(pos != 0, pltpu.roll(x, 1, axis=0), 0.0)
    nxt = jnp.where(pos != seq_len - 1, pltpu.roll(x, n - 1, axis=0), 0.0)
    return prev, nxt


def _rot_half(x, half):
    lane = lax.broadcasted_iota(jnp.int32, x.shape, 1) & (2 * half - 1)
    return jnp.where(lane < half, pltpu.roll(x, LANES - half, axis=1), pltpu.roll(x, half, axis=1))


def _lane_iota():
    return lax.broadcasted_iota(jnp.int32, (1, LANES), 1)


def _split_heads(t, lo):
    return jnp.concatenate([jnp.where(lo, t, 0.0), jnp.where(lo, 0.0, t)], axis=0)


def _merge_heads(o, lo):
    n = o.shape[0] // 2
    return jnp.where(lo, o[:n], o[n:])


def _per_head(vals, width):
    lane = lax.broadcasted_iota(jnp.int32, (1, width), 1)
    out = vals[-1]
    for h in range(len(vals) - 2, -1, -1):
        out = jnp.where(lane < (h + 1) * HEAD_DIM, vals[h], out)
    return out


def _mod_kernel(cond_ref, w_ref, b_ref, out_ref):
    s = _silu(cond_ref[...]).astype(BF16)
    out_ref[0] = _dot(s, w_ref[0].astype(BF16)) + b_ref[0]


def _modulation(cond, ada_w, ada_b):
    tn = 1536
    return pl.pallas_call(
        _mod_kernel,
        out_shape=jax.ShapeDtypeStruct((DEPTH, N_COND, 6 * D_MODEL), F32),
        grid=(DEPTH, 6 * D_MODEL // tn),
        in_specs=[
            pl.BlockSpec((N_COND, D_MODEL), lambda l, j: (0, 0)),
            pl.BlockSpec((1, D_MODEL, tn), lambda l, j: (l, 0, j)),
            pl.BlockSpec((1, 1, tn), lambda l, j: (l, 0, j)),
        ],
        out_specs=pl.BlockSpec((1, N_COND, tn), lambda l, j: (l, 0, j)),
        compiler_params=_params(2),
        name="modulation",
    )(cond, ada_w, ada_b.reshape(DEPTH, 1, 6 * D_MODEL))


def _filter_kernel(z_ref, w1_ref, b1_ref, w2_ref, b2_ref, w3_ref, dec_ref, f_ref, out_ref, *, seq_len):
    hi = lax.Precision.HIGHEST
    h = jnp.sin(jnp.dot(z_ref[...], w1_ref[0], precision=hi, preferred_element_type=F32) + b1_ref[0])
    h = jnp.sin(jnp.dot(h, w2_ref[0], precision=hi, preferred_element_type=F32) + b2_ref[0])
    h = jnp.dot(h, w3_ref[0], precision=hi, preferred_element_type=F32)
    row = lax.broadcasted_iota(jnp.int32, (seq_len, 1), 0)
    t = row.astype(F32) / seq_len
    dec = jnp.abs(dec_ref[0])
    win_f = jnp.exp(-t * dec[0:1, :])
    win_b = jnp.exp(-t * dec[1:2, :])
    fm = f_ref[...]
    inv = 1.0 / seq_len
    scale = jnp.where(row == 0, 0.5 * inv, inv)
    for o in range(HY_ORDER):
        base = o * 2 * HY_WIDTH
        hf = h[:, base:base + HY_WIDTH] * win_f
        hb = h[:, base + HY_WIDTH:base + 2 * HY_WIDTH] * win_b
        hbs = jnp.where(row != 0, pltpu.roll(hb, 1, axis=0), 0.0)
        sf = _dot(fm, hf.astype(BF16))
        sb = _dot(fm, hbs.astype(BF16))
        g_re = sf[:seq_len] + sb[:seq_len]
        g_im = sf[seq_len:] - sb[seq_len:]
        g_ny = sf[seq_len:] + sb[seq_len:]
        a = g_re * scale
        out_ref[0, 3 * o] = a
        out_ref[0, 3 * o + 1] = jnp.where(row == 0, g_ny * (0.5 * inv), a)
        out_ref[0, 3 * o + 2] = jnp.where(row == 0, 0.0, g_im * inv)


def _hyena_tables(seq_len, z, w1p, b1, w2, b2, w3, decay, fmat):
    hid = HY_FILTER_HIDDEN
    wide = HY_ORDER * 2 * HY_WIDTH
    return pl.pallas_call(
        functools.partial(_filter_kernel, seq_len=seq_len),
        out_shape=jax.ShapeDtypeStruct((DEPTH, 3 * HY_ORDER, seq_len, HY_WIDTH), F32),
        grid=(DEPTH,),
        in_specs=[
            pl.BlockSpec((seq_len, HY_POS_PAD), lambda l: (0, 0)),
            pl.BlockSpec((1, HY_POS_PAD, hid), lambda l: (l, 0, 0)),
            pl.BlockSpec((1, 1, hid), lambda l: (l, 0, 0)),
            pl.BlockSpec((1, hid, hid), lambda l: (l, 0, 0)),
            pl.BlockSpec((1, 1, hid), lambda l: (l, 0, 0)),
            pl.BlockSpec((1, hid, wide), lambda l: (l, 0, 0)),
            pl.BlockSpec((1, 2, HY_WIDTH), lambda l: (l, 0, 0)),
            pl.BlockSpec((2 * seq_len, seq_len), lambda l: (0, 0)),
        ],
        out_specs=pl.BlockSpec((1, 3 * HY_ORDER, seq_len, HY_WIDTH), lambda l: (l, 0, 0, 0)),
        compiler_params=_params(1),
        name=f"hyena_tables_{seq_len}",
    )(z, w1p, b1, w2, b2, w3, decay, fmat)


def _layer_stacked(tail, n_batch, layer, nb):
    zeros = (0,) * len(tail)
    return (jax.ShapeDtypeStruct((n_batch, DEPTH) + tail, F32),
            pl.BlockSpec((nb, 1) + tail, lambda b: (b, layer) + zeros))


def _carry_specs(in_specs, args, carried, first_out):
    aliases = {}
    if carried is not None:
        for k, arr in enumerate(carried):
            aliases[len(args)] = first_out + k
            in_specs.append(pl.BlockSpec(memory_space=pl.ANY))
            args.append(arr)
    return aliases


def _in_proj_kernel(x_ref, mod_ref, g_ref, w_ref, ret_ref, hy_ref, gqa_ref, mla_ref):
    shift = mod_ref[0, 0, 0:1, :]
    scale = mod_ref[0, 0, 1:2, :]
    h = (_rms(x_ref[...], g_ref[0, 0:1, :]) * (1.0 + scale) + shift).astype(BF16)
    c0 = 0
    for ref in (ret_ref, hy_ref, gqa_ref, mla_ref):
        width = ref.shape[1]
        ref[...] = _dot(h, w_ref[0, :, c0:c0 + width])
        c0 += width


def _in_proj(x, mod, norm_g, w_in, layer, tm, cond_of_tile):
    rows = x.shape[0]
    widths = (RET_IN, HY_IN, GQA_IN, MLA_IN_TILED)
    return pl.pallas_call(
        _in_proj_kernel,
        out_shape=[jax.ShapeDtypeStruct((rows, w), F32) for w in widths],
        grid=(rows // tm,),
        in_specs=[
            pl.BlockSpec((tm, D_MODEL), lambda i: (i, 0)),
            pl.BlockSpec((1, 1, 6, D_MODEL), lambda i: (layer, cond_of_tile(i), 0, 0)),
            pl.BlockSpec((1, 4, D_MODEL), lambda i: (layer, 0, 0)),
            pl.BlockSpec((1, D_MODEL, IN_WIDTH_TILED), lambda i: (layer, 0, 0)),
        ],
        out_specs=[pl.BlockSpec((tm, w), lambda i: (i, 0)) for w in widths],
        compiler_params=_params(1),
        name="in_proj",
    )(x, mod, norm_g, w_in)


def _out_proj_kernel(x_ref, m0_ref, m1_ref, m2_ref, m3_ref, mod_ref, g_ref, w_ref, xm_ref, h2_ref):
    y = None
    for i, m_ref in enumerate((m0_ref, m1_ref, m2_ref, m3_ref)):
        part = _dot(m_ref[...], w_ref[0, i * MIX_PART:(i + 1) * MIX_PART, :])
        y = part if y is None else y + part
    gate1 = mod_ref[0, 0, 2:3, :]
    shift2 = mod_ref[0, 0, 3:4, :]
    scale2 = mod_ref[0, 0, 4:5, :]
    xm = x_ref[...] + gate1 * _rms(y, g_ref[0, 1:2, :])
    xm_ref[...] = xm
    h2_ref[...] = (_rms(xm, g_ref[0, 2:3, :]) * (1.0 + scale2) + shift2).astype(BF16)


def _out_proj(x, mixes, mod, norm_g, w_out, layer, tm, cond_of_tile):
    rows = x.shape[0]
    return pl.pallas_call(
        _out_proj_kernel,
        out_shape=[jax.ShapeDtypeStruct((rows, D_MODEL), F32),
                   jax.ShapeDtypeStruct((rows, D_MODEL), BF16)],
        grid=(rows // tm,),
        in_specs=[pl.BlockSpec((tm, D_MODEL), lambda i: (i, 0))]
        + [pl.BlockSpec((tm, MIX_PART), lambda i: (i, 0))] * 4
        + [
            pl.BlockSpec((1, 1, 6, D_MODEL), lambda i: (layer, cond_of_tile(i), 0, 0)),
            pl.BlockSpec((1, 4, D_MODEL), lambda i: (layer, 0, 0)),
            pl.BlockSpec((1, D_MODEL, D_MODEL), lambda i: (layer, 0, 0)),
        ],
        out_specs=[pl.BlockSpec((tm, D_MODEL), lambda i: (i, 0))] * 2,
        compiler_params=_params(1),
        name="out_proj",
    )(x, *mixes, mod, norm_g, w_out)


def _ffn_kernel(h2_ref, xm_ref, mod_ref, g_ref, wu_ref, cw_ref, cb_ref, wd_ref, out_ref,
                acc_ref, *, seq_len):
    h2 = h2_ref[...]
    for c in range(N_FFN_CHUNKS):
        sl = slice(c * FFN_CHUNK, (c + 1) * FFN_CHUNK)
        gate = _dot(h2, wu_ref[0, :, sl])
        up = _dot(h2, wu_ref[0, :, D_FF + c * FFN_CHUNK:D_FF + (c + 1) * FFN_CHUNK])
        prev, nxt = _shift_rows(gate, seq_len)
        gate = (prev * cw_ref[0, 0:1, sl] + gate * cw_ref[0, 1:2, sl] + nxt * cw_ref[0, 2:3, sl]
                + cb_ref[0, :, sl])
        act = (_silu(gate) * up).astype(BF16)
        part = _dot(act, wd_ref[0, sl, :])
        if c == 0:
            acc_ref[...] = part
        else:
            acc_ref[...] += part
    gate2 = mod_ref[0, 0, 5:6, :]
    out_ref[...] = xm_ref[...] + gate2 * _rms(acc_ref[...], g_ref[0, 3:4, :])


def _ffn(h2, xm, mod, norm_g, w_up, conv_w, conv_b, w_down, layer, tm, seq_len, cond_of_tile):
    rows = xm.shape[0]
    resident = dict(pipeline_mode=pl.Buffered(1))
    return pl.pallas_call(
        functools.partial(_ffn_kernel, seq_len=seq_len),
        out_shape=jax.ShapeDtypeStruct((rows, D_MODEL), F32),
        grid=(rows // tm,),
        in_specs=[
            pl.BlockSpec((tm, D_MODEL), lambda i: (i, 0)),
            pl.BlockSpec((tm, D_MODEL), lambda i: (i, 0)),
            pl.BlockSpec((1, 1, 6, D_MODEL), lambda i: (layer, cond_of_tile(i), 0, 0)),
            pl.BlockSpec((1, 4, D_MODEL), lambda i: (layer, 0, 0)),
            pl.BlockSpec((1, D_MODEL, 2 * D_FF), lambda i: (layer, 0, 0), **resident),
            pl.BlockSpec((1, 3, D_FF), lambda i: (layer, 0, 0)),
            pl.BlockSpec((1, 1, D_FF), lambda i: (layer, 0, 0)),
            pl.BlockSpec((1, D_FF, D_MODEL), lambda i: (layer, 0, 0), **resident),
        ],
        out_specs=pl.BlockSpec((tm, D_MODEL), lambda i: (i, 0)),
        scratch_shapes=[pltpu.VMEM((tm, D_MODEL), F32)],
        compiler_params=_params(1),
        name="conv_ffn",
    )(h2, xm, mod, norm_g, w_up, conv_w, conv_b, w_down)


def _log_gamma(dl_ref):
    dl = dl_ref[0]
    return jnp.minimum(dl, 0.0) - jnp.log(1.0 + jnp.exp(-jnp.abs(dl)))


def _pair_decay(lg, t, r0, bq, seq_len):
    top = lax.broadcasted_iota(jnp.int32, (2 * bq, 1), 0) < bq
    lgf = jnp.where(top, lg[0:1, 2 * t:2 * t + 1], lg[0:1, 2 * t + 1:2 * t + 2])
    lgb = jnp.where(top, lg[1:2, 2 * t:2 * t + 1], lg[1:2, 2 * t + 1:2 * t + 2])
    rowf = ((lax.broadcasted_iota(jnp.int32, (2 * bq, 1), 0) & (bq - 1)) + r0).astype(F32)
    colf = lax.broadcasted_iota(jnp.int32, (1, seq_len), 1).astype(F32)
    lag = rowf - colf
    decay = jnp.exp(jnp.where(lag >= 0.0, lag * lgf, -lag * lgb))
    return jnp.where(lag == 0.0, 2.0, decay)


def _head_norm(o, lo):
    inv = 1.0 / HEAD_DIM
    s_all = jnp.sum(o, axis=-1, keepdims=True)
    s_lo = jnp.sum(jnp.where(lo, o, 0.0), axis=-1, keepdims=True)
    d = o - jnp.where(lo, s_lo, s_all - s_lo) * inv
    d2 = d * d
    v_all = jnp.sum(d2, axis=-1, keepdims=True)
    v_lo = jnp.sum(jnp.where(lo, d2, 0.0), axis=-1, keepdims=True)
    return d * lax.rsqrt(jnp.where(lo, v_lo, v_all - v_lo) * inv + EPS)


def _ret_ctx_kernel(*refs, seq_len, nb):
    x_ref, dl_ref, gn_ref, *_, y_ref, st_ref, dec_scr, kdec_scr = refs
    lo = _lane_iota() < HEAD_DIM

    @pl.when(pl.program_id(0) == 0)
    def _():
        lg = _log_gamma(dl_ref)
        posf = lax.broadcasted_iota(jnp.int32, (seq_len, 1), 0).astype(F32)
        lgf = _per_head([lg[0:1, h:h + 1] for h in range(RET_HEADS)], RET_WIDTH)
        lgb = _per_head([lg[1:2, h:h + 1] for h in range(RET_HEADS)], RET_WIDTH)
        kdec_scr[0] = jnp.exp((seq_len - 1.0 - posf) * lgf)
        kdec_scr[1] = jnp.exp(posf * lgb)
        for t in range(RET_HEADS // 2):
            dec_scr[t] = _pair_decay(lg, t, 0, seq_len, seq_len)

    for i in range(nb):
        rows = slice(i * seq_len, (i + 1) * seq_len)
        for t in range(RET_HEADS // 2):
            cs = slice(t * LANES, (t + 1) * LANES)
            q = x_ref[rows, cs]
            k = x_ref[rows, RET_WIDTH + t * LANES:RET_WIDTH + (t + 1) * LANES] * (HEAD_DIM ** -0.5)
            vb = x_ref[rows, 2 * RET_WIDTH + t * LANES:2 * RET_WIDTH + (t + 1) * LANES].astype(BF16)
            gate = x_ref[rows, 3 * RET_WIDTH + t * LANES:3 * RET_WIDTH + (t + 1) * LANES]
            s = _dot_nt(_split_heads(q, lo).astype(BF16), k.astype(BF16)) * dec_scr[t]
            o = _merge_heads(_dot(s.astype(BF16), vb), lo)
            y = _silu(gate) * (_head_norm(o, lo) * gn_ref[0, :, cs])
            y_ref[rows, cs] = y.astype(BF16)
            for d in range(2):
                st = _dot_tn((k * kdec_scr[d, :, cs]).astype(BF16), vb)
                st_ref[i, 0, d, 2 * t] = st[0:HEAD_DIM, 0:HEAD_DIM]
                st_ref[i, 0, d, 2 * t + 1] = st[HEAD_DIM:LANES, HEAD_DIM:LANES]


def _ret_lat_kernel(x_ref, dl_ref, gn_ref, s0_ref, y_ref, *, seq_len, bq):
    lo = _lane_iota() < HEAD_DIM
    lg = _log_gamma(dl_ref)
    zero = jnp.zeros((HEAD_DIM, HEAD_DIM), F32)
    for t in range(RET_HEADS // 2):
        cs = slice(t * LANES, (t + 1) * LANES)
        kb = (x_ref[:, RET_WIDTH + t * LANES:RET_WIDTH + (t + 1) * LANES] * (HEAD_DIM ** -0.5)).astype(BF16)
        vb = x_ref[:, 2 * RET_WIDTH + t * LANES:2 * RET_WIDTH + (t + 1) * LANES].astype(BF16)
        lgf = _per_head([lg[0:1, 2 * t:2 * t + 1], lg[0:1, 2 * t + 1:2 * t + 2]], LANES)
        lgb = _per_head([lg[1:2, 2 * t:2 * t + 1], lg[1:2, 2 * t + 1:2 * t + 2]], LANES)
        s0 = []
        for d in range(2):
            a = s0_ref[0, 0, d, 2 * t]
            b = s0_ref[0, 0, d, 2 * t + 1]
            s0.append(jnp.concatenate([jnp.concatenate([a, zero], axis=1),
                                       jnp.concatenate([zero, b], axis=1)], axis=0).astype(BF16))
        for r0 in range(0, seq_len, bq):
            q = x_ref[r0:r0 + bq, cs]
            s = _dot_nt(_split_heads(q, lo).astype(BF16), kb) * _pair_decay(lg, t, r0, bq, seq_len)
            o = _merge_heads(_dot(s.astype(BF16), vb), lo)
            rowf = (lax.broadcasted_iota(jnp.int32, (bq, 1), 0) + r0).astype(F32)
            qf = (q * jnp.exp((rowf + 1.0) * lgf)).astype(BF16)
            qr = (q * jnp.exp((seq_len - rowf) * lgb)).astype(BF16)
            o = o + _dot(qf, s0[0]) + _dot(qr, s0[1])
            gate = x_ref[r0:r0 + bq, 3 * RET_WIDTH + t * LANES:3 * RET_WIDTH + (t + 1) * LANES]
            y = _silu(gate) * (_head_norm(o, lo) * gn_ref[0, :, cs])
            y_ref[r0:r0 + bq, cs] = y.astype(BF16)


def _retention(ret_in, decay_logit, gn_g, state0, carried, layer, n_batch, seq_len, nb):
    latent = state0 is not None
    in_specs = [
        pl.BlockSpec((nb * seq_len, RET_IN), lambda b: (b, 0)),
        pl.BlockSpec((1, 2, RET_HEADS), lambda b: (layer, 0, 0)),
        pl.BlockSpec((1, 1, RET_WIDTH), lambda b: (layer, 0, 0)),
    ]
    args = [ret_in, decay_logit, gn_g]
    y_shape = jax.ShapeDtypeStruct((n_batch * seq_len, MIX_PART), BF16)
    y_spec = pl.BlockSpec((nb * seq_len, MIX_PART), lambda b: (b, 0))
    if latent:
        assert nb == 1
        kern = functools.partial(_ret_lat_kernel, seq_len=seq_len, bq=256)
        in_specs.append(pl.BlockSpec((1, 1, 2, RET_HEADS, HEAD_DIM, HEAD_DIM),
                                     lambda b: (b, layer, 0, 0, 0, 0)))
        args.append(state0)
        out_shape, out_specs, aliases, scratch = y_shape, y_spec, {}, []
    else:
        kern = functools.partial(_ret_ctx_kernel, seq_len=seq_len, nb=nb)
        st_shape, st_spec = _layer_stacked((2, RET_HEADS, HEAD_DIM, HEAD_DIM), n_batch, layer, nb)
        out_shape = [y_shape, st_shape]
        out_specs = [y_spec, st_spec]
        aliases = _carry_specs(in_specs, args, carried, 1)
        scratch = [pltpu.VMEM((RET_HEADS // 2, 2 * seq_len, seq_len), F32),
                   pltpu.VMEM((2, seq_len, RET_WIDTH), F32)]
    return pl.pallas_call(
        kern, out_shape=out_shape, grid=(n_batch // nb,), in_specs=in_specs, out_specs=out_specs,
        input_output_aliases=aliases, scratch_shapes=scratch,
        compiler_params=_params(1), name=f"retention_{seq_len}",
    )(*args)


def _hyena_kernel(x_ref, sw_ref, sb_ref, bias_ref, tab_ref, f_ref, ft_ref, y_ref, *, seq_len):
    x = x_ref[...]
    prev, nxt = _shift_rows(x, seq_len)
    u = prev * sw_ref[0, 0:1, :] + x * sw_ref[0, 1:2, :] + nxt * sw_ref[0, 2:3, :] + sb_ref[0]
    x1 = u[:, 0:HY_WIDTH]
    x2 = u[:, HY_WIDTH:2 * HY_WIDTH]
    z = u[:, 2 * HY_WIDTH:3 * HY_WIDTH]
    for o, gate in enumerate((x1, x2)):
        spec = _dot(f_ref[...], z.astype(BF16))
        s_re = spec[:seq_len]
        s_im = spec[seq_len:]
        a = tab_ref[0, 3 * o]
        a_ny = tab_ref[0, 3 * o + 1]
        b = tab_ref[0, 3 * o + 2]
        y_re = (s_re * a - s_im * b).astype(BF16)
        y_im = (s_re * b + s_im * a_ny).astype(BF16)
        conv = _dot(ft_ref[:, 0:seq_len], y_re) + _dot(ft_ref[:, seq_len:2 * seq_len], y_im)
        z = gate * (conv + z * bias_ref[0, o:o + 1, :])
    y_ref[...] = z.astype(BF16)


def _hyena(hy_in, short_w, short_b, hy_bias, tables, fmat, fmat_t, layer, n_batch, seq_len):
    const = dict(pipeline_mode=pl.Buffered(1))
    return pl.pallas_call(
        functools.partial(_hyena_kernel, seq_len=seq_len),
        out_shape=jax.ShapeDtypeStruct((n_batch * seq_len, MIX_PART), BF16),
        grid=(n_batch,),
        in_specs=[
            pl.BlockSpec((seq_len, HY_IN), lambda b: (b, 0)),
            pl.BlockSpec((1, 3, HY_IN), lambda b: (layer, 0, 0)),
            pl.BlockSpec((1, 1, HY_IN), lambda b: (layer, 0, 0)),
            pl.BlockSpec((1, HY_ORDER, HY_WIDTH), lambda b: (layer, 0, 0)),
            pl.BlockSpec((1, 3 * HY_ORDER, seq_len, HY_WIDTH), lambda b: (layer, 0, 0, 0), **const),
            pl.BlockSpec((2 * seq_len, seq_len), lambda b: (0, 0), **const),
            pl.BlockSpec((seq_len, 2 * seq_len), lambda b: (0, 0), **const),
        ],
        out_specs=pl.BlockSpec((seq_len, MIX_PART), lambda b: (b, 0)),
        compiler_params=_params(1),
        name=f"hyena_{seq_len}",
    )(hy_in, short_w, short_b, hy_bias, tables, fmat, fmat_t)


def _pair_sink(sink_ref, g, n):
    top = lax.broadcasted_iota(jnp.int32, (2 * n, 1), 0) < n
    return jnp.where(top, sink_ref[0, :, g:g + 1], sink_ref[0, :, GQA_GROUPS + g:GQA_GROUPS + g + 1])


def _gqa_ctx_kernel(*refs, seq_len, nb):
    x_ref, sink_ref, *_, y_ref, k_out_ref, v_out_ref = refs
    nq = GQA_Q_HEADS * HEAD_DIM
    scale = HEAD_DIM ** -0.5
    lo = _lane_iota() < HEAD_DIM
    ones = jnp.ones((seq_len, LANES), BF16)
    for i in range(nb):
        rows = slice(i * seq_len, (i + 1) * seq_len)
        k = x_ref[rows, nq:nq + LANES]
        v = x_ref[rows, nq + LANES:nq + 2 * LANES]
        for kv in range(GQA_KV_HEADS):
            k_out_ref[i, 0, kv] = k[:, kv * HEAD_DIM:(kv + 1) * HEAD_DIM]
            v_out_ref[i, 0, kv] = v[:, kv * HEAD_DIM:(kv + 1) * HEAD_DIM]
        kb = k.astype(BF16)
        v_ext = jnp.concatenate([v.astype(BF16), ones], axis=1)
        for g in range(GQA_GROUPS):
            cs = slice(g * LANES, (g + 1) * LANES)
            s = _dot_nt(_split_heads(x_ref[rows, cs], lo).astype(BF16), kb) * scale
            sink = _pair_sink(sink_ref, g, seq_len)
            m = jnp.maximum(jnp.max(s, axis=-1, keepdims=True), sink)
            pv = _dot(jnp.exp(s - m).astype(BF16), v_ext)
            den = pv[:, LANES:LANES + 1] + jnp.exp(sink - m)
            y_ref[rows, cs] = _merge_heads(pv[:, 0:LANES] / den, lo).astype(BF16)


def _gqa_ctx(gqa_in, sink, carried, layer, n_batch, seq_len, nb):
    kv_shape, kv_spec = _layer_stacked((GQA_KV_HEADS, seq_len, HEAD_DIM), n_batch, layer, nb)
    in_specs = [
        pl.BlockSpec((nb * seq_len, GQA_IN), lambda b: (b, 0)),
        pl.BlockSpec((1, 1, GQA_Q_HEADS), lambda b: (layer, 0, 0)),
    ]
    args = [gqa_in, sink]
    aliases = _carry_specs(in_specs, args, carried, 1)
    return pl.pallas_call(
        functools.partial(_gqa_ctx_kernel, seq_len=seq_len, nb=nb),
        out_shape=[jax.ShapeDtypeStruct((n_batch * seq_len, MIX_PART), BF16), kv_shape, kv_shape],
        grid=(n_batch // nb,),
        in_specs=in_specs,
        out_specs=[pl.BlockSpec((nb * seq_len, MIX_PART), lambda b: (b, 0)), kv_spec, kv_spec],
        input_output_aliases=aliases,
        compiler_params=_params(1),
        name=f"gqa_ctx_{seq_len}",
    )(*args)


def _gqa_win_kernel(x_ref, sink_ref, kc_ref, vc_ref, cos_ref, sin_ref, y_ref,
                    q_scr, k_scr, v_scr, kc_scr, vc_scr, *, seq_len):
    nq = GQA_Q_HEADS * HEAD_DIM
    scale = HEAD_DIM ** -0.5
    lo = _lane_iota() < HEAD_DIM
    past = kc_scr.shape[0]
    cos = cos_ref[...]
    sin = sin_ref[...]

    def rope(t):
        return t * cos + _rot_half(t, HEAD_DIM // 2) * sin

    for g in range(GQA_GROUPS):
        q = rope(x_ref[:, g * LANES:(g + 1) * LANES])
        q_scr[g, 0] = jnp.where(lo, q, 0.0).astype(BF16)
        q_scr[g, 1] = jnp.where(lo, 0.0, q).astype(BF16)
    zeros = jnp.zeros((BLOCK, LANES), BF16)
    for r0 in (0, BLOCK + seq_len):
        k_scr[r0:r0 + BLOCK, :] = zeros
        v_scr[r0:r0 + BLOCK, 0:LANES] = zeros
    k_scr[BLOCK:BLOCK + seq_len, :] = rope(x_ref[:, nq:nq + LANES]).astype(BF16)
    v_scr[BLOCK:BLOCK + seq_len, 0:LANES] = x_ref[:, nq + LANES:nq + 2 * LANES].astype(BF16)
    v_scr[:, LANES:2 * LANES] = jnp.ones((seq_len + 2 * BLOCK, LANES), BF16)
    kc_scr[...] = jnp.concatenate([kc_ref[0, 0, 0], kc_ref[0, 0, 1]], axis=1).astype(BF16)
    vc_scr[:, 0:LANES] = jnp.concatenate([vc_ref[0, 0, 0], vc_ref[0, 0, 1]], axis=1).astype(BF16)
    vc_scr[:, LANES:2 * LANES] = jnp.ones((past, LANES), BF16)

    ii = lax.broadcasted_iota(jnp.int32, (2 * BLOCK, 3 * BLOCK), 0) & (BLOCK - 1)
    jj = lax.broadcasted_iota(jnp.int32, (2 * BLOCK, 3 * BLOCK), 1)
    band = (jj >= ii) & (jj <= ii + 2 * WINDOW)

    def block(n, carry):
        r0 = pl.multiple_of(n * BLOCK, BLOCK)
        kpos = jj + (n - 1) * BLOCK
        valid = band & (kpos >= 0) & (kpos < seq_len)
        kw = k_scr[pl.ds(r0, 3 * BLOCK), :]
        vw = v_scr[pl.ds(r0, 3 * BLOCK), :]
        for g in range(GQA_GROUPS):
            q2 = jnp.concatenate([q_scr[g, 0, pl.ds(r0, BLOCK), :], q_scr[g, 1, pl.ds(r0, BLOCK), :]], axis=0)
            s_win = jnp.where(valid, _dot_nt(q2, kw) * scale, NEG_INF)
            s_ctx = _dot_nt(q2, kc_scr[...]) * scale
            sink = _pair_sink(sink_ref, g, BLOCK)
            m = jnp.maximum(jnp.maximum(jnp.max(s_win, axis=-1, keepdims=True),
                                        jnp.max(s_ctx, axis=-1, keepdims=True)), sink)
            pv = (_dot(jnp.exp(s_win - m).astype(BF16), vw)
                  + _dot(jnp.exp(s_ctx - m).astype(BF16), vc_scr[...]))
            den = pv[:, LANES:LANES + 1] + jnp.exp(sink - m)
            y_ref[pl.ds(r0, BLOCK), g * LANES:(g + 1) * LANES] = _merge_heads(
                pv[:, 0:LANES] / den, lo).astype(BF16)
        return carry

    lax.fori_loop(0, seq_len // BLOCK, block, 0)


def _gqa_win(gqa_in, sink, cache_k, cache_v, cos, sin, layer, n_batch, seq_len):
    past = cache_k.shape[3]
    return pl.pallas_call(
        functools.partial(_gqa_win_kernel, seq_len=seq_len),
        out_shape=jax.ShapeDtypeStruct((n_batch * seq_len, MIX_PART), BF16),
        grid=(n_batch,),
        in_specs=[
            pl.BlockSpec((seq_len, GQA_IN), lambda b: (b, 0)),
            pl.BlockSpec((1, 1, GQA_Q_HEADS), lambda b: (layer, 0, 0)),
            pl.BlockSpec((1, 1, GQA_KV_HEADS, past, HEAD_DIM), lambda b: (b, layer, 0, 0, 0)),
            pl.BlockSpec((1, 1, GQA_KV_HEADS, past, HEAD_DIM), lambda b: (b, layer, 0, 0, 0)),
            pl.BlockSpec((seq_len, LANES), lambda b: (0, 0)),
            pl.BlockSpec((seq_len, LANES), lambda b: (0, 0)),
        ],
        out_specs=pl.BlockSpec((seq_len, MIX_PART), lambda b: (b, 0)),
        scratch_shapes=[
            pltpu.VMEM((GQA_GROUPS, 2, seq_len, LANES), BF16),
            pltpu.VMEM((seq_len + 2 * BLOCK, LANES), BF16),
            pltpu.VMEM((seq_len + 2 * BLOCK, 2 * LANES), BF16),
            pltpu.VMEM((past, LANES), BF16),
            pltpu.VMEM((past, 2 * LANES), BF16),
        ],
        compiler_params=_params(1),
        name=f"gqa_win_{seq_len}",
    )(gqa_in, sink, cache_k, cache_v, cos, sin)


def _mla_kernel(*refs, seq_len, latent, bq, nb):
    if latent:
        (x_ref, qn_ref, kvn_ref, wq_ref, wk_ref, wv_ref, ckv_c_ref, kr_c_ref, cos_ref, sin_ref,
         y_ref, q_all, k_all, v_all) = refs
        past = ckv_c_ref.shape[2]
    else:
        (x_ref, qn_ref, kvn_ref, wq_ref, wk_ref, wv_ref, *_,
         y_ref, ckv_out_ref, kr_out_ref, q_all, k_all, v_all) = refs
        past = 0
    keys = past + seq_len
    nope_w = MLA_HEADS * MLA_NOPE
    scale = (MLA_NOPE + MLA_ROPE) ** -0.5
    lane = _lane_iota()
    lo = lane < HEAD_DIM
    rope_lanes = [(lane >= h * MLA_ROPE) & (lane < (h + 1) * MLA_ROPE) for h in range(MLA_HEADS)]
    for i in range(nb):
        rows = slice(i * seq_len, (i + 1) * seq_len)
        q_scr, k_scr, v_scr = q_all.at[i], k_all.at[i], v_all.at[i]
        q_lat = x_ref[rows, 0:MLA_Q_RANK]
        kv_lat = x_ref[rows, MLA_Q_RANK:MLA_Q_RANK + MLA_KV_RANK]
        k_rope = x_ref[rows, MLA_Q_RANK + MLA_KV_RANK:MLA_IN_TILED]
        mq = _dot(_rms(q_lat, qn_ref[0]).astype(BF16), wq_ref[0])
        ckv = _rms(kv_lat, kvn_ref[0])
        q_rope = mq[:, nope_w:nope_w + LANES]
        if latent:
            cos = cos_ref[...]
            sin = sin_ref[...]
            q_rope = q_rope * cos + _rot_half(q_rope, MLA_ROPE // 2) * sin
            k_rope = k_rope * cos + _rot_half(k_rope, MLA_ROPE // 2) * sin
            ckv_c = ckv_c_ref[0, 0].astype(BF16)
            k_scr[0:past, 0:nope_w] = _dot(ckv_c, wk_ref[0]).astype(BF16)
            v_scr[0:past, :] = _dot(ckv_c, wv_ref[0]).astype(BF16)
            kr_c = kr_c_ref[0, 0]
            k_scr[0:past, nope_w:nope_w + LANES] = jnp.concatenate([kr_c] * MLA_HEADS, axis=1).astype(BF16)
        else:
            ckv_out_ref[i, 0] = ckv
            kr_out_ref[i, 0] = k_rope[:, 0:MLA_ROPE]
        q_scr[:, 0:nope_w] = mq[:, 0:nope_w]
        q_scr[:, nope_w:nope_w + LANES] = q_rope
        ckv_b = ckv.astype(BF16)
        k_scr[past:keys, 0:nope_w] = _dot(ckv_b, wk_ref[0]).astype(BF16)
        v_scr[past:keys, :] = _dot(ckv_b, wv_ref[0]).astype(BF16)
        k_scr[past:keys, nope_w:nope_w + LANES] = k_rope.astype(BF16)

        ones = jnp.ones((keys, LANES), BF16)
        kr_all = k_scr[:, nope_w:nope_w + LANES]
        for t in range(MLA_HEADS // 2):
            cs = slice(t * LANES, (t + 1) * LANES)
            k_cat = jnp.concatenate([k_scr[:, cs], kr_all], axis=1)
            v_ext = jnp.concatenate([v_scr[:, cs], ones], axis=1)
            for r0 in range(0, seq_len, bq):
                qn = q_scr[r0:r0 + bq, cs]
                qr = q_scr[r0:r0 + bq, nope_w:nope_w + LANES]
                q2 = jnp.concatenate([
                    jnp.concatenate([jnp.where(lo, qn, 0.0), jnp.where(rope_lanes[2 * t], qr, 0.0)], axis=1),
                    jnp.concatenate([jnp.where(lo, 0.0, qn), jnp.where(rope_lanes[2 * t + 1], qr, 0.0)], axis=1),
                ], axis=0).astype(BF16)
                s = _dot_nt(q2, k_cat) * scale
                m = jnp.max(s, axis=-1, keepdims=True)
                pv = _dot(jnp.exp(s - m).astype(BF16), v_ext)
                o = pv[:, 0:LANES] / pv[:, LANES:LANES + 1]
                y_ref[i * seq_len + r0:i * seq_len + r0 + bq, cs] = _merge_heads(o, lo).astype(BF16)


def _mla(mla_in, q_norm, kv_norm, w_uq, w_uk, w_uv, cache, rope, carried, layer, n_batch, seq_len, nb):
    latent = cache is not None
    aliases = {}
    past = cache[0].shape[2] if latent else 0
    kern = functools.partial(_mla_kernel, seq_len=seq_len, latent=latent, bq=256, nb=nb)
    q_w = MLA_HEADS * (MLA_NOPE + MLA_ROPE)
    kv_w = MLA_HEADS * MLA_NOPE
    in_specs = [
        pl.BlockSpec((nb * seq_len, MLA_IN_TILED), lambda b: (b, 0)),
        pl.BlockSpec((1, 1, MLA_Q_RANK), lambda b: (layer, 0, 0)),
        pl.BlockSpec((1, 1, MLA_KV_RANK), lambda b: (layer, 0, 0)),
        pl.BlockSpec((1, MLA_Q_RANK, q_w), lambda b: (layer, 0, 0)),
        pl.BlockSpec((1, MLA_KV_RANK, kv_w), lambda b: (layer, 0, 0)),
        pl.BlockSpec((1, MLA_KV_RANK, kv_w), lambda b: (layer, 0, 0)),
    ]
    args = [mla_in, q_norm, kv_norm, w_uq, w_uk, w_uv]
    y_shape = jax.ShapeDtypeStruct((n_batch * seq_len, MIX_PART), BF16)
    y_spec = pl.BlockSpec((nb * seq_len, MIX_PART), lambda b: (b, 0))
    if latent:
        assert nb == 1
        in_specs += [
            pl.BlockSpec((1, 1, past, MLA_KV_RANK), lambda b: (b, layer, 0, 0)),
            pl.BlockSpec((1, 1, past, MLA_ROPE), lambda b: (b, layer, 0, 0)),
            pl.BlockSpec((seq_len, LANES), lambda b: (0, 0)),
            pl.BlockSpec((seq_len, LANES), lambda b: (0, 0)),
        ]
        args += [cache[0], cache[1], rope[0], rope[1]]
        out_shape, out_specs = y_shape, y_spec
    else:
        ckv_shape, ckv_spec = _layer_stacked((seq_len, MLA_KV_RANK), n_batch, layer, nb)
        kr_shape, kr_spec = _layer_stacked((seq_len, MLA_ROPE), n_batch, layer, nb)
        out_shape = [y_shape, ckv_shape, kr_shape]
        out_specs = [y_spec, ckv_spec, kr_spec]
        aliases = _carry_specs(in_specs, args, carried, 1)
    keys = past + seq_len
    return pl.pallas_call(
        kern, out_shape=out_shape, grid=(n_batch // nb,), in_specs=in_specs, out_specs=out_specs,
        input_output_aliases=aliases,
        scratch_shapes=[
            pltpu.VMEM((nb, seq_len, q_w), F32),
            pltpu.VMEM((nb, keys, kv_w + LANES), BF16),
            pltpu.VMEM((nb, keys, kv_w), BF16),
        ],
        compiler_params=_params(1),
        name=f"mla_{seq_len}",
    )(*args)


def _dft_matrix(seq_len):
    k = np.arange(seq_len, dtype=np.float64)[:, None]
    j = np.arange(seq_len, dtype=np.float64)[None, :]
    ang = np.pi * k * j / seq_len
    re = np.cos(ang)
    im = -np.sin(ang)
    im[0, :] = np.where(np.arange(seq_len) % 2 == 0, 1.0, -1.0)
    return np.concatenate([re, im], axis=0).astype(np.float32)


def _hyena_positions(seq_len):
    t = np.arange(seq_len, dtype=np.float64) / seq_len
    bands = np.arange(1, HY_BANDS + 1, dtype=np.float64)
    ang = 2.0 * math.pi * t[:, None] * bands
    z = np.concatenate([t[:, None], np.cos(ang), np.sin(ang)], axis=-1)
    return np.pad(z, ((0, 0), (0, HY_POS_PAD - HY_POS_DIM))).astype(np.float32)


def _rope_tables(n_tokens, dim):
    rows = np.repeat(np.arange(n_tokens // GRID_W, dtype=np.float64), GRID_W)
    cols = np.tile(np.arange(GRID_W, dtype=np.float64), n_tokens // GRID_W)
    quarter = dim // 4
    inv = ROPE_BASE ** (-np.arange(quarter, dtype=np.float64) / quarter)
    ang = np.concatenate([rows[:, None] * inv, cols[:, None] * inv], axis=-1)
    cos = np.concatenate([np.cos(ang), np.cos(ang)], axis=-1)
    sin = np.concatenate([-np.sin(ang), np.sin(ang)], axis=-1)
    reps = LANES // dim
    return (np.tile(cos, (1, reps)).astype(np.float32), np.tile(sin, (1, reps)).astype(np.float32))


def _arrange_w_in(w_in):
    g0 = RET_IN + HY_IN
    m0 = g0 + GQA_IN
    q = [w_in[..., g0 + j * HEAD_DIM:g0 + (j + 1) * HEAD_DIM] for j in GQA_HEAD_ORDER]
    kv = w_in[..., g0 + GQA_Q_HEADS * HEAD_DIM:m0]
    lat = w_in[..., m0:m0 + MLA_Q_RANK + MLA_KV_RANK]
    kr = w_in[..., m0 + MLA_Q_RANK + MLA_KV_RANK:m0 + MLA_IN]
    return jnp.concatenate([w_in[..., :g0], *q, kv, lat] + [kr] * MLA_HEADS, axis=-1).astype(BF16)


def _arrange_w_out(w_out):
    g0 = 2 * MIX_PART
    rows = [w_out[:, g0 + j * HEAD_DIM:g0 + (j + 1) * HEAD_DIM] for j in GQA_HEAD_ORDER]
    return jnp.concatenate([w_out[:, :g0], *rows, w_out[:, g0 + MIX_PART:]], axis=1).astype(BF16)


def kernel(x_prompt, x_sample, state_ret, cache_gqa_k, cache_gqa_v, cache_mla_ckv, cache_mla_krope, c, c_ctx, ada_w, ada_b, norm_g, w_in, ret_decay_logit, ret_gn_g, hy_short_w, hy_short_b, hy_w1, hy_b1, hy_w2, hy_b2, hy_w3, hy_decay, hy_bias, gqa_sink, mla_q_norm, mla_kv_norm, mla_w_uq, mla_w_uk, mla_w_uv, w_out, ffn_w_up, ffn_conv_w, ffn_conv_b, ffn_w_down):
    n_p, len_p, _ = x_prompt.shape
    n_s, len_s, _ = x_sample.shape

    w_in_b = _arrange_w_in(w_in)
    w_out_b = _arrange_w_out(w_out)
    w_up_b = ffn_w_up.astype(BF16)
    w_down_b = ffn_w_down.astype(BF16)
    uq = mla_w_uq.reshape(DEPTH, MLA_Q_RANK, MLA_HEADS, MLA_NOPE + MLA_ROPE)
    w_uq_b = jnp.concatenate(
        [uq[..., :MLA_NOPE].reshape(DEPTH, MLA_Q_RANK, MLA_HEADS * MLA_NOPE),
         uq[..., MLA_NOPE:].reshape(DEPTH, MLA_Q_RANK, MLA_HEADS * MLA_ROPE)], axis=-1).astype(BF16)
    w_uk_b = mla_w_uk.astype(BF16)
    w_uv_b = mla_w_uv.astype(BF16)

    cond = jnp.concatenate([c_ctx[None], c, jnp.zeros((N_COND - 1 - n_s, D_MODEL), F32)], axis=0)
    mod = _modulation(cond, ada_w, ada_b).reshape(DEPTH, N_COND, 6, D_MODEL)

    w1p = jnp.pad(hy_w1, ((0, 0), (0, HY_POS_PAD - HY_POS_DIM), (0, 0)))
    b1 = hy_b1.reshape(DEPTH, 1, HY_FILTER_HIDDEN)
    b2 = hy_b2.reshape(DEPTH, 1, HY_FILTER_HIDDEN)
    groups = {}
    for seq_len in (len_p, len_s):
        f32mat = _dft_matrix(seq_len)
        fmat = jnp.asarray(f32mat).astype(BF16)
        fmat_t = jnp.asarray(np.ascontiguousarray(f32mat.T)).astype(BF16)
        tables = _hyena_tables(seq_len, jnp.asarray(_hyena_positions(seq_len)), w1p, b1, hy_w2, b2,
                               hy_w3, hy_decay, fmat)
        groups[seq_len] = (tables, fmat, fmat_t)

    rope_g = tuple(jnp.asarray(t) for t in _rope_tables(len_s, HEAD_DIM))
    rope_m = tuple(jnp.asarray(t) for t in _rope_tables(len_s, MLA_ROPE))

    gn_g = ret_gn_g.reshape(DEPTH, 1, RET_WIDTH)
    short_b = hy_short_b.reshape(DEPTH, 1, HY_IN)
    sink = gqa_sink.reshape(DEPTH, 1, GQA_Q_HEADS)
    q_norm = mla_q_norm.reshape(DEPTH, 1, MLA_Q_RANK)
    kv_norm = mla_kv_norm.reshape(DEPTH, 1, MLA_KV_RANK)
    conv_b = ffn_conv_b.reshape(DEPTH, 1, D_FF)

    xp = x_prompt.reshape(n_p * len_p, D_MODEL)
    xs = x_sample.reshape(n_s * len_s, D_MODEL)
    tm_p = 512
    tm_s = 512
    nb_p = 2
    cond_p = lambda i: 0
    cond_s = lambda i: 1 + (i * tm_s) // len_s
    tm_ffn = 1024
    cond_s_ffn = lambda i: 1 + (i * tm_ffn) // len_s

    st = gkv = ckr = None
    for l in range(DEPTH):
        tables, fmat, fmat_t = groups[len_p]
        ret_in, hy_in, gqa_in, mla_in = _in_proj(xp, mod, norm_g, w_in_b, l, tm_p, cond_p)
        y_ret, *st = _retention(ret_in, ret_decay_logit, gn_g, None, st, l, n_p, len_p, nb_p)
        y_hy = _hyena(hy_in, hy_short_w, short_b, hy_bias, tables, fmat, fmat_t, l, n_p, len_p)
        y_gqa, *gkv = _gqa_ctx(gqa_in, sink, gkv, l, n_p, len_p, nb_p)
        y_mla, *ckr = _mla(mla_in, q_norm, kv_norm, w_uq_b, w_uk_b, w_uv_b, None, None, ckr, l, n_p, len_p,
                           nb_p)
        xm, h2 = _out_proj(xp, (y_ret, y_hy, y_gqa, y_mla), mod, norm_g, w_out_b, l, tm_p, cond_p)
        xp = _ffn(h2, xm, mod, norm_g, w_up_b, ffn_conv_w, conv_b, w_down_b, l, tm_ffn, len_p, cond_p)
        tables, fmat, fmat_t = groups[len_s]
        ret_in, hy_in, gqa_in, mla_in = _in_proj(xs, mod, norm_g, w_in_b, l, tm_s, cond_s)
        y_ret = _retention(ret_in, ret_decay_logit, gn_g, state_ret, None, l, n_s, len_s, 1)
        y_hy = _hyena(hy_in, hy_short_w, short_b, hy_bias, tables, fmat, fmat_t, l, n_s, len_s)
        y_gqa = _gqa_win(gqa_in, sink, cache_gqa_k, cache_gqa_v, rope_g[0], rope_g[1], l, n_s, len_s)
        y_mla = _mla(mla_in, q_norm, kv_norm, w_uq_b, w_uk_b, w_uv_b, (cache_mla_ckv, cache_mla_krope),
                     rope_m, None, l, n_s, len_s, 1)
        xm, h2 = _out_proj(xs, (y_ret, y_hy, y_gqa, y_mla), mod, norm_g, w_out_b, l, tm_s, cond_s)
        xs = _ffn(h2, xm, mod, norm_g, w_up_b, ffn_conv_w, conv_b, w_down_b, l, tm_ffn, len_s, cond_s_ffn)

    return (xp.reshape(n_p, len_p, D_MODEL), xs.reshape(n_s, len_s, D_MODEL),
            st[0], gkv[0], gkv[1], ckr[0], ckr[1])
```

```python
import functools
import math

import numpy as np
import jax
import jax.numpy as jnp
from jax import lax
from jax.experimental import pallas as pl
from jax.experimental.pallas import tpu as pltpu

F32 = jnp.float32
BF16 = jnp.bfloat16

D_MODEL = 1024
DEPTH = 4
GRID_W = 64
EPS = 1e-6
NEG_INF = -1e30
ROPE_BASE = 10000.0
BLOCK = 128
WINDOW = 128
HEAD_DIM = 64
LANES = 128
RET_HEADS = 4
RET_WIDTH = RET_HEADS * HEAD_DIM
HY_WIDTH = 256
HY_ORDER = 2
HY_BANDS = 8
HY_POS_DIM = 1 + 2 * HY_BANDS
HY_POS_PAD = 32
HY_FILTER_HIDDEN = 64
GQA_Q_HEADS = 4
GQA_KV_HEADS = 2
GQA_GROUPS = GQA_Q_HEADS // GQA_KV_HEADS
MLA_HEADS = 4
MLA_Q_RANK = 256
MLA_KV_RANK = 128
MLA_NOPE = 64
MLA_ROPE = 32
MLA_V = 64
D_FF = 2816
RET_IN = 4 * RET_WIDTH
HY_IN = (HY_ORDER + 1) * HY_WIDTH
GQA_IN = (GQA_Q_HEADS + 2 * GQA_KV_HEADS) * HEAD_DIM
MLA_IN = MLA_Q_RANK + MLA_KV_RANK + MLA_ROPE
MLA_IN_TILED = MLA_Q_RANK + MLA_KV_RANK + MLA_HEADS * MLA_ROPE
IN_WIDTH = RET_IN + HY_IN + GQA_IN + MLA_IN
IN_WIDTH_TILED = RET_IN + HY_IN + GQA_IN + MLA_IN_TILED
GQA_HEAD_ORDER = (0, 2, 1, 3)
MIX_PART = 256
N_COND = 8
FFN_CHUNK = 256
N_FFN_CHUNKS = D_FF // FFN_CHUNK

VMEM_LIMIT = 56 * 1024 * 1024


def _params(n_axes=1):
    return pltpu.CompilerParams(
        dimension_semantics=("arbitrary",) * n_axes, vmem_limit_bytes=VMEM_LIMIT)


def _dot(a, b):
    return jnp.dot(a, b, preferred_element_type=F32)


def _dot_nt(a, b):
    return lax.dot_general(a, b, (((1,), (1,)), ((), ())), preferred_element_type=F32)


def _dot_tn(a, b):
    return lax.dot_general(a, b, (((0,), (0,)), ((), ())), preferred_element_type=F32)


def _rms(x, g):
    return x * lax.rsqrt(jnp.mean(x * x, axis=-1, keepdims=True) + EPS) * g


def _sigmoid(x):
    return 1.0 / (1.0 + jnp.exp(-x))


def _silu(x):
    return x * _sigmoid(x)


def _shift_rows(x, seq_len):
    n = x.shape[0]
    assert seq_len & (seq_len - 1) == 0
    pos = lax.broadcasted_iota(jnp.int32, (n, 1), 0) & (seq_len - 1)
    prev = jnp.where(pos != 0, pltpu.roll(x, 1, axis=0), 0.0)
    nxt = jnp.where(pos != seq_len - 1, pltpu.roll(x, n - 1, axis=0), 0.0)
    return prev, nxt


def _rot_half(x, half):
    lane = lax.broadcasted_iota(jnp.int32, x.shape, 1) & (2 * half - 1)
    return jnp.where(lane < half, pltpu.roll(x, LANES - half, axis=1), pltpu.roll(x, half, axis=1))


def _lane_iota():
    return lax.broadcasted_iota(jnp.int32, (1, LANES), 1)


def _split_heads(t, lo):
    return jnp.concatenate([jnp.where(lo, t, 0.0), jnp.where(lo, 0.0, t)], axis=0)


def _merge_heads(o, lo):
    n = o.shape[0] // 2
    return jnp.where(lo, o[:n], o[n:])


def _per_head(vals, width):
    lane = lax.broadcasted_iota(jnp.int32, (1, width), 1)
    out = vals[-1]
    for h in range(len(vals) - 2, -1, -1):
        out = jnp.where(lane < (h + 1) * HEAD_DIM, vals[h], out)
    return out


def _mod_kernel(cond_ref, w_ref, b_ref, out_ref):
    s = _silu(cond_ref[...]).astype(BF16)
    out_ref[0] = _dot(s, w_ref[0].astype(BF16)) + b_ref[0]


def _modulation(cond, ada_w, ada_b):
    tn = 1536
    return pl.pallas_call(
        _mod_kernel,
        out_shape=jax.ShapeDtypeStruct((DEPTH, N_COND, 6 * D_MODEL), F32),
        grid=(DEPTH, 6 * D_MODEL // tn),
        in_specs=[
            pl.BlockSpec((N_COND, D_MODEL), lambda l, j: (0, 0)),
            pl.BlockSpec((1, D_MODEL, tn), lambda l, j: (l, 0, j)),
            pl.BlockSpec((1, 1, tn), lambda l, j: (l, 0, j)),
        ],
        out_specs=pl.BlockSpec((1, N_COND, tn), lambda l, j: (l, 0, j)),
        compiler_params=_params(2),
        name="modulation",
    )(cond, ada_w, ada_b.reshape(DEPTH, 1, 6 * D_MODEL))


def _filter_kernel(z_ref, w1_ref, b1_ref, w2_ref, b2_ref, w3_ref, dec_ref, f_ref, out_ref, *, seq_len):
    hi = lax.Precision.HIGHEST
    h = jnp.sin(jnp.dot(z_ref[...], w1_ref[0], precision=hi, preferred_element_type=F32) + b1_ref[0])
    h = jnp.sin(jnp.dot(h, w2_ref[0], precision=hi, preferred_element_type=F32) + b2_ref[0])
    h = jnp.dot(h, w3_ref[0], precision=hi, preferred_element_type=F32)
    row = lax.broadcasted_iota(jnp.int32, (seq_len, 1), 0)
    t = row.astype(F32) / seq_len
    dec = jnp.abs(dec_ref[0])
    win_f = jnp.exp(-t * dec[0:1, :])
    win_b = jnp.exp(-t * dec[1:2, :])
    fm = f_ref[...]
    inv = 1.0 / seq_len
    scale = jnp.where(row == 0, 0.5 * inv, inv)
    for o in range(HY_ORDER):
        base = o * 2 * HY_WIDTH
        hf = h[:, base:base + HY_WIDTH] * win_f
        hb = h[:, base + HY_WIDTH:base + 2 * HY_WIDTH] * win_b
        hbs = jnp.where(row != 0, pltpu.roll(hb, 1, axis=0), 0.0)
        sf = _dot(fm, hf.astype(BF16))
        sb = _dot(fm, hbs.astype(BF16))
        g_re = sf[:seq_len] + sb[:seq_len]
        g_im = sf[seq_len:] - sb[seq_len:]
        g_ny = sf[seq_len:] + sb[seq_len:]
        a = g_re * scale
        out_ref[0, 3 * o] = a
        out_ref[0, 3 * o + 1] = jnp.where(row == 0, g_ny * (0.5 * inv), a)
        out_ref[0, 3 * o + 2] = jnp.where(row == 0, 0.0, g_im * inv)


def _hyena_tables(seq_len, z, w1p, b1, w2, b2, w3, decay, fmat):
    hid = HY_FILTER_HIDDEN
    wide = HY_ORDER * 2 * HY_WIDTH
    return pl.pallas_call(
        functools.partial(_filter_kernel, seq_len=seq_len),
        out_shape=jax.ShapeDtypeStruct((DEPTH, 3 * HY_ORDER, seq_len, HY_WIDTH), F32),
        grid=(DEPTH,),
        in_specs=[
            pl.BlockSpec((seq_len, HY_POS_PAD), lambda l: (0, 0)),
            pl.BlockSpec((1, HY_POS_PAD, hid), lambda l: (l, 0, 0)),
            pl.BlockSpec((1, 1, hid), lambda l: (l, 0, 0)),
            pl.BlockSpec((1, hid, hid), lambda l: (l, 0, 0)),
            pl.BlockSpec((1, 1, hid), lambda l: (l, 0, 0)),
            pl.BlockSpec((1, hid, wide), lambda l: (l, 0, 0)),
            pl.BlockSpec((1, 2, HY_WIDTH), lambda l: (l, 0, 0)),
            pl.BlockSpec((2 * seq_len, seq_len), lambda l: (0, 0)),
        ],
        out_specs=pl.BlockSpec((1, 3 * HY_ORDER, seq_len, HY_WIDTH), lambda l: (l, 0, 0, 0)),
        compiler_params=_params(1),
        name=f"hyena_tables_{seq_len}",
    )(z, w1p, b1, w2, b2, w3, decay, fmat)


def _layer_stacked(tail, n_batch, layer, nb):
    zeros = (0,) * len(tail)
    return (jax.ShapeDtypeStruct((n_batch, DEPTH) + tail, F32),
            pl.BlockSpec((nb, 1) + tail, lambda b: (b, layer) + zeros))


def _carry_specs(in_specs, args, carried, first_out):
    aliases = {}
    if carried is not None:
        for k, arr in enumerate(carried):
            aliases[len(args)] = first_out + k
            in_specs.append(pl.BlockSpec(memory_space=pl.ANY))
            args.append(arr)
    return aliases


def _in_proj_kernel(x_ref, mod_ref, g_ref, w_ref, ret_ref, hy_ref, gqa_ref, mla_ref):
    shift = mod_ref[0, 0, 0:1, :]
    scale = mod_ref[0, 0, 1:2, :]
    h = (_rms(x_ref[...], g_ref[0, 0:1, :]) * (1.0 + scale) + shift).astype(BF16)
    c0 = 0
    for ref in (ret_ref, hy_ref, gqa_ref, mla_ref):
        width = ref.shape[1]
        ref[...] = _dot(h, w_ref[0, :, c0:c0 + width])
        c0 += width


def _in_proj(x, mod, norm_g, w_in, layer, tm, cond_of_tile):
    rows = x.shape[0]
    widths = (RET_IN, HY_IN, GQA_IN, MLA_IN_TILED)
    return pl.pallas_call(
        _in_proj_kernel,
        out_shape=[jax.ShapeDtypeStruct((rows, w), F32) for w in widths],
        grid=(rows // tm,),
        in_specs=[
            pl.BlockSpec((tm, D_MODEL), lambda i: (i, 0)),
            pl.BlockSpec((1, 1, 6, D_MODEL), lambda i: (layer, cond_of_tile(i), 0, 0)),
            pl.BlockSpec((1, 4, D_MODEL), lambda i: (layer, 0, 0)),
            pl.BlockSpec((1, D_MODEL, IN_WIDTH_TILED), lambda i: (layer, 0, 0)),
        ],
        out_specs=[pl.BlockSpec((tm, w), lambda i: (i, 0)) for w in widths],
        compiler_params=_params(1),
        name="in_proj",
    )(x, mod, norm_g, w_in)


def _channel_kernel(x_ref, m0_ref, m1_ref, m2_ref, m3_ref, mod_ref, g_ref, wo_ref, wu_ref, cw_ref, cb_ref,
                    wd_ref, out_ref, act_ref, *, seq_len):
    y = None
    for i, m_ref in enumerate((m0_ref, m1_ref, m2_ref, m3_ref)):
        part = _dot(m_ref[...], wo_ref[0, i * MIX_PART:(i + 1) * MIX_PART, :])
        y = part if y is None else y + part
    gate1 = mod_ref[0, 0, 2:3, :]
    shift2 = mod_ref[0, 0, 3:4, :]
    scale2 = mod_ref[0, 0, 4:5, :]
    xm = x_ref[...] + gate1 * _rms(y, g_ref[0, 1:2, :])
    out_ref[...] = xm
    h2 = (_rms(xm, g_ref[0, 2:3, :]) * (1.0 + scale2) + shift2).astype(BF16)
    for c in range(N_FFN_CHUNKS):
        sl = slice(c * FFN_CHUNK, (c + 1) * FFN_CHUNK)
        gate = _dot(h2, wu_ref[0, :, sl])
        up = _dot(h2, wu_ref[0, :, D_FF + c * FFN_CHUNK:D_FF + (c + 1) * FFN_CHUNK])
        prev, nxt = _shift_rows(gate, seq_len)
        gate = (prev * cw_ref[0, 0:1, sl] + gate * cw_ref[0, 1:2, sl] + nxt * cw_ref[0, 2:3, sl]
                + cb_ref[0, :, sl])
        act_ref[:, sl] = (_silu(gate) * up).astype(BF16)
    ffn = _dot(act_ref[...], wd_ref[0])
    gate2 = mod_ref[0, 0, 5:6, :]
    out_ref[...] = out_ref[...] + gate2 * _rms(ffn, g_ref[0, 3:4, :])


def _channel_mix(x, mixes, mod, norm_g, w_out, w_up, conv_w, conv_b, w_down, layer, tm, seq_len, cond_of_tile):
    rows = x.shape[0]
    resident = dict(pipeline_mode=pl.Buffered(1))
    return pl.pallas_call(
        functools.partial(_channel_kernel, seq_len=seq_len),
        out_shape=jax.ShapeDtypeStruct((rows, D_MODEL), F32),
        grid=(rows // tm,),
        in_specs=[pl.BlockSpec((tm, D_MODEL), lambda i: (i, 0))]
        + [pl.BlockSpec((tm, MIX_PART), lambda i: (i, 0))] * 4
        + [
            pl.BlockSpec((1, 1, 6, D_MODEL), lambda i: (layer, cond_of_tile(i), 0, 0)),
            pl.BlockSpec((1, 4, D_MODEL), lambda i: (layer, 0, 0)),
            pl.BlockSpec((1, D_MODEL, D_MODEL), lambda i: (layer, 0, 0), **resident),
            pl.BlockSpec((1, D_MODEL, 2 * D_FF), lambda i: (layer, 0, 0), **resident),
            pl.BlockSpec((1, 3, D_FF), lambda i: (layer, 0, 0)),
            pl.BlockSpec((1, 1, D_FF), lambda i: (layer, 0, 0)),
            pl.BlockSpec((1, D_FF, D_MODEL), lambda i: (layer, 0, 0), **resident),
        ],
        out_specs=pl.BlockSpec((tm, D_MODEL), lambda i: (i, 0)),
        scratch_shapes=[pltpu.VMEM((tm, D_FF), BF16)],
        compiler_params=_params(1),
        name="channel_mix",
    )(x, *mixes, mod, norm_g, w_out, w_up, conv_w, conv_b, w_down)


def _log_gamma(dl_ref):
    dl = dl_ref[0]
    return jnp.minimum(dl, 0.0) - jnp.log(1.0 + jnp.exp(-jnp.abs(dl)))


def _pair_decay(lg, t, r0, bq, seq_len):
    top = lax.broadcasted_iota(jnp.int32, (2 * bq, 1), 0) < bq
    lgf = jnp.where(top, lg[0:1, 2 * t:2 * t + 1], lg[0:1, 2 * t + 1:2 * t + 2])
    lgb = jnp.where(top, lg[1:2, 2 * t:2 * t + 1], lg[1:2, 2 * t + 1:2 * t + 2])
    rowf = ((lax.broadcasted_iota(jnp.int32, (2 * bq, 1), 0) & (bq - 1)) + r0).astype(F32)
    colf = lax.broadcasted_iota(jnp.int32, (1, seq_len), 1).astype(F32)
    lag = rowf - colf
    decay = jnp.exp(jnp.where(lag >= 0.0, lag * lgf, -lag * lgb))
    return jnp.where(lag == 0.0, 2.0, decay)


def _head_norm(o, lo):
    inv = 1.0 / HEAD_DIM
    s_all = jnp.sum(o, axis=-1, keepdims=True)
    s_lo = jnp.sum(jnp.where(lo, o, 0.0), axis=-1, keepdims=True)
    d = o - jnp.where(lo, s_lo, s_all - s_lo) * inv
    d2 = d * d
    v_all = jnp.sum(d2, axis=-1, keepdims=True)
    v_lo = jnp.sum(jnp.where(lo, d2, 0.0), axis=-1, keepdims=True)
    return d * lax.rsqrt(jnp.where(lo, v_lo, v_all - v_lo) * inv + EPS)


def _ret_ctx_kernel(*refs, seq_len, nb):
    x_ref, dl_ref, gn_ref, *_, y_ref, st_ref, dec_scr, kdec_scr = refs
    lo = _lane_iota() < HEAD_DIM

    @pl.when(pl.program_id(0) == 0)
    def _():
        lg = _log_gamma(dl_ref)
        posf = lax.broadcasted_iota(jnp.int32, (seq_len, 1), 0).astype(F32)
        lgf = _per_head([lg[0:1, h:h + 1] for h in range(RET_HEADS)], RET_WIDTH)
        lgb = _per_head([lg[1:2, h:h + 1] for h in range(RET_HEADS)], RET_WIDTH)
        kdec_scr[0] = jnp.exp((seq_len - 1.0 - posf) * lgf)
        kdec_scr[1] = jnp.exp(posf * lgb)
        for t in range(RET_HEADS // 2):
            dec_scr[t] = _pair_decay(lg, t, 0, seq_len, seq_len)

    for i in range(nb):
        rows = slice(i * seq_len, (i + 1) * seq_len)
        for t in range(RET_HEADS // 2):
            cs = slice(t * LANES, (t + 1) * LANES)
            q = x_ref[rows, cs]
            k = x_ref[rows, RET_WIDTH + t * LANES:RET_WIDTH + (t + 1) * LANES] * (HEAD_DIM ** -0.5)
            vb = x_ref[rows, 2 * RET_WIDTH + t * LANES:2 * RET_WIDTH + (t + 1) * LANES].astype(BF16)
            gate = x_ref[rows, 3 * RET_WIDTH + t * LANES:3 * RET_WIDTH + (t + 1) * LANES]
            s = _dot_nt(_split_heads(q, lo).astype(BF16), k.astype(BF16)) * dec_scr[t]
            o = _merge_heads(_dot(s.astype(BF16), vb), lo)
            y = _silu(gate) * (_head_norm(o, lo) * gn_ref[0, :, cs])
            y_ref[rows, cs] = y.astype(BF16)
            for d in range(2):
                st = _dot_tn((k * kdec_scr[d, :, cs]).astype(BF16), vb)
                st_ref[i, 0, d, 2 * t] = st[0:HEAD_DIM, 0:HEAD_DIM]
                st_ref[i, 0, d, 2 * t + 1] = st[HEAD_DIM:LANES, HEAD_DIM:LANES]


def _ret_lat_kernel(x_ref, dl_ref, gn_ref, s0_ref, y_ref, *, seq_len, bq):
    lo = _lane_iota() < HEAD_DIM
    lg = _log_gamma(dl_ref)
    zero = jnp.zeros((HEAD_DIM, HEAD_DIM), F32)
    for t in range(RET_HEADS // 2):
        cs = slice(t * LANES, (t + 1) * LANES)
        kb = (x_ref[:, RET_WIDTH + t * LANES:RET_WIDTH + (t + 1) * LANES] * (HEAD_DIM ** -0.5)).astype(BF16)
        vb = x_ref[:, 2 * RET_WIDTH + t * LANES:2 * RET_WIDTH + (t + 1) * LANES].astype(BF16)
        lgf = _per_head([lg[0:1, 2 * t:2 * t + 1], lg[0:1, 2 * t + 1:2 * t + 2]], LANES)
        lgb = _per_head([lg[1:2, 2 * t:2 * t + 1], lg[1:2, 2 * t + 1:2 * t + 2]], LANES)
        s0 = []
        for d in range(2):
            a = s0_ref[0, 0, d, 2 * t]
            b = s0_ref[0, 0, d, 2 * t + 1]
            s0.append(jnp.concatenate([jnp.concatenate([a, zero], axis=1),
                                       jnp.concatenate([zero, b], axis=1)], axis=0).astype(BF16))
        for r0 in range(0, seq_len, bq):
            q = x_ref[r0:r0 + bq, cs]
            s = _dot_nt(_split_heads(q, lo).astype(BF16), kb) * _pair_decay(lg, t, r0, bq, seq_len)
            o = _merge_heads(_dot(s.astype(BF16), vb), lo)
            rowf = (lax.broadcasted_iota(jnp.int32, (bq, 1), 0) + r0).astype(F32)
            qf = (q * jnp.exp((rowf + 1.0) * lgf)).astype(BF16)
            qr = (q * jnp.exp((seq_len - rowf) * lgb)).astype(BF16)
            o = o + _dot(qf, s0[0]) + _dot(qr, s0[1])
            gate = x_ref[r0:r0 + bq, 3 * RET_WIDTH + t * LANES:3 * RET_WIDTH + (t + 1) * LANES]
            y = _silu(gate) * (_head_norm(o, lo) * gn_ref[0, :, cs])
            y_ref[r0:r0 + bq, cs] = y.astype(BF16)


def _retention(ret_in, decay_logit, gn_g, state0, carried, layer, n_batch, seq_len, nb):
    latent = state0 is not None
    in_specs = [
        pl.BlockSpec((nb * seq_len, RET_IN), lambda b: (b, 0)),
        pl.BlockSpec((1, 2, RET_HEADS), lambda b: (layer, 0, 0)),
        pl.BlockSpec((1, 1, RET_WIDTH), lambda b: (layer, 0, 0)),
    ]
    args = [ret_in, decay_logit, gn_g]
    y_shape = jax.ShapeDtypeStruct((n_batch * seq_len, MIX_PART), BF16)
    y_spec = pl.BlockSpec((nb * seq_len, MIX_PART), lambda b: (b, 0))
    if latent:
        assert nb == 1
        kern = functools.partial(_ret_lat_kernel, seq_len=seq_len, bq=256)
        in_specs.append(pl.BlockSpec((1, 1, 2, RET_HEADS, HEAD_DIM, HEAD_DIM),
                                     lambda b: (b, layer, 0, 0, 0, 0)))
        args.append(state0)
        out_shape, out_specs, aliases, scratch = y_shape, y_spec, {}, []
    else:
        kern = functools.partial(_ret_ctx_kernel, seq_len=seq_len, nb=nb)
        st_shape, st_spec = _layer_stacked((2, RET_HEADS, HEAD_DIM, HEAD_DIM), n_batch, layer, nb)
        out_shape = [y_shape, st_shape]
        out_specs = [y_spec, st_spec]
        aliases = _carry_specs(in_specs, args, carried, 1)
        scratch = [pltpu.VMEM((RET_HEADS // 2, 2 * seq_len, seq_len), F32),
                   pltpu.VMEM((2, seq_len, RET_WIDTH), F32)]
    return pl.pallas_call(
        kern, out_shape=out_shape, grid=(n_batch // nb,), in_specs=in_specs, out_specs=out_specs,
        input_output_aliases=aliases, scratch_shapes=scratch,
        compiler_params=_params(1), name=f"retention_{seq_len}",
    )(*args)


def _hyena_kernel(x_ref, sw_ref, sb_ref, bias_ref, tab_ref, f_ref, ft_ref, y_ref, *, seq_len):
    x = x_ref[...]
    prev, nxt = _shift_rows(x, seq_len)
    u = prev * sw_ref[0, 0:1, :] + x * sw_ref[0, 1:2, :] + nxt * sw_ref[0, 2:3, :] + sb_ref[0]
    x1 = u[:, 0:HY_WIDTH]
    x2 = u[:, HY_WIDTH:2 * HY_WIDTH]
    z = u[:, 2 * HY_WIDTH:3 * HY_WIDTH]
    for o, gate in enumerate((x1, x2)):
        spec = _dot(f_ref[...], z.astype(BF16))
        s_re = spec[:seq_len]
        s_im = spec[seq_len:]
        a = tab_ref[0, 3 * o]
        a_ny = tab_ref[0, 3 * o + 1]
        b = tab_ref[0, 3 * o + 2]
        y_re = (s_re * a - s_im * b).astype(BF16)
        y_im = (s_re * b + s_im * a_ny).astype(BF16)
        conv = _dot(ft_ref[:, 0:seq_len], y_re) + _dot(ft_ref[:, seq_len:2 * seq_len], y_im)
        z = gate * (conv + z * bias_ref[0, o:o + 1, :])
    y_ref[...] = z.astype(BF16)


def _hyena(hy_in, short_w, short_b, hy_bias, tables, fmat, fmat_t, layer, n_batch, seq_len):
    const = dict(pipeline_mode=pl.Buffered(1))
    return pl.pallas_call(
        functools.partial(_hyena_kernel, seq_len=seq_len),
        out_shape=jax.ShapeDtypeStruct((n_batch * seq_len, MIX_PART), BF16),
        grid=(n_batch,),
        in_specs=[
            pl.BlockSpec((seq_len, HY_IN), lambda b: (b, 0)),
            pl.BlockSpec((1, 3, HY_IN), lambda b: (layer, 0, 0)),
            pl.BlockSpec((1, 1, HY_IN), lambda b: (layer, 0, 0)),
            pl.BlockSpec((1, HY_ORDER, HY_WIDTH), lambda b: (layer, 0, 0)),
            pl.BlockSpec((1, 3 * HY_ORDER, seq_len, HY_WIDTH), lambda b: (layer, 0, 0, 0), **const),
            pl.BlockSpec((2 * seq_len, seq_len), lambda b: (0, 0), **const),
            pl.BlockSpec((seq_len, 2 * seq_len), lambda b: (0, 0), **const),
        ],
        out_specs=pl.BlockSpec((seq_len, MIX_PART), lambda b: (b, 0)),
        compiler_params=_params(1),
        name=f"hyena_{seq_len}",
    )(hy_in, short_w, short_b, hy_bias, tables, fmat, fmat_t)


def _pair_sink(sink_ref, g, n):
    top = lax.broadcasted_iota(jnp.int32, (2 * n, 1), 0) < n
    return jnp.where(top, sink_ref[0, :, g:g + 1], sink_ref[0, :, GQA_GROUPS + g:GQA_GROUPS + g + 1])


def _gqa_ctx_kernel(*refs, seq_len, nb):
    x_ref, sink_ref, *_, y_ref, k_out_ref, v_out_ref = refs
    nq = GQA_Q_HEADS * HEAD_DIM
    scale = HEAD_DIM ** -0.5
    lo = _lane_iota() < HEAD_DIM
    ones = jnp.ones((seq_len, LANES), BF16)
    for i in range(nb):
        rows = slice(i * seq_len, (i + 1) * seq_len)
        k = x_ref[rows, nq:nq + LANES]
        v = x_ref[rows, nq + LANES:nq + 2 * LANES]
        for kv in range(GQA_KV_HEADS):
            k_out_ref[i, 0, kv] = k[:, kv * HEAD_DIM:(kv + 1) * HEAD_DIM]
            v_out_ref[i, 0, kv] = v[:, kv * HEAD_DIM:(kv + 1) * HEAD_DIM]
        kb = k.astype(BF16)
        v_ext = jnp.concatenate([v.astype(BF16), ones], axis=1)
        for g in range(GQA_GROUPS):
            cs = slice(g * LANES, (g + 1) * LANES)
            s = _dot_nt(_split_heads(x_ref[rows, cs], lo).astype(BF16), kb) * scale
            sink = _pair_sink(sink_ref, g, seq_len)
            m = jnp.maximum(jnp.max(s, axis=-1, keepdims=True), sink)
            pv = _dot(jnp.exp(s - m).astype(BF16), v_ext)
            den = pv[:, LANES:LANES + 1] + jnp.exp(sink - m)
            y_ref[rows, cs] = _merge_heads(pv[:, 0:LANES] / den, lo).astype(BF16)


def _gqa_ctx(gqa_in, sink, carried, layer, n_batch, seq_len, nb):
    kv_shape, kv_spec = _layer_stacked((GQA_KV_HEADS, seq_len, HEAD_DIM), n_batch, layer, nb)
    in_specs = [
        pl.BlockSpec((nb * seq_len, GQA_IN), lambda b: (b, 0)),
        pl.BlockSpec((1, 1, GQA_Q_HEADS), lambda b: (layer, 0, 0)),
    ]
    args = [gqa_in, sink]
    aliases = _carry_specs(in_specs, args, carried, 1)
    return pl.pallas_call(
        functools.partial(_gqa_ctx_kernel, seq_len=seq_len, nb=nb),
        out_shape=[jax.ShapeDtypeStruct((n_batch * seq_len, MIX_PART), BF16), kv_shape, kv_shape],
        grid=(n_batch // nb,),
        in_specs=in_specs,
        out_specs=[pl.BlockSpec((nb * seq_len, MIX_PART), lambda b: (b, 0)), kv_spec, kv_spec],
        input_output_aliases=aliases,
        compiler_params=_params(1),
        name=f"gqa_ctx_{seq_len}",
    )(*args)


def _gqa_win_kernel(x_ref, sink_ref, kc_ref, vc_ref, cos_ref, sin_ref, y_ref,
                    q_scr, k_scr, v_scr, kc_scr, vc_scr, *, seq_len):
    nq = GQA_Q_HEADS * HEAD_DIM
    scale = HEAD_DIM ** -0.5
    lo = _lane_iota() < HEAD_DIM
    past = kc_scr.shape[0]
    cos = cos_ref[...]
    sin = sin_ref[...]

    def rope(t):
        return t * cos + _rot_half(t, HEAD_DIM // 2) * sin

    for g in range(GQA_GROUPS):
        q = rope(x_ref[:, g * LANES:(g + 1) * LANES])
        q_scr[g, 0] = jnp.where(lo, q, 0.0).astype(BF16)
        q_scr[g, 1] = jnp.where(lo, 0.0, q).astype(BF16)
    zeros = jnp.zeros((BLOCK, LANES), BF16)
    for r0 in (0, BLOCK + seq_len):
        k_scr[r0:r0 + BLOCK, :] = zeros
        v_scr[r0:r0 + BLOCK, 0:LANES] = zeros
    k_scr[BLOCK:BLOCK + seq_len, :] = rope(x_ref[:, nq:nq + LANES]).astype(BF16)
    v_scr[BLOCK:BLOCK + seq_len, 0:LANES] = x_ref[:, nq + LANES:nq + 2 * LANES].astype(BF16)
    v_scr[:, LANES:2 * LANES] = jnp.ones((seq_len + 2 * BLOCK, LANES), BF16)
    kc_scr[...] = jnp.concatenate([kc_ref[0, 0, 0], kc_ref[0, 0, 1]], axis=1).astype(BF16)
    vc_scr[:, 0:LANES] = jnp.concatenate([vc_ref[0, 0, 0], vc_ref[0, 0, 1]], axis=1).astype(BF16)
    vc_scr[:, LANES:2 * LANES] = jnp.ones((past, LANES), BF16)

    ii = lax.broadcasted_iota(jnp.int32, (2 * BLOCK, 3 * BLOCK), 0) & (BLOCK - 1)
    jj = lax.broadcasted_iota(jnp.int32, (2 * BLOCK, 3 * BLOCK), 1)
    band = (jj >= ii) & (jj <= ii + 2 * WINDOW)

    def block(n, carry):
        r0 = pl.multiple_of(n * BLOCK, BLOCK)
        kpos = jj + (n - 1) * BLOCK
        valid = band & (kpos >= 0) & (kpos < seq_len)
        kw = k_scr[pl.ds(r0, 3 * BLOCK), :]
        vw = v_scr[pl.ds(r0, 3 * BLOCK), :]
        for g in range(GQA_GROUPS):
            q2 = jnp.concatenate([q_scr[g, 0, pl.ds(r0, BLOCK), :], q_scr[g, 1, pl.ds(r0, BLOCK), :]], axis=0)
            s_win = jnp.where(valid, _dot_nt(q2, kw) * scale, NEG_INF)
            s_ctx = _dot_nt(q2, kc_scr[...]) * scale
            sink = _pair_sink(sink_ref, g, BLOCK)
            m = jnp.maximum(jnp.maximum(jnp.max(s_win, axis=-1, keepdims=True),
                                        jnp.max(s_ctx, axis=-1, keepdims=True)), sink)
            pv = (_dot(jnp.exp(s_win - m).astype(BF16), vw)
                  + _dot(jnp.exp(s_ctx - m).astype(BF16), vc_scr[...]))
            den = pv[:, LANES:LANES + 1] + jnp.exp(sink - m)
            y_ref[pl.ds(r0, BLOCK), g * LANES:(g + 1) * LANES] = _merge_heads(
                pv[:, 0:LANES] / den, lo).astype(BF16)
        return carry

    lax.fori_loop(0, seq_len // BLOCK, block, 0)


def _gqa_win(gqa_in, sink, cache_k, cache_v, cos, sin, layer, n_batch, seq_len):
    past = cache_k.shape[3]
    return pl.pallas_call(
        functools.partial(_gqa_win_kernel, seq_len=seq_len),
        out_shape=jax.ShapeDtypeStruct((n_batch * seq_len, MIX_PART), BF16),
        grid=(n_batch,),
        in_specs=[
            pl.BlockSpec((seq_len, GQA_IN), lambda b: (b, 0)),
            pl.BlockSpec((1, 1, GQA_Q_HEADS), lambda b: (layer, 0, 0)),
            pl.BlockSpec((1, 1, GQA_KV_HEADS, past, HEAD_DIM), lambda b: (b, layer, 0, 0, 0)),
            pl.BlockSpec((1, 1, GQA_KV_HEADS, past, HEAD_DIM), lambda b: (b, layer, 0, 0, 0)),
            pl.BlockSpec((seq_len, LANES), lambda b: (0, 0)),
            pl.BlockSpec((seq_len, LANES), lambda b: (0, 0)),
        ],
        out_specs=pl.BlockSpec((seq_len, MIX_PART), lambda b: (b, 0)),
        scratch_shapes=[
            pltpu.VMEM((GQA_GROUPS, 2, seq_len, LANES), BF16),
            pltpu.VMEM((seq_len + 2 * BLOCK, LANES), BF16),
            pltpu.VMEM((seq_len + 2 * BLOCK, 2 * LANES), BF16),
            pltpu.VMEM((past, LANES), BF16),
            pltpu.VMEM((past, 2 * LANES), BF16),
        ],
        compiler_params=_params(1),
        name=f"gqa_win_{seq_len}",
    )(gqa_in, sink, cache_k, cache_v, cos, sin)


def _mla_kernel(*refs, seq_len, latent, bq, nb):
    if latent:
        (x_ref, qn_ref, kvn_ref, wq_ref, wk_ref, wv_ref, ckv_c_ref, kr_c_ref, cos_ref, sin_ref,
         y_ref, q_all, k_all, v_all) = refs
        past = ckv_c_ref.shape[2]
    else:
        (x_ref, qn_ref, kvn_ref, wq_ref, wk_ref, wv_ref, *_,
         y_ref, ckv_out_ref, kr_out_ref, q_all, k_all, v_all) = refs
        past = 0
    keys = past + seq_len
    nope_w = MLA_HEADS * MLA_NOPE
    scale = (MLA_NOPE + MLA_ROPE) ** -0.5
    lane = _lane_iota()
    lo = lane < HEAD_DIM
    rope_lanes = [(lane >= h * MLA_ROPE) & (lane < (h + 1) * MLA_ROPE) for h in range(MLA_HEADS)]
    for i in range(nb):
        rows = slice(i * seq_len, (i + 1) * seq_len)
        q_scr, k_scr, v_scr = q_all.at[i], k_all.at[i], v_all.at[i]
        q_lat = x_ref[rows, 0:MLA_Q_RANK]
        kv_lat = x_ref[rows, MLA_Q_RANK:MLA_Q_RANK + MLA_KV_RANK]
        k_rope = x_ref[rows, MLA_Q_RANK + MLA_KV_RANK:MLA_IN_TILED]
        mq = _dot(_rms(q_lat, qn_ref[0]).astype(BF16), wq_ref[0])
        ckv = _rms(kv_lat, kvn_ref[0])
        q_rope = mq[:, nope_w:nope_w + LANES]
        if latent:
            cos = cos_ref[...]
            sin = sin_ref[...]
            q_rope = q_rope * cos + _rot_half(q_rope, MLA_ROPE // 2) * sin
            k_rope = k_rope * cos + _rot_half(k_rope, MLA_ROPE // 2) * sin
            ckv_c = ckv_c_ref[0, 0].astype(BF16)
            k_scr[0:past, 0:nope_w] = _dot(ckv_c, wk_ref[0]).astype(BF16)
            v_scr[0:past, :] = _dot(ckv_c, wv_ref[0]).astype(BF16)
            kr_c = kr_c_ref[0, 0]
            k_scr[0:past, nope_w:nope_w + LANES] = jnp.concatenate([kr_c] * MLA_HEADS, axis=1).astype(BF16)
        else:
            ckv_out_ref[i, 0] = ckv
            kr_out_ref[i, 0] = k_rope[:, 0:MLA_ROPE]
        q_scr[:, 0:nope_w] = mq[:, 0:nope_w]
        q_scr[:, nope_w:nope_w + LANES] = q_rope
        ckv_b = ckv.astype(BF16)
        k_scr[past:keys, 0:nope_w] = _dot(ckv_b, wk_ref[0]).astype(BF16)
        v_scr[past:keys, :] = _dot(ckv_b, wv_ref[0]).astype(BF16)
        k_scr[past:keys, nope_w:nope_w + LANES] = k_rope.astype(BF16)

        ones = jnp.ones((keys, LANES), BF16)
        kr_all = k_scr[:, nope_w:nope_w + LANES]
        for t in range(MLA_HEADS // 2):
            cs = slice(t * LANES, (t + 1) * LANES)
            k_cat = jnp.concatenate([k_scr[:, cs], kr_all], axis=1)
            v_ext = jnp.concatenate([v_scr[:, cs], ones], axis=1)
            for r0 in range(0, seq_len, bq):
                qn = q_scr[r0:r0 + bq, cs]
                qr = q_scr[r0:r0 + bq, nope_w:nope_w + LANES]
                q2 = jnp.concatenate([
                    jnp.concatenate([jnp.where(lo, qn, 0.0), jnp.where(rope_lanes[2 * t], qr, 0.0)], axis=1),
                    jnp.concatenate([jnp.where(lo, 0.0, qn), jnp.where(rope_lanes[2 * t + 1], qr, 0.0)], axis=1),
                ], axis=0).astype(BF16)
                s = _dot_nt(q2, k_cat) * scale
                m = jnp.max(s, axis=-1, keepdims=True)
                pv = _dot(jnp.exp(s - m).astype(BF16), v_ext)
                o = pv[:, 0:LANES] / pv[:, LANES:LANES + 1]
                y_ref[i * seq_len + r0:i * seq_len + r0 + bq, cs] = _merge_heads(o, lo).astype(BF16)


def _mla(mla_in, q_norm, kv_norm, w_uq, w_uk, w_uv, cache, rope, carried, layer, n_batch, seq_len, nb):
    latent = cache is not None
    aliases = {}
    past = cache[0].shape[2] if latent else 0
    kern = functools.partial(_mla_kernel, seq_len=seq_len, latent=latent, bq=256, nb=nb)
    q_w = MLA_HEADS * (MLA_NOPE + MLA_ROPE)
    kv_w = MLA_HEADS * MLA_NOPE
    in_specs = [
        pl.BlockSpec((nb * seq_len, MLA_IN_TILED), lambda b: (b, 0)),
        pl.BlockSpec((1, 1, MLA_Q_RANK), lambda b: (layer, 0, 0)),
        pl.BlockSpec((1, 1, MLA_KV_RANK), lambda b: (layer, 0, 0)),
        pl.BlockSpec((1, MLA_Q_RANK, q_w), lambda b: (layer, 0, 0)),
        pl.BlockSpec((1, MLA_KV_RANK, kv_w), lambda b: (layer, 0, 0)),
        pl.BlockSpec((1, MLA_KV_RANK, kv_w), lambda b: (layer, 0, 0)),
    ]
    args = [mla_in, q_norm, kv_norm, w_uq, w_uk, w_uv]
    y_shape = jax.ShapeDtypeStruct((n_batch * seq_len, MIX_PART), BF16)
    y_spec = pl.BlockSpec((nb * seq_len, MIX_PART), lambda b: (b, 0))
    if latent:
        assert nb == 1
        in_specs += [
            pl.BlockSpec((1, 1, past, MLA_KV_RANK), lambda b: (b, layer, 0, 0)),
            pl.BlockSpec((1, 1, past, MLA_ROPE), lambda b: (b, layer, 0, 0)),
            pl.BlockSpec((seq_len, LANES), lambda b: (0, 0)),
            pl.BlockSpec((seq_len, LANES), lambda b: (0, 0)),
        ]
        args += [cache[0], cache[1], rope[0], rope[1]]
        out_shape, out_specs = y_shape, y_spec
    else:
        ckv_shape, ckv_spec = _layer_stacked((seq_len, MLA_KV_RANK), n_batch, layer, nb)
        kr_shape, kr_spec = _layer_stacked((seq_len, MLA_ROPE), n_batch, layer, nb)
        out_shape = [y_shape, ckv_shape, kr_shape]
        out_specs = [y_spec, ckv_spec, kr_spec]
        aliases = _carry_specs(in_specs, args, carried, 1)
    keys = past + seq_len
    return pl.pallas_call(
        kern, out_shape=out_shape, grid=(n_batch // nb,), in_specs=in_specs, out_specs=out_specs,
        input_output_aliases=aliases,
        scratch_shapes=[
            pltpu.VMEM((nb, seq_len, q_w), F32),
            pltpu.VMEM((nb, keys, kv_w + LANES), BF16),
            pltpu.VMEM((nb, keys, kv_w), BF16),
        ],
        compiler_params=_params(1),
        name=f"mla_{seq_len}",
    )(*args)


def _dft_matrix(seq_len):
    k = np.arange(seq_len, dtype=np.float64)[:, None]
    j = np.arange(seq_len, dtype=np.float64)[None, :]
    ang = np.pi * k * j / seq_len
    re = np.cos(ang)
    im = -np.sin(ang)
    im[0, :] = np.where(np.arange(seq_len) % 2 == 0, 1.0, -1.0)
    return np.concatenate([re, im], axis=0).astype(np.float32)


def _hyena_positions(seq_len):
    t = np.arange(seq_len, dtype=np.float64) / seq_len
    bands = np.arange(1, HY_BANDS + 1, dtype=np.float64)
    ang = 2.0 * math.pi * t[:, None] * bands
    z = np.concatenate([t[:, None], np.cos(ang), np.sin(ang)], axis=-1)
    return np.pad(z, ((0, 0), (0, HY_POS_PAD - HY_POS_DIM))).astype(np.float32)


def _rope_tables(n_tokens, dim):
    rows = np.repeat(np.arange(n_tokens // GRID_W, dtype=np.float64), GRID_W)
    cols = np.tile(np.arange(GRID_W, dtype=np.float64), n_tokens // GRID_W)
    quarter = dim // 4
    inv = ROPE_BASE ** (-np.arange(quarter, dtype=np.float64) / quarter)
    ang = np.concatenate([rows[:, None] * inv, cols[:, None] * inv], axis=-1)
    cos = np.concatenate([np.cos(ang), np.cos(ang)], axis=-1)
    sin = np.concatenate([-np.sin(ang), np.sin(ang)], axis=-1)
    reps = LANES // dim
    return (np.tile(cos, (1, reps)).astype(np.float32), np.tile(sin, (1, reps)).astype(np.float32))


def _arrange_w_in(w_in):
    g0 = RET_IN + HY_IN
    m0 = g0 + GQA_IN
    q = [w_in[..., g0 + j * HEAD_DIM:g0 + (j + 1) * HEAD_DIM] for j in GQA_HEAD_ORDER]
    kv = w_in[..., g0 + GQA_Q_HEADS * HEAD_DIM:m0]
    lat = w_in[..., m0:m0 + MLA_Q_RANK + MLA_KV_RANK]
    kr = w_in[..., m0 + MLA_Q_RANK + MLA_KV_RANK:m0 + MLA_IN]
    return jnp.concatenate([w_in[..., :g0], *q, kv, lat] + [kr] * MLA_HEADS, axis=-1).astype(BF16)


def _arrange_w_out(w_out):
    g0 = 2 * MIX_PART
    rows = [w_out[:, g0 + j * HEAD_DIM:g0 + (j + 1) * HEAD_DIM] for j in GQA_HEAD_ORDER]
    return jnp.concatenate([w_out[:, :g0], *rows, w_out[:, g0 + MIX_PART:]], axis=1).astype(BF16)


def kernel(x_prompt, x_sample, state_ret, cache_gqa_k, cache_gqa_v, cache_mla_ckv, cache_mla_krope, c, c_ctx, ada_w, ada_b, norm_g, w_in, ret_decay_logit, ret_gn_g, hy_short_w, hy_short_b, hy_w1, hy_b1, hy_w2, hy_b2, hy_w3, hy_decay, hy_bias, gqa_sink, mla_q_norm, mla_kv_norm, mla_w_uq, mla_w_uk, mla_w_uv, w_out, ffn_w_up, ffn_conv_w, ffn_conv_b, ffn_w_down):
    n_p, len_p, _ = x_prompt.shape
    n_s, len_s, _ = x_sample.shape

    w_in_b = _arrange_w_in(w_in)
    w_out_b = _arrange_w_out(w_out)
    w_up_b = ffn_w_up.astype(BF16)
    w_down_b = ffn_w_down.astype(BF16)
    uq = mla_w_uq.reshape(DEPTH, MLA_Q_RANK, MLA_HEADS, MLA_NOPE + MLA_ROPE)
    w_uq_b = jnp.concatenate(
        [uq[..., :MLA_NOPE].reshape(DEPTH, MLA_Q_RANK, MLA_HEADS * MLA_NOPE),
         uq[..., MLA_NOPE:].reshape(DEPTH, MLA_Q_RANK, MLA_HEADS * MLA_ROPE)], axis=-1).astype(BF16)
    w_uk_b = mla_w_uk.astype(BF16)
    w_uv_b = mla_w_uv.astype(BF16)

    cond = jnp.concatenate([c_ctx[None], c, jnp.zeros((N_COND - 1 - n_s, D_MODEL), F32)], axis=0)
    mod = _modulation(cond, ada_w, ada_b).reshape(DEPTH, N_COND, 6, D_MODEL)

    w1p = jnp.pad(hy_w1, ((0, 0), (0, HY_POS_PAD - HY_POS_DIM), (0, 0)))
    b1 = hy_b1.reshape(DEPTH, 1, HY_FILTER_HIDDEN)
    b2 = hy_b2.reshape(DEPTH, 1, HY_FILTER_HIDDEN)
    groups = {}
    for seq_len in (len_p, len_s):
        f32mat = _dft_matrix(seq_len)
        fmat = jnp.asarray(f32mat).astype(BF16)
        fmat_t = jnp.asarray(np.ascontiguousarray(f32mat.T)).astype(BF16)
        tables = _hyena_tables(seq_len, jnp.asarray(_hyena_positions(seq_len)), w1p, b1, hy_w2, b2,
                               hy_w3, hy_decay, fmat)
        groups[seq_len] = (tables, fmat, fmat_t)

    rope_g = tuple(jnp.asarray(t) for t in _rope_tables(len_s, HEAD_DIM))
    rope_m = tuple(jnp.asarray(t) for t in _rope_tables(len_s, MLA_ROPE))

    gn_g = ret_gn_g.reshape(DEPTH, 1, RET_WIDTH)
    short_b = hy_short_b.reshape(DEPTH, 1, HY_IN)
    sink = gqa_sink.reshape(DEPTH, 1, GQA_Q_HEADS)
    q_norm = mla_q_norm.reshape(DEPTH, 1, MLA_Q_RANK)
    kv_norm = mla_kv_norm.reshape(DEPTH, 1, MLA_KV_RANK)
    conv_b = ffn_conv_b.reshape(DEPTH, 1, D_FF)

    xp = x_prompt.reshape(n_p * len_p, D_MODEL)
    xs = x_sample.reshape(n_s * len_s, D_MODEL)
    tm_p = 512
    tm_s = 512
    nb_p = 2
    cond_p = lambda i: 0
    cond_s = lambda i: 1 + (i * tm_s) // len_s
    tm_ffn = 1024
    cond_s_ffn = lambda i: 1 + (i * tm_ffn) // len_s

    st = [jnp.zeros((n_p, DEPTH, 2, RET_HEADS, HEAD_DIM, HEAD_DIM), F32)]
    gkv = [jnp.zeros((n_p, DEPTH, GQA_KV_HEADS, len_p, HEAD_DIM), F32) for _ in range(2)]
    ckr = [jnp.zeros((n_p, DEPTH, len_p, MLA_KV_RANK), F32), jnp.zeros((n_p, DEPTH, len_p, MLA_ROPE), F32)]
    for l in range(DEPTH):
        tables, fmat, fmat_t = groups[len_p]
        ret_in, hy_in, gqa_in, mla_in = _in_proj(xp, mod, norm_g, w_in_b, l, tm_p, cond_p)
        y_ret, *st = _retention(ret_in, ret_decay_logit, gn_g, None, st, l, n_p, len_p, nb_p)
        y_hy = _hyena(hy_in, hy_short_w, short_b, hy_bias, tables, fmat, fmat_t, l, n_p, len_p)
        y_gqa, *gkv = _gqa_ctx(gqa_in, sink, gkv, l, n_p, len_p, nb_p)
        y_mla, *ckr = _mla(mla_in, q_norm, kv_norm, w_uq_b, w_uk_b, w_uv_b, None, None, ckr, l, n_p, len_p,
                           nb_p)
        xp = _channel_mix(xp, (y_ret, y_hy, y_gqa, y_mla), mod, norm_g, w_out_b, w_up_b, ffn_conv_w, conv_b,
                          w_down_b, l, tm_ffn, len_p, cond_p)
        tables, fmat, fmat_t = groups[len_s]
        ret_in, hy_in, gqa_in, mla_in = _in_proj(xs, mod, norm_g, w_in_b, l, tm_s, cond_s)
        y_ret = _retention(ret_in, ret_decay_logit, gn_g, state_ret, None, l, n_s, len_s, 1)
        y_hy = _hyena(hy_in, hy_short_w, short_b, hy_bias, tables, fmat, fmat_t, l, n_s, len_s)
        y_gqa = _gqa_win(gqa_in, sink, cache_gqa_k, cache_gqa_v, rope_g[0], rope_g[1], l, n_s, len_s)
        y_mla = _mla(mla_in, q_norm, kv_norm, w_uq_b, w_uk_b, w_uv_b, (cache_mla_ckv, cache_mla_krope),
                     rope_m, None, l, n_s, len_s, 1)
        xs = _channel_mix(xs, (y_ret, y_hy, y_gqa, y_mla), mod, norm_g, w_out_b, w_up_b, ffn_conv_w, conv_b,
                          w_down_b, l, tm_ffn, len_s, cond_s_ffn)

    return (xp.reshape(n_p, len_p, D_MODEL), xs.reshape(n_s, len_s, D_MODEL),
            st[0], gkv[0], gkv[1], ckr[0], ckr[1])
```

```python
import functools
import math

import numpy as np
import jax
import jax.numpy as jnp
from jax import lax
from jax.experimental import pallas as pl
from jax.experimental.pallas import tpu as pltpu

F32 = jnp.float32
BF16 = jnp.bfloat16

D_MODEL = 1024
DEPTH = 4
GRID_W = 64
EPS = 1e-6
NEG_INF = -1e30
ROPE_BASE = 10000.0
BLOCK = 128
WINDOW = 128
HEAD_DIM = 64
LANES = 128
RET_HEADS = 4
RET_WIDTH = RET_HEADS * HEAD_DIM
HY_WIDTH = 256
HY_ORDER = 2
HY_BANDS = 8
HY_POS_DIM = 1 + 2 * HY_BANDS
HY_POS_PAD = 32
HY_FILTER_HIDDEN = 64
GQA_Q_HEADS = 4
GQA_KV_HEADS = 2
GQA_GROUPS = GQA_Q_HEADS // GQA_KV_HEADS
MLA_HEADS = 4
MLA_Q_RANK = 256
MLA_KV_RANK = 128
MLA_NOPE = 64
MLA_ROPE = 32
MLA_V = 64
D_FF = 2816
RET_IN = 4 * RET_WIDTH
HY_IN = (HY_ORDER + 1) * HY_WIDTH
GQA_IN = (GQA_Q_HEADS + 2 * GQA_KV_HEADS) * HEAD_DIM
MLA_IN = MLA_Q_RANK + MLA_KV_RANK + MLA_ROPE
MLA_IN_TILED = MLA_Q_RANK + MLA_KV_RANK + MLA_HEADS * MLA_ROPE
IN_WIDTH = RET_IN + HY_IN + GQA_IN + MLA_IN
IN_WIDTH_TILED = RET_IN + HY_IN + GQA_IN + MLA_IN_TILED
GQA_HEAD_ORDER = (0, 2, 1, 3)
MIX_PART = 256
N_COND = 8
FFN_CHUNK = 256
N_FFN_CHUNKS = D_FF // FFN_CHUNK

VMEM_LIMIT = 56 * 1024 * 1024


def _params(n_axes=1):
    return pltpu.CompilerParams(
        dimension_semantics=("arbitrary",) * n_axes, vmem_limit_bytes=VMEM_LIMIT)


def _dot(a, b):
    return jnp.dot(a, b, preferred_element_type=F32)


def _dot_nt(a, b):
    return lax.dot_general(a, b, (((1,), (1,)), ((), ())), preferred_element_type=F32)


def _dot_tn(a, b):
    return lax.dot_general(a, b, (((0,), (0,)), ((), ())), preferred_element_type=F32)


def _rms(x, g):
    return x * lax.rsqrt(jnp.mean(x * x, axis=-1, keepdims=True) + EPS) * g


def _sigmoid(x):
    return 1.0 / (1.0 + jnp.exp(-x))


def _silu(x):
    return x * _sigmoid(x)


def _shift_rows(x, seq_len):
    n = x.shape[0]
    assert seq_len & (seq_len - 1) == 0
    pos = lax.broadcasted_iota(jnp.int32, (n, 1), 0) & (seq_len - 1)
    prev = jnp.where(pos != 0, pltpu.roll(x, 1, axis=0), 0.0)
    nxt = jnp.where(pos != seq_len - 1, pltpu.roll(x, n - 1, axis=0), 0.0)
    return prev, nxt


def _rot_half(x, half):
    lane = lax.broadcasted_iota(jnp.int32, x.shape, 1) & (2 * half - 1)
    return jnp.where(lane < half, pltpu.roll(x, LANES - half, axis=1), pltpu.roll(x, half, axis=1))


def _lane_iota():
    return lax.broadcasted_iota(jnp.int32, (1, LANES), 1)


def _split_heads(t, lo):
    return jnp.concatenate([jnp.where(lo, t, 0.0), jnp.where(lo, 0.0, t)], axis=0)


def _merge_heads(o, lo):
    n = o.shape[0] // 2
    return jnp.where(lo, o[:n], o[n:])


def _per_head(vals, width):
    lane = lax.broadcasted_iota(jnp.int32, (1, width), 1)
    out = vals[-1]
    for h in range(len(vals) - 2, -1, -1):
        out = jnp.where(lane < (h + 1) * HEAD_DIM, vals[h], out)
    return out


def _mod_kernel(cond_ref, w_ref, b_ref, out_ref):
    s = _silu(cond_ref[...]).astype(BF16)
    out_ref[0] = _dot(s, w_ref[0].astype(BF16)) + b_ref[0]


def _modulation(cond, ada_w, ada_b):
    tn = 1536
    return pl.pallas_call(
        _mod_kernel,
        out_shape=jax.ShapeDtypeStruct((DEPTH, N_COND, 6 * D_MODEL), F32),
        grid=(DEPTH, 6 * D_MODEL // tn),
        in_specs=[
            pl.BlockSpec((N_COND, D_MODEL), lambda l, j: (0, 0)),
            pl.BlockSpec((1, D_MODEL, tn), lambda l, j: (l, 0, j)),
            pl.BlockSpec((1, 1, tn), lambda l, j: (l, 0, j)),
        ],
        out_specs=pl.BlockSpec((1, N_COND, tn), lambda l, j: (l, 0, j)),
        compiler_params=_params(2),
        name="modulation",
    )(cond, ada_w, ada_b.reshape(DEPTH, 1, 6 * D_MODEL))


def _filter_kernel(z_ref, w1_ref, b1_ref, w2_ref, b2_ref, w3_ref, dec_ref, f_ref, out_ref, *, seq_len):
    hi = lax.Precision.HIGHEST
    h = jnp.sin(jnp.dot(z_ref[...], w1_ref[0], precision=hi, preferred_element_type=F32) + b1_ref[0])
    h = jnp.sin(jnp.dot(h, w2_ref[0], precision=hi, preferred_element_type=F32) + b2_ref[0])
    h = jnp.dot(h, w3_ref[0], precision=hi, preferred_element_type=F32)
    row = lax.broadcasted_iota(jnp.int32, (seq_len, 1), 0)
    t = row.astype(F32) / seq_len
    dec = jnp.abs(dec_ref[0])
    win_f = jnp.exp(-t * dec[0:1, :])
    win_b = jnp.exp(-t * dec[1:2, :])
    fm = f_ref[...]
    inv = 1.0 / seq_len
    scale = jnp.where(row == 0, 0.5 * inv, inv)
    for o in range(HY_ORDER):
        base = o * 2 * HY_WIDTH
        hf = h[:, base:base + HY_WIDTH] * win_f
        hb = h[:, base + HY_WIDTH:base + 2 * HY_WIDTH] * win_b
        hbs = jnp.where(row != 0, pltpu.roll(hb, 1, axis=0), 0.0)
        sf = _dot(fm, hf.astype(BF16))
        sb = _dot(fm, hbs.astype(BF16))
        g_re = sf[:seq_len] + sb[:seq_len]
        g_im = sf[seq_len:] - sb[seq_len:]
        g_ny = sf[seq_len:] + sb[seq_len:]
        a = g_re * scale
        out_ref[0, 3 * o] = a
        out_ref[0, 3 * o + 1] = jnp.where(row == 0, g_ny * (0.5 * inv), a)
        out_ref[0, 3 * o + 2] = jnp.where(row == 0, 0.0, g_im * inv)


def _hyena_tables(seq_len, z, w1p, b1, w2, b2, w3, decay, fmat):
    hid = HY_FILTER_HIDDEN
    wide = HY_ORDER * 2 * HY_WIDTH
    return pl.pallas_call(
        functools.partial(_filter_kernel, seq_len=seq_len),
        out_shape=jax.ShapeDtypeStruct((DEPTH, 3 * HY_ORDER, seq_len, HY_WIDTH), F32),
        grid=(DEPTH,),
        in_specs=[
            pl.BlockSpec((seq_len, HY_POS_PAD), lambda l: (0, 0)),
            pl.BlockSpec((1, HY_POS_PAD, hid), lambda l: (l, 0, 0)),
            pl.BlockSpec((1, 1, hid), lambda l: (l, 0, 0)),
            pl.BlockSpec((1, hid, hid), lambda l: (l, 0, 0)),
            pl.BlockSpec((1, 1, hid), lambda l: (l, 0, 0)),
            pl.BlockSpec((1, hid, wide), lambda l: (l, 0, 0)),
            pl.BlockSpec((1, 2, HY_WIDTH), lambda l: (l, 0, 0)),
            pl.BlockSpec((2 * seq_len, seq_len), lambda l: (0, 0)),
        ],
        out_specs=pl.BlockSpec((1, 3 * HY_ORDER, seq_len, HY_WIDTH), lambda l: (l, 0, 0, 0)),
        compiler_params=_params(1),
        name=f"hyena_tables_{seq_len}",
    )(z, w1p, b1, w2, b2, w3, decay, fmat)


def _layer_stacked(tail, n_batch, layer, nb):
    zeros = (0,) * len(tail)
    n_layers = DEPTH if layer == 0 else 1
    return (jax.ShapeDtypeStruct((n_batch, DEPTH) + tail, F32),
            pl.BlockSpec((nb, n_layers) + tail, lambda b: (b, layer) + zeros))


def _zero_other_layers(*out_refs):
    for ref in out_refs:
        if ref.shape[1] > 1:
            ref[:, 1:] = jnp.zeros((ref.shape[0], ref.shape[1] - 1) + ref.shape[2:], ref.dtype)


def _carry_specs(in_specs, args, carried, first_out):
    aliases = {}
    if carried is not None:
        for k, arr in enumerate(carried):
            aliases[len(args)] = first_out + k
            in_specs.append(pl.BlockSpec(memory_space=pl.ANY))
            args.append(arr)
    return aliases


def _in_proj_kernel(x_ref, mod_ref, g_ref, w_ref, ret_ref, hy_ref, gqa_ref, mla_ref, w_scr):
    @pl.when(pl.program_id(0) == 0)
    def _():
        g0 = RET_IN + HY_IN
        m0 = g0 + GQA_IN
        kr0 = m0 + MLA_Q_RANK + MLA_KV_RANK
        for c0 in list(range(0, g0, LANES)) + list(range(g0 + GQA_Q_HEADS * HEAD_DIM, kr0, LANES)):
            w_scr[:, c0:c0 + LANES] = w_ref[0, :, c0:c0 + LANES].astype(BF16)
        for dst, j in enumerate(GQA_HEAD_ORDER):
            w_scr[:, g0 + dst * HEAD_DIM:g0 + (dst + 1) * HEAD_DIM] = (
                w_ref[0, :, g0 + j * HEAD_DIM:g0 + (j + 1) * HEAD_DIM].astype(BF16))
        kr = w_ref[0, :, kr0:kr0 + MLA_ROPE]
        w_scr[:, kr0:kr0 + LANES] = jnp.concatenate([kr] * MLA_HEADS, axis=1).astype(BF16)

    shift = mod_ref[0, 0, 0:1, :]
    scale = mod_ref[0, 0, 1:2, :]
    h = (_rms(x_ref[...], g_ref[0, 0:1, :]) * (1.0 + scale) + shift).astype(BF16)
    c0 = 0
    for ref in (ret_ref, hy_ref, gqa_ref, mla_ref):
        width = ref.shape[1]
        ref[...] = _dot(h, w_scr[:, c0:c0 + width])
        c0 += width


def _in_proj(x, mod, norm_g, w_in, layer, tm, cond_of_tile):
    rows = x.shape[0]
    widths = (RET_IN, HY_IN, GQA_IN, MLA_IN_TILED)
    return pl.pallas_call(
        _in_proj_kernel,
        out_shape=[jax.ShapeDtypeStruct((rows, w), F32) for w in widths],
        grid=(rows // tm,),
        in_specs=[
            pl.BlockSpec((tm, D_MODEL), lambda i: (i, 0)),
            pl.BlockSpec((1, 1, 6, D_MODEL), lambda i: (layer, cond_of_tile(i), 0, 0)),
            pl.BlockSpec((1, 4, D_MODEL), lambda i: (layer, 0, 0)),
            pl.BlockSpec((1, D_MODEL, IN_WIDTH), lambda i: (layer, 0, 0), pipeline_mode=pl.Buffered(1)),
        ],
        out_specs=[pl.BlockSpec((tm, w), lambda i: (i, 0)) for w in widths],
        scratch_shapes=[pltpu.VMEM((D_MODEL, IN_WIDTH_TILED), BF16)],
        compiler_params=_params(1),
        name="in_proj",
    )(x, mod, norm_g, w_in)


def _channel_kernel(x_ref, m0_ref, m1_ref, m2_ref, m3_ref, mod_ref, g_ref, wo_ref, wu_ref, cw_ref, cb_ref,
                    wd_ref, out_ref, act_ref, *, seq_len):
    y = None
    for i, m_ref in enumerate((m0_ref, m1_ref, m2_ref, m3_ref)):
        r0 = i * MIX_PART
        if i == 2:
            w = jnp.concatenate([wo_ref[0, r0 + j * HEAD_DIM:r0 + (j + 1) * HEAD_DIM, :]
                                 for j in GQA_HEAD_ORDER], axis=0)
        else:
            w = wo_ref[0, r0:r0 + MIX_PART, :]
        part = _dot(m_ref[...], w)
        y = part if y is None else y + part
    gate1 = mod_ref[0, 0, 2:3, :]
    shift2 = mod_ref[0, 0, 3:4, :]
    scale2 = mod_ref[0, 0, 4:5, :]
    xm = x_ref[...] + gate1 * _rms(y, g_ref[0, 1:2, :])
    out_ref[...] = xm
    h2 = (_rms(xm, g_ref[0, 2:3, :]) * (1.0 + scale2) + shift2).astype(BF16)
    for c in range(N_FFN_CHUNKS):
        sl = slice(c * FFN_CHUNK, (c + 1) * FFN_CHUNK)
        gate = _dot(h2, wu_ref[0, :, sl])
        up = _dot(h2, wu_ref[0, :, D_FF + c * FFN_CHUNK:D_FF + (c + 1) * FFN_CHUNK])
        prev, nxt = _shift_rows(gate, seq_len)
        gate = (prev * cw_ref[0, 0:1, sl] + gate * cw_ref[0, 1:2, sl] + nxt * cw_ref[0, 2:3, sl]
                + cb_ref[0, :, sl])
        act_ref[:, sl] = (_silu(gate) * up).astype(BF16)
    ffn = _dot(act_ref[...], wd_ref[0])
    gate2 = mod_ref[0, 0, 5:6, :]
    out_ref[...] = out_ref[...] + gate2 * _rms(ffn, g_ref[0, 3:4, :])


def _channel_mix(x, mixes, mod, norm_g, w_out, w_up, conv_w, conv_b, w_down, layer, tm, seq_len, cond_of_tile):
    rows = x.shape[0]
    resident = dict(pipeline_mode=pl.Buffered(1))
    return pl.pallas_call(
        functools.partial(_channel_kernel, seq_len=seq_len),
        out_shape=jax.ShapeDtypeStruct((rows, D_MODEL), F32),
        grid=(rows // tm,),
        in_specs=[pl.BlockSpec((tm, D_MODEL), lambda i: (i, 0))]
        + [pl.BlockSpec((tm, MIX_PART), lambda i: (i, 0))] * 4
        + [
            pl.BlockSpec((1, 1, 6, D_MODEL), lambda i: (layer, cond_of_tile(i), 0, 0)),
            pl.BlockSpec((1, 4, D_MODEL), lambda i: (layer, 0, 0)),
            pl.BlockSpec((1, D_MODEL, D_MODEL), lambda i: (layer, 0, 0), **resident),
            pl.BlockSpec((1, D_MODEL, 2 * D_FF), lambda i: (layer, 0, 0), **resident),
            pl.BlockSpec((1, 3, D_FF), lambda i: (layer, 0, 0)),
            pl.BlockSpec((1, 1, D_FF), lambda i: (layer, 0, 0)),
            pl.BlockSpec((1, D_FF, D_MODEL), lambda i: (layer, 0, 0), **resident),
        ],
        out_specs=pl.BlockSpec((tm, D_MODEL), lambda i: (i, 0)),
        scratch_shapes=[pltpu.VMEM((tm, D_FF), BF16)],
        compiler_params=_params(1),
        name="channel_mix",
    )(x, *mixes, mod, norm_g, w_out, w_up, conv_w, conv_b, w_down)


def _log_gamma(dl_ref):
    dl = dl_ref[0]
    return jnp.minimum(dl, 0.0) - jnp.log(1.0 + jnp.exp(-jnp.abs(dl)))


def _pair_decay(lg, t, r0, bq, seq_len):
    top = lax.broadcasted_iota(jnp.int32, (2 * bq, 1), 0) < bq
    lgf = jnp.where(top, lg[0:1, 2 * t:2 * t + 1], lg[0:1, 2 * t + 1:2 * t + 2])
    lgb = jnp.where(top, lg[1:2, 2 * t:2 * t + 1], lg[1:2, 2 * t + 1:2 * t + 2])
    rowf = ((lax.broadcasted_iota(jnp.int32, (2 * bq, 1), 0) & (bq - 1)) + r0).astype(F32)
    colf = lax.broadcasted_iota(jnp.int32, (1, seq_len), 1).astype(F32)
    lag = rowf - colf
    decay = jnp.exp(jnp.where(lag >= 0.0, lag * lgf, -lag * lgb))
    return jnp.where(lag == 0.0, 2.0, decay)


def _head_norm(o, lo):
    inv = 1.0 / HEAD_DIM
    s_all = jnp.sum(o, axis=-1, keepdims=True)
    s_lo = jnp.sum(jnp.where(lo, o, 0.0), axis=-1, keepdims=True)
    d = o - jnp.where(lo, s_lo, s_all - s_lo) * inv
    d2 = d * d
    v_all = jnp.sum(d2, axis=-1, keepdims=True)
    v_lo = jnp.sum(jnp.where(lo, d2, 0.0), axis=-1, keepdims=True)
    return d * lax.rsqrt(jnp.where(lo, v_lo, v_all - v_lo) * inv + EPS)


def _ret_ctx_kernel(*refs, seq_len, nb):
    x_ref, dl_ref, gn_ref, *_, y_ref, st_ref, dec_scr, kdec_scr = refs
    lo = _lane_iota() < HEAD_DIM
    _zero_other_layers(st_ref)

    @pl.when(pl.program_id(0) == 0)
    def _():
        lg = _log_gamma(dl_ref)
        posf = lax.broadcasted_iota(jnp.int32, (seq_len, 1), 0).astype(F32)
        lgf = _per_head([lg[0:1, h:h + 1] for h in range(RET_HEADS)], RET_WIDTH)
        lgb = _per_head([lg[1:2, h:h + 1] for h in range(RET_HEADS)], RET_WIDTH)
        kdec_scr[0] = jnp.exp((seq_len - 1.0 - posf) * lgf)
        kdec_scr[1] = jnp.exp(posf * lgb)
        for t in range(RET_HEADS // 2):
            dec_scr[t] = _pair_decay(lg, t, 0, seq_len, seq_len)

    for i in range(nb):
        rows = slice(i * seq_len, (i + 1) * seq_len)
        for t in range(RET_HEADS // 2):
            cs = slice(t * LANES, (t + 1) * LANES)
            q = x_ref[rows, cs]
            k = x_ref[rows, RET_WIDTH + t * LANES:RET_WIDTH + (t + 1) * LANES] * (HEAD_DIM ** -0.5)
            vb = x_ref[rows, 2 * RET_WIDTH + t * LANES:2 * RET_WIDTH + (t + 1) * LANES].astype(BF16)
            gate = x_ref[rows, 3 * RET_WIDTH + t * LANES:3 * RET_WIDTH + (t + 1) * LANES]
            s = _dot_nt(_split_heads(q, lo).astype(BF16), k.astype(BF16)) * dec_scr[t]
            o = _merge_heads(_dot(s.astype(BF16), vb), lo)
            y = _silu(gate) * (_head_norm(o, lo) * gn_ref[0, :, cs])
            y_ref[rows, cs] = y.astype(BF16)
            for d in range(2):
                st = _dot_tn((k * kdec_scr[d, :, cs]).astype(BF16), vb)
                st_ref[i, 0, d, 2 * t] = st[0:HEAD_DIM, 0:HEAD_DIM]
                st_ref[i, 0, d, 2 * t + 1] = st[HEAD_DIM:LANES, HEAD_DIM:LANES]


def _ret_lat_kernel(x_ref, dl_ref, gn_ref, s0_ref, y_ref, *, seq_len, bq):
    lo = _lane_iota() < HEAD_DIM
    lg = _log_gamma(dl_ref)
    zero = jnp.zeros((HEAD_DIM, HEAD_DIM), F32)
    for t in range(RET_HEADS // 2):
        cs = slice(t * LANES, (t + 1) * LANES)
        kb = (x_ref[:, RET_WIDTH + t * LANES:RET_WIDTH + (t + 1) * LANES] * (HEAD_DIM ** -0.5)).astype(BF16)
        vb = x_ref[:, 2 * RET_WIDTH + t * LANES:2 * RET_WIDTH + (t + 1) * LANES].astype(BF16)
        lgf = _per_head([lg[0:1, 2 * t:2 * t + 1], lg[0:1, 2 * t + 1:2 * t + 2]], LANES)
        lgb = _per_head([lg[1:2, 2 * t:2 * t + 1], lg[1:2, 2 * t + 1:2 * t + 2]], LANES)
        s0 = []
        for d in range(2):
            a = s0_ref[0, 0, d, 2 * t]
            b = s0_ref[0, 0, d, 2 * t + 1]
            s0.append(jnp.concatenate([jnp.concatenate([a, zero], axis=1),
                                       jnp.concatenate([zero, b], axis=1)], axis=0).astype(BF16))
        for r0 in range(0, seq_len, bq):
            q = x_ref[r0:r0 + bq, cs]
            s = _dot_nt(_split_heads(q, lo).astype(BF16), kb) * _pair_decay(lg, t, r0, bq, seq_len)
            o = _merge_heads(_dot(s.astype(BF16), vb), lo)
            rowf = (lax.broadcasted_iota(jnp.int32, (bq, 1), 0) + r0).astype(F32)
            qf = (q * jnp.exp((rowf + 1.0) * lgf)).astype(BF16)
            qr = (q * jnp.exp((seq_len - rowf) * lgb)).astype(BF16)
            o = o + _dot(qf, s0[0]) + _dot(qr, s0[1])
            gate = x_ref[r0:r0 + bq, 3 * RET_WIDTH + t * LANES:3 * RET_WIDTH + (t + 1) * LANES]
            y = _silu(gate) * (_head_norm(o, lo) * gn_ref[0, :, cs])
            y_ref[r0:r0 + bq, cs] = y.astype(BF16)


def _retention(ret_in, decay_logit, gn_g, state0, carried, layer, n_batch, seq_len, nb):
    latent = state0 is not None
    in_specs = [
        pl.BlockSpec((nb * seq_len, RET_IN), lambda b: (b, 0)),
        pl.BlockSpec((1, 2, RET_HEADS), lambda b: (layer, 0, 0)),
        pl.BlockSpec((1, 1, RET_WIDTH), lambda b: (layer, 0, 0)),
    ]
    args = [ret_in, decay_logit, gn_g]
    y_shape = jax.ShapeDtypeStruct((n_batch * seq_len, MIX_PART), BF16)
    y_spec = pl.BlockSpec((nb * seq_len, MIX_PART), lambda b: (b, 0))
    if latent:
        assert nb == 1
        kern = functools.partial(_ret_lat_kernel, seq_len=seq_len, bq=256)
        in_specs.append(pl.BlockSpec((1, 1, 2, RET_HEADS, HEAD_DIM, HEAD_DIM),
                                     lambda b: (b, layer, 0, 0, 0, 0)))
        args.append(state0)
        out_shape, out_specs, aliases, scratch = y_shape, y_spec, {}, []
    else:
        kern = functools.partial(_ret_ctx_kernel, seq_len=seq_len, nb=nb)
        st_shape, st_spec = _layer_stacked((2, RET_HEADS, HEAD_DIM, HEAD_DIM), n_batch, layer, nb)
        out_shape = [y_shape, st_shape]
        out_specs = [y_spec, st_spec]
        aliases = _carry_specs(in_specs, args, carried, 1)
        scratch = [pltpu.VMEM((RET_HEADS // 2, 2 * seq_len, seq_len), F32),
                   pltpu.VMEM((2, seq_len, RET_WIDTH), F32)]
    return pl.pallas_call(
        kern, out_shape=out_shape, grid=(n_batch // nb,), in_specs=in_specs, out_specs=out_specs,
        input_output_aliases=aliases, scratch_shapes=scratch,
        compiler_params=_params(1), name=f"retention_{seq_len}",
    )(*args)


def _hyena_kernel(x_ref, sw_ref, sb_ref, bias_ref, tab_ref, f_ref, ft_ref, y_ref, *, seq_len, nb):
    for i in range(nb):
        rows = slice(i * seq_len, (i + 1) * seq_len)
        x = x_ref[rows, :]
        prev, nxt = _shift_rows(x, seq_len)
        u = prev * sw_ref[0, 0:1, :] + x * sw_ref[0, 1:2, :] + nxt * sw_ref[0, 2:3, :] + sb_ref[0]
        x1 = u[:, 0:HY_WIDTH]
        x2 = u[:, HY_WIDTH:2 * HY_WIDTH]
        z = u[:, 2 * HY_WIDTH:3 * HY_WIDTH]
        for o, gate in enumerate((x1, x2)):
            spec = _dot(f_ref[...], z.astype(BF16))
            s_re = spec[:seq_len]
            s_im = spec[seq_len:]
            a = tab_ref[0, 3 * o]
            a_ny = tab_ref[0, 3 * o + 1]
            b = tab_ref[0, 3 * o + 2]
            y_re = (s_re * a - s_im * b).astype(BF16)
            y_im = (s_re * b + s_im * a_ny).astype(BF16)
            conv = _dot(ft_ref[:, 0:seq_len], y_re) + _dot(ft_ref[:, seq_len:2 * seq_len], y_im)
            z = gate * (conv + z * bias_ref[0, o:o + 1, :])
        y_ref[rows, :] = z.astype(BF16)


def _hyena(hy_in, short_w, short_b, hy_bias, tables, fmat, fmat_t, layer, n_batch, seq_len, nb):
    const = dict(pipeline_mode=pl.Buffered(1))
    return pl.pallas_call(
        functools.partial(_hyena_kernel, seq_len=seq_len, nb=nb),
        out_shape=jax.ShapeDtypeStruct((n_batch * seq_len, MIX_PART), BF16),
        grid=(n_batch // nb,),
        in_specs=[
            pl.BlockSpec((nb * seq_len, HY_IN), lambda b: (b, 0)),
            pl.BlockSpec((1, 3, HY_IN), lambda b: (layer, 0, 0)),
            pl.BlockSpec((1, 1, HY_IN), lambda b: (layer, 0, 0)),
            pl.BlockSpec((1, HY_ORDER, HY_WIDTH), lambda b: (layer, 0, 0)),
            pl.BlockSpec((1, 3 * HY_ORDER, seq_len, HY_WIDTH), lambda b: (layer, 0, 0, 0), **const),
            pl.BlockSpec((2 * seq_len, seq_len), lambda b: (0, 0), **const),
            pl.BlockSpec((seq_len, 2 * seq_len), lambda b: (0, 0), **const),
        ],
        out_specs=pl.BlockSpec((nb * seq_len, MIX_PART), lambda b: (b, 0)),
        compiler_params=_params(1),
        name=f"hyena_{seq_len}",
    )(hy_in, short_w, short_b, hy_bias, tables, fmat, fmat_t)


def _pair_sink(sink_ref, g, n):
    top = lax.broadcasted_iota(jnp.int32, (2 * n, 1), 0) < n
    return jnp.where(top, sink_ref[0, :, g:g + 1], sink_ref[0, :, GQA_GROUPS + g:GQA_GROUPS + g + 1])


def _gqa_ctx_kernel(*refs, seq_len, nb):
    x_ref, sink_ref, *_, y_ref, k_out_ref, v_out_ref = refs
    nq = GQA_Q_HEADS * HEAD_DIM
    scale = HEAD_DIM ** -0.5
    lo = _lane_iota() < HEAD_DIM
    ones = jnp.ones((seq_len, LANES), BF16)
    _zero_other_layers(k_out_ref, v_out_ref)
    for i in range(nb):
        rows = slice(i * seq_len, (i + 1) * seq_len)
        k = x_ref[rows, nq:nq + LANES]
        v = x_ref[rows, nq + LANES:nq + 2 * LANES]
        for kv in range(GQA_KV_HEADS):
            k_out_ref[i, 0, kv] = k[:, kv * HEAD_DIM:(kv + 1) * HEAD_DIM]
            v_out_ref[i, 0, kv] = v[:, kv * HEAD_DIM:(kv + 1) * HEAD_DIM]
        kb = k.astype(BF16)
        v_ext = jnp.concatenate([v.astype(BF16), ones], axis=1)
        for g in range(GQA_GROUPS):
            cs = slice(g * LANES, (g + 1) * LANES)
            s = _dot_nt(_split_heads(x_ref[rows, cs], lo).astype(BF16), kb) * scale
            sink = _pair_sink(sink_ref, g, seq_len)
            m = jnp.maximum(jnp.max(s, axis=-1, keepdims=True), sink)
            pv = _dot(jnp.exp(s - m).astype(BF16), v_ext)
            den = pv[:, LANES:LANES + 1] + jnp.exp(sink - m)
            y_ref[rows, cs] = _merge_heads(pv[:, 0:LANES] / den, lo).astype(BF16)


def _gqa_ctx(gqa_in, sink, carried, layer, n_batch, seq_len, nb):
    kv_shape, kv_spec = _layer_stacked((GQA_KV_HEADS, seq_len, HEAD_DIM), n_batch, layer, nb)
    in_specs = [
        pl.BlockSpec((nb * seq_len, GQA_IN), lambda b: (b, 0)),
        pl.BlockSpec((1, 1, GQA_Q_HEADS), lambda b: (layer, 0, 0)),
    ]
    args = [gqa_in, sink]
    aliases = _carry_specs(in_specs, args, carried, 1)
    return pl.pallas_call(
        functools.partial(_gqa_ctx_kernel, seq_len=seq_len, nb=nb),
        out_shape=[jax.ShapeDtypeStruct((n_batch * seq_len, MIX_PART), BF16), kv_shape, kv_shape],
        grid=(n_batch // nb,),
        in_specs=in_specs,
        out_specs=[pl.BlockSpec((nb * seq_len, MIX_PART), lambda b: (b, 0)), kv_spec, kv_spec],
        input_output_aliases=aliases,
        compiler_params=_params(1),
        name=f"gqa_ctx_{seq_len}",
    )(*args)


def _gqa_win_kernel(x_ref, sink_ref, kc_ref, vc_ref, cos_ref, sin_ref, y_ref,
                    q_scr, k_scr, v_scr, kc_scr, vc_scr, *, seq_len):
    nq = GQA_Q_HEADS * HEAD_DIM
    scale = HEAD_DIM ** -0.5
    lo = _lane_iota() < HEAD_DIM
    past = kc_scr.shape[0]
    cos = cos_ref[...]
    sin = sin_ref[...]

    def rope(t):
        return t * cos + _rot_half(t, HEAD_DIM // 2) * sin

    for g in range(GQA_GROUPS):
        q = rope(x_ref[:, g * LANES:(g + 1) * LANES])
        q_scr[g, 0] = jnp.where(lo, q, 0.0).astype(BF16)
        q_scr[g, 1] = jnp.where(lo, 0.0, q).astype(BF16)
    zeros = jnp.zeros((BLOCK, LANES), BF16)
    for r0 in (0, BLOCK + seq_len):
        k_scr[r0:r0 + BLOCK, :] = zeros
        v_scr[r0:r0 + BLOCK, 0:LANES] = zeros
    k_scr[BLOCK:BLOCK + seq_len, :] = rope(x_ref[:, nq:nq + LANES]).astype(BF16)
    v_scr[BLOCK:BLOCK + seq_len, 0:LANES] = x_ref[:, nq + LANES:nq + 2 * LANES].astype(BF16)
    v_scr[:, LANES:2 * LANES] = jnp.ones((seq_len + 2 * BLOCK, LANES), BF16)
    kc_scr[...] = jnp.concatenate([kc_ref[0, 0, 0], kc_ref[0, 0, 1]], axis=1).astype(BF16)
    vc_scr[:, 0:LANES] = jnp.concatenate([vc_ref[0, 0, 0], vc_ref[0, 0, 1]], axis=1).astype(BF16)
    vc_scr[:, LANES:2 * LANES] = jnp.ones((past, LANES), BF16)

    ii = lax.broadcasted_iota(jnp.int32, (2 * BLOCK, 3 * BLOCK), 0) & (BLOCK - 1)
    jj = lax.broadcasted_iota(jnp.int32, (2 * BLOCK, 3 * BLOCK), 1)
    band = (jj >= ii) & (jj <= ii + 2 * WINDOW)

    def block(n, carry):
        r0 = pl.multiple_of(n * BLOCK, BLOCK)
        kpos = jj + (n - 1) * BLOCK
        valid = band & (kpos >= 0) & (kpos < seq_len)
        kw = k_scr[pl.ds(r0, 3 * BLOCK), :]
        vw = v_scr[pl.ds(r0, 3 * BLOCK), :]
        for g in range(GQA_GROUPS):
            q2 = jnp.concatenate([q_scr[g, 0, pl.ds(r0, BLOCK), :], q_scr[g, 1, pl.ds(r0, BLOCK), :]], axis=0)
            s_win = jnp.where(valid, _dot_nt(q2, kw) * scale, NEG_INF)
            s_ctx = _dot_nt(q2, kc_scr[...]) * scale
            sink = _pair_sink(sink_ref, g, BLOCK)
            m = jnp.maximum(jnp.maximum(jnp.max(s_win, axis=-1, keepdims=True),
                                        jnp.max(s_ctx, axis=-1, keepdims=True)), sink)
            pv = (_dot(jnp.exp(s_win - m).astype(BF16), vw)
                  + _dot(jnp.exp(s_ctx - m).astype(BF16), vc_scr[...]))
            den = pv[:, LANES:LANES + 1] + jnp.exp(sink - m)
            y_ref[pl.ds(r0, BLOCK), g * LANES:(g + 1) * LANES] = _merge_heads(
                pv[:, 0:LANES] / den, lo).astype(BF16)
        return carry

    lax.fori_loop(0, seq_len // BLOCK, block, 0)


def _gqa_win(gqa_in, sink, cache_k, cache_v, cos, sin, layer, n_batch, seq_len):
    past = cache_k.shape[3]
    return pl.pallas_call(
        functools.partial(_gqa_win_kernel, seq_len=seq_len),
        out_shape=jax.ShapeDtypeStruct((n_batch * seq_len, MIX_PART), BF16),
        grid=(n_batch,),
        in_specs=[
            pl.BlockSpec((seq_len, GQA_IN), lambda b: (b, 0)),
            pl.BlockSpec((1, 1, GQA_Q_HEADS), lambda b: (layer, 0, 0)),
            pl.BlockSpec((1, 1, GQA_KV_HEADS, past, HEAD_DIM), lambda b: (b, layer, 0, 0, 0)),
            pl.BlockSpec((1, 1, GQA_KV_HEADS, past, HEAD_DIM), lambda b: (b, layer, 0, 0, 0)),
            pl.BlockSpec((seq_len, LANES), lambda b: (0, 0)),
            pl.BlockSpec((seq_len, LANES), lambda b: (0, 0)),
        ],
        out_specs=pl.BlockSpec((seq_len, MIX_PART), lambda b: (b, 0)),
        scratch_shapes=[
            pltpu.VMEM((GQA_GROUPS, 2, seq_len, LANES), BF16),
            pltpu.VMEM((seq_len + 2 * BLOCK, LANES), BF16),
            pltpu.VMEM((seq_len + 2 * BLOCK, 2 * LANES), BF16),
            pltpu.VMEM((past, LANES), BF16),
            pltpu.VMEM((past, 2 * LANES), BF16),
        ],
        compiler_params=_params(1),
        name=f"gqa_win_{seq_len}",
    )(gqa_in, sink, cache_k, cache_v, cos, sin)


def _mla_kernel(*refs, seq_len, latent, bq, nb):
    if latent:
        (x_ref, qn_ref, kvn_ref, wq_ref, wk_ref, wv_ref, ckv_c_ref, kr_c_ref, cos_ref, sin_ref,
         y_ref, q_all, k_all, v_all) = refs
        past = ckv_c_ref.shape[2]
    else:
        (x_ref, qn_ref, kvn_ref, wq_ref, wk_ref, wv_ref, *_,
         y_ref, ckv_out_ref, kr_out_ref, q_all, k_all, v_all) = refs
        past = 0
        _zero_other_layers(ckv_out_ref, kr_out_ref)
    keys = past + seq_len
    nope_w = MLA_HEADS * MLA_NOPE
    scale = (MLA_NOPE + MLA_ROPE) ** -0.5
    lane = _lane_iota()
    lo = lane < HEAD_DIM
    rope_lanes = [(lane >= h * MLA_ROPE) & (lane < (h + 1) * MLA_ROPE) for h in range(MLA_HEADS)]
    for i in range(nb):
        rows = slice(i * seq_len, (i + 1) * seq_len)
        q_scr, k_scr, v_scr = q_all.at[i], k_all.at[i], v_all.at[i]
        q_lat = x_ref[rows, 0:MLA_Q_RANK]
        kv_lat = x_ref[rows, MLA_Q_RANK:MLA_Q_RANK + MLA_KV_RANK]
        k_rope = x_ref[rows, MLA_Q_RANK + MLA_KV_RANK:MLA_IN_TILED]
        mq = _dot(_rms(q_lat, qn_ref[0]).astype(BF16), wq_ref[0])
        ckv = _rms(kv_lat, kvn_ref[0])
        q_rope = mq[:, nope_w:nope_w + LANES]
        if latent:
            cos = cos_ref[...]
            sin = sin_ref[...]
            q_rope = q_rope * cos + _rot_half(q_rope, MLA_ROPE // 2) * sin
            k_rope = k_rope * cos + _rot_half(k_rope, MLA_ROPE // 2) * sin
            ckv_c = ckv_c_ref[0, 0].astype(BF16)
            k_scr[0:past, 0:nope_w] = _dot(ckv_c, wk_ref[0]).astype(BF16)
            v_scr[0:past, :] = _dot(ckv_c, wv_ref[0]).astype(BF16)
            kr_c = kr_c_ref[0, 0]
            k_scr[0:past, nope_w:nope_w + LANES] = jnp.concatenate([kr_c] * MLA_HEADS, axis=1).astype(BF16)
        else:
            ckv_out_ref[i, 0] = ckv
            kr_out_ref[i, 0] = k_rope[:, 0:MLA_ROPE]
        q_scr[:, 0:nope_w] = mq[:, 0:nope_w]
        q_scr[:, nope_w:nope_w + LANES] = q_rope
        ckv_b = ckv.astype(BF16)
        k_scr[past:keys, 0:nope_w] = _dot(ckv_b, wk_ref[0]).astype(BF16)
        v_scr[past:keys, :] = _dot(ckv_b, wv_ref[0]).astype(BF16)
        k_scr[past:keys, nope_w:nope_w + LANES] = k_rope.astype(BF16)

        ones = jnp.ones((keys, LANES), BF16)
        kr_all = k_scr[:, nope_w:nope_w + LANES]
        for t in range(MLA_HEADS // 2):
            cs = slice(t * LANES, (t + 1) * LANES)
            k_cat = jnp.concatenate([k_scr[:, cs], kr_all], axis=1)
            v_ext = jnp.concatenate([v_scr[:, cs], ones], axis=1)
            for r0 in range(0, seq_len, bq):
                qn = q_scr[r0:r0 + bq, cs]
                qr = q_scr[r0:r0 + bq, nope_w:nope_w + LANES]
                q2 = jnp.concatenate([
                    jnp.concatenate([jnp.where(lo, qn, 0.0), jnp.where(rope_lanes[2 * t], qr, 0.0)], axis=1),
                    jnp.concatenate([jnp.where(lo, 0.0, qn), jnp.where(rope_lanes[2 * t + 1], qr, 0.0)], axis=1),
                ], axis=0).astype(BF16)
                s = _dot_nt(q2, k_cat) * scale
                m = jnp.max(s, axis=-1, keepdims=True)
                pv = _dot(jnp.exp(s - m).astype(BF16), v_ext)
                o = pv[:, 0:LANES] / pv[:, LANES:LANES + 1]
                y_ref[i * seq_len + r0:i * seq_len + r0 + bq, cs] = _merge_heads(o, lo).astype(BF16)


def _mla(mla_in, q_norm, kv_norm, w_uq, w_uk, w_uv, cache, rope, carried, layer, n_batch, seq_len, nb):
    latent = cache is not None
    aliases = {}
    past = cache[0].shape[2] if latent else 0
    kern = functools.partial(_mla_kernel, seq_len=seq_len, latent=latent, bq=256, nb=nb)
    q_w = MLA_HEADS * (MLA_NOPE + MLA_ROPE)
    kv_w = MLA_HEADS * MLA_NOPE
    in_specs = [
        pl.BlockSpec((nb * seq_len, MLA_IN_TILED), lambda b: (b, 0)),
        pl.BlockSpec((1, 1, MLA_Q_RANK), lambda b: (layer, 0, 0)),
        pl.BlockSpec((1, 1, MLA_KV_RANK), lambda b: (layer, 0, 0)),
        pl.BlockSpec((1, MLA_Q_RANK, q_w), lambda b: (layer, 0, 0)),
        pl.BlockSpec((1, MLA_KV_RANK, kv_w), lambda b: (layer, 0, 0)),
        pl.BlockSpec((1, MLA_KV_RANK, kv_w), lambda b: (layer, 0, 0)),
    ]
    args = [mla_in, q_norm, kv_norm, w_uq, w_uk, w_uv]
    y_shape = jax.ShapeDtypeStruct((n_batch * seq_len, MIX_PART), BF16)
    y_spec = pl.BlockSpec((nb * seq_len, MIX_PART), lambda b: (b, 0))
    if latent:
        assert nb == 1
        in_specs += [
            pl.BlockSpec((1, 1, past, MLA_KV_RANK), lambda b: (b, layer, 0, 0)),
            pl.BlockSpec((1, 1, past, MLA_ROPE), lambda b: (b, layer, 0, 0)),
            pl.BlockSpec((seq_len, LANES), lambda b: (0, 0)),
            pl.BlockSpec((seq_len, LANES), lambda b: (0, 0)),
        ]
        args += [cache[0], cache[1], rope[0], rope[1]]
        out_shape, out_specs = y_shape, y_spec
    else:
        ckv_shape, ckv_spec = _layer_stacked((seq_len, MLA_KV_RANK), n_batch, layer, nb)
        kr_shape, kr_spec = _layer_stacked((seq_len, MLA_ROPE), n_batch, layer, nb)
        out_shape = [y_shape, ckv_shape, kr_shape]
        out_specs = [y_spec, ckv_spec, kr_spec]
        aliases = _carry_specs(in_specs, args, carried, 1)
    keys = past + seq_len
    return pl.pallas_call(
        kern, out_shape=out_shape, grid=(n_batch // nb,), in_specs=in_specs, out_specs=out_specs,
        input_output_aliases=aliases,
        scratch_shapes=[
            pltpu.VMEM((nb, seq_len, q_w), F32),
            pltpu.VMEM((nb, keys, kv_w + LANES), BF16),
            pltpu.VMEM((nb, keys, kv_w), BF16),
        ],
        compiler_params=_params(1),
        name=f"mla_{seq_len}",
    )(*args)


def _dft_matrix(seq_len):
    k = np.arange(seq_len, dtype=np.float64)[:, None]
    j = np.arange(seq_len, dtype=np.float64)[None, :]
    ang = np.pi * k * j / seq_len
    re = np.cos(ang)
    im = -np.sin(ang)
    im[0, :] = np.where(np.arange(seq_len) % 2 == 0, 1.0, -1.0)
    return np.concatenate([re, im], axis=0).astype(np.float32)


def _hyena_positions(seq_len):
    t = np.arange(seq_len, dtype=np.float64) / seq_len
    bands = np.arange(1, HY_BANDS + 1, dtype=np.float64)
    ang = 2.0 * math.pi * t[:, None] * bands
    z = np.concatenate([t[:, None], np.cos(ang), np.sin(ang)], axis=-1)
    return np.pad(z, ((0, 0), (0, HY_POS_PAD - HY_POS_DIM))).astype(np.float32)


def _rope_tables(n_tokens, dim):
    rows = np.repeat(np.arange(n_tokens // GRID_W, dtype=np.float64), GRID_W)
    cols = np.tile(np.arange(GRID_W, dtype=np.float64), n_tokens // GRID_W)
    quarter = dim // 4
    inv = ROPE_BASE ** (-np.arange(quarter, dtype=np.float64) / quarter)
    ang = np.concatenate([rows[:, None] * inv, cols[:, None] * inv], axis=-1)
    cos = np.concatenate([np.cos(ang), np.cos(ang)], axis=-1)
    sin = np.concatenate([-np.sin(ang), np.sin(ang)], axis=-1)
    reps = LANES // dim
    return (np.tile(cos, (1, reps)).astype(np.float32), np.tile(sin, (1, reps)).astype(np.float32))


def kernel(x_prompt, x_sample, state_ret, cache_gqa_k, cache_gqa_v, cache_mla_ckv, cache_mla_krope, c, c_ctx, ada_w, ada_b, norm_g, w_in, ret_decay_logit, ret_gn_g, hy_short_w, hy_short_b, hy_w1, hy_b1, hy_w2, hy_b2, hy_w3, hy_decay, hy_bias, gqa_sink, mla_q_norm, mla_kv_norm, mla_w_uq, mla_w_uk, mla_w_uv, w_out, ffn_w_up, ffn_conv_w, ffn_conv_b, ffn_w_down):
    n_p, len_p, _ = x_prompt.shape
    n_s, len_s, _ = x_sample.shape

    w_out_b = w_out.astype(BF16)
    w_up_b = ffn_w_up.astype(BF16)
    w_down_b = ffn_w_down.astype(BF16)
    uq = mla_w_uq.reshape(DEPTH, MLA_Q_RANK, MLA_HEADS, MLA_NOPE + MLA_ROPE)
    w_uq_b = jnp.concatenate(
        [uq[..., :MLA_NOPE].reshape(DEPTH, MLA_Q_RANK, MLA_HEADS * MLA_NOPE),
         uq[..., MLA_NOPE:].reshape(DEPTH, MLA_Q_RANK, MLA_HEADS * MLA_ROPE)], axis=-1).astype(BF16)
    w_uk_b = mla_w_uk.astype(BF16)
    w_uv_b = mla_w_uv.astype(BF16)

    cond = jnp.concatenate([c_ctx[None], c, jnp.zeros((N_COND - 1 - n_s, D_MODEL), F32)], axis=0)
    mod = _modulation(cond, ada_w, ada_b).reshape(DEPTH, N_COND, 6, D_MODEL)

    w1p = jnp.pad(hy_w1, ((0, 0), (0, HY_POS_PAD - HY_POS_DIM), (0, 0)))
    b1 = hy_b1.reshape(DEPTH, 1, HY_FILTER_HIDDEN)
    b2 = hy_b2.reshape(DEPTH, 1, HY_FILTER_HIDDEN)
    groups = {}
    for seq_len in (len_p, len_s):
        f32mat = _dft_matrix(seq_len)
        fmat = jnp.asarray(f32mat).astype(BF16)
        fmat_t = jnp.asarray(np.ascontiguousarray(f32mat.T)).astype(BF16)
        tables = _hyena_tables(seq_len, jnp.asarray(_hyena_positions(seq_len)), w1p, b1, hy_w2, b2,
                               hy_w3, hy_decay, fmat)
        groups[seq_len] = (tables, fmat, fmat_t)

    rope_g = tuple(jnp.asarray(t) for t in _rope_tables(len_s, HEAD_DIM))
    rope_m = tuple(jnp.asarray(t) for t in _rope_tables(len_s, MLA_ROPE))

    gn_g = ret_gn_g.reshape(DEPTH, 1, RET_WIDTH)
    short_b = hy_short_b.reshape(DEPTH, 1, HY_IN)
    sink = gqa_sink.reshape(DEPTH, 1, GQA_Q_HEADS)
    q_norm = mla_q_norm.reshape(DEPTH, 1, MLA_Q_RANK)
    kv_norm = mla_kv_norm.reshape(DEPTH, 1, MLA_KV_RANK)
    conv_b = ffn_conv_b.reshape(DEPTH, 1, D_FF)

    xp = x_prompt.reshape(n_p * len_p, D_MODEL)
    xs = x_sample.reshape(n_s * len_s, D_MODEL)
    tm_p = 512
    tm_s = 512
    nb_p = 2
    cond_p = lambda i: 0
    cond_s = lambda i: 1 + (i * tm_s) // len_s
    tm_ffn = 1024
    cond_s_ffn = lambda i: 1 + (i * tm_ffn) // len_s

    st = gkv = ckr = None
    for l in range(DEPTH):
        tables, fmat, fmat_t = groups[len_p]
        ret_in, hy_in, gqa_in, mla_in = _in_proj(xp, mod, norm_g, w_in, l, tm_p, cond_p)
        y_ret, *st = _retention(ret_in, ret_decay_logit, gn_g, None, st, l, n_p, len_p, nb_p)
        y_hy = _hyena(hy_in, hy_short_w, short_b, hy_bias, tables, fmat, fmat_t, l, n_p, len_p, nb_p)
        y_gqa, *gkv = _gqa_ctx(gqa_in, sink, gkv, l, n_p, len_p, nb_p)
        y_mla, *ckr = _mla(mla_in, q_norm, kv_norm, w_uq_b, w_uk_b, w_uv_b, None, None, ckr, l, n_p, len_p,
                           nb_p)
        xp = _channel_mix(xp, (y_ret, y_hy, y_gqa, y_mla), mod, norm_g, w_out_b, w_up_b, ffn_conv_w, conv_b,
                          w_down_b, l, tm_ffn, len_p, cond_p)
        tables, fmat, fmat_t = groups[len_s]
        ret_in, hy_in, gqa_in, mla_in = _in_proj(xs, mod, norm_g, w_in, l, tm_s, cond_s)
        y_ret = _retention(ret_in, ret_decay_logit, gn_g, state_ret, None, l, n_s, len_s, 1)
        y_hy = _hyena(hy_in, hy_short_w, short_b, hy_bias, tables, fmat, fmat_t, l, n_s, len_s, 1)
        y_gqa = _gqa_win(gqa_in, sink, cache_gqa_k, cache_gqa_v, rope_g[0], rope_g[1], l, n_s, len_s)
        y_mla = _mla(mla_in, q_norm, kv_norm, w_uq_b, w_uk_b, w_uv_b, (cache_mla_ckv, cache_mla_krope),
                     rope_m, None, l, n_s, len_s, 1)
        xs = _channel_mix(xs, (y_ret, y_hy, y_gqa, y_mla), mod, norm_g, w_out_b, w_up_b, ffn_conv_w, conv_b,
                          w_down_b, l, tm_ffn, len_s, cond_s_ffn)

    return (xp.reshape(n_p, len_p, D_MODEL), xs.reshape(n_s, len_s, D_MODEL),
            st[0], gkv[0], gkv[1], ckr[0], ckr[1])
```

```python
import functools
import math

import numpy as np
import jax
import jax.numpy as jnp
from jax import lax
from jax.experimental import pallas as pl
from jax.experimental.pallas import tpu as pltpu

F32 = jnp.float32
BF16 = jnp.bfloat16

D_MODEL = 1024
DEPTH = 4
GRID_W = 64
EPS = 1e-6
NEG_INF = -1e30
ROPE_BASE = 10000.0
BLOCK = 128
WINDOW = 128
HEAD_DIM = 64
LANES = 128
RET_HEADS = 4
RET_WIDTH = RET_HEADS * HEAD_DIM
HY_WIDTH = 256
HY_ORDER = 2
HY_BANDS = 8
HY_POS_DIM = 1 + 2 * HY_BANDS
HY_POS_PAD = 32
HY_FILTER_HIDDEN = 64
GQA_Q_HEADS = 4
GQA_KV_HEADS = 2
GQA_GROUPS = GQA_Q_HEADS // GQA_KV_HEADS
MLA_HEADS = 4
MLA_Q_RANK = 256
MLA_KV_RANK = 128
MLA_NOPE = 64
MLA_ROPE = 32
MLA_V = 64
D_FF = 2816
RET_IN = 4 * RET_WIDTH
HY_IN = (HY_ORDER + 1) * HY_WIDTH
GQA_IN = (GQA_Q_HEADS + 2 * GQA_KV_HEADS) * HEAD_DIM
MLA_IN = MLA_Q_RANK + MLA_KV_RANK + MLA_ROPE
MLA_IN_TILED = MLA_Q_RANK + MLA_KV_RANK + MLA_HEADS * MLA_ROPE
IN_WIDTH = RET_IN + HY_IN + GQA_IN + MLA_IN
IN_WIDTH_TILED = RET_IN + HY_IN + GQA_IN + MLA_IN_TILED
GQA_HEAD_ORDER = (0, 2, 1, 3)
MIX_PART = 256
N_COND = 8
FFN_CHUNK = 256
N_FFN_CHUNKS = D_FF // FFN_CHUNK

VMEM_LIMIT = 56 * 1024 * 1024


def _params(n_axes=1):
    return pltpu.CompilerParams(
        dimension_semantics=("arbitrary",) * n_axes, vmem_limit_bytes=VMEM_LIMIT)


def _dot(a, b):
    return jnp.dot(a, b, preferred_element_type=F32)


def _dot_nt(a, b):
    return lax.dot_general(a, b, (((1,), (1,)), ((), ())), preferred_element_type=F32)


def _dot_tn(a, b):
    return lax.dot_general(a, b, (((0,), (0,)), ((), ())), preferred_element_type=F32)


def _rms(x, g):
    return x * lax.rsqrt(jnp.mean(x * x, axis=-1, keepdims=True) + EPS) * g


def _sigmoid(x):
    return 1.0 / (1.0 + jnp.exp(-x))


def _silu(x):
    return x * _sigmoid(x)


def _shift_rows(x, seq_len):
    n = x.shape[0]
    assert seq_len & (seq_len - 1) == 0
    pos = lax.broadcasted_iota(jnp.int32, (n, 1), 0) & (seq_len - 1)
    prev = jnp.where(pos != 0, pltpu.roll(x, 1, axis=0), 0.0)
    nxt = jnp.where(pos != seq_len - 1, pltpu.roll(x, n - 1, axis=0), 0.0)
    return prev, nxt


def _rot_half(x, half):
    lane = lax.broadcasted_iota(jnp.int32, x.shape, 1) & (2 * half - 1)
    return jnp.where(lane < half, pltpu.roll(x, LANES - half, axis=1), pltpu.roll(x, half, axis=1))


def _lane_iota():
    return lax.broadcasted_iota(jnp.int32, (1, LANES), 1)


def _split_heads(t, lo):
    return jnp.concatenate([jnp.where(lo, t, 0.0), jnp.where(lo, 0.0, t)], axis=0)


def _merge_heads(o, lo):
    n = o.shape[0] // 2
    return jnp.where(lo, o[:n], o[n:])


def _per_head(vals, width):
    lane = lax.broadcasted_iota(jnp.int32, (1, width), 1)
    out = vals[-1]
    for h in range(len(vals) - 2, -1, -1):
        out = jnp.where(lane < (h + 1) * HEAD_DIM, vals[h], out)
    return out


def _mod_kernel(cond_ref, w_ref, b_ref, out_ref):
    s = _silu(cond_ref[...]).astype(BF16)
    out_ref[0] = _dot(s, w_ref[0].astype(BF16)) + b_ref[0]


def _modulation(cond, ada_w, ada_b):
    tn = 1536
    return pl.pallas_call(
        _mod_kernel,
        out_shape=jax.ShapeDtypeStruct((DEPTH, N_COND, 6 * D_MODEL), F32),
        grid=(DEPTH, 6 * D_MODEL // tn),
        in_specs=[
            pl.BlockSpec((N_COND, D_MODEL), lambda l, j: (0, 0)),
            pl.BlockSpec((1, D_MODEL, tn), lambda l, j: (l, 0, j)),
            pl.BlockSpec((1, 1, tn), lambda l, j: (l, 0, j)),
        ],
        out_specs=pl.BlockSpec((1, N_COND, tn), lambda l, j: (l, 0, j)),
        compiler_params=_params(2),
        name="modulation",
    )(cond, ada_w, ada_b.reshape(DEPTH, 1, 6 * D_MODEL))


def _filter_kernel(z_ref, w1_ref, b1_ref, w2_ref, b2_ref, w3_ref, dec_ref, f_ref, out_ref, *, seq_len):
    hi = lax.Precision.HIGHEST
    h = jnp.sin(jnp.dot(z_ref[...], w1_ref[0], precision=hi, preferred_element_type=F32) + b1_ref[0])
    h = jnp.sin(jnp.dot(h, w2_ref[0], precision=hi, preferred_element_type=F32) + b2_ref[0])
    h = jnp.dot(h, w3_ref[0], precision=hi, preferred_element_type=F32)
    row = lax.broadcasted_iota(jnp.int32, (seq_len, 1), 0)
    t = row.astype(F32) / seq_len
    dec = jnp.abs(dec_ref[0])
    win_f = jnp.exp(-t * dec[0:1, :])
    win_b = jnp.exp(-t * dec[1:2, :])
    fm = f_ref[...]
    inv = 1.0 / seq_len
    scale = jnp.where(row == 0, 0.5 * inv, inv)
    for o in range(HY_ORDER):
        base = o * 2 * HY_WIDTH
        hf = h[:, base:base + HY_WIDTH] * win_f
        hb = h[:, base + HY_WIDTH:base + 2 * HY_WIDTH] * win_b
        hbs = jnp.where(row != 0, pltpu.roll(hb, 1, axis=0), 0.0)
        sf = _dot(fm, hf.astype(BF16))
        sb = _dot(fm, hbs.astype(BF16))
        g_re = sf[:seq_len] + sb[:seq_len]
        g_im = sf[seq_len:] - sb[seq_len:]
        g_ny = sf[seq_len:] + sb[seq_len:]
        a = g_re * scale
        out_ref[0, 3 * o] = a
        out_ref[0, 3 * o + 1] = jnp.where(row == 0, g_ny * (0.5 * inv), a)
        out_ref[0, 3 * o + 2] = jnp.where(row == 0, 0.0, g_im * inv)


def _hyena_tables(seq_len, z, w1p, b1, w2, b2, w3, decay, fmat):
    hid = HY_FILTER_HIDDEN
    wide = HY_ORDER * 2 * HY_WIDTH
    return pl.pallas_call(
        functools.partial(_filter_kernel, seq_len=seq_len),
        out_shape=jax.ShapeDtypeStruct((DEPTH, 3 * HY_ORDER, seq_len, HY_WIDTH), F32),
        grid=(DEPTH,),
        in_specs=[
            pl.BlockSpec((seq_len, HY_POS_PAD), lambda l: (0, 0)),
            pl.BlockSpec((1, HY_POS_PAD, hid), lambda l: (l, 0, 0)),
            pl.BlockSpec((1, 1, hid), lambda l: (l, 0, 0)),
            pl.BlockSpec((1, hid, hid), lambda l: (l, 0, 0)),
            pl.BlockSpec((1, 1, hid), lambda l: (l, 0, 0)),
            pl.BlockSpec((1, hid, wide), lambda l: (l, 0, 0)),
            pl.BlockSpec((1, 2, HY_WIDTH), lambda l: (l, 0, 0)),
            pl.BlockSpec((2 * seq_len, seq_len), lambda l: (0, 0)),
        ],
        out_specs=pl.BlockSpec((1, 3 * HY_ORDER, seq_len, HY_WIDTH), lambda l: (l, 0, 0, 0)),
        compiler_params=_params(1),
        name=f"hyena_tables_{seq_len}",
    )(z, w1p, b1, w2, b2, w3, decay, fmat)


def _layer_stacked(tail, n_batch, layer, nb):
    zeros = (0,) * len(tail)
    n_layers = DEPTH if layer == 0 else 1
    return (jax.ShapeDtypeStruct((n_batch, DEPTH) + tail, F32),
            pl.BlockSpec((nb, n_layers) + tail, lambda b: (b, layer) + zeros))


def _zero_other_layers(*out_refs):
    for ref in out_refs:
        if ref.shape[1] > 1:
            ref[:, 1:] = jnp.zeros((ref.shape[0], ref.shape[1] - 1) + ref.shape[2:], ref.dtype)


def _carry_specs(in_specs, args, carried, first_out):
    aliases = {}
    if carried is not None:
        for k, arr in enumerate(carried):
            aliases[len(args)] = first_out + k
            in_specs.append(pl.BlockSpec(memory_space=pl.ANY))
            args.append(arr)
    return aliases


def _in_proj_kernel(x_ref, mod_ref, g_ref, w_ref, ret_ref, hy_ref, gqa_ref, mla_ref, w_scr):
    @pl.when(pl.program_id(0) == 0)
    def _():
        g0 = RET_IN + HY_IN
        m0 = g0 + GQA_IN
        kr0 = m0 + MLA_Q_RANK + MLA_KV_RANK
        step = 2 * LANES
        for r0 in list(range(0, g0, step)) + list(range(g0 + GQA_Q_HEADS * HEAD_DIM, kr0, LANES)):
            n = step if r0 < g0 else LANES
            w_scr[r0:r0 + n, :] = w_ref[0, r0:r0 + n, :].astype(BF16)
        for dst, j in enumerate(GQA_HEAD_ORDER):
            w_scr[g0 + dst * HEAD_DIM:g0 + (dst + 1) * HEAD_DIM, :] = (
                w_ref[0, g0 + j * HEAD_DIM:g0 + (j + 1) * HEAD_DIM, :].astype(BF16))
        kr = w_ref[0, kr0:kr0 + MLA_ROPE, :].astype(BF16)
        for hd in range(MLA_HEADS):
            w_scr[kr0 + hd * MLA_ROPE:kr0 + (hd + 1) * MLA_ROPE, :] = kr

    shift = mod_ref[0, 0, 0:1, :]
    scale = mod_ref[0, 0, 1:2, :]
    h = (_rms(x_ref[...], g_ref[0, 0:1, :]) * (1.0 + scale) + shift).astype(BF16)
    c0 = 0
    for ref in (ret_ref, hy_ref, gqa_ref, mla_ref):
        width = ref.shape[1]
        ref[...] = _dot_nt(h, w_scr[c0:c0 + width, :])
        c0 += width


def _in_proj(x, mod, norm_g, w_in, layer, tm, cond_of_tile):
    rows = x.shape[0]
    widths = (RET_IN, HY_IN, GQA_IN, MLA_IN_TILED)
    return pl.pallas_call(
        _in_proj_kernel,
        out_shape=[jax.ShapeDtypeStruct((rows, w), F32) for w in widths],
        grid=(rows // tm,),
        in_specs=[
            pl.BlockSpec((tm, D_MODEL), lambda i: (i, 0)),
            pl.BlockSpec((1, 1, 6, D_MODEL), lambda i: (layer, cond_of_tile(i), 0, 0)),
            pl.BlockSpec((1, 4, D_MODEL), lambda i: (layer, 0, 0)),
            pl.BlockSpec((1, IN_WIDTH, D_MODEL), lambda i: (layer, 0, 0), pipeline_mode=pl.Buffered(1)),
        ],
        out_specs=[pl.BlockSpec((tm, w), lambda i: (i, 0)) for w in widths],
        scratch_shapes=[pltpu.VMEM((IN_WIDTH_TILED, D_MODEL), BF16)],
        compiler_params=_params(1),
        name="in_proj",
    )(x, mod, norm_g, w_in)


def _channel_kernel(x_ref, m0_ref, m1_ref, m2_ref, m3_ref, mod_ref, g_ref, wo_ref, wu_ref, cw_ref, cb_ref,
                    wd_ref, out_ref, act_ref, *, seq_len):
    y = None
    for i, m_ref in enumerate((m0_ref, m1_ref, m2_ref, m3_ref)):
        r0 = i * MIX_PART
        if i == 2:
            w = jnp.concatenate([wo_ref[0, r0 + j * HEAD_DIM:r0 + (j + 1) * HEAD_DIM, :]
                                 for j in GQA_HEAD_ORDER], axis=0)
        else:
            w = wo_ref[0, r0:r0 + MIX_PART, :]
        part = _dot(m_ref[...], w)
        y = part if y is None else y + part
    gate1 = mod_ref[0, 0, 2:3, :]
    shift2 = mod_ref[0, 0, 3:4, :]
    scale2 = mod_ref[0, 0, 4:5, :]
    xm = x_ref[...] + gate1 * _rms(y, g_ref[0, 1:2, :])
    out_ref[...] = xm
    h2 = (_rms(xm, g_ref[0, 2:3, :]) * (1.0 + scale2) + shift2).astype(BF16)
    for c in range(N_FFN_CHUNKS):
        sl = slice(c * FFN_CHUNK, (c + 1) * FFN_CHUNK)
        gate = _dot(h2, wu_ref[0, :, sl])
        up = _dot(h2, wu_ref[0, :, D_FF + c * FFN_CHUNK:D_FF + (c + 1) * FFN_CHUNK])
        prev, nxt = _shift_rows(gate, seq_len)
        gate = (prev * cw_ref[0, 0:1, sl] + gate * cw_ref[0, 1:2, sl] + nxt * cw_ref[0, 2:3, sl]
                + cb_ref[0, :, sl])
        act_ref[:, sl] = (_silu(gate) * up).astype(BF16)
    ffn = _dot(act_ref[...], wd_ref[0])
    gate2 = mod_ref[0, 0, 5:6, :]
    out_ref[...] = out_ref[...] + gate2 * _rms(ffn, g_ref[0, 3:4, :])


def _channel_mix(x, mixes, mod, norm_g, w_out, w_up, conv_w, conv_b, w_down, layer, tm, seq_len, cond_of_tile):
    rows = x.shape[0]
    resident = dict(pipeline_mode=pl.Buffered(1))
    return pl.pallas_call(
        functools.partial(_channel_kernel, seq_len=seq_len),
        out_shape=jax.ShapeDtypeStruct((rows, D_MODEL), F32),
        grid=(rows // tm,),
        in_specs=[pl.BlockSpec((tm, D_MODEL), lambda i: (i, 0))]
        + [pl.BlockSpec((tm, MIX_PART), lambda i: (i, 0))] * 4
        + [
            pl.BlockSpec((1, 1, 6, D_MODEL), lambda i: (layer, cond_of_tile(i), 0, 0)),
            pl.BlockSpec((1, 4, D_MODEL), lambda i: (layer, 0, 0)),
            pl.BlockSpec((1, D_MODEL, D_MODEL), lambda i: (layer, 0, 0), **resident),
            pl.BlockSpec((1, D_MODEL, 2 * D_FF), lambda i: (layer, 0, 0), **resident),
            pl.BlockSpec((1, 3, D_FF), lambda i: (layer, 0, 0)),
            pl.BlockSpec((1, 1, D_FF), lambda i: (layer, 0, 0)),
            pl.BlockSpec((1, D_FF, D_MODEL), lambda i: (layer, 0, 0), **resident),
        ],
        out_specs=pl.BlockSpec((tm, D_MODEL), lambda i: (i, 0)),
        scratch_shapes=[pltpu.VMEM((tm, D_FF), BF16)],
        compiler_params=_params(1),
        name="channel_mix",
    )(x, *mixes, mod, norm_g, w_out, w_up, conv_w, conv_b, w_down)


def _log_gamma(dl_ref):
    dl = dl_ref[0]
    return jnp.minimum(dl, 0.0) - jnp.log(1.0 + jnp.exp(-jnp.abs(dl)))


def _pair_decay(lg, t, r0, bq, seq_len):
    top = lax.broadcasted_iota(jnp.int32, (2 * bq, 1), 0) < bq
    lgf = jnp.where(top, lg[0:1, 2 * t:2 * t + 1], lg[0:1, 2 * t + 1:2 * t + 2])
    lgb = jnp.where(top, lg[1:2, 2 * t:2 * t + 1], lg[1:2, 2 * t + 1:2 * t + 2])
    rowf = ((lax.broadcasted_iota(jnp.int32, (2 * bq, 1), 0) & (bq - 1)) + r0).astype(F32)
    colf = lax.broadcasted_iota(jnp.int32, (1, seq_len), 1).astype(F32)
    lag = rowf - colf
    decay = jnp.exp(jnp.where(lag >= 0.0, lag * lgf, -lag * lgb))
    return jnp.where(lag == 0.0, 2.0, decay)


def _head_norm(o, lo):
    inv = 1.0 / HEAD_DIM
    s_all = jnp.sum(o, axis=-1, keepdims=True)
    s_lo = jnp.sum(jnp.where(lo, o, 0.0), axis=-1, keepdims=True)
    d = o - jnp.where(lo, s_lo, s_all - s_lo) * inv
    d2 = d * d
    v_all = jnp.sum(d2, axis=-1, keepdims=True)
    v_lo = jnp.sum(jnp.where(lo, d2, 0.0), axis=-1, keepdims=True)
    return d * lax.rsqrt(jnp.where(lo, v_lo, v_all - v_lo) * inv + EPS)


def _ret_ctx_kernel(*refs, seq_len, nb):
    x_ref, dl_ref, gn_ref, *_, y_ref, st_ref, dec_scr, kdec_scr = refs
    lo = _lane_iota() < HEAD_DIM
    _zero_other_layers(st_ref)

    @pl.when(pl.program_id(0) == 0)
    def _():
        lg = _log_gamma(dl_ref)
        posf = lax.broadcasted_iota(jnp.int32, (seq_len, 1), 0).astype(F32)
        lgf = _per_head([lg[0:1, h:h + 1] for h in range(RET_HEADS)], RET_WIDTH)
        lgb = _per_head([lg[1:2, h:h + 1] for h in range(RET_HEADS)], RET_WIDTH)
        kdec_scr[0] = jnp.exp((seq_len - 1.0 - posf) * lgf)
        kdec_scr[1] = jnp.exp(posf * lgb)
        for t in range(RET_HEADS // 2):
            dec_scr[t] = _pair_decay(lg, t, 0, seq_len, seq_len)

    for i in range(nb):
        rows = slice(i * seq_len, (i + 1) * seq_len)
        for t in range(RET_HEADS // 2):
            cs = slice(t * LANES, (t + 1) * LANES)
            q = x_ref[rows, cs]
            k = x_ref[rows, RET_WIDTH + t * LANES:RET_WIDTH + (t + 1) * LANES] * (HEAD_DIM ** -0.5)
            vb = x_ref[rows, 2 * RET_WIDTH + t * LANES:2 * RET_WIDTH + (t + 1) * LANES].astype(BF16)
            gate = x_ref[rows, 3 * RET_WIDTH + t * LANES:3 * RET_WIDTH + (t + 1) * LANES]
            s = _dot_nt(_split_heads(q, lo).astype(BF16), k.astype(BF16)) * dec_scr[t]
            o = _merge_heads(_dot(s.astype(BF16), vb), lo)
            y = _silu(gate) * (_head_norm(o, lo) * gn_ref[0, :, cs])
            y_ref[rows, cs] = y.astype(BF16)
            for d in range(2):
                st = _dot_tn((k * kdec_scr[d, :, cs]).astype(BF16), vb)
                st_ref[i, 0, d, 2 * t] = st[0:HEAD_DIM, 0:HEAD_DIM]
                st_ref[i, 0, d, 2 * t + 1] = st[HEAD_DIM:LANES, HEAD_DIM:LANES]


def _ret_lat_kernel(x_ref, dl_ref, gn_ref, s0_ref, y_ref, *, seq_len, bq):
    lo = _lane_iota() < HEAD_DIM
    lg = _log_gamma(dl_ref)
    zero = jnp.zeros((HEAD_DIM, HEAD_DIM), F32)
    for t in range(RET_HEADS // 2):
        cs = slice(t * LANES, (t + 1) * LANES)
        kb = (x_ref[:, RET_WIDTH + t * LANES:RET_WIDTH + (t + 1) * LANES] * (HEAD_DIM ** -0.5)).astype(BF16)
        vb = x_ref[:, 2 * RET_WIDTH + t * LANES:2 * RET_WIDTH + (t + 1) * LANES].astype(BF16)
        lgf = _per_head([lg[0:1, 2 * t:2 * t + 1], lg[0:1, 2 * t + 1:2 * t + 2]], LANES)
        lgb = _per_head([lg[1:2, 2 * t:2 * t + 1], lg[1:2, 2 * t + 1:2 * t + 2]], LANES)
        s0 = []
        for d in range(2):
            a = s0_ref[0, 0, d, 2 * t]
            b = s0_ref[0, 0, d, 2 * t + 1]
            s0.append(jnp.concatenate([jnp.concatenate([a, zero], axis=1),
                                       jnp.concatenate([zero, b], axis=1)], axis=0).astype(BF16))
        for r0 in range(0, seq_len, bq):
            q = x_ref[r0:r0 + bq, cs]
            s = _dot_nt(_split_heads(q, lo).astype(BF16), kb) * _pair_decay(lg, t, r0, bq, seq_len)
            o = _merge_heads(_dot(s.astype(BF16), vb), lo)
            rowf = (lax.broadcasted_iota(jnp.int32, (bq, 1), 0) + r0).astype(F32)
            qf = (q * jnp.exp((rowf + 1.0) * lgf)).astype(BF16)
            qr = (q * jnp.exp((seq_len - rowf) * lgb)).astype(BF16)
            o = o + _dot(qf, s0[0]) + _dot(qr, s0[1])
            gate = x_ref[r0:r0 + bq, 3 * RET_WIDTH + t * LANES:3 * RET_WIDTH + (t + 1) * LANES]
            y = _silu(gate) * (_head_norm(o, lo) * gn_ref[0, :, cs])
            y_ref[r0:r0 + bq, cs] = y.astype(BF16)


def _retention(ret_in, decay_logit, gn_g, state0, carried, layer, n_batch, seq_len, nb):
    latent = state0 is not None
    in_specs = [
        pl.BlockSpec((nb * seq_len, RET_IN), lambda b: (b, 0)),
        pl.BlockSpec((1, 2, RET_HEADS), lambda b: (layer, 0, 0)),
        pl.BlockSpec((1, 1, RET_WIDTH), lambda b: (layer, 0, 0)),
    ]
    args = [ret_in, decay_logit, gn_g]
    y_shape = jax.ShapeDtypeStruct((n_batch * seq_len, MIX_PART), BF16)
    y_spec = pl.BlockSpec((nb * seq_len, MIX_PART), lambda b: (b, 0))
    if latent:
        assert nb == 1
        kern = functools.partial(_ret_lat_kernel, seq_len=seq_len, bq=256)
        in_specs.append(pl.BlockSpec((1, 1, 2, RET_HEADS, HEAD_DIM, HEAD_DIM),
                                     lambda b: (b, layer, 0, 0, 0, 0)))
        args.append(state0)
        out_shape, out_specs, aliases, scratch = y_shape, y_spec, {}, []
    else:
        kern = functools.partial(_ret_ctx_kernel, seq_len=seq_len, nb=nb)
        st_shape, st_spec = _layer_stacked((2, RET_HEADS, HEAD_DIM, HEAD_DIM), n_batch, layer, nb)
        out_shape = [y_shape, st_shape]
        out_specs = [y_spec, st_spec]
        aliases = _carry_specs(in_specs, args, carried, 1)
        scratch = [pltpu.VMEM((RET_HEADS // 2, 2 * seq_len, seq_len), F32),
                   pltpu.VMEM((2, seq_len, RET_WIDTH), F32)]
    return pl.pallas_call(
        kern, out_shape=out_shape, grid=(n_batch // nb,), in_specs=in_specs, out_specs=out_specs,
        input_output_aliases=aliases, scratch_shapes=scratch,
        compiler_params=_params(1), name=f"retention_{seq_len}",
    )(*args)


def _hyena_kernel(x_ref, sw_ref, sb_ref, bias_ref, tab_ref, f_ref, ft_ref, y_ref, *, seq_len, nb):
    for i in range(nb):
        rows = slice(i * seq_len, (i + 1) * seq_len)
        x = x_ref[rows, :]
        prev, nxt = _shift_rows(x, seq_len)
        u = prev * sw_ref[0, 0:1, :] + x * sw_ref[0, 1:2, :] + nxt * sw_ref[0, 2:3, :] + sb_ref[0]
        x1 = u[:, 0:HY_WIDTH]
        x2 = u[:, HY_WIDTH:2 * HY_WIDTH]
        z = u[:, 2 * HY_WIDTH:3 * HY_WIDTH]
        for o, gate in enumerate((x1, x2)):
            spec = _dot(f_ref[...], z.astype(BF16))
            s_re = spec[:seq_len]
            s_im = spec[seq_len:]
            a = tab_ref[0, 3 * o]
            a_ny = tab_ref[0, 3 * o + 1]
            b = tab_ref[0, 3 * o + 2]
            y_re = (s_re * a - s_im * b).astype(BF16)
            y_im = (s_re * b + s_im * a_ny).astype(BF16)
            conv = _dot(ft_ref[:, 0:seq_len], y_re) + _dot(ft_ref[:, seq_len:2 * seq_len], y_im)
            z = gate * (conv + z * bias_ref[0, o:o + 1, :])
        y_ref[rows, :] = z.astype(BF16)


def _hyena(hy_in, short_w, short_b, hy_bias, tables, fmat, fmat_t, layer, n_batch, seq_len, nb):
    const = dict(pipeline_mode=pl.Buffered(1))
    return pl.pallas_call(
        functools.partial(_hyena_kernel, seq_len=seq_len, nb=nb),
        out_shape=jax.ShapeDtypeStruct((n_batch * seq_len, MIX_PART), BF16),
        grid=(n_batch // nb,),
        in_specs=[
            pl.BlockSpec((nb * seq_len, HY_IN), lambda b: (b, 0)),
            pl.BlockSpec((1, 3, HY_IN), lambda b: (layer, 0, 0)),
            pl.BlockSpec((1, 1, HY_IN), lambda b: (layer, 0, 0)),
            pl.BlockSpec((1, HY_ORDER, HY_WIDTH), lambda b: (layer, 0, 0)),
            pl.BlockSpec((1, 3 * HY_ORDER, seq_len, HY_WIDTH), lambda b: (layer, 0, 0, 0), **const),
            pl.BlockSpec((2 * seq_len, seq_len), lambda b: (0, 0), **const),
            pl.BlockSpec((seq_len, 2 * seq_len), lambda b: (0, 0), **const),
        ],
        out_specs=pl.BlockSpec((nb * seq_len, MIX_PART), lambda b: (b, 0)),
        compiler_params=_params(1),
        name=f"hyena_{seq_len}",
    )(hy_in, short_w, short_b, hy_bias, tables, fmat, fmat_t)


def _pair_sink(sink_ref, g, n):
    top = lax.broadcasted_iota(jnp.int32, (2 * n, 1), 0) < n
    return jnp.where(top, sink_ref[0, :, g:g + 1], sink_ref[0, :, GQA_GROUPS + g:GQA_GROUPS + g + 1])


def _gqa_ctx_kernel(*refs, seq_len, nb):
    x_ref, sink_ref, *_, y_ref, k_out_ref, v_out_ref = refs
    nq = GQA_Q_HEADS * HEAD_DIM
    scale = HEAD_DIM ** -0.5
    lo = _lane_iota() < HEAD_DIM
    ones = jnp.ones((seq_len, LANES), BF16)
    _zero_other_layers(k_out_ref, v_out_ref)
    for i in range(nb):
        rows = slice(i * seq_len, (i + 1) * seq_len)
        k = x_ref[rows, nq:nq + LANES]
        v = x_ref[rows, nq + LANES:nq + 2 * LANES]
        k_t = k.T
        v_t = v.T
        for kv in range(GQA_KV_HEADS):
            k_out_ref[i, 0, kv] = k_t[kv * HEAD_DIM:(kv + 1) * HEAD_DIM, :]
            v_out_ref[i, 0, kv] = v_t[kv * HEAD_DIM:(kv + 1) * HEAD_DIM, :]
        kb = k.astype(BF16)
        v_ext = jnp.concatenate([v.astype(BF16), ones], axis=1)
        for g in range(GQA_GROUPS):
            cs = slice(g * LANES, (g + 1) * LANES)
            s = _dot_nt(_split_heads(x_ref[rows, cs], lo).astype(BF16), kb) * scale
            sink = _pair_sink(sink_ref, g, seq_len)
            m = jnp.maximum(jnp.max(s, axis=-1, keepdims=True), sink)
            pv = _dot(jnp.exp(s - m).astype(BF16), v_ext)
            den = pv[:, LANES:LANES + 1] + jnp.exp(sink - m)
            y_ref[rows, cs] = _merge_heads(pv[:, 0:LANES] / den, lo).astype(BF16)


def _gqa_ctx(gqa_in, sink, carried, layer, n_batch, seq_len, nb):
    kv_shape, kv_spec = _layer_stacked((GQA_KV_HEADS, HEAD_DIM, seq_len), n_batch, layer, nb)
    in_specs = [
        pl.BlockSpec((nb * seq_len, GQA_IN), lambda b: (b, 0)),
        pl.BlockSpec((1, 1, GQA_Q_HEADS), lambda b: (layer, 0, 0)),
    ]
    args = [gqa_in, sink]
    aliases = _carry_specs(in_specs, args, carried, 1)
    return pl.pallas_call(
        functools.partial(_gqa_ctx_kernel, seq_len=seq_len, nb=nb),
        out_shape=[jax.ShapeDtypeStruct((n_batch * seq_len, MIX_PART), BF16), kv_shape, kv_shape],
        grid=(n_batch // nb,),
        in_specs=in_specs,
        out_specs=[pl.BlockSpec((nb * seq_len, MIX_PART), lambda b: (b, 0)), kv_spec, kv_spec],
        input_output_aliases=aliases,
        compiler_params=_params(1),
        name=f"gqa_ctx_{seq_len}",
    )(*args)


def _gqa_win_kernel(x_ref, sink_ref, kc_ref, vc_ref, cos_ref, sin_ref, y_ref,
                    q_scr, k_scr, v_scr, kc_scr, vc_scr, *, seq_len):
    nq = GQA_Q_HEADS * HEAD_DIM
    scale = HEAD_DIM ** -0.5
    lo = _lane_iota() < HEAD_DIM
    past = kc_scr.shape[0]
    cos = cos_ref[...]
    sin = sin_ref[...]

    def rope(t):
        return t * cos + _rot_half(t, HEAD_DIM // 2) * sin

    for g in range(GQA_GROUPS):
        q = rope(x_ref[:, g * LANES:(g + 1) * LANES])
        q_scr[g, 0] = jnp.where(lo, q, 0.0).astype(BF16)
        q_scr[g, 1] = jnp.where(lo, 0.0, q).astype(BF16)
    zeros = jnp.zeros((BLOCK, LANES), BF16)
    for r0 in (0, BLOCK + seq_len):
        k_scr[r0:r0 + BLOCK, :] = zeros
        v_scr[r0:r0 + BLOCK, 0:LANES] = zeros
    k_scr[BLOCK:BLOCK + seq_len, :] = rope(x_ref[:, nq:nq + LANES]).astype(BF16)
    v_scr[BLOCK:BLOCK + seq_len, 0:LANES] = x_ref[:, nq + LANES:nq + 2 * LANES].astype(BF16)
    v_scr[:, LANES:2 * LANES] = jnp.ones((seq_len + 2 * BLOCK, LANES), BF16)
    kc_scr[...] = jnp.concatenate([kc_ref[0, 0, 0], kc_ref[0, 0, 1]], axis=1).astype(BF16)
    vc_scr[:, 0:LANES] = jnp.concatenate([vc_ref[0, 0, 0], vc_ref[0, 0, 1]], axis=1).astype(BF16)
    vc_scr[:, LANES:2 * LANES] = jnp.ones((past, LANES), BF16)

    ii = lax.broadcasted_iota(jnp.int32, (2 * BLOCK, 3 * BLOCK), 0) & (BLOCK - 1)
    jj = lax.broadcasted_iota(jnp.int32, (2 * BLOCK, 3 * BLOCK), 1)
    band = (jj >= ii) & (jj <= ii + 2 * WINDOW)

    def block(n, carry):
        r0 = pl.multiple_of(n * BLOCK, BLOCK)
        kpos = jj + (n - 1) * BLOCK
        valid = band & (kpos >= 0) & (kpos < seq_len)
        kw = k_scr[pl.ds(r0, 3 * BLOCK), :]
        vw = v_scr[pl.ds(r0, 3 * BLOCK), :]
        for g in range(GQA_GROUPS):
            q2 = jnp.concatenate([q_scr[g, 0, pl.ds(r0, BLOCK), :], q_scr[g, 1, pl.ds(r0, BLOCK), :]], axis=0)
            s_win = jnp.where(valid, _dot_nt(q2, kw) * scale, NEG_INF)
            s_ctx = _dot_nt(q2, kc_scr[...]) * scale
            sink = _pair_sink(sink_ref, g, BLOCK)
            m = jnp.maximum(jnp.maximum(jnp.max(s_win, axis=-1, keepdims=True),
                                        jnp.max(s_ctx, axis=-1, keepdims=True)), sink)
            pv = (_dot(jnp.exp(s_win - m).astype(BF16), vw)
                  + _dot(jnp.exp(s_ctx - m).astype(BF16), vc_scr[...]))
            den = pv[:, LANES:LANES + 1] + jnp.exp(sink - m)
            y_ref[pl.ds(r0, BLOCK), g * LANES:(g + 1) * LANES] = _merge_heads(
                pv[:, 0:LANES] / den, lo).astype(BF16)
        return carry

    lax.fori_loop(0, seq_len // BLOCK, block, 0)


def _gqa_win(gqa_in, sink, cache_k, cache_v, cos, sin, layer, n_batch, seq_len):
    past = cache_k.shape[3]
    return pl.pallas_call(
        functools.partial(_gqa_win_kernel, seq_len=seq_len),
        out_shape=jax.ShapeDtypeStruct((n_batch * seq_len, MIX_PART), BF16),
        grid=(n_batch,),
        in_specs=[
            pl.BlockSpec((seq_len, GQA_IN), lambda b: (b, 0)),
            pl.BlockSpec((1, 1, GQA_Q_HEADS), lambda b: (layer, 0, 0)),
            pl.BlockSpec((1, 1, GQA_KV_HEADS, past, HEAD_DIM), lambda b: (b, layer, 0, 0, 0)),
            pl.BlockSpec((1, 1, GQA_KV_HEADS, past, HEAD_DIM), lambda b: (b, layer, 0, 0, 0)),
            pl.BlockSpec((seq_len, LANES), lambda b: (0, 0)),
            pl.BlockSpec((seq_len, LANES), lambda b: (0, 0)),
        ],
        out_specs=pl.BlockSpec((seq_len, MIX_PART), lambda b: (b, 0)),
        scratch_shapes=[
            pltpu.VMEM((GQA_GROUPS, 2, seq_len, LANES), BF16),
            pltpu.VMEM((seq_len + 2 * BLOCK, LANES), BF16),
            pltpu.VMEM((seq_len + 2 * BLOCK, 2 * LANES), BF16),
            pltpu.VMEM((past, LANES), BF16),
            pltpu.VMEM((past, 2 * LANES), BF16),
        ],
        compiler_params=_params(1),
        name=f"gqa_win_{seq_len}",
    )(gqa_in, sink, cache_k, cache_v, cos, sin)


def _mla_kernel(*refs, seq_len, latent, bq, nb):
    if latent:
        (x_ref, qn_ref, kvn_ref, wq_ref, wk_ref, wv_ref, ckv_c_ref, kr_c_ref, cos_ref, sin_ref,
         y_ref, q_all, k_all, v_all) = refs
        past = ckv_c_ref.shape[2]
    else:
        (x_ref, qn_ref, kvn_ref, wq_ref, wk_ref, wv_ref, *_,
         y_ref, ckv_out_ref, kr_out_ref, q_all, k_all, v_all) = refs
        past = 0
        _zero_other_layers(ckv_out_ref, kr_out_ref)
    keys = past + seq_len
    nope_w = MLA_HEADS * MLA_NOPE
    scale = (MLA_NOPE + MLA_ROPE) ** -0.5
    lane = _lane_iota()
    lo = lane < HEAD_DIM
    rope_lanes = [(lane >= h * MLA_ROPE) & (lane < (h + 1) * MLA_ROPE) for h in range(MLA_HEADS)]
    for i in range(nb):
        rows = slice(i * seq_len, (i + 1) * seq_len)
        q_scr, k_scr, v_scr = q_all.at[i], k_all.at[i], v_all.at[i]
        q_lat = x_ref[rows, 0:MLA_Q_RANK]
        kv_lat = x_ref[rows, MLA_Q_RANK:MLA_Q_RANK + MLA_KV_RANK]
        k_rope = x_ref[rows, MLA_Q_RANK + MLA_KV_RANK:MLA_IN_TILED]
        mq = _dot(_rms(q_lat, qn_ref[0]).astype(BF16), wq_ref[0])
        ckv = _rms(kv_lat, kvn_ref[0])
        q_rope = mq[:, nope_w:nope_w + LANES]
        if latent:
            cos = cos_ref[...]
            sin = sin_ref[...]
            q_rope = q_rope * cos + _rot_half(q_rope, MLA_ROPE // 2) * sin
            k_rope = k_rope * cos + _rot_half(k_rope, MLA_ROPE // 2) * sin
            ckv_c = ckv_c_ref[0, 0].astype(BF16)
            k_scr[0:past, 0:nope_w] = _dot(ckv_c, wk_ref[0]).astype(BF16)
            v_scr[0:past, :] = _dot(ckv_c, wv_ref[0]).astype(BF16)
            kr_c = kr_c_ref[0, 0]
            k_scr[0:past, nope_w:nope_w + LANES] = jnp.concatenate([kr_c] * MLA_HEADS, axis=1).astype(BF16)
        else:
            ckv_out_ref[i, 0] = ckv
            kr_out_ref[i, 0] = k_rope.T[0:MLA_ROPE, :]
        q_scr[:, 0:nope_w] = mq[:, 0:nope_w]
        q_scr[:, nope_w:nope_w + LANES] = q_rope
        ckv_b = ckv.astype(BF16)
        k_scr[past:keys, 0:nope_w] = _dot(ckv_b, wk_ref[0]).astype(BF16)
        v_scr[past:keys, :] = _dot(ckv_b, wv_ref[0]).astype(BF16)
        k_scr[past:keys, nope_w:nope_w + LANES] = k_rope.astype(BF16)

        ones = jnp.ones((keys, LANES), BF16)
        kr_all = k_scr[:, nope_w:nope_w + LANES]
        for t in range(MLA_HEADS // 2):
            cs = slice(t * LANES, (t + 1) * LANES)
            k_cat = jnp.concatenate([k_scr[:, cs], kr_all], axis=1)
            v_ext = jnp.concatenate([v_scr[:, cs], ones], axis=1)
            for r0 in range(0, seq_len, bq):
                qn = q_scr[r0:r0 + bq, cs]
                qr = q_scr[r0:r0 + bq, nope_w:nope_w + LANES]
                q2 = jnp.concatenate([
                    jnp.concatenate([jnp.where(lo, qn, 0.0), jnp.where(rope_lanes[2 * t], qr, 0.0)], axis=1),
                    jnp.concatenate([jnp.where(lo, 0.0, qn), jnp.where(rope_lanes[2 * t + 1], qr, 0.0)], axis=1),
                ], axis=0).astype(BF16)
                s = _dot_nt(q2, k_cat) * scale
                m = jnp.max(s, axis=-1, keepdims=True)
                pv = _dot(jnp.exp(s - m).astype(BF16), v_ext)
                o = pv[:, 0:LANES] / pv[:, LANES:LANES + 1]
                y_ref[i * seq_len + r0:i * seq_len + r0 + bq, cs] = _merge_heads(o, lo).astype(BF16)


def _mla(mla_in, q_norm, kv_norm, w_uq, w_uk, w_uv, cache, rope, carried, layer, n_batch, seq_len, nb):
    latent = cache is not None
    aliases = {}
    past = cache[0].shape[2] if latent else 0
    kern = functools.partial(_mla_kernel, seq_len=seq_len, latent=latent, bq=256, nb=nb)
    q_w = MLA_HEADS * (MLA_NOPE + MLA_ROPE)
    kv_w = MLA_HEADS * MLA_NOPE
    in_specs = [
        pl.BlockSpec((nb * seq_len, MLA_IN_TILED), lambda b: (b, 0)),
        pl.BlockSpec((1, 1, MLA_Q_RANK), lambda b: (layer, 0, 0)),
        pl.BlockSpec((1, 1, MLA_KV_RANK), lambda b: (layer, 0, 0)),
        pl.BlockSpec((1, MLA_Q_RANK, q_w), lambda b: (layer, 0, 0)),
        pl.BlockSpec((1, MLA_KV_RANK, kv_w), lambda b: (layer, 0, 0)),
        pl.BlockSpec((1, MLA_KV_RANK, kv_w), lambda b: (layer, 0, 0)),
    ]
    args = [mla_in, q_norm, kv_norm, w_uq, w_uk, w_uv]
    y_shape = jax.ShapeDtypeStruct((n_batch * seq_len, MIX_PART), BF16)
    y_spec = pl.BlockSpec((nb * seq_len, MIX_PART), lambda b: (b, 0))
    if latent:
        assert nb == 1
        in_specs += [
            pl.BlockSpec((1, 1, past, MLA_KV_RANK), lambda b: (b, layer, 0, 0)),
            pl.BlockSpec((1, 1, past, MLA_ROPE), lambda b: (b, layer, 0, 0)),
            pl.BlockSpec((seq_len, LANES), lambda b: (0, 0)),
            pl.BlockSpec((seq_len, LANES), lambda b: (0, 0)),
        ]
        args += [cache[0], cache[1], rope[0], rope[1]]
        out_shape, out_specs = y_shape, y_spec
    else:
        ckv_shape, ckv_spec = _layer_stacked((seq_len, MLA_KV_RANK), n_batch, layer, nb)
        kr_shape, kr_spec = _layer_stacked((MLA_ROPE, seq_len), n_batch, layer, nb)
        out_shape = [y_shape, ckv_shape, kr_shape]
        out_specs = [y_spec, ckv_spec, kr_spec]
        aliases = _carry_specs(in_specs, args, carried, 1)
    keys = past + seq_len
    return pl.pallas_call(
        kern, out_shape=out_shape, grid=(n_batch // nb,), in_specs=in_specs, out_specs=out_specs,
        input_output_aliases=aliases,
        scratch_shapes=[
            pltpu.VMEM((nb, seq_len, q_w), F32),
            pltpu.VMEM((nb, keys, kv_w + LANES), BF16),
            pltpu.VMEM((nb, keys, kv_w), BF16),
        ],
        compiler_params=_params(1),
        name=f"mla_{seq_len}",
    )(*args)


def _dft_matrix(seq_len):
    k = np.arange(seq_len, dtype=np.float64)[:, None]
    j = np.arange(seq_len, dtype=np.float64)[None, :]
    ang = np.pi * k * j / seq_len
    re = np.cos(ang)
    im = -np.sin(ang)
    im[0, :] = np.where(np.arange(seq_len) % 2 == 0, 1.0, -1.0)
    return np.concatenate([re, im], axis=0).astype(np.float32)


def _hyena_positions(seq_len):
    t = np.arange(seq_len, dtype=np.float64) / seq_len
    bands = np.arange(1, HY_BANDS + 1, dtype=np.float64)
    ang = 2.0 * math.pi * t[:, None] * bands
    z = np.concatenate([t[:, None], np.cos(ang), np.sin(ang)], axis=-1)
    return np.pad(z, ((0, 0), (0, HY_POS_PAD - HY_POS_DIM))).astype(np.float32)


def _rope_tables(n_tokens, dim):
    rows = np.repeat(np.arange(n_tokens // GRID_W, dtype=np.float64), GRID_W)
    cols = np.tile(np.arange(GRID_W, dtype=np.float64), n_tokens // GRID_W)
    quarter = dim // 4
    inv = ROPE_BASE ** (-np.arange(quarter, dtype=np.float64) / quarter)
    ang = np.concatenate([rows[:, None] * inv, cols[:, None] * inv], axis=-1)
    cos = np.concatenate([np.cos(ang), np.cos(ang)], axis=-1)
    sin = np.concatenate([-np.sin(ang), np.sin(ang)], axis=-1)
    reps = LANES // dim
    return (np.tile(cos, (1, reps)).astype(np.float32), np.tile(sin, (1, reps)).astype(np.float32))


def kernel(x_prompt, x_sample, state_ret, cache_gqa_k, cache_gqa_v, cache_mla_ckv, cache_mla_krope, c, c_ctx, ada_w, ada_b, norm_g, w_in, ret_decay_logit, ret_gn_g, hy_short_w, hy_short_b, hy_w1, hy_b1, hy_w2, hy_b2, hy_w3, hy_decay, hy_bias, gqa_sink, mla_q_norm, mla_kv_norm, mla_w_uq, mla_w_uk, mla_w_uv, w_out, ffn_w_up, ffn_conv_w, ffn_conv_b, ffn_w_down):
    n_p, len_p, _ = x_prompt.shape
    n_s, len_s, _ = x_sample.shape

    w_in_t = jnp.swapaxes(w_in, 1, 2)
    w_out_b = w_out.astype(BF16)
    w_up_b = ffn_w_up.astype(BF16)
    w_down_b = ffn_w_down.astype(BF16)
    uq = mla_w_uq.reshape(DEPTH, MLA_Q_RANK, MLA_HEADS, MLA_NOPE + MLA_ROPE)
    w_uq_b = jnp.concatenate(
        [uq[..., :MLA_NOPE].reshape(DEPTH, MLA_Q_RANK, MLA_HEADS * MLA_NOPE),
         uq[..., MLA_NOPE:].reshape(DEPTH, MLA_Q_RANK, MLA_HEADS * MLA_ROPE)], axis=-1).astype(BF16)
    w_uk_b = mla_w_uk.astype(BF16)
    w_uv_b = mla_w_uv.astype(BF16)

    cond = jnp.concatenate([c_ctx[None], c, jnp.zeros((N_COND - 1 - n_s, D_MODEL), F32)], axis=0)
    mod = _modulation(cond, ada_w, ada_b).reshape(DEPTH, N_COND, 6, D_MODEL)

    w1p = jnp.pad(hy_w1, ((0, 0), (0, HY_POS_PAD - HY_POS_DIM), (0, 0)))
    b1 = hy_b1.reshape(DEPTH, 1, HY_FILTER_HIDDEN)
    b2 = hy_b2.reshape(DEPTH, 1, HY_FILTER_HIDDEN)
    groups = {}
    for seq_len in (len_p, len_s):
        f32mat = _dft_matrix(seq_len)
        fmat = jnp.asarray(f32mat).astype(BF16)
        fmat_t = jnp.asarray(np.ascontiguousarray(f32mat.T)).astype(BF16)
        tables = _hyena_tables(seq_len, jnp.asarray(_hyena_positions(seq_len)), w1p, b1, hy_w2, b2,
                               hy_w3, hy_decay, fmat)
        groups[seq_len] = (tables, fmat, fmat_t)

    rope_g = tuple(jnp.asarray(t) for t in _rope_tables(len_s, HEAD_DIM))
    rope_m = tuple(jnp.asarray(t) for t in _rope_tables(len_s, MLA_ROPE))

    gn_g = ret_gn_g.reshape(DEPTH, 1, RET_WIDTH)
    short_b = hy_short_b.reshape(DEPTH, 1, HY_IN)
    sink = gqa_sink.reshape(DEPTH, 1, GQA_Q_HEADS)
    q_norm = mla_q_norm.reshape(DEPTH, 1, MLA_Q_RANK)
    kv_norm = mla_kv_norm.reshape(DEPTH, 1, MLA_KV_RANK)
    conv_b = ffn_conv_b.reshape(DEPTH, 1, D_FF)

    xp = x_prompt.reshape(n_p * len_p, D_MODEL)
    xs = x_sample.reshape(n_s * len_s, D_MODEL)
    tm_p = 512
    tm_s = 512
    nb_p = 2
    cond_p = lambda i: 0
    cond_s = lambda i: 1 + (i * tm_s) // len_s
    tm_ffn = 1024
    cond_s_ffn = lambda i: 1 + (i * tm_ffn) // len_s

    st = gkv = ckr = None
    for l in range(DEPTH):
        tables, fmat, fmat_t = groups[len_p]
        ret_in, hy_in, gqa_in, mla_in = _in_proj(xp, mod, norm_g, w_in_t, l, tm_p, cond_p)
        y_ret, *st = _retention(ret_in, ret_decay_logit, gn_g, None, st, l, n_p, len_p, nb_p)
        y_hy = _hyena(hy_in, hy_short_w, short_b, hy_bias, tables, fmat, fmat_t, l, n_p, len_p, nb_p)
        y_gqa, *gkv = _gqa_ctx(gqa_in, sink, gkv, l, n_p, len_p, nb_p)
        y_mla, *ckr = _mla(mla_in, q_norm, kv_norm, w_uq_b, w_uk_b, w_uv_b, None, None, ckr, l, n_p, len_p,
                           nb_p)
        xp = _channel_mix(xp, (y_ret, y_hy, y_gqa, y_mla), mod, norm_g, w_out_b, w_up_b, ffn_conv_w, conv_b,
                          w_down_b, l, tm_ffn, len_p, cond_p)
        tables, fmat, fmat_t = groups[len_s]
        ret_in, hy_in, gqa_in, mla_in = _in_proj(xs, mod, norm_g, w_in_t, l, tm_s, cond_s)
        y_ret = _retention(ret_in, ret_decay_logit, gn_g, state_ret, None, l, n_s, len_s, 1)
        y_hy = _hyena(hy_in, hy_short_w, short_b, hy_bias, tables, fmat, fmat_t, l, n_s, len_s, 1)
        y_gqa = _gqa_win(gqa_in, sink, cache_gqa_k, cache_gqa_v, rope_g[0], rope_g[1], l, n_s, len_s)
        y_mla = _mla(mla_in, q_norm, kv_norm, w_uq_b, w_uk_b, w_uv_b, (cache_mla_ckv, cache_mla_krope),
                     rope_m, None, l, n_s, len_s, 1)
        xs = _channel_mix(xs, (y_ret, y_hy, y_gqa, y_mla), mod, norm_g, w_out_b, w_up_b, ffn_conv_w, conv_b,
                          w_down_b, l, tm_ffn, len_s, cond_s_ffn)

    return (xp.reshape(n_p, len_p, D_MODEL), xs.reshape(n_s, len_s, D_MODEL),
            st[0], jnp.swapaxes(gkv[0], -1, -2), jnp.swapaxes(gkv[1], -1, -2),
            ckr[0], jnp.swapaxes(ckr[1], -1, -2))
```

```python
import functools
import math

import numpy as np
import jax
import jax.numpy as jnp
from jax import lax
from jax.experimental import pallas as pl
from jax.experimental.pallas import tpu as pltpu

F32 = jnp.float32
BF16 = jnp.bfloat16

D_MODEL = 1024
DEPTH = 4
GRID_W = 64
EPS = 1e-6
NEG_INF = -1e30
ROPE_BASE = 10000.0
BLOCK = 128
WINDOW = 128
HEAD_DIM = 64
LANES = 128
RET_HEADS = 4
RET_WIDTH = RET_HEADS * HEAD_DIM
HY_WIDTH = 256
HY_ORDER = 2
HY_BANDS = 8
HY_POS_DIM = 1 + 2 * HY_BANDS
HY_POS_PAD = 32
HY_FILTER_HIDDEN = 64
GQA_Q_HEADS = 4
GQA_KV_HEADS = 2
GQA_GROUPS = GQA_Q_HEADS // GQA_KV_HEADS
MLA_HEADS = 4
MLA_Q_RANK = 256
MLA_KV_RANK = 128
MLA_NOPE = 64
MLA_ROPE = 32
MLA_V = 64
D_FF = 2816
RET_IN = 4 * RET_WIDTH
HY_IN = (HY_ORDER + 1) * HY_WIDTH
GQA_IN = (GQA_Q_HEADS + 2 * GQA_KV_HEADS) * HEAD_DIM
MLA_IN = MLA_Q_RANK + MLA_KV_RANK + MLA_ROPE
MLA_IN_TILED = MLA_Q_RANK + MLA_KV_RANK + MLA_HEADS * MLA_ROPE
IN_WIDTH = RET_IN + HY_IN + GQA_IN + MLA_IN
IN_WIDTH_TILED = RET_IN + HY_IN + GQA_IN + MLA_IN_TILED
GQA_HEAD_ORDER = (0, 2, 1, 3)
MIX_PART = 256
N_COND = 8
FFN_CHUNK = 256
N_FFN_CHUNKS = D_FF // FFN_CHUNK

VMEM_LIMIT = 56 * 1024 * 1024


def _params(n_axes=1):
    return pltpu.CompilerParams(
        dimension_semantics=("arbitrary",) * n_axes, vmem_limit_bytes=VMEM_LIMIT)


def _dot(a, b):
    return jnp.dot(a, b, preferred_element_type=F32)


def _dot_nt(a, b):
    return lax.dot_general(a, b, (((1,), (1,)), ((), ())), preferred_element_type=F32)


def _dot_tn(a, b):
    return lax.dot_general(a, b, (((0,), (0,)), ((), ())), preferred_element_type=F32)


def _rms(x, g):
    return x * lax.rsqrt(jnp.mean(x * x, axis=-1, keepdims=True) + EPS) * g


def _sigmoid(x):
    return 1.0 / (1.0 + jnp.exp(-x))


def _silu(x):
    return x * _sigmoid(x)


def _shift_rows(x, seq_len):
    n = x.shape[0]
    assert seq_len & (seq_len - 1) == 0
    pos = lax.broadcasted_iota(jnp.int32, (n, 1), 0) & (seq_len - 1)
    prev = jnp.where(pos != 0, pltpu.roll(x, 1, axis=0), 0.0)
    nxt = jnp.where(pos != seq_len - 1, pltpu.roll(x, n - 1, axis=0), 0.0)
    return prev, nxt


def _rot_half(x, half):
    lane = lax.broadcasted_iota(jnp.int32, x.shape, 1) & (2 * half - 1)
    return jnp.where(lane < half, pltpu.roll(x, LANES - half, axis=1), pltpu.roll(x, half, axis=1))


def _lane_iota():
    return lax.broadcasted_iota(jnp.int32, (1, LANES), 1)


def _split_heads(t, lo):
    return jnp.concatenate([jnp.where(lo, t, 0.0), jnp.where(lo, 0.0, t)], axis=0)


def _merge_heads(o, lo):
    n = o.shape[0] // 2
    return jnp.where(lo, o[:n], o[n:])


def _per_head(vals, width):
    lane = lax.broadcasted_iota(jnp.int32, (1, width), 1)
    out = vals[-1]
    for h in range(len(vals) - 2, -1, -1):
        out = jnp.where(lane < (h + 1) * HEAD_DIM, vals[h], out)
    return out


def _mod_kernel(cond_ref, w_ref, b_ref, out_ref):
    s = _silu(cond_ref[...]).astype(BF16)
    out_ref[0] = _dot(s, w_ref[0].astype(BF16)) + b_ref[0]


def _modulation(cond, ada_w, ada_b):
    tn = 1536
    return pl.pallas_call(
        _mod_kernel,
        out_shape=jax.ShapeDtypeStruct((DEPTH, N_COND, 6 * D_MODEL), F32),
        grid=(DEPTH, 6 * D_MODEL // tn),
        in_specs=[
            pl.BlockSpec((N_COND, D_MODEL), lambda l, j: (0, 0)),
            pl.BlockSpec((1, D_MODEL, tn), lambda l, j: (l, 0, j)),
            pl.BlockSpec((1, 1, tn), lambda l, j: (l, 0, j)),
        ],
        out_specs=pl.BlockSpec((1, N_COND, tn), lambda l, j: (l, 0, j)),
        compiler_params=_params(2),
        name="modulation",
    )(cond, ada_w, ada_b.reshape(DEPTH, 1, 6 * D_MODEL))


def _filter_kernel(z_ref, w1_ref, b1_ref, w2_ref, b2_ref, w3_ref, dec_ref, f_ref, out_ref, *, seq_len):
    hi = lax.Precision.HIGHEST
    h = jnp.sin(jnp.dot(z_ref[...], w1_ref[0], precision=hi, preferred_element_type=F32) + b1_ref[0])
    h = jnp.sin(jnp.dot(h, w2_ref[0], precision=hi, preferred_element_type=F32) + b2_ref[0])
    h = jnp.dot(h, w3_ref[0], precision=hi, preferred_element_type=F32)
    row = lax.broadcasted_iota(jnp.int32, (seq_len, 1), 0)
    t = row.astype(F32) / seq_len
    dec = jnp.abs(dec_ref[0])
    win_f = jnp.exp(-t * dec[0:1, :])
    win_b = jnp.exp(-t * dec[1:2, :])
    fm = f_ref[...]
    inv = 1.0 / seq_len
    scale = jnp.where(row == 0, 0.5 * inv, inv)
    for o in range(HY_ORDER):
        base = o * 2 * HY_WIDTH
        hf = h[:, base:base + HY_WIDTH] * win_f
        hb = h[:, base + HY_WIDTH:base + 2 * HY_WIDTH] * win_b
        hbs = jnp.where(row != 0, pltpu.roll(hb, 1, axis=0), 0.0)
        sf = _dot(fm, hf.astype(BF16))
        sb = _dot(fm, hbs.astype(BF16))
        g_re = sf[:seq_len] + sb[:seq_len]
        g_im = sf[seq_len:] - sb[seq_len:]
        g_ny = sf[seq_len:] + sb[seq_len:]
        a = g_re * scale
        out_ref[0, 3 * o] = a
        out_ref[0, 3 * o + 1] = jnp.where(row == 0, g_ny * (0.5 * inv), a)
        out_ref[0, 3 * o + 2] = jnp.where(row == 0, 0.0, g_im * inv)


def _hyena_tables(seq_len, z, w1p, b1, w2, b2, w3, decay, fmat):
    hid = HY_FILTER_HIDDEN
    wide = HY_ORDER * 2 * HY_WIDTH
    return pl.pallas_call(
        functools.partial(_filter_kernel, seq_len=seq_len),
        out_shape=jax.ShapeDtypeStruct((DEPTH, 3 * HY_ORDER, seq_len, HY_WIDTH), F32),
        grid=(DEPTH,),
        in_specs=[
            pl.BlockSpec((seq_len, HY_POS_PAD), lambda l: (0, 0)),
            pl.BlockSpec((1, HY_POS_PAD, hid), lambda l: (l, 0, 0)),
            pl.BlockSpec((1, 1, hid), lambda l: (l, 0, 0)),
            pl.BlockSpec((1, hid, hid), lambda l: (l, 0, 0)),
            pl.BlockSpec((1, 1, hid), lambda l: (l, 0, 0)),
            pl.BlockSpec((1, hid, wide), lambda l: (l, 0, 0)),
            pl.BlockSpec((1, 2, HY_WIDTH), lambda l: (l, 0, 0)),
            pl.BlockSpec((2 * seq_len, seq_len), lambda l: (0, 0)),
        ],
        out_specs=pl.BlockSpec((1, 3 * HY_ORDER, seq_len, HY_WIDTH), lambda l: (l, 0, 0, 0)),
        compiler_params=_params(1),
        name=f"hyena_tables_{seq_len}",
    )(z, w1p, b1, w2, b2, w3, decay, fmat)


def _layer_stacked(tail, n_batch, layer, nb):
    zeros = (0,) * len(tail)
    n_layers = DEPTH if layer == 0 else 1
    return (jax.ShapeDtypeStruct((n_batch, DEPTH) + tail, F32),
            pl.BlockSpec((nb, n_layers) + tail, lambda b: (b, layer) + zeros))


def _zero_other_layers(*out_refs):
    for ref in out_refs:
        if ref.shape[1] > 1:
            ref[:, 1:] = jnp.zeros((ref.shape[0], ref.shape[1] - 1) + ref.shape[2:], ref.dtype)


def _carry_specs(in_specs, args, carried, first_out):
    aliases = {}
    if carried is not None:
        for k, arr in enumerate(carried):
            aliases[len(args)] = first_out + k
            in_specs.append(pl.BlockSpec(memory_space=pl.ANY))
            args.append(arr)
    return aliases


def _in_proj_kernel(x_ref, mod_ref, g_ref, w_ref, ret_ref, hy_ref, gqa_ref, mla_ref, w_scr):
    @pl.when(pl.program_id(0) == 0)
    def _():
        g0 = RET_IN + HY_IN
        m0 = g0 + GQA_IN
        kr0 = m0 + MLA_Q_RANK + MLA_KV_RANK
        step = 2 * LANES
        for r0 in list(range(0, g0, step)) + list(range(g0 + GQA_Q_HEADS * HEAD_DIM, kr0, LANES)):
            n = step if r0 < g0 else LANES
            w_scr[r0:r0 + n, :] = w_ref[0, r0:r0 + n, :].astype(BF16)
        for dst, j in enumerate(GQA_HEAD_ORDER):
            w_scr[g0 + dst * HEAD_DIM:g0 + (dst + 1) * HEAD_DIM, :] = (
                w_ref[0, g0 + j * HEAD_DIM:g0 + (j + 1) * HEAD_DIM, :].astype(BF16))
        kr = w_ref[0, kr0:kr0 + MLA_ROPE, :].astype(BF16)
        for hd in range(MLA_HEADS):
            w_scr[kr0 + hd * MLA_ROPE:kr0 + (hd + 1) * MLA_ROPE, :] = kr

    shift = mod_ref[0, 0, 0:1, :]
    scale = mod_ref[0, 0, 1:2, :]
    h = (_rms(x_ref[...], g_ref[0, 0:1, :]) * (1.0 + scale) + shift).astype(BF16)
    c0 = 0
    for ref in (ret_ref, hy_ref, gqa_ref, mla_ref):
        width = ref.shape[1]
        ref[...] = _dot_nt(h, w_scr[c0:c0 + width, :])
        c0 += width


def _in_proj(x, mod, norm_g, w_in, layer, tm, cond_of_tile):
    rows = x.shape[0]
    widths = (RET_IN, HY_IN, GQA_IN, MLA_IN_TILED)
    return pl.pallas_call(
        _in_proj_kernel,
        out_shape=[jax.ShapeDtypeStruct((rows, w), F32) for w in widths],
        grid=(rows // tm,),
        in_specs=[
            pl.BlockSpec((tm, D_MODEL), lambda i: (i, 0)),
            pl.BlockSpec((1, 1, 6, D_MODEL), lambda i: (layer, cond_of_tile(i), 0, 0)),
            pl.BlockSpec((1, 4, D_MODEL), lambda i: (layer, 0, 0)),
            pl.BlockSpec((1, IN_WIDTH, D_MODEL), lambda i: (layer, 0, 0), pipeline_mode=pl.Buffered(1)),
        ],
        out_specs=[pl.BlockSpec((tm, w), lambda i: (i, 0)) for w in widths],
        scratch_shapes=[pltpu.VMEM((IN_WIDTH_TILED, D_MODEL), BF16)],
        compiler_params=_params(1),
        name="in_proj",
    )(x, mod, norm_g, w_in)


def _channel_kernel(x_ref, m0_ref, m1_ref, m2_ref, m3_ref, mod_ref, g_ref, wo_ref, wu_ref, cw_ref, cb_ref,
                    wd_ref, out_ref, act_ref, *, seq_len):
    y = None
    for i, m_ref in enumerate((m0_ref, m1_ref, m2_ref, m3_ref)):
        r0 = i * MIX_PART
        if i == 2:
            w = jnp.concatenate([wo_ref[0, r0 + j * HEAD_DIM:r0 + (j + 1) * HEAD_DIM, :]
                                 for j in GQA_HEAD_ORDER], axis=0)
        else:
            w = wo_ref[0, r0:r0 + MIX_PART, :]
        part = _dot(m_ref[...], w)
        y = part if y is None else y + part
    gate1 = mod_ref[0, 0, 2:3, :]
    shift2 = mod_ref[0, 0, 3:4, :]
    scale2 = mod_ref[0, 0, 4:5, :]
    xm = x_ref[...] + gate1 * _rms(y, g_ref[0, 1:2, :])
    out_ref[...] = xm
    h2 = (_rms(xm, g_ref[0, 2:3, :]) * (1.0 + scale2) + shift2).astype(BF16)
    for c in range(N_FFN_CHUNKS):
        sl = slice(c * FFN_CHUNK, (c + 1) * FFN_CHUNK)
        gate = _dot(h2, wu_ref[0, :, sl])
        up = _dot(h2, wu_ref[0, :, D_FF + c * FFN_CHUNK:D_FF + (c + 1) * FFN_CHUNK])
        prev, nxt = _shift_rows(gate, seq_len)
        gate = (prev * cw_ref[0, 0:1, sl] + gate * cw_ref[0, 1:2, sl] + nxt * cw_ref[0, 2:3, sl]
                + cb_ref[0, :, sl])
        act_ref[:, sl] = (_silu(gate) * up).astype(BF16)
    ffn = _dot(act_ref[...], wd_ref[0])
    gate2 = mod_ref[0, 0, 5:6, :]
    out_ref[...] = out_ref[...] + gate2 * _rms(ffn, g_ref[0, 3:4, :])


def _channel_mix(x, mixes, mod, norm_g, w_out, w_up, conv_w, conv_b, w_down, layer, tm, seq_len, cond_of_tile):
    rows = x.shape[0]
    resident = dict(pipeline_mode=pl.Buffered(1))
    return pl.pallas_call(
        functools.partial(_channel_kernel, seq_len=seq_len),
        out_shape=jax.ShapeDtypeStruct((rows, D_MODEL), F32),
        grid=(rows // tm,),
        in_specs=[pl.BlockSpec((tm, D_MODEL), lambda i: (i, 0))]
        + [pl.BlockSpec((tm, MIX_PART), lambda i: (i, 0))] * 4
        + [
            pl.BlockSpec((1, 1, 6, D_MODEL), lambda i: (layer, cond_of_tile(i), 0, 0)),
            pl.BlockSpec((1, 4, D_MODEL), lambda i: (layer, 0, 0)),
            pl.BlockSpec((1, D_MODEL, D_MODEL), lambda i: (layer, 0, 0), **resident),
            pl.BlockSpec((1, D_MODEL, 2 * D_FF), lambda i: (layer, 0, 0), **resident),
            pl.BlockSpec((1, 3, D_FF), lambda i: (layer, 0, 0)),
            pl.BlockSpec((1, 1, D_FF), lambda i: (layer, 0, 0)),
            pl.BlockSpec((1, D_FF, D_MODEL), lambda i: (layer, 0, 0), **resident),
        ],
        out_specs=pl.BlockSpec((tm, D_MODEL), lambda i: (i, 0)),
        scratch_shapes=[pltpu.VMEM((tm, D_FF), BF16)],
        compiler_params=_params(1),
        name="channel_mix",
    )(x, *mixes, mod, norm_g, w_out, w_up, conv_w, conv_b, w_down)


def _log_gamma(dl_ref):
    dl = dl_ref[0]
    return jnp.minimum(dl, 0.0) - jnp.log(1.0 + jnp.exp(-jnp.abs(dl)))


def _pair_decay(lg, t, r0, bq, seq_len):
    top = lax.broadcasted_iota(jnp.int32, (2 * bq, 1), 0) < bq
    lgf = jnp.where(top, lg[0:1, 2 * t:2 * t + 1], lg[0:1, 2 * t + 1:2 * t + 2])
    lgb = jnp.where(top, lg[1:2, 2 * t:2 * t + 1], lg[1:2, 2 * t + 1:2 * t + 2])
    rowf = ((lax.broadcasted_iota(jnp.int32, (2 * bq, 1), 0) & (bq - 1)) + r0).astype(F32)
    colf = lax.broadcasted_iota(jnp.int32, (1, seq_len), 1).astype(F32)
    lag = rowf - colf
    decay = jnp.exp(jnp.where(lag >= 0.0, lag * lgf, -lag * lgb))
    return jnp.where(lag == 0.0, 2.0, decay)


def _head_norm(o, lo):
    inv = 1.0 / HEAD_DIM
    s_all = jnp.sum(o, axis=-1, keepdims=True)
    s_lo = jnp.sum(jnp.where(lo, o, 0.0), axis=-1, keepdims=True)
    d = o - jnp.where(lo, s_lo, s_all - s_lo) * inv
    d2 = d * d
    v_all = jnp.sum(d2, axis=-1, keepdims=True)
    v_lo = jnp.sum(jnp.where(lo, d2, 0.0), axis=-1, keepdims=True)
    return d * lax.rsqrt(jnp.where(lo, v_lo, v_all - v_lo) * inv + EPS)


def _ret_ctx_kernel(*refs, seq_len, nb):
    x_ref, dl_ref, gn_ref, *_, y_ref, st_ref, dec_scr, kdec_scr = refs
    lo = _lane_iota() < HEAD_DIM
    _zero_other_layers(st_ref)

    @pl.when(pl.program_id(0) == 0)
    def _():
        lg = _log_gamma(dl_ref)
        posf = lax.broadcasted_iota(jnp.int32, (seq_len, 1), 0).astype(F32)
        lgf = _per_head([lg[0:1, h:h + 1] for h in range(RET_HEADS)], RET_WIDTH)
        lgb = _per_head([lg[1:2, h:h + 1] for h in range(RET_HEADS)], RET_WIDTH)
        kdec_scr[0] = jnp.exp((seq_len - 1.0 - posf) * lgf)
        kdec_scr[1] = jnp.exp(posf * lgb)
        for t in range(RET_HEADS // 2):
            dec_scr[t] = _pair_decay(lg, t, 0, seq_len, seq_len)

    for i in range(nb):
        rows = slice(i * seq_len, (i + 1) * seq_len)
        for t in range(RET_HEADS // 2):
            cs = slice(t * LANES, (t + 1) * LANES)
            q = x_ref[rows, cs]
            k = x_ref[rows, RET_WIDTH + t * LANES:RET_WIDTH + (t + 1) * LANES] * (HEAD_DIM ** -0.5)
            vb = x_ref[rows, 2 * RET_WIDTH + t * LANES:2 * RET_WIDTH + (t + 1) * LANES].astype(BF16)
            gate = x_ref[rows, 3 * RET_WIDTH + t * LANES:3 * RET_WIDTH + (t + 1) * LANES]
            s = _dot_nt(_split_heads(q, lo).astype(BF16), k.astype(BF16)) * dec_scr[t]
            o = _merge_heads(_dot(s.astype(BF16), vb), lo)
            y = _silu(gate) * (_head_norm(o, lo) * gn_ref[0, :, cs])
            y_ref[rows, cs] = y.astype(BF16)
            for d in range(2):
                st = _dot_tn((k * kdec_scr[d, :, cs]).astype(BF16), vb)
                st_ref[i, 0, d, 2 * t] = st[0:HEAD_DIM, 0:HEAD_DIM]
                st_ref[i, 0, d, 2 * t + 1] = st[HEAD_DIM:LANES, HEAD_DIM:LANES]


def _ret_lat_kernel(x_ref, dl_ref, gn_ref, s0_ref, y_ref, *, seq_len, bq):
    lo = _lane_iota() < HEAD_DIM
    lg = _log_gamma(dl_ref)
    zero = jnp.zeros((HEAD_DIM, HEAD_DIM), F32)
    for t in range(RET_HEADS // 2):
        cs = slice(t * LANES, (t + 1) * LANES)
        kb = (x_ref[:, RET_WIDTH + t * LANES:RET_WIDTH + (t + 1) * LANES] * (HEAD_DIM ** -0.5)).astype(BF16)
        vb = x_ref[:, 2 * RET_WIDTH + t * LANES:2 * RET_WIDTH + (t + 1) * LANES].astype(BF16)
        lgf = _per_head([lg[0:1, 2 * t:2 * t + 1], lg[0:1, 2 * t + 1:2 * t + 2]], LANES)
        lgb = _per_head([lg[1:2, 2 * t:2 * t + 1], lg[1:2, 2 * t + 1:2 * t + 2]], LANES)
        s0 = []
        for d in range(2):
            a = s0_ref[0, 0, d, 2 * t]
            b = s0_ref[0, 0, d, 2 * t + 1]
            s0.append(jnp.concatenate([jnp.concatenate([a, zero], axis=1),
                                       jnp.concatenate([zero, b], axis=1)], axis=0).astype(BF16))
        for r0 in range(0, seq_len, bq):
            q = x_ref[r0:r0 + bq, cs]
            s = _dot_nt(_split_heads(q, lo).astype(BF16), kb) * _pair_decay(lg, t, r0, bq, seq_len)
            o = _merge_heads(_dot(s.astype(BF16), vb), lo)
            rowf = (lax.broadcasted_iota(jnp.int32, (bq, 1), 0) + r0).astype(F32)
            qf = (q * jnp.exp((rowf + 1.0) * lgf)).astype(BF16)
            qr = (q * jnp.exp((seq_len - rowf) * lgb)).astype(BF16)
            o = o + _dot(qf, s0[0]) + _dot(qr, s0[1])
            gate = x_ref[r0:r0 + bq, 3 * RET_WIDTH + t * LANES:3 * RET_WIDTH + (t + 1) * LANES]
            y = _silu(gate) * (_head_norm(o, lo) * gn_ref[0, :, cs])
            y_ref[r0:r0 + bq, cs] = y.astype(BF16)


def _retention(ret_in, decay_logit, gn_g, state0, carried, layer, n_batch, seq_len, nb):
    latent = state0 is not None
    in_specs = [
        pl.BlockSpec((nb * seq_len, RET_IN), lambda b: (b, 0)),
        pl.BlockSpec((1, 2, RET_HEADS), lambda b: (layer, 0, 0)),
        pl.BlockSpec((1, 1, RET_WIDTH), lambda b: (layer, 0, 0)),
    ]
    args = [ret_in, decay_logit, gn_g]
    y_shape = jax.ShapeDtypeStruct((n_batch * seq_len, MIX_PART), BF16)
    y_spec = pl.BlockSpec((nb * seq_len, MIX_PART), lambda b: (b, 0))
    if latent:
        assert nb == 1
        kern = functools.partial(_ret_lat_kernel, seq_len=seq_len, bq=256)
        in_specs.append(pl.BlockSpec((1, 1, 2, RET_HEADS, HEAD_DIM, HEAD_DIM),
                                     lambda b: (b, layer, 0, 0, 0, 0)))
        args.append(state0)
        out_shape, out_specs, aliases, scratch = y_shape, y_spec, {}, []
    else:
        kern = functools.partial(_ret_ctx_kernel, seq_len=seq_len, nb=nb)
        st_shape, st_spec = _layer_stacked((2, RET_HEADS, HEAD_DIM, HEAD_DIM), n_batch, layer, nb)
        out_shape = [y_shape, st_shape]
        out_specs = [y_spec, st_spec]
        aliases = _carry_specs(in_specs, args, carried, 1)
        scratch = [pltpu.VMEM((RET_HEADS // 2, 2 * seq_len, seq_len), F32),
                   pltpu.VMEM((2, seq_len, RET_WIDTH), F32)]
    return pl.pallas_call(
        kern, out_shape=out_shape, grid=(n_batch // nb,), in_specs=in_specs, out_specs=out_specs,
        input_output_aliases=aliases, scratch_shapes=scratch,
        compiler_params=_params(1), name=f"retention_{seq_len}",
    )(*args)


def _hyena_kernel(x_ref, sw_ref, sb_ref, bias_ref, tab_ref, f_ref, ft_ref, y_ref, *, seq_len, nb):
    for i in range(nb):
        rows = slice(i * seq_len, (i + 1) * seq_len)
        x = x_ref[rows, :]
        prev, nxt = _shift_rows(x, seq_len)
        u = prev * sw_ref[0, 0:1, :] + x * sw_ref[0, 1:2, :] + nxt * sw_ref[0, 2:3, :] + sb_ref[0]
        x1 = u[:, 0:HY_WIDTH]
        x2 = u[:, HY_WIDTH:2 * HY_WIDTH]
        z = u[:, 2 * HY_WIDTH:3 * HY_WIDTH]
        for o, gate in enumerate((x1, x2)):
            spec = _dot(f_ref[...], z.astype(BF16))
            s_re = spec[:seq_len]
            s_im = spec[seq_len:]
            a = tab_ref[0, 3 * o]
            a_ny = tab_ref[0, 3 * o + 1]
            b = tab_ref[0, 3 * o + 2]
            y_re = (s_re * a - s_im * b).astype(BF16)
            y_im = (s_re * b + s_im * a_ny).astype(BF16)
            conv = _dot(ft_ref[:, 0:seq_len], y_re) + _dot(ft_ref[:, seq_len:2 * seq_len], y_im)
            z = gate * (conv + z * bias_ref[0, o:o + 1, :])
        y_ref[rows, :] = z.astype(BF16)


def _hyena(hy_in, short_w, short_b, hy_bias, tables, fmat, fmat_t, layer, n_batch, seq_len, nb):
    const = dict(pipeline_mode=pl.Buffered(1))
    return pl.pallas_call(
        functools.partial(_hyena_kernel, seq_len=seq_len, nb=nb),
        out_shape=jax.ShapeDtypeStruct((n_batch * seq_len, MIX_PART), BF16),
        grid=(n_batch // nb,),
        in_specs=[
            pl.BlockSpec((nb * seq_len, HY_IN), lambda b: (b, 0)),
            pl.BlockSpec((1, 3, HY_IN), lambda b: (layer, 0, 0)),
            pl.BlockSpec((1, 1, HY_IN), lambda b: (layer, 0, 0)),
            pl.BlockSpec((1, HY_ORDER, HY_WIDTH), lambda b: (layer, 0, 0)),
            pl.BlockSpec((1, 3 * HY_ORDER, seq_len, HY_WIDTH), lambda b: (layer, 0, 0, 0), **const),
            pl.BlockSpec((2 * seq_len, seq_len), lambda b: (0, 0), **const),
            pl.BlockSpec((seq_len, 2 * seq_len), lambda b: (0, 0), **const),
        ],
        out_specs=pl.BlockSpec((nb * seq_len, MIX_PART), lambda b: (b, 0)),
        compiler_params=_params(1),
        name=f"hyena_{seq_len}",
    )(hy_in, short_w, short_b, hy_bias, tables, fmat, fmat_t)


def _pair_sink(sink_ref, g, n):
    top = lax.broadcasted_iota(jnp.int32, (2 * n, 1), 0) < n
    return jnp.where(top, sink_ref[0, :, g:g + 1], sink_ref[0, :, GQA_GROUPS + g:GQA_GROUPS + g + 1])


def _gqa_ctx_kernel(*refs, seq_len, nb):
    x_ref, sink_ref, *_, y_ref, k_out_ref, v_out_ref = refs
    nq = GQA_Q_HEADS * HEAD_DIM
    scale = HEAD_DIM ** -0.5
    head_a = lax.broadcasted_iota(jnp.int32, (LANES, 1), 0) < HEAD_DIM
    first = lax.broadcasted_iota(jnp.int32, (1, 2 * seq_len), 1) < seq_len
    ones = jnp.ones((LANES, seq_len), BF16)
    _zero_other_layers(k_out_ref, v_out_ref)
    for i in range(nb):
        rows = slice(i * seq_len, (i + 1) * seq_len)
        k = x_ref[rows, nq:nq + LANES]
        k_t = k.T
        v_t = x_ref[rows, nq + LANES:nq + 2 * LANES].T
        for kv in range(GQA_KV_HEADS):
            k_out_ref[i, 0, kv] = k_t[kv * HEAD_DIM:(kv + 1) * HEAD_DIM, :]
            v_out_ref[i, 0, kv] = v_t[kv * HEAD_DIM:(kv + 1) * HEAD_DIM, :]
        kb = k.astype(BF16)
        v_ext = jnp.concatenate([v_t.astype(BF16), ones], axis=0)
        for g in range(GQA_GROUPS):
            cs = slice(g * LANES, (g + 1) * LANES)
            q_t = x_ref[rows, cs].T
            q2 = jnp.concatenate([jnp.where(head_a, q_t, 0.0), jnp.where(head_a, 0.0, q_t)], axis=1)
            s = _dot(kb, q2.astype(BF16)) * scale
            sink = jnp.where(first, sink_ref[0, :, g:g + 1], sink_ref[0, :, GQA_GROUPS + g:GQA_GROUPS + g + 1])
            m = jnp.maximum(jnp.max(s, axis=0, keepdims=True), sink)
            pv = _dot(v_ext, jnp.exp(s - m).astype(BF16))
            o = pv[0:LANES, :] / (pv[LANES:LANES + 1, :] + jnp.exp(sink - m))
            o = jnp.where(head_a, o[:, 0:seq_len], o[:, seq_len:2 * seq_len])
            y_ref[rows, cs] = o.T.astype(BF16)


def _gqa_ctx(gqa_in, sink, carried, layer, n_batch, seq_len, nb):
    kv_shape, kv_spec = _layer_stacked((GQA_KV_HEADS, HEAD_DIM, seq_len), n_batch, layer, nb)
    in_specs = [
        pl.BlockSpec((nb * seq_len, GQA_IN), lambda b: (b, 0)),
        pl.BlockSpec((1, 1, GQA_Q_HEADS), lambda b: (layer, 0, 0)),
    ]
    args = [gqa_in, sink]
    aliases = _carry_specs(in_specs, args, carried, 1)
    return pl.pallas_call(
        functools.partial(_gqa_ctx_kernel, seq_len=seq_len, nb=nb),
        out_shape=[jax.ShapeDtypeStruct((n_batch * seq_len, MIX_PART), BF16), kv_shape, kv_shape],
        grid=(n_batch // nb,),
        in_specs=in_specs,
        out_specs=[pl.BlockSpec((nb * seq_len, MIX_PART), lambda b: (b, 0)), kv_spec, kv_spec],
        input_output_aliases=aliases,
        compiler_params=_params(1),
        name=f"gqa_ctx_{seq_len}",
    )(*args)


def _gqa_win_kernel(x_ref, sink_ref, kc_ref, vc_ref, cos_ref, sin_ref, y_ref,
                    q_scr, k_scr, v_scr, kc_scr, vc_scr, *, seq_len):
    nq = GQA_Q_HEADS * HEAD_DIM
    scale = HEAD_DIM ** -0.5
    lo = _lane_iota() < HEAD_DIM
    past = kc_scr.shape[0]
    cos = cos_ref[...]
    sin = sin_ref[...]

    def rope(t):
        return t * cos + _rot_half(t, HEAD_DIM // 2) * sin

    for g in range(GQA_GROUPS):
        q = rope(x_ref[:, g * LANES:(g + 1) * LANES])
        q_scr[g, 0] = jnp.where(lo, q, 0.0).astype(BF16)
        q_scr[g, 1] = jnp.where(lo, 0.0, q).astype(BF16)
    zeros = jnp.zeros((BLOCK, LANES), BF16)
    for r0 in (0, BLOCK + seq_len):
        k_scr[r0:r0 + BLOCK, :] = zeros
        v_scr[r0:r0 + BLOCK, 0:LANES] = zeros
    k_scr[BLOCK:BLOCK + seq_len, :] = rope(x_ref[:, nq:nq + LANES]).astype(BF16)
    v_scr[BLOCK:BLOCK + seq_len, 0:LANES] = x_ref[:, nq + LANES:nq + 2 * LANES].astype(BF16)
    v_scr[:, LANES:2 * LANES] = jnp.ones((seq_len + 2 * BLOCK, LANES), BF16)
    kc_scr[...] = jnp.concatenate([kc_ref[0, 0, 0], kc_ref[0, 0, 1]], axis=1).astype(BF16)
    vc_scr[:, 0:LANES] = jnp.concatenate([vc_ref[0, 0, 0], vc_ref[0, 0, 1]], axis=1).astype(BF16)
    vc_scr[:, LANES:2 * LANES] = jnp.ones((past, LANES), BF16)

    ii = lax.broadcasted_iota(jnp.int32, (2 * BLOCK, 3 * BLOCK), 0) & (BLOCK - 1)
    jj = lax.broadcasted_iota(jnp.int32, (2 * BLOCK, 3 * BLOCK), 1)
    band = (jj >= ii) & (jj <= ii + 2 * WINDOW)

    def block(n, carry):
        r0 = pl.multiple_of(n * BLOCK, BLOCK)
        kpos = jj + (n - 1) * BLOCK
        valid = band & (kpos >= 0) & (kpos < seq_len)
        kw = k_scr[pl.ds(r0, 3 * BLOCK), :]
        vw = v_scr[pl.ds(r0, 3 * BLOCK), :]
        for g in range(GQA_GROUPS):
            q2 = jnp.concatenate([q_scr[g, 0, pl.ds(r0, BLOCK), :], q_scr[g, 1, pl.ds(r0, BLOCK), :]], axis=0)
            s_win = jnp.where(valid, _dot_nt(q2, kw) * scale, NEG_INF)
            s_ctx = _dot_nt(q2, kc_scr[...]) * scale
            sink = _pair_sink(sink_ref, g, BLOCK)
            m = jnp.maximum(jnp.maximum(jnp.max(s_win, axis=-1, keepdims=True),
                                        jnp.max(s_ctx, axis=-1, keepdims=True)), sink)
            pv = (_dot(jnp.exp(s_win - m).astype(BF16), vw)
                  + _dot(jnp.exp(s_ctx - m).astype(BF16), vc_scr[...]))
            den = pv[:, LANES:LANES + 1] + jnp.exp(sink - m)
            y_ref[pl.ds(r0, BLOCK), g * LANES:(g + 1) * LANES] = _merge_heads(
                pv[:, 0:LANES] / den, lo).astype(BF16)
        return carry

    lax.fori_loop(0, seq_len // BLOCK, block, 0)


def _gqa_win(gqa_in, sink, cache_k, cache_v, cos, sin, layer, n_batch, seq_len):
    past = cache_k.shape[3]
    return pl.pallas_call(
        functools.partial(_gqa_win_kernel, seq_len=seq_len),
        out_shape=jax.ShapeDtypeStruct((n_batch * seq_len, MIX_PART), BF16),
        grid=(n_batch,),
        in_specs=[
            pl.BlockSpec((seq_len, GQA_IN), lambda b: (b, 0)),
            pl.BlockSpec((1, 1, GQA_Q_HEADS), lambda b: (layer, 0, 0)),
            pl.BlockSpec((1, 1, GQA_KV_HEADS, past, HEAD_DIM), lambda b: (b, layer, 0, 0, 0)),
            pl.BlockSpec((1, 1, GQA_KV_HEADS, past, HEAD_DIM), lambda b: (b, layer, 0, 0, 0)),
            pl.BlockSpec((seq_len, LANES), lambda b: (0, 0)),
            pl.BlockSpec((seq_len, LANES), lambda b: (0, 0)),
        ],
        out_specs=pl.BlockSpec((seq_len, MIX_PART), lambda b: (b, 0)),
        scratch_shapes=[
            pltpu.VMEM((GQA_GROUPS, 2, seq_len, LANES), BF16),
            pltpu.VMEM((seq_len + 2 * BLOCK, LANES), BF16),
            pltpu.VMEM((seq_len + 2 * BLOCK, 2 * LANES), BF16),
            pltpu.VMEM((past, LANES), BF16),
            pltpu.VMEM((past, 2 * LANES), BF16),
        ],
        compiler_params=_params(1),
        name=f"gqa_win_{seq_len}",
    )(gqa_in, sink, cache_k, cache_v, cos, sin)


def _mla_kernel(*refs, seq_len, latent, bq, nb):
    if latent:
        (x_ref, qn_ref, kvn_ref, wq_ref, wk_ref, wv_ref, ckv_c_ref, kr_c_ref, cos_ref, sin_ref,
         y_ref, q_all, k_all, v_all) = refs
        past = ckv_c_ref.shape[2]
    else:
        (x_ref, qn_ref, kvn_ref, wq_ref, wk_ref, wv_ref, *_,
         y_ref, ckv_out_ref, kr_out_ref, q_all, k_all, v_all) = refs
        past = 0
        _zero_other_layers(ckv_out_ref, kr_out_ref)
    keys = past + seq_len
    nope_w = MLA_HEADS * MLA_NOPE
    scale = (MLA_NOPE + MLA_ROPE) ** -0.5
    lane = _lane_iota()
    lo = lane < HEAD_DIM
    rope_lanes = [(lane >= h * MLA_ROPE) & (lane < (h + 1) * MLA_ROPE) for h in range(MLA_HEADS)]
    for i in range(nb):
        rows = slice(i * seq_len, (i + 1) * seq_len)
        q_scr, k_scr, v_scr = q_all.at[i], k_all.at[i], v_all.at[i]
        q_lat = x_ref[rows, 0:MLA_Q_RANK]
        kv_lat = x_ref[rows, MLA_Q_RANK:MLA_Q_RANK + MLA_KV_RANK]
        k_rope = x_ref[rows, MLA_Q_RANK + MLA_KV_RANK:MLA_IN_TILED]
        mq = _dot(_rms(q_lat, qn_ref[0]).astype(BF16), wq_ref[0])
        ckv = _rms(kv_lat, kvn_ref[0])
        q_rope = mq[:, nope_w:nope_w + LANES]
        if latent:
            cos = cos_ref[...]
            sin = sin_ref[...]
            q_rope = q_rope * cos + _rot_half(q_rope, MLA_ROPE // 2) * sin
            k_rope = k_rope * cos + _rot_half(k_rope, MLA_ROPE // 2) * sin
            ckv_c = ckv_c_ref[0, 0].astype(BF16)
            k_scr[0:past, 0:nope_w] = _dot(ckv_c, wk_ref[0]).astype(BF16)
            v_scr[0:past, :] = _dot(ckv_c, wv_ref[0]).astype(BF16)
            kr_c = kr_c_ref[0, 0]
            k_scr[0:past, nope_w:nope_w + LANES] = jnp.concatenate([kr_c] * MLA_HEADS, axis=1).astype(BF16)
        else:
            ckv_out_ref[i, 0] = ckv
            kr_out_ref[i, 0] = k_rope.T[0:MLA_ROPE, :]
        q_scr[:, 0:nope_w] = mq[:, 0:nope_w]
        q_scr[:, nope_w:nope_w + LANES] = q_rope
        ckv_b = ckv.astype(BF16)
        k_scr[past:keys, 0:nope_w] = _dot(ckv_b, wk_ref[0]).astype(BF16)
        v_scr[past:keys, :] = _dot(ckv_b, wv_ref[0]).astype(BF16)
        k_scr[past:keys, nope_w:nope_w + LANES] = k_rope.astype(BF16)

        ones = jnp.ones((keys, LANES), BF16)
        kr_all = k_scr[:, nope_w:nope_w + LANES]
        for t in range(MLA_HEADS // 2):
            cs = slice(t * LANES, (t + 1) * LANES)
            k_cat = jnp.concatenate([k_scr[:, cs], kr_all], axis=1)
            v_ext = jnp.concatenate([v_scr[:, cs], ones], axis=1)
            for r0 in range(0, seq_len, bq):
                qn = q_scr[r0:r0 + bq, cs]
                qr = q_scr[r0:r0 + bq, nope_w:nope_w + LANES]
                q2 = jnp.concatenate([
                    jnp.concatenate([jnp.where(lo, qn, 0.0), jnp.where(rope_lanes[2 * t], qr, 0.0)], axis=1),
                    jnp.concatenate([jnp.where(lo, 0.0, qn), jnp.where(rope_lanes[2 * t + 1], qr, 0.0)], axis=1),
                ], axis=0).astype(BF16)
                s = _dot_nt(q2, k_cat) * scale
                m = jnp.max(s, axis=-1, keepdims=True)
                pv = _dot(jnp.exp(s - m).astype(BF16), v_ext)
                o = pv[:, 0:LANES] / pv[:, LANES:LANES + 1]
                y_ref[i * seq_len + r0:i * seq_len + r0 + bq, cs] = _merge_heads(o, lo).astype(BF16)


def _mla(mla_in, q_norm, kv_norm, w_uq, w_uk, w_uv, cache, rope, carried, layer, n_batch, seq_len, nb):
    latent = cache is not None
    aliases = {}
    past = cache[0].shape[2] if latent else 0
    kern = functools.partial(_mla_kernel, seq_len=seq_len, latent=latent, bq=256, nb=nb)
    q_w = MLA_HEADS * (MLA_NOPE + MLA_ROPE)
    kv_w = MLA_HEADS * MLA_NOPE
    in_specs = [
        pl.BlockSpec((nb * seq_len, MLA_IN_TILED), lambda b: (b, 0)),
        pl.BlockSpec((1, 1, MLA_Q_RANK), lambda b: (layer, 0, 0)),
        pl.BlockSpec((1, 1, MLA_KV_RANK), lambda b: (layer, 0, 0)),
        pl.BlockSpec((1, MLA_Q_RANK, q_w), lambda b: (layer, 0, 0)),
        pl.BlockSpec((1, MLA_KV_RANK, kv_w), lambda b: (layer, 0, 0)),
        pl.BlockSpec((1, MLA_KV_RANK, kv_w), lambda b: (layer, 0, 0)),
    ]
    args = [mla_in, q_norm, kv_norm, w_uq, w_uk, w_uv]
    y_shape = jax.ShapeDtypeStruct((n_batch * seq_len, MIX_PART), BF16)
    y_spec = pl.BlockSpec((nb * seq_len, MIX_PART), lambda b: (b, 0))
    if latent:
        assert nb == 1
        in_specs += [
            pl.BlockSpec((1, 1, past, MLA_KV_RANK), lambda b: (b, layer, 0, 0)),
            pl.BlockSpec((1, 1, past, MLA_ROPE), lambda b: (b, layer, 0, 0)),
            pl.BlockSpec((seq_len, LANES), lambda b: (0, 0)),
            pl.BlockSpec((seq_len, LANES), lambda b: (0, 0)),
        ]
        args += [cache[0], cache[1], rope[0], rope[1]]
        out_shape, out_specs = y_shape, y_spec
    else:
        ckv_shape, ckv_spec = _layer_stacked((seq_len, MLA_KV_RANK), n_batch, layer, nb)
        kr_shape, kr_spec = _layer_stacked((MLA_ROPE, seq_len), n_batch, layer, nb)
        out_shape = [y_shape, ckv_shape, kr_shape]
        out_specs = [y_spec, ckv_spec, kr_spec]
        aliases = _carry_specs(in_specs, args, carried, 1)
    keys = past + seq_len
    return pl.pallas_call(
        kern, out_shape=out_shape, grid=(n_batch // nb,), in_specs=in_specs, out_specs=out_specs,
        input_output_aliases=aliases,
        scratch_shapes=[
            pltpu.VMEM((nb, seq_len, q_w), F32),
            pltpu.VMEM((nb, keys, kv_w + LANES), BF16),
            pltpu.VMEM((nb, keys, kv_w), BF16),
        ],
        compiler_params=_params(1),
        name=f"mla_{seq_len}",
    )(*args)


def _dft_matrix(seq_len):
    k = np.arange(seq_len, dtype=np.float64)[:, None]
    j = np.arange(seq_len, dtype=np.float64)[None, :]
    ang = np.pi * k * j / seq_len
    re = np.cos(ang)
    im = -np.sin(ang)
    im[0, :] = np.where(np.arange(seq_len) % 2 == 0, 1.0, -1.0)
    return np.concatenate([re, im], axis=0).astype(np.float32)


def _hyena_positions(seq_len):
    t = np.arange(seq_len, dtype=np.float64) / seq_len
    bands = np.arange(1, HY_BANDS + 1, dtype=np.float64)
    ang = 2.0 * math.pi * t[:, None] * bands
    z = np.concatenate([t[:, None], np.cos(ang), np.sin(ang)], axis=-1)
    return np.pad(z, ((0, 0), (0, HY_POS_PAD - HY_POS_DIM))).astype(np.float32)


def _rope_tables(n_tokens, dim):
    rows = np.repeat(np.arange(n_tokens // GRID_W, dtype=np.float64), GRID_W)
    cols = np.tile(np.arange(GRID_W, dtype=np.float64), n_tokens // GRID_W)
    quarter = dim // 4
    inv = ROPE_BASE ** (-np.arange(quarter, dtype=np.float64) / quarter)
    ang = np.concatenate([rows[:, None] * inv, cols[:, None] * inv], axis=-1)
    cos = np.concatenate([np.cos(ang), np.cos(ang)], axis=-1)
    sin = np.concatenate([-np.sin(ang), np.sin(ang)], axis=-1)
    reps = LANES // dim
    return (np.tile(cos, (1, reps)).astype(np.float32), np.tile(sin, (1, reps)).astype(np.float32))


def kernel(x_prompt, x_sample, state_ret, cache_gqa_k, cache_gqa_v, cache_mla_ckv, cache_mla_krope, c, c_ctx, ada_w, ada_b, norm_g, w_in, ret_decay_logit, ret_gn_g, hy_short_w, hy_short_b, hy_w1, hy_b1, hy_w2, hy_b2, hy_w3, hy_decay, hy_bias, gqa_sink, mla_q_norm, mla_kv_norm, mla_w_uq, mla_w_uk, mla_w_uv, w_out, ffn_w_up, ffn_conv_w, ffn_conv_b, ffn_w_down):
    n_p, len_p, _ = x_prompt.shape
    n_s, len_s, _ = x_sample.shape

    w_in_t = jnp.swapaxes(w_in, 1, 2)
    w_out_b = w_out.astype(BF16)
    w_up_b = ffn_w_up.astype(BF16)
    w_down_b = ffn_w_down.astype(BF16)
    uq = mla_w_uq.reshape(DEPTH, MLA_Q_RANK, MLA_HEADS, MLA_NOPE + MLA_ROPE)
    w_uq_b = jnp.concatenate(
        [uq[..., :MLA_NOPE].reshape(DEPTH, MLA_Q_RANK, MLA_HEADS * MLA_NOPE),
         uq[..., MLA_NOPE:].reshape(DEPTH, MLA_Q_RANK, MLA_HEADS * MLA_ROPE)], axis=-1).astype(BF16)
    w_uk_b = mla_w_uk.astype(BF16)
    w_uv_b = mla_w_uv.astype(BF16)

    cond = jnp.concatenate([c_ctx[None], c, jnp.zeros((N_COND - 1 - n_s, D_MODEL), F32)], axis=0)
    mod = _modulation(cond, ada_w, ada_b).reshape(DEPTH, N_COND, 6, D_MODEL)

    w1p = jnp.pad(hy_w1, ((0, 0), (0, HY_POS_PAD - HY_POS_DIM), (0, 0)))
    b1 = hy_b1.reshape(DEPTH, 1, HY_FILTER_HIDDEN)
    b2 = hy_b2.reshape(DEPTH, 1, HY_FILTER_HIDDEN)
    groups = {}
    for seq_len in (len_p, len_s):
        f32mat = _dft_matrix(seq_len)
        fmat = jnp.asarray(f32mat).astype(BF16)
        fmat_t = jnp.asarray(np.ascontiguousarray(f32mat.T)).astype(BF16)
        tables = _hyena_tables(seq_len, jnp.asarray(_hyena_positions(seq_len)), w1p, b1, hy_w2, b2,
                               hy_w3, hy_decay, fmat)
        groups[seq_len] = (tables, fmat, fmat_t)

    rope_g = tuple(jnp.asarray(t) for t in _rope_tables(len_s, HEAD_DIM))
    rope_m = tuple(jnp.asarray(t) for t in _rope_tables(len_s, MLA_ROPE))

    gn_g = ret_gn_g.reshape(DEPTH, 1, RET_WIDTH)
    short_b = hy_short_b.reshape(DEPTH, 1, HY_IN)
    sink = gqa_sink.reshape(DEPTH, 1, GQA_Q_HEADS)
    q_norm = mla_q_norm.reshape(DEPTH, 1, MLA_Q_RANK)
    kv_norm = mla_kv_norm.reshape(DEPTH, 1, MLA_KV_RANK)
    conv_b = ffn_conv_b.reshape(DEPTH, 1, D_FF)

    xp = x_prompt.reshape(n_p * len_p, D_MODEL)
    xs = x_sample.reshape(n_s * len_s, D_MODEL)
    tm_p = 1024
    tm_s = 512
    nb_p = 2
    cond_p = lambda i: 0
    cond_s = lambda i: 1 + (i * tm_s) // len_s
    tm_ffn = 1024
    cond_s_ffn = lambda i: 1 + (i * tm_ffn) // len_s

    st = gkv = ckr = None
    for l in range(DEPTH):
        tables, fmat, fmat_t = groups[len_p]
        ret_in, hy_in, gqa_in, mla_in = _in_proj(xp, mod, norm_g, w_in_t, l, tm_p, cond_p)
        y_ret, *st = _retention(ret_in, ret_decay_logit, gn_g, None, st, l, n_p, len_p, nb_p)
        y_hy = _hyena(hy_in, hy_short_w, short_b, hy_bias, tables, fmat, fmat_t, l, n_p, len_p, nb_p)
        y_gqa, *gkv = _gqa_ctx(gqa_in, sink, gkv, l, n_p, len_p, nb_p)
        y_mla, *ckr = _mla(mla_in, q_norm, kv_norm, w_uq_b, w_uk_b, w_uv_b, None, None, ckr, l, n_p, len_p,
                           nb_p)
        xp = _channel_mix(xp, (y_ret, y_hy, y_gqa, y_mla), mod, norm_g, w_out_b, w_up_b, ffn_conv_w, conv_b,
                          w_down_b, l, tm_ffn, len_p, cond_p)
        tables, fmat, fmat_t = groups[len_s]
        ret_in, hy_in, gqa_in, mla_in = _in_proj(xs, mod, norm_g, w_in_t, l, tm_s, cond_s)
        y_ret = _retention(ret_in, ret_decay_logit, gn_g, state_ret, None, l, n_s, len_s, 1)
        y_hy = _hyena(hy_in, hy_short_w, short_b, hy_bias, tables, fmat, fmat_t, l, n_s, len_s, 1)
        y_gqa = _gqa_win(gqa_in, sink, cache_gqa_k, cache_gqa_v, rope_g[0], rope_g[1], l, n_s, len_s)
        y_mla = _mla(mla_in, q_norm, kv_norm, w_uq_b, w_uk_b, w_uv_b, (cache_mla_ckv, cache_mla_krope),
                     rope_m, None, l, n_s, len_s, 1)
        xs = _channel_mix(xs, (y_ret, y_hy, y_gqa, y_mla), mod, norm_g, w_out_b, w_up_b, ffn_conv_w, conv_b,
                          w_down_b, l, tm_ffn, len_s, cond_s_ffn)

    return (xp.reshape(n_p, len_p, D_MODEL), xs.reshape(n_s, len_s, D_MODEL),
            st[0], jnp.swapaxes(gkv[0], -1, -2), jnp.swapaxes(gkv[1], -1, -2),
            ckr[0], jnp.swapaxes(ckr[1], -1, -2))
```

```python
import functools
import math

import numpy as np
import jax
import jax.numpy as jnp
from jax import lax
from jax.experimental import pallas as pl
from jax.experimental.pallas import tpu as pltpu

F32 = jnp.float32
BF16 = jnp.bfloat16

D_MODEL = 1024
DEPTH = 4
GRID_W = 64
EPS = 1e-6
NEG_INF = -1e30
ROPE_BASE = 10000.0
BLOCK = 128
WINDOW = 128
HEAD_DIM = 64
LANES = 128
RET_HEADS = 4
RET_WIDTH = RET_HEADS * HEAD_DIM
HY_WIDTH = 256
HY_ORDER = 2
HY_BANDS = 8
HY_POS_DIM = 1 + 2 * HY_BANDS
HY_POS_PAD = 32
HY_FILTER_HIDDEN = 64
GQA_Q_HEADS = 4
GQA_KV_HEADS = 2
GQA_GROUPS = GQA_Q_HEADS // GQA_KV_HEADS
MLA_HEADS = 4
MLA_Q_RANK = 256
MLA_KV_RANK = 128
MLA_NOPE = 64
MLA_ROPE = 32
MLA_V = 64
D_FF = 2816
RET_IN = 4 * RET_WIDTH
HY_IN = (HY_ORDER + 1) * HY_WIDTH
GQA_IN = (GQA_Q_HEADS + 2 * GQA_KV_HEADS) * HEAD_DIM
MLA_IN = MLA_Q_RANK + MLA_KV_RANK + MLA_ROPE
MLA_IN_TILED = MLA_Q_RANK + MLA_KV_RANK + MLA_HEADS * MLA_ROPE
IN_WIDTH = RET_IN + HY_IN + GQA_IN + MLA_IN
IN_WIDTH_TILED = RET_IN + HY_IN + GQA_IN + MLA_IN_TILED
GQA_HEAD_ORDER = (0, 2, 1, 3)
MIX_PART = 256
N_COND = 8
FFN_CHUNK = 256
N_FFN_CHUNKS = D_FF // FFN_CHUNK

VMEM_LIMIT = 56 * 1024 * 1024


def _params(n_axes=1):
    return pltpu.CompilerParams(
        dimension_semantics=("arbitrary",) * n_axes, vmem_limit_bytes=VMEM_LIMIT)


def _dot(a, b):
    return jnp.dot(a, b, preferred_element_type=F32)


def _dot_nt(a, b):
    return lax.dot_general(a, b, (((1,), (1,)), ((), ())), preferred_element_type=F32)


def _dot_tn(a, b):
    return lax.dot_general(a, b, (((0,), (0,)), ((), ())), preferred_element_type=F32)


def _rms(x, g):
    return x * lax.rsqrt(jnp.mean(x * x, axis=-1, keepdims=True) + EPS) * g


def _sigmoid(x):
    return 1.0 / (1.0 + jnp.exp(-x))


def _silu(x):
    return x * _sigmoid(x)


def _shift_rows(x, seq_len):
    n = x.shape[0]
    assert seq_len & (seq_len - 1) == 0
    pos = lax.broadcasted_iota(jnp.int32, (n, 1), 0) & (seq_len - 1)
    prev = jnp.where(pos != 0, pltpu.roll(x, 1, axis=0), 0.0)
    nxt = jnp.where(pos != seq_len - 1, pltpu.roll(x, n - 1, axis=0), 0.0)
    return prev, nxt


def _rot_half(x, half):
    lane = lax.broadcasted_iota(jnp.int32, x.shape, 1) & (2 * half - 1)
    return jnp.where(lane < half, pltpu.roll(x, LANES - half, axis=1), pltpu.roll(x, half, axis=1))


def _lane_iota():
    return lax.broadcasted_iota(jnp.int32, (1, LANES), 1)


def _split_heads(t, lo):
    return jnp.concatenate([jnp.where(lo, t, 0.0), jnp.where(lo, 0.0, t)], axis=0)


def _merge_heads(o, lo):
    n = o.shape[0] // 2
    return jnp.where(lo, o[:n], o[n:])


def _per_head(vals, width):
    lane = lax.broadcasted_iota(jnp.int32, (1, width), 1)
    out = vals[-1]
    for h in range(len(vals) - 2, -1, -1):
        out = jnp.where(lane < (h + 1) * HEAD_DIM, vals[h], out)
    return out


def _mod_kernel(cond_ref, w_ref, b_ref, out_ref):
    s = _silu(cond_ref[...]).astype(BF16)
    out_ref[0] = _dot(s, w_ref[0].astype(BF16)) + b_ref[0]


def _modulation(cond, ada_w, ada_b):
    tn = 1536
    return pl.pallas_call(
        _mod_kernel,
        out_shape=jax.ShapeDtypeStruct((DEPTH, N_COND, 6 * D_MODEL), F32),
        grid=(DEPTH, 6 * D_MODEL // tn),
        in_specs=[
            pl.BlockSpec((N_COND, D_MODEL), lambda l, j: (0, 0)),
            pl.BlockSpec((1, D_MODEL, tn), lambda l, j: (l, 0, j)),
            pl.BlockSpec((1, 1, tn), lambda l, j: (l, 0, j)),
        ],
        out_specs=pl.BlockSpec((1, N_COND, tn), lambda l, j: (l, 0, j)),
        compiler_params=_params(2),
        name="modulation",
    )(cond, ada_w, ada_b.reshape(DEPTH, 1, 6 * D_MODEL))


def _filter_kernel(z_ref, w1_ref, b1_ref, w2_ref, b2_ref, w3_ref, dec_ref, f_ref, out_ref, *, seq_len):
    hi = lax.Precision.HIGHEST
    h = jnp.sin(jnp.dot(z_ref[...], w1_ref[0], precision=hi, preferred_element_type=F32) + b1_ref[0])
    h = jnp.sin(jnp.dot(h, w2_ref[0], precision=hi, preferred_element_type=F32) + b2_ref[0])
    h = jnp.dot(h, w3_ref[0], precision=hi, preferred_element_type=F32)
    row = lax.broadcasted_iota(jnp.int32, (seq_len, 1), 0)
    t = row.astype(F32) / seq_len
    dec = jnp.abs(dec_ref[0])
    win_f = jnp.exp(-t * dec[0:1, :])
    win_b = jnp.exp(-t * dec[1:2, :])
    fm = f_ref[...]
    inv = 1.0 / seq_len
    scale = jnp.where(row == 0, 0.5 * inv, inv)
    for o in range(HY_ORDER):
        base = o * 2 * HY_WIDTH
        hf = h[:, base:base + HY_WIDTH] * win_f
        hb = h[:, base + HY_WIDTH:base + 2 * HY_WIDTH] * win_b
        hbs = jnp.where(row != 0, pltpu.roll(hb, 1, axis=0), 0.0)
        sf = _dot(fm, hf.astype(BF16))
        sb = _dot(fm, hbs.astype(BF16))
        g_re = sf[:seq_len] + sb[:seq_len]
        g_im = sf[seq_len:] - sb[seq_len:]
        g_ny = sf[seq_len:] + sb[seq_len:]
        a = g_re * scale
        out_ref[0, 3 * o] = a
        out_ref[0, 3 * o + 1] = jnp.where(row == 0, g_ny * (0.5 * inv), a)
        out_ref[0, 3 * o + 2] = jnp.where(row == 0, 0.0, g_im * inv)


def _hyena_tables(seq_len, z, w1p, b1, w2, b2, w3, decay, fmat):
    hid = HY_FILTER_HIDDEN
    wide = HY_ORDER * 2 * HY_WIDTH
    return pl.pallas_call(
        functools.partial(_filter_kernel, seq_len=seq_len),
        out_shape=jax.ShapeDtypeStruct((DEPTH, 3 * HY_ORDER, seq_len, HY_WIDTH), F32),
        grid=(DEPTH,),
        in_specs=[
            pl.BlockSpec((seq_len, HY_POS_PAD), lambda l: (0, 0)),
            pl.BlockSpec((1, HY_POS_PAD, hid), lambda l: (l, 0, 0)),
            pl.BlockSpec((1, 1, hid), lambda l: (l, 0, 0)),
            pl.BlockSpec((1, hid, hid), lambda l: (l, 0, 0)),
            pl.BlockSpec((1, 1, hid), lambda l: (l, 0, 0)),
            pl.BlockSpec((1, hid, wide), lambda l: (l, 0, 0)),
            pl.BlockSpec((1, 2, HY_WIDTH), lambda l: (l, 0, 0)),
            pl.BlockSpec((2 * seq_len, seq_len), lambda l: (0, 0)),
        ],
        out_specs=pl.BlockSpec((1, 3 * HY_ORDER, seq_len, HY_WIDTH), lambda l: (l, 0, 0, 0)),
        compiler_params=_params(1),
        name=f"hyena_tables_{seq_len}",
    )(z, w1p, b1, w2, b2, w3, decay, fmat)


def _layer_stacked(tail, n_batch, layer, nb):
    zeros = (0,) * len(tail)
    n_layers = DEPTH if layer == 0 else 1
    return (jax.ShapeDtypeStruct((n_batch, DEPTH) + tail, F32),
            pl.BlockSpec((nb, n_layers) + tail, lambda b: (b, layer) + zeros))


def _zero_other_layers(*out_refs):
    for ref in out_refs:
        if ref.shape[1] > 1:
            ref[:, 1:] = jnp.zeros((ref.shape[0], ref.shape[1] - 1) + ref.shape[2:], ref.dtype)


def _carry_specs(in_specs, args, carried, first_out):
    aliases = {}
    if carried is not None:
        for k, arr in enumerate(carried):
            aliases[len(args)] = first_out + k
            in_specs.append(pl.BlockSpec(memory_space=pl.ANY))
            args.append(arr)
    return aliases


def _in_proj_kernel(x_ref, mod_ref, g_ref, w_ref, ret_ref, hy_ref, gqa_ref, mla_ref, w_scr):
    @pl.when(pl.program_id(0) == 0)
    def _():
        g0 = RET_IN + HY_IN
        m0 = g0 + GQA_IN
        kr0 = m0 + MLA_Q_RANK + MLA_KV_RANK
        step = 2 * LANES
        for r0 in list(range(0, g0, step)) + list(range(g0 + GQA_Q_HEADS * HEAD_DIM, kr0, LANES)):
            n = step if r0 < g0 else LANES
            w_scr[r0:r0 + n, :] = w_ref[0, r0:r0 + n, :].astype(BF16)
        for dst, j in enumerate(GQA_HEAD_ORDER):
            w_scr[g0 + dst * HEAD_DIM:g0 + (dst + 1) * HEAD_DIM, :] = (
                w_ref[0, g0 + j * HEAD_DIM:g0 + (j + 1) * HEAD_DIM, :].astype(BF16))
        kr = w_ref[0, kr0:kr0 + MLA_ROPE, :].astype(BF16)
        for hd in range(MLA_HEADS):
            w_scr[kr0 + hd * MLA_ROPE:kr0 + (hd + 1) * MLA_ROPE, :] = kr

    shift = mod_ref[0, 0, 0:1, :]
    scale = mod_ref[0, 0, 1:2, :]
    h = (_rms(x_ref[...], g_ref[0, 0:1, :]) * (1.0 + scale) + shift).astype(BF16)
    c0 = 0
    for ref in (ret_ref, hy_ref, gqa_ref, mla_ref):
        width = ref.shape[1]
        ref[...] = _dot_nt(h, w_scr[c0:c0 + width, :])
        c0 += width


def _in_proj(x, mod, norm_g, w_in, layer, tm, cond_of_tile):
    rows = x.shape[0]
    widths = (RET_IN, HY_IN, GQA_IN, MLA_IN_TILED)
    return pl.pallas_call(
        _in_proj_kernel,
        out_shape=[jax.ShapeDtypeStruct((rows, w), F32) for w in widths],
        grid=(rows // tm,),
        in_specs=[
            pl.BlockSpec((tm, D_MODEL), lambda i: (i, 0)),
            pl.BlockSpec((1, 1, 6, D_MODEL), lambda i: (layer, cond_of_tile(i), 0, 0)),
            pl.BlockSpec((1, 4, D_MODEL), lambda i: (layer, 0, 0)),
            pl.BlockSpec((1, IN_WIDTH, D_MODEL), lambda i: (layer, 0, 0), pipeline_mode=pl.Buffered(1)),
        ],
        out_specs=[pl.BlockSpec((tm, w), lambda i: (i, 0)) for w in widths],
        scratch_shapes=[pltpu.VMEM((IN_WIDTH_TILED, D_MODEL), BF16)],
        compiler_params=_params(1),
        name="in_proj",
    )(x, mod, norm_g, w_in)


def _channel_kernel(x_ref, m0_ref, m1_ref, m2_ref, m3_ref, mod_ref, g_ref, wo_ref, wu_ref, cw_ref, cb_ref,
                    wd_ref, out_ref, act_ref, *, seq_len):
    y = None
    for i, m_ref in enumerate((m0_ref, m1_ref, m2_ref, m3_ref)):
        r0 = i * MIX_PART
        if i == 2:
            w = jnp.concatenate([wo_ref[0, r0 + j * HEAD_DIM:r0 + (j + 1) * HEAD_DIM, :]
                                 for j in GQA_HEAD_ORDER], axis=0)
        else:
            w = wo_ref[0, r0:r0 + MIX_PART, :]
        part = _dot(m_ref[...], w)
        y = part if y is None else y + part
    gate1 = mod_ref[0, 0, 2:3, :]
    shift2 = mod_ref[0, 0, 3:4, :]
    scale2 = mod_ref[0, 0, 4:5, :]
    xm = x_ref[...] + gate1 * _rms(y, g_ref[0, 1:2, :])
    out_ref[...] = xm
    h2 = (_rms(xm, g_ref[0, 2:3, :]) * (1.0 + scale2) + shift2).astype(BF16)
    for c in range(N_FFN_CHUNKS):
        sl = slice(c * FFN_CHUNK, (c + 1) * FFN_CHUNK)
        gate = _dot(h2, wu_ref[0, :, sl])
        up = _dot(h2, wu_ref[0, :, D_FF + c * FFN_CHUNK:D_FF + (c + 1) * FFN_CHUNK])
        prev, nxt = _shift_rows(gate, seq_len)
        gate = (prev * cw_ref[0, 0:1, sl] + gate * cw_ref[0, 1:2, sl] + nxt * cw_ref[0, 2:3, sl]
                + cb_ref[0, :, sl])
        act_ref[:, sl] = (_silu(gate) * up).astype(BF16)
    ffn = _dot(act_ref[...], wd_ref[0])
    gate2 = mod_ref[0, 0, 5:6, :]
    out_ref[...] = out_ref[...] + gate2 * _rms(ffn, g_ref[0, 3:4, :])


def _channel_mix(x, mixes, mod, norm_g, w_out, w_up, conv_w, conv_b, w_down, layer, tm, seq_len, cond_of_tile):
    rows = x.shape[0]
    resident = dict(pipeline_mode=pl.Buffered(1))
    return pl.pallas_call(
        functools.partial(_channel_kernel, seq_len=seq_len),
        out_shape=jax.ShapeDtypeStruct((rows, D_MODEL), F32),
        grid=(rows // tm,),
        in_specs=[pl.BlockSpec((tm, D_MODEL), lambda i: (i, 0))]
        + [pl.BlockSpec((tm, MIX_PART), lambda i: (i, 0))] * 4
        + [
            pl.BlockSpec((1, 1, 6, D_MODEL), lambda i: (layer, cond_of_tile(i), 0, 0)),
            pl.BlockSpec((1, 4, D_MODEL), lambda i: (layer, 0, 0)),
            pl.BlockSpec((1, D_MODEL, D_MODEL), lambda i: (layer, 0, 0), **resident),
            pl.BlockSpec((1, D_MODEL, 2 * D_FF), lambda i: (layer, 0, 0), **resident),
            pl.BlockSpec((1, 3, D_FF), lambda i: (layer, 0, 0)),
            pl.BlockSpec((1, 1, D_FF), lambda i: (layer, 0, 0)),
            pl.BlockSpec((1, D_FF, D_MODEL), lambda i: (layer, 0, 0), **resident),
        ],
        out_specs=pl.BlockSpec((tm, D_MODEL), lambda i: (i, 0)),
        scratch_shapes=[pltpu.VMEM((tm, D_FF), BF16)],
        compiler_params=_params(1),
        name="channel_mix",
    )(x, *mixes, mod, norm_g, w_out, w_up, conv_w, conv_b, w_down)


def _log_gamma(dl_ref):
    dl = dl_ref[0]
    return jnp.minimum(dl, 0.0) - jnp.log(1.0 + jnp.exp(-jnp.abs(dl)))


def _pair_decay(lg, t, r0, bq, seq_len):
    top = lax.broadcasted_iota(jnp.int32, (2 * bq, 1), 0) < bq
    lgf = jnp.where(top, lg[0:1, 2 * t:2 * t + 1], lg[0:1, 2 * t + 1:2 * t + 2])
    lgb = jnp.where(top, lg[1:2, 2 * t:2 * t + 1], lg[1:2, 2 * t + 1:2 * t + 2])
    rowf = ((lax.broadcasted_iota(jnp.int32, (2 * bq, 1), 0) & (bq - 1)) + r0).astype(F32)
    colf = lax.broadcasted_iota(jnp.int32, (1, seq_len), 1).astype(F32)
    lag = rowf - colf
    decay = jnp.exp(jnp.where(lag >= 0.0, lag * lgf, -lag * lgb))
    return jnp.where(lag == 0.0, 2.0, decay)


def _head_norm(o, lo):
    inv = 1.0 / HEAD_DIM
    s_all = jnp.sum(o, axis=-1, keepdims=True)
    s_lo = jnp.sum(jnp.where(lo, o, 0.0), axis=-1, keepdims=True)
    d = o - jnp.where(lo, s_lo, s_all - s_lo) * inv
    d2 = d * d
    v_all = jnp.sum(d2, axis=-1, keepdims=True)
    v_lo = jnp.sum(jnp.where(lo, d2, 0.0), axis=-1, keepdims=True)
    return d * lax.rsqrt(jnp.where(lo, v_lo, v_all - v_lo) * inv + EPS)


def _ret_ctx_kernel(*refs, seq_len, nb):
    x_ref, dl_ref, gn_ref, *_, y_ref, st_ref, dec_scr, kdec_scr = refs
    lo = _lane_iota() < HEAD_DIM
    _zero_other_layers(st_ref)

    @pl.when(pl.program_id(0) == 0)
    def _():
        lg = _log_gamma(dl_ref)
        posf = lax.broadcasted_iota(jnp.int32, (seq_len, 1), 0).astype(F32)
        lgf = _per_head([lg[0:1, h:h + 1] for h in range(RET_HEADS)], RET_WIDTH)
        lgb = _per_head([lg[1:2, h:h + 1] for h in range(RET_HEADS)], RET_WIDTH)
        kdec_scr[0] = jnp.exp((seq_len - 1.0 - posf) * lgf)
        kdec_scr[1] = jnp.exp(posf * lgb)
        for t in range(RET_HEADS // 2):
            dec_scr[t] = _pair_decay(lg, t, 0, seq_len, seq_len)

    for i in range(nb):
        rows = slice(i * seq_len, (i + 1) * seq_len)
        for t in range(RET_HEADS // 2):
            cs = slice(t * LANES, (t + 1) * LANES)
            q = x_ref[rows, cs]
            k = x_ref[rows, RET_WIDTH + t * LANES:RET_WIDTH + (t + 1) * LANES] * (HEAD_DIM ** -0.5)
            vb = x_ref[rows, 2 * RET_WIDTH + t * LANES:2 * RET_WIDTH + (t + 1) * LANES].astype(BF16)
            gate = x_ref[rows, 3 * RET_WIDTH + t * LANES:3 * RET_WIDTH + (t + 1) * LANES]
            s = _dot_nt(_split_heads(q, lo).astype(BF16), k.astype(BF16)) * dec_scr[t]
            o = _merge_heads(_dot(s.astype(BF16), vb), lo)
            y = _silu(gate) * (_head_norm(o, lo) * gn_ref[0, :, cs])
            y_ref[rows, cs] = y.astype(BF16)
            for d in range(2):
                st = _dot_tn((k * kdec_scr[d, :, cs]).astype(BF16), vb)
                st_ref[i, 0, d, 2 * t] = st[0:HEAD_DIM, 0:HEAD_DIM]
                st_ref[i, 0, d, 2 * t + 1] = st[HEAD_DIM:LANES, HEAD_DIM:LANES]


def _ret_lat_kernel(x_ref, dl_ref, gn_ref, s0_ref, y_ref, *, seq_len, bq):
    lo = _lane_iota() < HEAD_DIM
    lg = _log_gamma(dl_ref)
    zero = jnp.zeros((HEAD_DIM, HEAD_DIM), F32)
    for t in range(RET_HEADS // 2):
        cs = slice(t * LANES, (t + 1) * LANES)
        kb = (x_ref[:, RET_WIDTH + t * LANES:RET_WIDTH + (t + 1) * LANES] * (HEAD_DIM ** -0.5)).astype(BF16)
        vb = x_ref[:, 2 * RET_WIDTH + t * LANES:2 * RET_WIDTH + (t + 1) * LANES].astype(BF16)
        lgf = _per_head([lg[0:1, 2 * t:2 * t + 1], lg[0:1, 2 * t + 1:2 * t + 2]], LANES)
        lgb = _per_head([lg[1:2, 2 * t:2 * t + 1], lg[1:2, 2 * t + 1:2 * t + 2]], LANES)
        s0 = []
        for d in range(2):
            a = s0_ref[0, 0, d, 2 * t]
            b = s0_ref[0, 0, d, 2 * t + 1]
            s0.append(jnp.concatenate([jnp.concatenate([a, zero], axis=1),
                                       jnp.concatenate([zero, b], axis=1)], axis=0).astype(BF16))
        for r0 in range(0, seq_len, bq):
            q = x_ref[r0:r0 + bq, cs]
            s = _dot_nt(_split_heads(q, lo).astype(BF16), kb) * _pair_decay(lg, t, r0, bq, seq_len)
            o = _merge_heads(_dot(s.astype(BF16), vb), lo)
            rowf = (lax.broadcasted_iota(jnp.int32, (bq, 1), 0) + r0).astype(F32)
            qf = (q * jnp.exp((rowf + 1.0) * lgf)).astype(BF16)
            qr = (q * jnp.exp((seq_len - rowf) * lgb)).astype(BF16)
            o = o + _dot(qf, s0[0]) + _dot(qr, s0[1])
            gate = x_ref[r0:r0 + bq, 3 * RET_WIDTH + t * LANES:3 * RET_WIDTH + (t + 1) * LANES]
            y = _silu(gate) * (_head_norm(o, lo) * gn_ref[0, :, cs])
            y_ref[r0:r0 + bq, cs] = y.astype(BF16)


def _retention(ret_in, decay_logit, gn_g, state0, carried, layer, n_batch, seq_len, nb):
    latent = state0 is not None
    in_specs = [
        pl.BlockSpec((nb * seq_len, RET_IN), lambda b: (b, 0)),
        pl.BlockSpec((1, 2, RET_HEADS), lambda b: (layer, 0, 0)),
        pl.BlockSpec((1, 1, RET_WIDTH), lambda b: (layer, 0, 0)),
    ]
    args = [ret_in, decay_logit, gn_g]
    y_shape = jax.ShapeDtypeStruct((n_batch * seq_len, MIX_PART), BF16)
    y_spec = pl.BlockSpec((nb * seq_len, MIX_PART), lambda b: (b, 0))
    if latent:
        assert nb == 1
        kern = functools.partial(_ret_lat_kernel, seq_len=seq_len, bq=256)
        in_specs.append(pl.BlockSpec((1, 1, 2, RET_HEADS, HEAD_DIM, HEAD_DIM),
                                     lambda b: (b, layer, 0, 0, 0, 0)))
        args.append(state0)
        out_shape, out_specs, aliases, scratch = y_shape, y_spec, {}, []
    else:
        kern = functools.partial(_ret_ctx_kernel, seq_len=seq_len, nb=nb)
        st_shape, st_spec = _layer_stacked((2, RET_HEADS, HEAD_DIM, HEAD_DIM), n_batch, layer, nb)
        out_shape = [y_shape, st_shape]
        out_specs = [y_spec, st_spec]
        aliases = _carry_specs(in_specs, args, carried, 1)
        scratch = [pltpu.VMEM((RET_HEADS // 2, 2 * seq_len, seq_len), F32),
                   pltpu.VMEM((2, seq_len, RET_WIDTH), F32)]
    return pl.pallas_call(
        kern, out_shape=out_shape, grid=(n_batch // nb,), in_specs=in_specs, out_specs=out_specs,
        input_output_aliases=aliases, scratch_shapes=scratch,
        compiler_params=_params(1), name=f"retention_{seq_len}",
    )(*args)


def _hyena_kernel(x_ref, sw_ref, sb_ref, bias_ref, tab_ref, f_ref, ft_ref, y_ref, *, seq_len, nb):
    for i in range(nb):
        rows = slice(i * seq_len, (i + 1) * seq_len)
        x = x_ref[rows, :]
        prev, nxt = _shift_rows(x, seq_len)
        u = prev * sw_ref[0, 0:1, :] + x * sw_ref[0, 1:2, :] + nxt * sw_ref[0, 2:3, :] + sb_ref[0]
        x1 = u[:, 0:HY_WIDTH]
        x2 = u[:, HY_WIDTH:2 * HY_WIDTH]
        z = u[:, 2 * HY_WIDTH:3 * HY_WIDTH]
        for o, gate in enumerate((x1, x2)):
            spec = _dot(f_ref[...], z.astype(BF16))
            s_re = spec[:seq_len]
            s_im = spec[seq_len:]
            a = tab_ref[0, 3 * o]
            a_ny = tab_ref[0, 3 * o + 1]
            b = tab_ref[0, 3 * o + 2]
            y_re = (s_re * a - s_im * b).astype(BF16)
            y_im = (s_re * b + s_im * a_ny).astype(BF16)
            conv = _dot(ft_ref[:, 0:seq_len], y_re) + _dot(ft_ref[:, seq_len:2 * seq_len], y_im)
            z = gate * (conv + z * bias_ref[0, o:o + 1, :])
        y_ref[rows, :] = z.astype(BF16)


def _hyena(hy_in, short_w, short_b, hy_bias, tables, fmat, fmat_t, layer, n_batch, seq_len, nb):
    const = dict(pipeline_mode=pl.Buffered(1))
    return pl.pallas_call(
        functools.partial(_hyena_kernel, seq_len=seq_len, nb=nb),
        out_shape=jax.ShapeDtypeStruct((n_batch * seq_len, MIX_PART), BF16),
        grid=(n_batch // nb,),
        in_specs=[
            pl.BlockSpec((nb * seq_len, HY_IN), lambda b: (b, 0)),
            pl.BlockSpec((1, 3, HY_IN), lambda b: (layer, 0, 0)),
            pl.BlockSpec((1, 1, HY_IN), lambda b: (layer, 0, 0)),
            pl.BlockSpec((1, HY_ORDER, HY_WIDTH), lambda b: (layer, 0, 0)),
            pl.BlockSpec((1, 3 * HY_ORDER, seq_len, HY_WIDTH), lambda b: (layer, 0, 0, 0), **const),
            pl.BlockSpec((2 * seq_len, seq_len), lambda b: (0, 0), **const),
            pl.BlockSpec((seq_len, 2 * seq_len), lambda b: (0, 0), **const),
        ],
        out_specs=pl.BlockSpec((nb * seq_len, MIX_PART), lambda b: (b, 0)),
        compiler_params=_params(1),
        name=f"hyena_{seq_len}",
    )(hy_in, short_w, short_b, hy_bias, tables, fmat, fmat_t)


def _gqa_ctx_kernel(*refs, seq_len, nb):
    x_ref, sink_ref, *_, y_ref, k_out_ref, v_out_ref = refs
    nq = GQA_Q_HEADS * HEAD_DIM
    scale = HEAD_DIM ** -0.5
    head_a = lax.broadcasted_iota(jnp.int32, (LANES, 1), 0) < HEAD_DIM
    first = lax.broadcasted_iota(jnp.int32, (1, 2 * seq_len), 1) < seq_len
    ones = jnp.ones((LANES, seq_len), BF16)
    _zero_other_layers(k_out_ref, v_out_ref)
    for i in range(nb):
        rows = slice(i * seq_len, (i + 1) * seq_len)
        k = x_ref[rows, nq:nq + LANES]
        k_t = k.T
        v_t = x_ref[rows, nq + LANES:nq + 2 * LANES].T
        for kv in range(GQA_KV_HEADS):
            k_out_ref[i, 0, kv] = k_t[kv * HEAD_DIM:(kv + 1) * HEAD_DIM, :]
            v_out_ref[i, 0, kv] = v_t[kv * HEAD_DIM:(kv + 1) * HEAD_DIM, :]
        kb = k.astype(BF16)
        v_ext = jnp.concatenate([v_t.astype(BF16), ones], axis=0)
        for g in range(GQA_GROUPS):
            cs = slice(g * LANES, (g + 1) * LANES)
            q_t = x_ref[rows, cs].T
            q2 = jnp.concatenate([jnp.where(head_a, q_t, 0.0), jnp.where(head_a, 0.0, q_t)], axis=1)
            s = _dot(kb, q2.astype(BF16)) * scale
            sink = jnp.where(first, sink_ref[0, :, g:g + 1], sink_ref[0, :, GQA_GROUPS + g:GQA_GROUPS + g + 1])
            m = jnp.maximum(jnp.max(s, axis=0, keepdims=True), sink)
            pv = _dot(v_ext, jnp.exp(s - m).astype(BF16))
            o = pv[0:LANES, :] / (pv[LANES:LANES + 1, :] + jnp.exp(sink - m))
            o = jnp.where(head_a, o[:, 0:seq_len], o[:, seq_len:2 * seq_len])
            y_ref[rows, cs] = o.T.astype(BF16)


def _gqa_ctx(gqa_in, sink, carried, layer, n_batch, seq_len, nb):
    kv_shape, kv_spec = _layer_stacked((GQA_KV_HEADS, HEAD_DIM, seq_len), n_batch, layer, nb)
    in_specs = [
        pl.BlockSpec((nb * seq_len, GQA_IN), lambda b: (b, 0)),
        pl.BlockSpec((1, 1, GQA_Q_HEADS), lambda b: (layer, 0, 0)),
    ]
    args = [gqa_in, sink]
    aliases = _carry_specs(in_specs, args, carried, 1)
    return pl.pallas_call(
        functools.partial(_gqa_ctx_kernel, seq_len=seq_len, nb=nb),
        out_shape=[jax.ShapeDtypeStruct((n_batch * seq_len, MIX_PART), BF16), kv_shape, kv_shape],
        grid=(n_batch // nb,),
        in_specs=in_specs,
        out_specs=[pl.BlockSpec((nb * seq_len, MIX_PART), lambda b: (b, 0)), kv_spec, kv_spec],
        input_output_aliases=aliases,
        compiler_params=_params(1),
        name=f"gqa_ctx_{seq_len}",
    )(*args)


def _gqa_win_kernel(x_ref, sink_ref, kct_ref, vct_ref, cos_ref, sin_ref, y_ref,
                    q_scr, k_scr, vt_scr, kc_scr, vct_scr, *, seq_len):
    nq = GQA_Q_HEADS * HEAD_DIM
    scale = HEAD_DIM ** -0.5
    head_a = lax.broadcasted_iota(jnp.int32, (LANES, 1), 0) < HEAD_DIM
    first = lax.broadcasted_iota(jnp.int32, (1, 2 * BLOCK), 1) < BLOCK
    cos = cos_ref[...]
    sin = sin_ref[...]

    def rope(t):
        return t * cos + _rot_half(t, HEAD_DIM // 2) * sin

    for g in range(GQA_GROUPS):
        q_t = rope(x_ref[:, g * LANES:(g + 1) * LANES]).T
        q_scr[g, 0] = jnp.where(head_a, q_t, 0.0).astype(BF16)
        q_scr[g, 1] = jnp.where(head_a, 0.0, q_t).astype(BF16)
    zeros = jnp.zeros((BLOCK, LANES), BF16)
    for r0 in (0, BLOCK + seq_len):
        k_scr[r0:r0 + BLOCK, :] = zeros
        vt_scr[:, r0:r0 + BLOCK] = zeros
    k_scr[BLOCK:BLOCK + seq_len, :] = rope(x_ref[:, nq:nq + LANES]).astype(BF16)
    vt_scr[:, BLOCK:BLOCK + seq_len] = x_ref[:, nq + LANES:nq + 2 * LANES].T.astype(BF16)
    kc_scr[...] = jnp.concatenate([kct_ref[0, 0, 0], kct_ref[0, 0, 1]], axis=0).T.astype(BF16)
    vct_scr[...] = jnp.concatenate([vct_ref[0, 0, 0], vct_ref[0, 0, 1]], axis=0).astype(BF16)

    jj = lax.broadcasted_iota(jnp.int32, (3 * BLOCK, 2 * BLOCK), 0)
    ii = lax.broadcasted_iota(jnp.int32, (3 * BLOCK, 2 * BLOCK), 1) & (BLOCK - 1)
    band = (jj >= ii) & (jj <= ii + 2 * WINDOW)
    for n in range(seq_len // BLOCK):
        r0 = n * BLOCK
        kpos = jj + (n - 1) * BLOCK
        valid = band & (kpos >= 0) & (kpos < seq_len)
        kw = k_scr[r0:r0 + 3 * BLOCK, :]
        vw = vt_scr[:, r0:r0 + 3 * BLOCK]
        for g in range(GQA_GROUPS):
            q2 = jnp.concatenate([q_scr[g, 0, :, r0:r0 + BLOCK], q_scr[g, 1, :, r0:r0 + BLOCK]], axis=1)
            s_win = jnp.where(valid, _dot(kw, q2) * scale, NEG_INF)
            s_ctx = _dot(kc_scr[...], q2) * scale
            sink = jnp.where(first, sink_ref[0, :, g:g + 1], sink_ref[0, :, GQA_GROUPS + g:GQA_GROUPS + g + 1])
            m = jnp.maximum(jnp.maximum(jnp.max(s_win, axis=0, keepdims=True),
                                        jnp.max(s_ctx, axis=0, keepdims=True)), sink)
            p_win = jnp.exp(s_win - m)
            p_ctx = jnp.exp(s_ctx - m)
            den = (jnp.sum(p_win, axis=0, keepdims=True) + jnp.sum(p_ctx, axis=0, keepdims=True)
                   + jnp.exp(sink - m))
            o = (_dot(vw, p_win.astype(BF16)) + _dot(vct_scr[...], p_ctx.astype(BF16))) / den
            o = jnp.where(head_a, o[:, 0:BLOCK], o[:, BLOCK:2 * BLOCK])
            y_ref[r0:r0 + BLOCK, g * LANES:(g + 1) * LANES] = o.T.astype(BF16)


def _gqa_win(gqa_in, sink, cache_k_t, cache_v_t, cos, sin, layer, n_batch, seq_len):
    past = cache_k_t.shape[4]
    return pl.pallas_call(
        functools.partial(_gqa_win_kernel, seq_len=seq_len),
        out_shape=jax.ShapeDtypeStruct((n_batch * seq_len, MIX_PART), BF16),
        grid=(n_batch,),
        in_specs=[
            pl.BlockSpec((seq_len, GQA_IN), lambda b: (b, 0)),
            pl.BlockSpec((1, 1, GQA_Q_HEADS), lambda b: (layer, 0, 0)),
            pl.BlockSpec((1, 1, GQA_KV_HEADS, HEAD_DIM, past), lambda b: (b, layer, 0, 0, 0)),
            pl.BlockSpec((1, 1, GQA_KV_HEADS, HEAD_DIM, past), lambda b: (b, layer, 0, 0, 0)),
            pl.BlockSpec((seq_len, LANES), lambda b: (0, 0)),
            pl.BlockSpec((seq_len, LANES), lambda b: (0, 0)),
        ],
        out_specs=pl.BlockSpec((seq_len, MIX_PART), lambda b: (b, 0)),
        scratch_shapes=[
            pltpu.VMEM((GQA_GROUPS, 2, LANES, seq_len), BF16),
            pltpu.VMEM((seq_len + 2 * BLOCK, LANES), BF16),
            pltpu.VMEM((LANES, seq_len + 2 * BLOCK), BF16),
            pltpu.VMEM((past, LANES), BF16),
            pltpu.VMEM((LANES, past), BF16),
        ],
        compiler_params=_params(1),
        name=f"gqa_win_{seq_len}",
    )(gqa_in, sink, cache_k_t, cache_v_t, cos, sin)


def _mla_kernel(*refs, seq_len, latent, bq, nb):
    if latent:
        (x_ref, qn_ref, kvn_ref, wq_ref, wk_ref, wv_ref, ckv_c_ref, kr_c_ref, cos_ref, sin_ref,
         y_ref, q_all, k_all, v_all) = refs
        past = ckv_c_ref.shape[2]
    else:
        (x_ref, qn_ref, kvn_ref, wq_ref, wk_ref, wv_ref, *_,
         y_ref, ckv_out_ref, kr_out_ref, q_all, k_all, v_all) = refs
        past = 0
        _zero_other_layers(ckv_out_ref, kr_out_ref)
    keys = past + seq_len
    nope_w = MLA_HEADS * MLA_NOPE
    scale = (MLA_NOPE + MLA_ROPE) ** -0.5
    lane = _lane_iota()
    lo = lane < HEAD_DIM
    rope_lanes = [(lane >= h * MLA_ROPE) & (lane < (h + 1) * MLA_ROPE) for h in range(MLA_HEADS)]
    for i in range(nb):
        rows = slice(i * seq_len, (i + 1) * seq_len)
        q_scr, k_scr, v_scr = q_all.at[i], k_all.at[i], v_all.at[i]
        q_lat = x_ref[rows, 0:MLA_Q_RANK]
        kv_lat = x_ref[rows, MLA_Q_RANK:MLA_Q_RANK + MLA_KV_RANK]
        k_rope = x_ref[rows, MLA_Q_RANK + MLA_KV_RANK:MLA_IN_TILED]
        mq = _dot(_rms(q_lat, qn_ref[0]).astype(BF16), wq_ref[0])
        ckv = _rms(kv_lat, kvn_ref[0])
        q_rope = mq[:, nope_w:nope_w + LANES]
        if latent:
            cos = cos_ref[...]
            sin = sin_ref[...]
            q_rope = q_rope * cos + _rot_half(q_rope, MLA_ROPE // 2) * sin
            k_rope = k_rope * cos + _rot_half(k_rope, MLA_ROPE // 2) * sin
            ckv_c = ckv_c_ref[0, 0].astype(BF16)
            k_scr[0:past, 0:nope_w] = _dot(ckv_c, wk_ref[0]).astype(BF16)
            v_scr[0:past, :] = _dot(ckv_c, wv_ref[0]).astype(BF16)
            kr_c = kr_c_ref[0, 0]
            k_scr[0:past, nope_w:nope_w + LANES] = jnp.concatenate([kr_c] * MLA_HEADS, axis=1).astype(BF16)
        else:
            ckv_out_ref[i, 0] = ckv
            kr_out_ref[i, 0] = k_rope.T[0:MLA_ROPE, :]
        q_scr[:, 0:nope_w] = mq[:, 0:nope_w]
        q_scr[:, nope_w:nope_w + LANES] = q_rope
        ckv_b = ckv.astype(BF16)
        k_scr[past:keys, 0:nope_w] = _dot(ckv_b, wk_ref[0]).astype(BF16)
        v_scr[past:keys, :] = _dot(ckv_b, wv_ref[0]).astype(BF16)
        k_scr[past:keys, nope_w:nope_w + LANES] = k_rope.astype(BF16)

        ones = jnp.ones((keys, LANES), BF16)
        kr_all = k_scr[:, nope_w:nope_w + LANES]
        for t in range(MLA_HEADS // 2):
            cs = slice(t * LANES, (t + 1) * LANES)
            k_cat = jnp.concatenate([k_scr[:, cs], kr_all], axis=1)
            v_ext = jnp.concatenate([v_scr[:, cs], ones], axis=1)
            for r0 in range(0, seq_len, bq):
                qn = q_scr[r0:r0 + bq, cs]
                qr = q_scr[r0:r0 + bq, nope_w:nope_w + LANES]
                q2 = jnp.concatenate([
                    jnp.concatenate([jnp.where(lo, qn, 0.0), jnp.where(rope_lanes[2 * t], qr, 0.0)], axis=1),
                    jnp.concatenate([jnp.where(lo, 0.0, qn), jnp.where(rope_lanes[2 * t + 1], qr, 0.0)], axis=1),
                ], axis=0).astype(BF16)
                s = _dot_nt(q2, k_cat) * scale
                m = jnp.max(s, axis=-1, keepdims=True)
                pv = _dot(jnp.exp(s - m).astype(BF16), v_ext)
                o = pv[:, 0:LANES] / pv[:, LANES:LANES + 1]
                y_ref[i * seq_len + r0:i * seq_len + r0 + bq, cs] = _merge_heads(o, lo).astype(BF16)


def _mla(mla_in, q_norm, kv_norm, w_uq, w_uk, w_uv, cache, rope, carried, layer, n_batch, seq_len, nb):
    latent = cache is not None
    aliases = {}
    past = cache[0].shape[2] if latent else 0
    kern = functools.partial(_mla_kernel, seq_len=seq_len, latent=latent, bq=256, nb=nb)
    q_w = MLA_HEADS * (MLA_NOPE + MLA_ROPE)
    kv_w = MLA_HEADS * MLA_NOPE
    in_specs = [
        pl.BlockSpec((nb * seq_len, MLA_IN_TILED), lambda b: (b, 0)),
        pl.BlockSpec((1, 1, MLA_Q_RANK), lambda b: (layer, 0, 0)),
        pl.BlockSpec((1, 1, MLA_KV_RANK), lambda b: (layer, 0, 0)),
        pl.BlockSpec((1, MLA_Q_RANK, q_w), lambda b: (layer, 0, 0)),
        pl.BlockSpec((1, MLA_KV_RANK, kv_w), lambda b: (layer, 0, 0)),
        pl.BlockSpec((1, MLA_KV_RANK, kv_w), lambda b: (layer, 0, 0)),
    ]
    args = [mla_in, q_norm, kv_norm, w_uq, w_uk, w_uv]
    y_shape = jax.ShapeDtypeStruct((n_batch * seq_len, MIX_PART), BF16)
    y_spec = pl.BlockSpec((nb * seq_len, MIX_PART), lambda b: (b, 0))
    if latent:
        assert nb == 1
        in_specs += [
            pl.BlockSpec((1, 1, past, MLA_KV_RANK), lambda b: (b, layer, 0, 0)),
            pl.BlockSpec((1, 1, past, MLA_ROPE), lambda b: (b, layer, 0, 0)),
            pl.BlockSpec((seq_len, LANES), lambda b: (0, 0)),
            pl.BlockSpec((seq_len, LANES), lambda b: (0, 0)),
        ]
        args += [cache[0], cache[1], rope[0], rope[1]]
        out_shape, out_specs = y_shape, y_spec
    else:
        ckv_shape, ckv_spec = _layer_stacked((seq_len, MLA_KV_RANK), n_batch, layer, nb)
        kr_shape, kr_spec = _layer_stacked((MLA_ROPE, seq_len), n_batch, layer, nb)
        out_shape = [y_shape, ckv_shape, kr_shape]
        out_specs = [y_spec, ckv_spec, kr_spec]
        aliases = _carry_specs(in_specs, args, carried, 1)
    keys = past + seq_len
    return pl.pallas_call(
        kern, out_shape=out_shape, grid=(n_batch // nb,), in_specs=in_specs, out_specs=out_specs,
        input_output_aliases=aliases,
        scratch_shapes=[
            pltpu.VMEM((nb, seq_len, q_w), F32),
            pltpu.VMEM((nb, keys, kv_w + LANES), BF16),
            pltpu.VMEM((nb, keys, kv_w), BF16),
        ],
        compiler_params=_params(1),
        name=f"mla_{seq_len}",
    )(*args)


def _dft_matrix(seq_len):
    k = np.arange(seq_len, dtype=np.float64)[:, None]
    j = np.arange(seq_len, dtype=np.float64)[None, :]
    ang = np.pi * k * j / seq_len
    re = np.cos(ang)
    im = -np.sin(ang)
    im[0, :] = np.where(np.arange(seq_len) % 2 == 0, 1.0, -1.0)
    return np.concatenate([re, im], axis=0).astype(np.float32)


def _hyena_positions(seq_len):
    t = np.arange(seq_len, dtype=np.float64) / seq_len
    bands = np.arange(1, HY_BANDS + 1, dtype=np.float64)
    ang = 2.0 * math.pi * t[:, None] * bands
    z = np.concatenate([t[:, None], np.cos(ang), np.sin(ang)], axis=-1)
    return np.pad(z, ((0, 0), (0, HY_POS_PAD - HY_POS_DIM))).astype(np.float32)


def _rope_tables(n_tokens, dim):
    rows = np.repeat(np.arange(n_tokens // GRID_W, dtype=np.float64), GRID_W)
    cols = np.tile(np.arange(GRID_W, dtype=np.float64), n_tokens // GRID_W)
    quarter = dim // 4
    inv = ROPE_BASE ** (-np.arange(quarter, dtype=np.float64) / quarter)
    ang = np.concatenate([rows[:, None] * inv, cols[:, None] * inv], axis=-1)
    cos = np.concatenate([np.cos(ang), np.cos(ang)], axis=-1)
    sin = np.concatenate([-np.sin(ang), np.sin(ang)], axis=-1)
    reps = LANES // dim
    return (np.tile(cos, (1, reps)).astype(np.float32), np.tile(sin, (1, reps)).astype(np.float32))


def kernel(x_prompt, x_sample, state_ret, cache_gqa_k, cache_gqa_v, cache_mla_ckv, cache_mla_krope, c, c_ctx, ada_w, ada_b, norm_g, w_in, ret_decay_logit, ret_gn_g, hy_short_w, hy_short_b, hy_w1, hy_b1, hy_w2, hy_b2, hy_w3, hy_decay, hy_bias, gqa_sink, mla_q_norm, mla_kv_norm, mla_w_uq, mla_w_uk, mla_w_uv, w_out, ffn_w_up, ffn_conv_w, ffn_conv_b, ffn_w_down):
    n_p, len_p, _ = x_prompt.shape
    n_s, len_s, _ = x_sample.shape

    w_in_t = jnp.swapaxes(w_in, 1, 2)
    cache_k_t = jnp.swapaxes(cache_gqa_k, -1, -2)
    cache_v_t = jnp.swapaxes(cache_gqa_v, -1, -2)
    w_out_b = w_out.astype(BF16)
    w_up_b = ffn_w_up.astype(BF16)
    w_down_b = ffn_w_down.astype(BF16)
    uq = mla_w_uq.reshape(DEPTH, MLA_Q_RANK, MLA_HEADS, MLA_NOPE + MLA_ROPE)
    w_uq_b = jnp.concatenate(
        [uq[..., :MLA_NOPE].reshape(DEPTH, MLA_Q_RANK, MLA_HEADS * MLA_NOPE),
         uq[..., MLA_NOPE:].reshape(DEPTH, MLA_Q_RANK, MLA_HEADS * MLA_ROPE)], axis=-1).astype(BF16)
    w_uk_b = mla_w_uk.astype(BF16)
    w_uv_b = mla_w_uv.astype(BF16)

    cond = jnp.concatenate([c_ctx[None], c, jnp.zeros((N_COND - 1 - n_s, D_MODEL), F32)], axis=0)
    mod = _modulation(cond, ada_w, ada_b).reshape(DEPTH, N_COND, 6, D_MODEL)

    w1p = jnp.pad(hy_w1, ((0, 0), (0, HY_POS_PAD - HY_POS_DIM), (0, 0)))
    b1 = hy_b1.reshape(DEPTH, 1, HY_FILTER_HIDDEN)
    b2 = hy_b2.reshape(DEPTH, 1, HY_FILTER_HIDDEN)
    groups = {}
    for seq_len in (len_p, len_s):
        f32mat = _dft_matrix(seq_len)
        fmat = jnp.asarray(f32mat).astype(BF16)
        fmat_t = jnp.asarray(np.ascontiguousarray(f32mat.T)).astype(BF16)
        tables = _hyena_tables(seq_len, jnp.asarray(_hyena_positions(seq_len)), w1p, b1, hy_w2, b2,
                               hy_w3, hy_decay, fmat)
        groups[seq_len] = (tables, fmat, fmat_t)

    rope_g = tuple(jnp.asarray(t) for t in _rope_tables(len_s, HEAD_DIM))
    rope_m = tuple(jnp.asarray(t) for t in _rope_tables(len_s, MLA_ROPE))

    gn_g = ret_gn_g.reshape(DEPTH, 1, RET_WIDTH)
    short_b = hy_short_b.reshape(DEPTH, 1, HY_IN)
    sink = gqa_sink.reshape(DEPTH, 1, GQA_Q_HEADS)
    q_norm = mla_q_norm.reshape(DEPTH, 1, MLA_Q_RANK)
    kv_norm = mla_kv_norm.reshape(DEPTH, 1, MLA_KV_RANK)
    conv_b = ffn_conv_b.reshape(DEPTH, 1, D_FF)

    xp = x_prompt.reshape(n_p * len_p, D_MODEL)
    xs = x_sample.reshape(n_s * len_s, D_MODEL)
    tm_p = 1024
    tm_s = 512
    nb_p = 2
    cond_p = lambda i: 0
    cond_s = lambda i: 1 + (i * tm_s) // len_s
    tm_ffn = 1024
    cond_s_ffn = lambda i: 1 + (i * tm_ffn) // len_s

    st = gkv = ckr = None
    for l in range(DEPTH):
        tables, fmat, fmat_t = groups[len_p]
        ret_in, hy_in, gqa_in, mla_in = _in_proj(xp, mod, norm_g, w_in_t, l, tm_p, cond_p)
        y_ret, *st = _retention(ret_in, ret_decay_logit, gn_g, None, st, l, n_p, len_p, nb_p)
        y_hy = _hyena(hy_in, hy_short_w, short_b, hy_bias, tables, fmat, fmat_t, l, n_p, len_p, nb_p)
        y_gqa, *gkv = _gqa_ctx(gqa_in, sink, gkv, l, n_p, len_p, nb_p)
        y_mla, *ckr = _mla(mla_in, q_norm, kv_norm, w_uq_b, w_uk_b, w_uv_b, None, None, ckr, l, n_p, len_p,
                           nb_p)
        xp = _channel_mix(xp, (y_ret, y_hy, y_gqa, y_mla), mod, norm_g, w_out_b, w_up_b, ffn_conv_w, conv_b,
                          w_down_b, l, tm_ffn, len_p, cond_p)
        tables, fmat, fmat_t = groups[len_s]
        ret_in, hy_in, gqa_in, mla_in = _in_proj(xs, mod, norm_g, w_in_t, l, tm_s, cond_s)
        y_ret = _retention(ret_in, ret_decay_logit, gn_g, state_ret, None, l, n_s, len_s, 1)
        y_hy = _hyena(hy_in, hy_short_w, short_b, hy_bias, tables, fmat, fmat_t, l, n_s, len_s, 1)
        y_gqa = _gqa_win(gqa_in, sink, cache_k_t, cache_v_t, rope_g[0], rope_g[1], l, n_s, len_s)
        y_mla = _mla(mla_in, q_norm, kv_norm, w_uq_b, w_uk_b, w_uv_b, (cache_mla_ckv, cache_mla_krope),
                     rope_m, None, l, n_s, len_s, 1)
        xs = _channel_mix(xs, (y_ret, y_hy, y_gqa, y_mla), mod, norm_g, w_out_b, w_up_b, ffn_conv_w, conv_b,
                          w_down_b, l, tm_ffn, len_s, cond_s_ffn)

    return (xp.reshape(n_p, len_p, D_MODEL), xs.reshape(n_s, len_s, D_MODEL),
            st[0], jnp.swapaxes(gkv[0], -1, -2), jnp.swapaxes(gkv[1], -1, -2),
            ckr[0], jnp.swapaxes(ckr[1], -1, -2))
```

```python
import functools
import math

import numpy as np
import jax
import jax.numpy as jnp
from jax import lax
from jax.experimental import pallas as pl
from jax.experimental.pallas import tpu as pltpu

F32 = jnp.float32
BF16 = jnp.bfloat16

D_MODEL = 1024
DEPTH = 4
GRID_W = 64
EPS = 1e-6
NEG_INF = -1e30
ROPE_BASE = 10000.0
BLOCK = 128
WINDOW = 128
HEAD_DIM = 64
LANES = 128
RET_HEADS = 4
RET_WIDTH = RET_HEADS * HEAD_DIM
HY_WIDTH = 256
HY_ORDER = 2
HY_BANDS = 8
HY_POS_DIM = 1 + 2 * HY_BANDS
HY_POS_PAD = 32
HY_FILTER_HIDDEN = 64
GQA_Q_HEADS = 4
GQA_KV_HEADS = 2
GQA_GROUPS = GQA_Q_HEADS // GQA_KV_HEADS
MLA_HEADS = 4
MLA_Q_RANK = 256
MLA_KV_RANK = 128
MLA_NOPE = 64
MLA_ROPE = 32
MLA_V = 64
D_FF = 2816
RET_IN = 4 * RET_WIDTH
HY_IN = (HY_ORDER + 1) * HY_WIDTH
GQA_IN = (GQA_Q_HEADS + 2 * GQA_KV_HEADS) * HEAD_DIM
MLA_IN = MLA_Q_RANK + MLA_KV_RANK + MLA_ROPE
MLA_IN_TILED = MLA_Q_RANK + MLA_KV_RANK + MLA_HEADS * MLA_ROPE
IN_WIDTH = RET_IN + HY_IN + GQA_IN + MLA_IN
IN_WIDTH_TILED = RET_IN + HY_IN + GQA_IN + MLA_IN_TILED
GQA_HEAD_ORDER = (0, 2, 1, 3)
MIX_PART = 256
N_COND = 8
FFN_CHUNK = 256
N_FFN_CHUNKS = D_FF // FFN_CHUNK

VMEM_LIMIT = 56 * 1024 * 1024


def _params(n_axes=1):
    return pltpu.CompilerParams(
        dimension_semantics=("arbitrary",) * n_axes, vmem_limit_bytes=VMEM_LIMIT)


def _dot(a, b):
    return jnp.dot(a, b, preferred_element_type=F32)


def _dot_nt(a, b):
    return lax.dot_general(a, b, (((1,), (1,)), ((), ())), preferred_element_type=F32)


def _dot_tn(a, b):
    return lax.dot_general(a, b, (((0,), (0,)), ((), ())), preferred_element_type=F32)


def _rms(x, g):
    return x * lax.rsqrt(jnp.mean(x * x, axis=-1, keepdims=True) + EPS) * g


def _sigmoid(x):
    return 1.0 / (1.0 + jnp.exp(-x))


def _silu(x):
    return x * _sigmoid(x)


def _shift_rows(x, seq_len):
    n = x.shape[0]
    assert seq_len & (seq_len - 1) == 0
    pos = lax.broadcasted_iota(jnp.int32, (n, 1), 0) & (seq_len - 1)
    prev = jnp.where(pos != 0, pltpu.roll(x, 1, axis=0), 0.0)
    nxt = jnp.where(pos != seq_len - 1, pltpu.roll(x, n - 1, axis=0), 0.0)
    return prev, nxt


def _rot_half(x, half):
    lane = lax.broadcasted_iota(jnp.int32, x.shape, 1) & (2 * half - 1)
    return jnp.where(lane < half, pltpu.roll(x, LANES - half, axis=1), pltpu.roll(x, half, axis=1))


def _lane_iota():
    return lax.broadcasted_iota(jnp.int32, (1, LANES), 1)


def _split_heads(t, lo):
    return jnp.concatenate([jnp.where(lo, t, 0.0), jnp.where(lo, 0.0, t)], axis=0)


def _merge_heads(o, lo):
    n = o.shape[0] // 2
    return jnp.where(lo, o[:n], o[n:])


def _per_head(vals, width):
    lane = lax.broadcasted_iota(jnp.int32, (1, width), 1)
    out = vals[-1]
    for h in range(len(vals) - 2, -1, -1):
        out = jnp.where(lane < (h + 1) * HEAD_DIM, vals[h], out)
    return out


def _mod_kernel(cond_ref, w_ref, b_ref, out_ref):
    s = _silu(cond_ref[...]).astype(BF16)
    out_ref[0] = _dot(s, w_ref[0].astype(BF16)) + b_ref[0]


def _modulation(cond, ada_w, ada_b):
    tn = 1536
    return pl.pallas_call(
        _mod_kernel,
        out_shape=jax.ShapeDtypeStruct((DEPTH, N_COND, 6 * D_MODEL), F32),
        grid=(DEPTH, 6 * D_MODEL // tn),
        in_specs=[
            pl.BlockSpec((N_COND, D_MODEL), lambda l, j: (0, 0)),
            pl.BlockSpec((1, D_MODEL, tn), lambda l, j: (l, 0, j)),
            pl.BlockSpec((1, 1, tn), lambda l, j: (l, 0, j)),
        ],
        out_specs=pl.BlockSpec((1, N_COND, tn), lambda l, j: (l, 0, j)),
        compiler_params=_params(2),
        name="modulation",
    )(cond, ada_w, ada_b.reshape(DEPTH, 1, 6 * D_MODEL))


def _filter_kernel(z_ref, w1_ref, b1_ref, w2_ref, b2_ref, w3_ref, dec_ref, f_ref, out_ref, *, seq_len):
    hi = lax.Precision.HIGHEST
    h = jnp.sin(jnp.dot(z_ref[...], w1_ref[0], precision=hi, preferred_element_type=F32) + b1_ref[0])
    h = jnp.sin(jnp.dot(h, w2_ref[0], precision=hi, preferred_element_type=F32) + b2_ref[0])
    h = jnp.dot(h, w3_ref[0], precision=hi, preferred_element_type=F32)
    row = lax.broadcasted_iota(jnp.int32, (seq_len, 1), 0)
    t = row.astype(F32) / seq_len
    dec = jnp.abs(dec_ref[0])
    win_f = jnp.exp(-t * dec[0:1, :])
    win_b = jnp.exp(-t * dec[1:2, :])
    fm = f_ref[...]
    inv = 1.0 / seq_len
    scale = jnp.where(row == 0, 0.5 * inv, inv)
    for o in range(HY_ORDER):
        base = o * 2 * HY_WIDTH
        hf = h[:, base:base + HY_WIDTH] * win_f
        hb = h[:, base + HY_WIDTH:base + 2 * HY_WIDTH] * win_b
        hbs = jnp.where(row != 0, pltpu.roll(hb, 1, axis=0), 0.0)
        sf = _dot(fm, hf.astype(BF16))
        sb = _dot(fm, hbs.astype(BF16))
        g_re = sf[:seq_len] + sb[:seq_len]
        g_im = sf[seq_len:] - sb[seq_len:]
        g_ny = sf[seq_len:] + sb[seq_len:]
        a = g_re * scale
        out_ref[0, 3 * o] = a
        out_ref[0, 3 * o + 1] = jnp.where(row == 0, g_ny * (0.5 * inv), a)
        out_ref[0, 3 * o + 2] = jnp.where(row == 0, 0.0, g_im * inv)


def _hyena_tables(seq_len, z, w1p, b1, w2, b2, w3, decay, fmat):
    hid = HY_FILTER_HIDDEN
    wide = HY_ORDER * 2 * HY_WIDTH
    return pl.pallas_call(
        functools.partial(_filter_kernel, seq_len=seq_len),
        out_shape=jax.ShapeDtypeStruct((DEPTH, 3 * HY_ORDER, seq_len, HY_WIDTH), F32),
        grid=(DEPTH,),
        in_specs=[
            pl.BlockSpec((seq_len, HY_POS_PAD), lambda l: (0, 0)),
            pl.BlockSpec((1, HY_POS_PAD, hid), lambda l: (l, 0, 0)),
            pl.BlockSpec((1, 1, hid), lambda l: (l, 0, 0)),
            pl.BlockSpec((1, hid, hid), lambda l: (l, 0, 0)),
            pl.BlockSpec((1, 1, hid), lambda l: (l, 0, 0)),
            pl.BlockSpec((1, hid, wide), lambda l: (l, 0, 0)),
            pl.BlockSpec((1, 2, HY_WIDTH), lambda l: (l, 0, 0)),
            pl.BlockSpec((2 * seq_len, seq_len), lambda l: (0, 0)),
        ],
        out_specs=pl.BlockSpec((1, 3 * HY_ORDER, seq_len, HY_WIDTH), lambda l: (l, 0, 0, 0)),
        compiler_params=_params(1),
        name=f"hyena_tables_{seq_len}",
    )(z, w1p, b1, w2, b2, w3, decay, fmat)


def _layer_stacked(tail, n_batch, layer, nb):
    zeros = (0,) * len(tail)
    n_layers = DEPTH if layer == 0 else 1
    return (jax.ShapeDtypeStruct((n_batch, DEPTH) + tail, F32),
            pl.BlockSpec((nb, n_layers) + tail, lambda b: (b, layer) + zeros))


def _zero_other_layers(*out_refs):
    for ref in out_refs:
        if ref.shape[1] > 1:
            ref[:, 1:] = jnp.zeros((ref.shape[0], ref.shape[1] - 1) + ref.shape[2:], ref.dtype)


def _carry_specs(in_specs, args, carried, first_out):
    aliases = {}
    if carried is not None:
        for k, arr in enumerate(carried):
            aliases[len(args)] = first_out + k
            in_specs.append(pl.BlockSpec(memory_space=pl.ANY))
            args.append(arr)
    return aliases


def _part(kernel, in_specs, args, out_shape, out_specs, scratch=(), aliases=None):
    as_list = lambda v: list(v) if isinstance(v, (list, tuple)) else [v]
    return dict(kernel=kernel, in_specs=list(in_specs), args=list(args), out_shape=as_list(out_shape),
                out_specs=as_list(out_specs), scratch=list(scratch), aliases=dict(aliases or {}))


def _run_parts(parts, grid, name):
    n_in = sum(len(p["args"]) for p in parts)
    n_out = sum(len(p["out_shape"]) for p in parts)

    def fused(*refs):
        i = o = s = 0
        for p in parts:
            a, b, c = len(p["args"]), len(p["out_shape"]), len(p["scratch"])
            p["kernel"](*refs[i:i + a], *refs[n_in + o:n_in + o + b],
                        *refs[n_in + n_out + s:n_in + n_out + s + c])
            i, o, s = i + a, o + b, s + c

    aliases = {}
    i = o = 0
    for p in parts:
        aliases.update({i + k: o + v for k, v in p["aliases"].items()})
        i, o = i + len(p["args"]), o + len(p["out_shape"])
    outs = pl.pallas_call(
        fused,
        out_shape=[s for p in parts for s in p["out_shape"]],
        grid=grid,
        in_specs=[s for p in parts for s in p["in_specs"]],
        out_specs=[s for p in parts for s in p["out_specs"]],
        input_output_aliases=aliases,
        scratch_shapes=[s for p in parts for s in p["scratch"]],
        compiler_params=_params(len(grid)),
        name=name,
    )(*[a for p in parts for a in p["args"]])
    split, o = [], 0
    for p in parts:
        split.append(list(outs[o:o + len(p["out_shape"])]))
        o += len(p["out_shape"])
    return split


def _in_proj_kernel(x_ref, mod_ref, g_ref, w_ref, ret_ref, hy_ref, gqa_ref, mla_ref, w_scr):
    @pl.when(pl.program_id(0) == 0)
    def _():
        g0 = RET_IN + HY_IN
        m0 = g0 + GQA_IN
        kr0 = m0 + MLA_Q_RANK + MLA_KV_RANK
        step = 2 * LANES
        for r0 in list(range(0, g0, step)) + list(range(g0 + GQA_Q_HEADS * HEAD_DIM, kr0, LANES)):
            n = step if r0 < g0 else LANES
            w_scr[r0:r0 + n, :] = w_ref[0, r0:r0 + n, :].astype(BF16)
        for dst, j in enumerate(GQA_HEAD_ORDER):
            w_scr[g0 + dst * HEAD_DIM:g0 + (dst + 1) * HEAD_DIM, :] = (
                w_ref[0, g0 + j * HEAD_DIM:g0 + (j + 1) * HEAD_DIM, :].astype(BF16))
        kr = w_ref[0, kr0:kr0 + MLA_ROPE, :].astype(BF16)
        for hd in range(MLA_HEADS):
            w_scr[kr0 + hd * MLA_ROPE:kr0 + (hd + 1) * MLA_ROPE, :] = kr

    shift = mod_ref[0, 0, 0:1, :]
    scale = mod_ref[0, 0, 1:2, :]
    h = (_rms(x_ref[...], g_ref[0, 0:1, :]) * (1.0 + scale) + shift).astype(BF16)
    c0 = 0
    for ref in (ret_ref, hy_ref, gqa_ref, mla_ref):
        width = ref.shape[1]
        ref[...] = _dot_nt(h, w_scr[c0:c0 + width, :])
        c0 += width


def _in_proj(x, mod, norm_g, w_in, layer, tm, cond_of_tile):
    rows = x.shape[0]
    widths = (RET_IN, HY_IN, GQA_IN, MLA_IN_TILED)
    return pl.pallas_call(
        _in_proj_kernel,
        out_shape=[jax.ShapeDtypeStruct((rows, w), F32) for w in widths],
        grid=(rows // tm,),
        in_specs=[
            pl.BlockSpec((tm, D_MODEL), lambda i: (i, 0)),
            pl.BlockSpec((1, 1, 6, D_MODEL), lambda i: (layer, cond_of_tile(i), 0, 0)),
            pl.BlockSpec((1, 4, D_MODEL), lambda i: (layer, 0, 0)),
            pl.BlockSpec((1, IN_WIDTH, D_MODEL), lambda i: (layer, 0, 0), pipeline_mode=pl.Buffered(1)),
        ],
        out_specs=[pl.BlockSpec((tm, w), lambda i: (i, 0)) for w in widths],
        scratch_shapes=[pltpu.VMEM((IN_WIDTH_TILED, D_MODEL), BF16)],
        compiler_params=_params(1),
        name="in_proj",
    )(x, mod, norm_g, w_in)


def _channel_kernel(x_ref, m0_ref, m1_ref, m2_ref, m3_ref, mod_ref, g_ref, wo_ref, wu_ref, cw_ref, cb_ref,
                    wd_ref, out_ref, act_ref, *, seq_len):
    y = None
    for i, m_ref in enumerate((m0_ref, m1_ref, m2_ref, m3_ref)):
        r0 = i * MIX_PART
        if i == 2:
            w = jnp.concatenate([wo_ref[0, r0 + j * HEAD_DIM:r0 + (j + 1) * HEAD_DIM, :]
                                 for j in GQA_HEAD_ORDER], axis=0)
        else:
            w = wo_ref[0, r0:r0 + MIX_PART, :]
        part = _dot(m_ref[...], w)
        y = part if y is None else y + part
    gate1 = mod_ref[0, 0, 2:3, :]
    shift2 = mod_ref[0, 0, 3:4, :]
    scale2 = mod_ref[0, 0, 4:5, :]
    xm = x_ref[...] + gate1 * _rms(y, g_ref[0, 1:2, :])
    out_ref[...] = xm
    h2 = (_rms(xm, g_ref[0, 2:3, :]) * (1.0 + scale2) + shift2).astype(BF16)
    for c in range(N_FFN_CHUNKS):
        sl = slice(c * FFN_CHUNK, (c + 1) * FFN_CHUNK)
        gate = _dot(h2, wu_ref[0, :, sl])
        up = _dot(h2, wu_ref[0, :, D_FF + c * FFN_CHUNK:D_FF + (c + 1) * FFN_CHUNK])
        prev, nxt = _shift_rows(gate, seq_len)
        gate = (prev * cw_ref[0, 0:1, sl] + gate * cw_ref[0, 1:2, sl] + nxt * cw_ref[0, 2:3, sl]
                + cb_ref[0, :, sl])
        act_ref[:, sl] = (_silu(gate) * up).astype(BF16)
    ffn = _dot(act_ref[...], wd_ref[0])
    gate2 = mod_ref[0, 0, 5:6, :]
    out_ref[...] = out_ref[...] + gate2 * _rms(ffn, g_ref[0, 3:4, :])


def _channel_mix(x, mixes, mod, norm_g, w_out, w_up, conv_w, conv_b, w_down, layer, tm, seq_len, cond_of_tile):
    rows = x.shape[0]
    resident = dict(pipeline_mode=pl.Buffered(1))
    return pl.pallas_call(
        functools.partial(_channel_kernel, seq_len=seq_len),
        out_shape=jax.ShapeDtypeStruct((rows, D_MODEL), F32),
        grid=(rows // tm,),
        in_specs=[pl.BlockSpec((tm, D_MODEL), lambda i: (i, 0))]
        + [pl.BlockSpec((tm, MIX_PART), lambda i: (i, 0))] * 4
        + [
            pl.BlockSpec((1, 1, 6, D_MODEL), lambda i: (layer, cond_of_tile(i), 0, 0)),
            pl.BlockSpec((1, 4, D_MODEL), lambda i: (layer, 0, 0)),
            pl.BlockSpec((1, D_MODEL, D_MODEL), lambda i: (layer, 0, 0), **resident),
            pl.BlockSpec((1, D_MODEL, 2 * D_FF), lambda i: (layer, 0, 0), **resident),
            pl.BlockSpec((1, 3, D_FF), lambda i: (layer, 0, 0)),
            pl.BlockSpec((1, 1, D_FF), lambda i: (layer, 0, 0)),
            pl.BlockSpec((1, D_FF, D_MODEL), lambda i: (layer, 0, 0), **resident),
        ],
        out_specs=pl.BlockSpec((tm, D_MODEL), lambda i: (i, 0)),
        scratch_shapes=[pltpu.VMEM((tm, D_FF), BF16)],
        compiler_params=_params(1),
        name="channel_mix",
    )(x, *mixes, mod, norm_g, w_out, w_up, conv_w, conv_b, w_down)


def _log_gamma(dl_ref):
    dl = dl_ref[0]
    return jnp.minimum(dl, 0.0) - jnp.log(1.0 + jnp.exp(-jnp.abs(dl)))


def _pair_decay(lg, t, r0, bq, seq_len):
    top = lax.broadcasted_iota(jnp.int32, (2 * bq, 1), 0) < bq
    lgf = jnp.where(top, lg[0:1, 2 * t:2 * t + 1], lg[0:1, 2 * t + 1:2 * t + 2])
    lgb = jnp.where(top, lg[1:2, 2 * t:2 * t + 1], lg[1:2, 2 * t + 1:2 * t + 2])
    rowf = ((lax.broadcasted_iota(jnp.int32, (2 * bq, 1), 0) & (bq - 1)) + r0).astype(F32)
    colf = lax.broadcasted_iota(jnp.int32, (1, seq_len), 1).astype(F32)
    lag = rowf - colf
    decay = jnp.exp(jnp.where(lag >= 0.0, lag * lgf, -lag * lgb))
    return jnp.where(lag == 0.0, 2.0, decay)


def _head_norm(o, lo):
    inv = 1.0 / HEAD_DIM
    s_all = jnp.sum(o, axis=-1, keepdims=True)
    s_lo = jnp.sum(jnp.where(lo, o, 0.0), axis=-1, keepdims=True)
    d = o - jnp.where(lo, s_lo, s_all - s_lo) * inv
    d2 = d * d
    v_all = jnp.sum(d2, axis=-1, keepdims=True)
    v_lo = jnp.sum(jnp.where(lo, d2, 0.0), axis=-1, keepdims=True)
    return d * lax.rsqrt(jnp.where(lo, v_lo, v_all - v_lo) * inv + EPS)


def _ret_ctx_kernel(*refs, seq_len, nb):
    x_ref, dl_ref, gn_ref, *_, y_ref, st_ref, dec_scr, kdec_scr = refs
    lo = _lane_iota() < HEAD_DIM
    _zero_other_layers(st_ref)

    @pl.when(pl.program_id(0) == 0)
    def _():
        lg = _log_gamma(dl_ref)
        posf = lax.broadcasted_iota(jnp.int32, (seq_len, 1), 0).astype(F32)
        lgf = _per_head([lg[0:1, h:h + 1] for h in range(RET_HEADS)], RET_WIDTH)
        lgb = _per_head([lg[1:2, h:h + 1] for h in range(RET_HEADS)], RET_WIDTH)
        kdec_scr[0] = jnp.exp((seq_len - 1.0 - posf) * lgf)
        kdec_scr[1] = jnp.exp(posf * lgb)
        for t in range(RET_HEADS // 2):
            dec_scr[t] = _pair_decay(lg, t, 0, seq_len, seq_len)

    for i in range(nb):
        rows = slice(i * seq_len, (i + 1) * seq_len)
        for t in range(RET_HEADS // 2):
            cs = slice(t * LANES, (t + 1) * LANES)
            q = x_ref[rows, cs]
            k = x_ref[rows, RET_WIDTH + t * LANES:RET_WIDTH + (t + 1) * LANES] * (HEAD_DIM ** -0.5)
            vb = x_ref[rows, 2 * RET_WIDTH + t * LANES:2 * RET_WIDTH + (t + 1) * LANES].astype(BF16)
            gate = x_ref[rows, 3 * RET_WIDTH + t * LANES:3 * RET_WIDTH + (t + 1) * LANES]
            s = _dot_nt(_split_heads(q, lo).astype(BF16), k.astype(BF16)) * dec_scr[t]
            o = _merge_heads(_dot(s.astype(BF16), vb), lo)
            y = _silu(gate) * (_head_norm(o, lo) * gn_ref[0, :, cs])
            y_ref[rows, cs] = y.astype(BF16)
            for d in range(2):
                st = _dot_tn((k * kdec_scr[d, :, cs]).astype(BF16), vb)
                st_ref[i, 0, d, 2 * t] = st[0:HEAD_DIM, 0:HEAD_DIM]
                st_ref[i, 0, d, 2 * t + 1] = st[HEAD_DIM:LANES, HEAD_DIM:LANES]


def _ret_lat_kernel(x_ref, dl_ref, gn_ref, s0_ref, y_ref, *, seq_len, bq):
    lo = _lane_iota() < HEAD_DIM
    lg = _log_gamma(dl_ref)
    zero = jnp.zeros((HEAD_DIM, HEAD_DIM), F32)
    for t in range(RET_HEADS // 2):
        cs = slice(t * LANES, (t + 1) * LANES)
        kb = (x_ref[:, RET_WIDTH + t * LANES:RET_WIDTH + (t + 1) * LANES] * (HEAD_DIM ** -0.5)).astype(BF16)
        vb = x_ref[:, 2 * RET_WIDTH + t * LANES:2 * RET_WIDTH + (t + 1) * LANES].astype(BF16)
        lgf = _per_head([lg[0:1, 2 * t:2 * t + 1], lg[0:1, 2 * t + 1:2 * t + 2]], LANES)
        lgb = _per_head([lg[1:2, 2 * t:2 * t + 1], lg[1:2, 2 * t + 1:2 * t + 2]], LANES)
        s0 = []
        for d in range(2):
            a = s0_ref[0, 0, d, 2 * t]
            b = s0_ref[0, 0, d, 2 * t + 1]
            s0.append(jnp.concatenate([jnp.concatenate([a, zero], axis=1),
                                       jnp.concatenate([zero, b], axis=1)], axis=0).astype(BF16))
        for r0 in range(0, seq_len, bq):
            q = x_ref[r0:r0 + bq, cs]
            s = _dot_nt(_split_heads(q, lo).astype(BF16), kb) * _pair_decay(lg, t, r0, bq, seq_len)
            o = _merge_heads(_dot(s.astype(BF16), vb), lo)
            rowf = (lax.broadcasted_iota(jnp.int32, (bq, 1), 0) + r0).astype(F32)
            qf = (q * jnp.exp((rowf + 1.0) * lgf)).astype(BF16)
            qr = (q * jnp.exp((seq_len - rowf) * lgb)).astype(BF16)
            o = o + _dot(qf, s0[0]) + _dot(qr, s0[1])
            gate = x_ref[r0:r0 + bq, 3 * RET_WIDTH + t * LANES:3 * RET_WIDTH + (t + 1) * LANES]
            y = _silu(gate) * (_head_norm(o, lo) * gn_ref[0, :, cs])
            y_ref[r0:r0 + bq, cs] = y.astype(BF16)


def _retention(ret_in, decay_logit, gn_g, state0, carried, layer, n_batch, seq_len, nb):
    latent = state0 is not None
    in_specs = [
        pl.BlockSpec((nb * seq_len, RET_IN), lambda b: (b, 0)),
        pl.BlockSpec((1, 2, RET_HEADS), lambda b: (layer, 0, 0)),
        pl.BlockSpec((1, 1, RET_WIDTH), lambda b: (layer, 0, 0)),
    ]
    args = [ret_in, decay_logit, gn_g]
    y_shape = jax.ShapeDtypeStruct((n_batch * seq_len, MIX_PART), BF16)
    y_spec = pl.BlockSpec((nb * seq_len, MIX_PART), lambda b: (b, 0))
    if latent:
        assert nb == 1
        kern = functools.partial(_ret_lat_kernel, seq_len=seq_len, bq=256)
        in_specs.append(pl.BlockSpec((1, 1, 2, RET_HEADS, HEAD_DIM, HEAD_DIM),
                                     lambda b: (b, layer, 0, 0, 0, 0)))
        args.append(state0)
        out_shape, out_specs, aliases, scratch = y_shape, y_spec, {}, []
    else:
        kern = functools.partial(_ret_ctx_kernel, seq_len=seq_len, nb=nb)
        st_shape, st_spec = _layer_stacked((2, RET_HEADS, HEAD_DIM, HEAD_DIM), n_batch, layer, nb)
        out_shape = [y_shape, st_shape]
        out_specs = [y_spec, st_spec]
        aliases = _carry_specs(in_specs, args, carried, 1)
        scratch = [pltpu.VMEM((RET_HEADS // 2, 2 * seq_len, seq_len), F32),
                   pltpu.VMEM((2, seq_len, RET_WIDTH), F32)]
    return _part(kern, in_specs, args, out_shape, out_specs, scratch, aliases)


def _hyena_kernel(x_ref, sw_ref, sb_ref, bias_ref, tab_ref, f_ref, ft_ref, y_ref, *, seq_len, nb):
    for i in range(nb):
        rows = slice(i * seq_len, (i + 1) * seq_len)
        x = x_ref[rows, :]
        prev, nxt = _shift_rows(x, seq_len)
        u = prev * sw_ref[0, 0:1, :] + x * sw_ref[0, 1:2, :] + nxt * sw_ref[0, 2:3, :] + sb_ref[0]
        x1 = u[:, 0:HY_WIDTH]
        x2 = u[:, HY_WIDTH:2 * HY_WIDTH]
        z = u[:, 2 * HY_WIDTH:3 * HY_WIDTH]
        for o, gate in enumerate((x1, x2)):
            spec = _dot(f_ref[...], z.astype(BF16))
            s_re = spec[:seq_len]
            s_im = spec[seq_len:]
            a = tab_ref[0, 3 * o]
            a_ny = tab_ref[0, 3 * o + 1]
            b = tab_ref[0, 3 * o + 2]
            y_re = (s_re * a - s_im * b).astype(BF16)
            y_im = (s_re * b + s_im * a_ny).astype(BF16)
            conv = _dot(ft_ref[:, 0:seq_len], y_re) + _dot(ft_ref[:, seq_len:2 * seq_len], y_im)
            z = gate * (conv + z * bias_ref[0, o:o + 1, :])
        y_ref[rows, :] = z.astype(BF16)


def _hyena(hy_in, short_w, short_b, hy_bias, tables, fmat, fmat_t, layer, n_batch, seq_len, nb):
    const = dict(pipeline_mode=pl.Buffered(1))
    return _part(
        functools.partial(_hyena_kernel, seq_len=seq_len, nb=nb),
        [
            pl.BlockSpec((nb * seq_len, HY_IN), lambda b: (b, 0)),
            pl.BlockSpec((1, 3, HY_IN), lambda b: (layer, 0, 0)),
            pl.BlockSpec((1, 1, HY_IN), lambda b: (layer, 0, 0)),
            pl.BlockSpec((1, HY_ORDER, HY_WIDTH), lambda b: (layer, 0, 0)),
            pl.BlockSpec((1, 3 * HY_ORDER, seq_len, HY_WIDTH), lambda b: (layer, 0, 0, 0), **const),
            pl.BlockSpec((2 * seq_len, seq_len), lambda b: (0, 0), **const),
            pl.BlockSpec((seq_len, 2 * seq_len), lambda b: (0, 0), **const),
        ],
        [hy_in, short_w, short_b, hy_bias, tables, fmat, fmat_t],
        jax.ShapeDtypeStruct((n_batch * seq_len, MIX_PART), BF16),
        pl.BlockSpec((nb * seq_len, MIX_PART), lambda b: (b, 0)))


def _gqa_ctx_kernel(*refs, seq_len, nb):
    x_ref, sink_ref, *_, y_ref, k_out_ref, v_out_ref = refs
    nq = GQA_Q_HEADS * HEAD_DIM
    scale = HEAD_DIM ** -0.5
    head_a = lax.broadcasted_iota(jnp.int32, (LANES, 1), 0) < HEAD_DIM
    first = lax.broadcasted_iota(jnp.int32, (1, 2 * seq_len), 1) < seq_len
    ones = jnp.ones((LANES, seq_len), BF16)
    _zero_other_layers(k_out_ref, v_out_ref)
    for i in range(nb):
        rows = slice(i * seq_len, (i + 1) * seq_len)
        k = x_ref[rows, nq:nq + LANES]
        k_t = k.T
        v_t = x_ref[rows, nq + LANES:nq + 2 * LANES].T
        for kv in range(GQA_KV_HEADS):
            k_out_ref[i, 0, kv] = k_t[kv * HEAD_DIM:(kv + 1) * HEAD_DIM, :]
            v_out_ref[i, 0, kv] = v_t[kv * HEAD_DIM:(kv + 1) * HEAD_DIM, :]
        kb = k.astype(BF16)
        v_ext = jnp.concatenate([v_t.astype(BF16), ones], axis=0)
        for g in range(GQA_GROUPS):
            cs = slice(g * LANES, (g + 1) * LANES)
            q_t = x_ref[rows, cs].T
            q2 = jnp.concatenate([jnp.where(head_a, q_t, 0.0), jnp.where(head_a, 0.0, q_t)], axis=1)
            s = _dot(kb, q2.astype(BF16)) * scale
            sink = jnp.where(first, sink_ref[0, :, g:g + 1], sink_ref[0, :, GQA_GROUPS + g:GQA_GROUPS + g + 1])
            m = jnp.maximum(jnp.max(s, axis=0, keepdims=True), sink)
            pv = _dot(v_ext, jnp.exp(s - m).astype(BF16))
            o = pv[0:LANES, :] / (pv[LANES:LANES + 1, :] + jnp.exp(sink - m))
            o = jnp.where(head_a, o[:, 0:seq_len], o[:, seq_len:2 * seq_len])
            y_ref[rows, cs] = o.T.astype(BF16)


def _gqa_ctx(gqa_in, sink, carried, layer, n_batch, seq_len, nb):
    kv_shape, kv_spec = _layer_stacked((GQA_KV_HEADS, HEAD_DIM, seq_len), n_batch, layer, nb)
    in_specs = [
        pl.BlockSpec((nb * seq_len, GQA_IN), lambda b: (b, 0)),
        pl.BlockSpec((1, 1, GQA_Q_HEADS), lambda b: (layer, 0, 0)),
    ]
    args = [gqa_in, sink]
    aliases = _carry_specs(in_specs, args, carried, 1)
    return _part(
        functools.partial(_gqa_ctx_kernel, seq_len=seq_len, nb=nb), in_specs, args,
        [jax.ShapeDtypeStruct((n_batch * seq_len, MIX_PART), BF16), kv_shape, kv_shape],
        [pl.BlockSpec((nb * seq_len, MIX_PART), lambda b: (b, 0)), kv_spec, kv_spec],
        aliases=aliases)


def _gqa_win_kernel(x_ref, sink_ref, kct_ref, vct_ref, cos_ref, sin_ref, y_ref,
                    q_scr, k_scr, vt_scr, kc_scr, vct_scr, *, seq_len):
    nq = GQA_Q_HEADS * HEAD_DIM
    scale = HEAD_DIM ** -0.5
    head_a = lax.broadcasted_iota(jnp.int32, (LANES, 1), 0) < HEAD_DIM
    first = lax.broadcasted_iota(jnp.int32, (1, 2 * BLOCK), 1) < BLOCK
    cos = cos_ref[...]
    sin = sin_ref[...]

    def rope(t):
        return t * cos + _rot_half(t, HEAD_DIM // 2) * sin

    for g in range(GQA_GROUPS):
        q_t = rope(x_ref[:, g * LANES:(g + 1) * LANES]).T
        q_scr[g, 0] = jnp.where(head_a, q_t, 0.0).astype(BF16)
        q_scr[g, 1] = jnp.where(head_a, 0.0, q_t).astype(BF16)
    zeros = jnp.zeros((BLOCK, LANES), BF16)
    for r0 in (0, BLOCK + seq_len):
        k_scr[r0:r0 + BLOCK, :] = zeros
        vt_scr[:, r0:r0 + BLOCK] = zeros
    k_scr[BLOCK:BLOCK + seq_len, :] = rope(x_ref[:, nq:nq + LANES]).astype(BF16)
    vt_scr[:, BLOCK:BLOCK + seq_len] = x_ref[:, nq + LANES:nq + 2 * LANES].T.astype(BF16)
    kc_scr[...] = jnp.concatenate([kct_ref[0, 0, 0], kct_ref[0, 0, 1]], axis=0).T.astype(BF16)
    vct_scr[...] = jnp.concatenate([vct_ref[0, 0, 0], vct_ref[0, 0, 1]], axis=0).astype(BF16)

    jj = lax.broadcasted_iota(jnp.int32, (3 * BLOCK, 2 * BLOCK), 0)
    ii = lax.broadcasted_iota(jnp.int32, (3 * BLOCK, 2 * BLOCK), 1) & (BLOCK - 1)
    band = (jj >= ii) & (jj <= ii + 2 * WINDOW)
    for n in range(seq_len // BLOCK):
        r0 = n * BLOCK
        kpos = jj + (n - 1) * BLOCK
        valid = band & (kpos >= 0) & (kpos < seq_len)
        kw = k_scr[r0:r0 + 3 * BLOCK, :]
        vw = vt_scr[:, r0:r0 + 3 * BLOCK]
        for g in range(GQA_GROUPS):
            q2 = jnp.concatenate([q_scr[g, 0, :, r0:r0 + BLOCK], q_scr[g, 1, :, r0:r0 + BLOCK]], axis=1)
            s_win = jnp.where(valid, _dot(kw, q2) * scale, NEG_INF)
            s_ctx = _dot(kc_scr[...], q2) * scale
            sink = jnp.where(first, sink_ref[0, :, g:g + 1], sink_ref[0, :, GQA_GROUPS + g:GQA_GROUPS + g + 1])
            m = jnp.maximum(jnp.maximum(jnp.max(s_win, axis=0, keepdims=True),
                                        jnp.max(s_ctx, axis=0, keepdims=True)), sink)
            p_win = jnp.exp(s_win - m)
            p_ctx = jnp.exp(s_ctx - m)
            den = (jnp.sum(p_win, axis=0, keepdims=True) + jnp.sum(p_ctx, axis=0, keepdims=True)
                   + jnp.exp(sink - m))
            o = (_dot(vw, p_win.astype(BF16)) + _dot(vct_scr[...], p_ctx.astype(BF16))) / den
            o = jnp.where(head_a, o[:, 0:BLOCK], o[:, BLOCK:2 * BLOCK])
            y_ref[r0:r0 + BLOCK, g * LANES:(g + 1) * LANES] = o.T.astype(BF16)


def _gqa_win(gqa_in, sink, cache_k_t, cache_v_t, cos, sin, layer, n_batch, seq_len):
    past = cache_k_t.shape[4]
    return _part(
        functools.partial(_gqa_win_kernel, seq_len=seq_len),
        [
            pl.BlockSpec((seq_len, GQA_IN), lambda b: (b, 0)),
            pl.BlockSpec((1, 1, GQA_Q_HEADS), lambda b: (layer, 0, 0)),
            pl.BlockSpec((1, 1, GQA_KV_HEADS, HEAD_DIM, past), lambda b: (b, layer, 0, 0, 0)),
            pl.BlockSpec((1, 1, GQA_KV_HEADS, HEAD_DIM, past), lambda b: (b, layer, 0, 0, 0)),
            pl.BlockSpec((seq_len, LANES), lambda b: (0, 0)),
            pl.BlockSpec((seq_len, LANES), lambda b: (0, 0)),
        ],
        [gqa_in, sink, cache_k_t, cache_v_t, cos, sin],
        jax.ShapeDtypeStruct((n_batch * seq_len, MIX_PART), BF16),
        pl.BlockSpec((seq_len, MIX_PART), lambda b: (b, 0)),
        scratch=[
            pltpu.VMEM((GQA_GROUPS, 2, LANES, seq_len), BF16),
            pltpu.VMEM((seq_len + 2 * BLOCK, LANES), BF16),
            pltpu.VMEM((LANES, seq_len + 2 * BLOCK), BF16),
            pltpu.VMEM((past, LANES), BF16),
            pltpu.VMEM((LANES, past), BF16),
        ])


def _mla_kernel(*refs, seq_len, latent, bq, nb):
    if latent:
        (x_ref, qn_ref, kvn_ref, wq_ref, wk_ref, wv_ref, ckv_c_ref, kr_c_ref, cos_ref, sin_ref,
         y_ref, q_all, k_all, v_all) = refs
        past = ckv_c_ref.shape[2]
    else:
        (x_ref, qn_ref, kvn_ref, wq_ref, wk_ref, wv_ref, *_,
         y_ref, ckv_out_ref, kr_out_ref, q_all, k_all, v_all) = refs
        past = 0
        _zero_other_layers(ckv_out_ref, kr_out_ref)
    keys = past + seq_len
    nope_w = MLA_HEADS * MLA_NOPE
    scale = (MLA_NOPE + MLA_ROPE) ** -0.5
    lane = _lane_iota()
    lo = lane < HEAD_DIM
    rope_lanes = [(lane >= h * MLA_ROPE) & (lane < (h + 1) * MLA_ROPE) for h in range(MLA_HEADS)]
    for i in range(nb):
        rows = slice(i * seq_len, (i + 1) * seq_len)
        q_scr, k_scr, v_scr = q_all.at[i], k_all.at[i], v_all.at[i]
        q_lat = x_ref[rows, 0:MLA_Q_RANK]
        kv_lat = x_ref[rows, MLA_Q_RANK:MLA_Q_RANK + MLA_KV_RANK]
        k_rope = x_ref[rows, MLA_Q_RANK + MLA_KV_RANK:MLA_IN_TILED]
        mq = _dot(_rms(q_lat, qn_ref[0]).astype(BF16), wq_ref[0])
        ckv = _rms(kv_lat, kvn_ref[0])
        q_rope = mq[:, nope_w:nope_w + LANES]
        if latent:
            cos = cos_ref[...]
            sin = sin_ref[...]
            q_rope = q_rope * cos + _rot_half(q_rope, MLA_ROPE // 2) * sin
            k_rope = k_rope * cos + _rot_half(k_rope, MLA_ROPE // 2) * sin
            ckv_c = ckv_c_ref[0, 0].astype(BF16)
            k_scr[0:past, 0:nope_w] = _dot(ckv_c, wk_ref[0]).astype(BF16)
            v_scr[0:past, :] = _dot(ckv_c, wv_ref[0]).astype(BF16)
            kr_c = kr_c_ref[0, 0]
            k_scr[0:past, nope_w:nope_w + LANES] = jnp.concatenate([kr_c] * MLA_HEADS, axis=1).astype(BF16)
        else:
            ckv_out_ref[i, 0] = ckv
            kr_out_ref[i, 0] = k_rope.T[0:MLA_ROPE, :]
        q_scr[:, 0:nope_w] = mq[:, 0:nope_w]
        q_scr[:, nope_w:nope_w + LANES] = q_rope
        ckv_b = ckv.astype(BF16)
        k_scr[past:keys, 0:nope_w] = _dot(ckv_b, wk_ref[0]).astype(BF16)
        v_scr[past:keys, :] = _dot(ckv_b, wv_ref[0]).astype(BF16)
        k_scr[past:keys, nope_w:nope_w + LANES] = k_rope.astype(BF16)

        ones = jnp.ones((keys, LANES), BF16)
        kr_all = k_scr[:, nope_w:nope_w + LANES]
        for t in range(MLA_HEADS // 2):
            cs = slice(t * LANES, (t + 1) * LANES)
            k_cat = jnp.concatenate([k_scr[:, cs], kr_all], axis=1)
            v_ext = jnp.concatenate([v_scr[:, cs], ones], axis=1)
            for r0 in range(0, seq_len, bq):
                qn = q_scr[r0:r0 + bq, cs]
                qr = q_scr[r0:r0 + bq, nope_w:nope_w + LANES]
                q2 = jnp.concatenate([
                    jnp.concatenate([jnp.where(lo, qn, 0.0), jnp.where(rope_lanes[2 * t], qr, 0.0)], axis=1),
                    jnp.concatenate([jnp.where(lo, 0.0, qn), jnp.where(rope_lanes[2 * t + 1], qr, 0.0)], axis=1),
                ], axis=0).astype(BF16)
                s = _dot_nt(q2, k_cat) * scale
                m = jnp.max(s, axis=-1, keepdims=True)
                pv = _dot(jnp.exp(s - m).astype(BF16), v_ext)
                o = pv[:, 0:LANES] / pv[:, LANES:LANES + 1]
                y_ref[i * seq_len + r0:i * seq_len + r0 + bq, cs] = _merge_heads(o, lo).astype(BF16)


def _mla(mla_in, q_norm, kv_norm, w_uq, w_uk, w_uv, cache, rope, carried, layer, n_batch, seq_len, nb):
    latent = cache is not None
    aliases = {}
    past = cache[0].shape[2] if latent else 0
    kern = functools.partial(_mla_kernel, seq_len=seq_len, latent=latent, bq=256, nb=nb)
    q_w = MLA_HEADS * (MLA_NOPE + MLA_ROPE)
    kv_w = MLA_HEADS * MLA_NOPE
    in_specs = [
        pl.BlockSpec((nb * seq_len, MLA_IN_TILED), lambda b: (b, 0)),
        pl.BlockSpec((1, 1, MLA_Q_RANK), lambda b: (layer, 0, 0)),
        pl.BlockSpec((1, 1, MLA_KV_RANK), lambda b: (layer, 0, 0)),
        pl.BlockSpec((1, MLA_Q_RANK, q_w), lambda b: (layer, 0, 0)),
        pl.BlockSpec((1, MLA_KV_RANK, kv_w), lambda b: (layer, 0, 0)),
        pl.BlockSpec((1, MLA_KV_RANK, kv_w), lambda b: (layer, 0, 0)),
    ]
    args = [mla_in, q_norm, kv_norm, w_uq, w_uk, w_uv]
    y_shape = jax.ShapeDtypeStruct((n_batch * seq_len, MIX_PART), BF16)
    y_spec = pl.BlockSpec((nb * seq_len, MIX_PART), lambda b: (b, 0))
    if latent:
        assert nb == 1
        in_specs += [
            pl.BlockSpec((1, 1, past, MLA_KV_RANK), lambda b: (b, layer, 0, 0)),
            pl.BlockSpec((1, 1, past, MLA_ROPE), lambda b: (b, layer, 0, 0)),
            pl.BlockSpec((seq_len, LANES), lambda b: (0, 0)),
            pl.BlockSpec((seq_len, LANES), lambda b: (0, 0)),
        ]
        args += [cache[0], cache[1], rope[0], rope[1]]
        out_shape, out_specs = y_shape, y_spec
    else:
        ckv_shape, ckv_spec = _layer_stacked((seq_len, MLA_KV_RANK), n_batch, layer, nb)
        kr_shape, kr_spec = _layer_stacked((MLA_ROPE, seq_len), n_batch, layer, nb)
        out_shape = [y_shape, ckv_shape, kr_shape]
        out_specs = [y_spec, ckv_spec, kr_spec]
        aliases = _carry_specs(in_specs, args, carried, 1)
    keys = past + seq_len
    return _part(
        kern, in_specs, args, out_shape, out_specs,
        scratch=[
            pltpu.VMEM((nb, seq_len, q_w), F32),
            pltpu.VMEM((nb, keys, kv_w + LANES), BF16),
            pltpu.VMEM((nb, keys, kv_w), BF16),
        ],
        aliases=aliases)


def _dft_matrix(seq_len):
    k = np.arange(seq_len, dtype=np.float64)[:, None]
    j = np.arange(seq_len, dtype=np.float64)[None, :]
    ang = np.pi * k * j / seq_len
    re = np.cos(ang)
    im = -np.sin(ang)
    im[0, :] = np.where(np.arange(seq_len) % 2 == 0, 1.0, -1.0)
    return np.concatenate([re, im], axis=0).astype(np.float32)


def _hyena_positions(seq_len):
    t = np.arange(seq_len, dtype=np.float64) / seq_len
    bands = np.arange(1, HY_BANDS + 1, dtype=np.float64)
    ang = 2.0 * math.pi * t[:, None] * bands
    z = np.concatenate([t[:, None], np.cos(ang), np.sin(ang)], axis=-1)
    return np.pad(z, ((0, 0), (0, HY_POS_PAD - HY_POS_DIM))).astype(np.float32)


def _rope_tables(n_tokens, dim):
    rows = np.repeat(np.arange(n_tokens // GRID_W, dtype=np.float64), GRID_W)
    cols = np.tile(np.arange(GRID_W, dtype=np.float64), n_tokens // GRID_W)
    quarter = dim // 4
    inv = ROPE_BASE ** (-np.arange(quarter, dtype=np.float64) / quarter)
    ang = np.concatenate([rows[:, None] * inv, cols[:, None] * inv], axis=-1)
    cos = np.concatenate([np.cos(ang), np.cos(ang)], axis=-1)
    sin = np.concatenate([-np.sin(ang), np.sin(ang)], axis=-1)
    reps = LANES // dim
    return (np.tile(cos, (1, reps)).astype(np.float32), np.tile(sin, (1, reps)).astype(np.float32))


def kernel(x_prompt, x_sample, state_ret, cache_gqa_k, cache_gqa_v, cache_mla_ckv, cache_mla_krope, c, c_ctx, ada_w, ada_b, norm_g, w_in, ret_decay_logit, ret_gn_g, hy_short_w, hy_short_b, hy_w1, hy_b1, hy_w2, hy_b2, hy_w3, hy_decay, hy_bias, gqa_sink, mla_q_norm, mla_kv_norm, mla_w_uq, mla_w_uk, mla_w_uv, w_out, ffn_w_up, ffn_conv_w, ffn_conv_b, ffn_w_down):
    n_p, len_p, _ = x_prompt.shape
    n_s, len_s, _ = x_sample.shape

    w_in_t = jnp.swapaxes(w_in, 1, 2)
    cache_k_t = jnp.swapaxes(cache_gqa_k, -1, -2)
    cache_v_t = jnp.swapaxes(cache_gqa_v, -1, -2)
    w_out_b = w_out.astype(BF16)
    w_up_b = ffn_w_up.astype(BF16)
    w_down_b = ffn_w_down.astype(BF16)
    uq = mla_w_uq.reshape(DEPTH, MLA_Q_RANK, MLA_HEADS, MLA_NOPE + MLA_ROPE)
    w_uq_b = jnp.concatenate(
        [uq[..., :MLA_NOPE].reshape(DEPTH, MLA_Q_RANK, MLA_HEADS * MLA_NOPE),
         uq[..., MLA_NOPE:].reshape(DEPTH, MLA_Q_RANK, MLA_HEADS * MLA_ROPE)], axis=-1).astype(BF16)
    w_uk_b = mla_w_uk.astype(BF16)
    w_uv_b = mla_w_uv.astype(BF16)

    cond = jnp.concatenate([c_ctx[None], c, jnp.zeros((N_COND - 1 - n_s, D_MODEL), F32)], axis=0)
    mod = _modulation(cond, ada_w, ada_b).reshape(DEPTH, N_COND, 6, D_MODEL)

    w1p = jnp.pad(hy_w1, ((0, 0), (0, HY_POS_PAD - HY_POS_DIM), (0, 0)))
    b1 = hy_b1.reshape(DEPTH, 1, HY_FILTER_HIDDEN)
    b2 = hy_b2.reshape(DEPTH, 1, HY_FILTER_HIDDEN)
    groups = {}
    for seq_len in (len_p, len_s):
        f32mat = _dft_matrix(seq_len)
        fmat = jnp.asarray(f32mat).astype(BF16)
        fmat_t = jnp.asarray(np.ascontiguousarray(f32mat.T)).astype(BF16)
        tables = _hyena_tables(seq_len, jnp.asarray(_hyena_positions(seq_len)), w1p, b1, hy_w2, b2,
                               hy_w3, hy_decay, fmat)
        groups[seq_len] = (tables, fmat, fmat_t)

    rope_g = tuple(jnp.asarray(t) for t in _rope_tables(len_s, HEAD_DIM))
    rope_m = tuple(jnp.asarray(t) for t in _rope_tables(len_s, MLA_ROPE))

    gn_g = ret_gn_g.reshape(DEPTH, 1, RET_WIDTH)
    short_b = hy_short_b.reshape(DEPTH, 1, HY_IN)
    sink = gqa_sink.reshape(DEPTH, 1, GQA_Q_HEADS)
    q_norm = mla_q_norm.reshape(DEPTH, 1, MLA_Q_RANK)
    kv_norm = mla_kv_norm.reshape(DEPTH, 1, MLA_KV_RANK)
    conv_b = ffn_conv_b.reshape(DEPTH, 1, D_FF)

    xp = x_prompt.reshape(n_p * len_p, D_MODEL)
    xs = x_sample.reshape(n_s * len_s, D_MODEL)
    tm_p = 1024
    tm_s = 512
    nb_p = 4
    nb_first = 2
    cond_p = lambda i: 0
    cond_s = lambda i: 1 + (i * tm_s) // len_s
    tm_ffn = 1024
    cond_s_ffn = lambda i: 1 + (i * tm_ffn) // len_s

    st = gkv = ckr = None
    for l in range(DEPTH):
        tables, fmat, fmat_t = groups[len_p]
        ret_in, hy_in, gqa_in, mla_in = _in_proj(xp, mod, norm_g, w_in_t, l, tm_p, cond_p)
        nb = nb_first if l == 0 else nb_p
        (y_ret, *st), (y_hy,), (y_gqa, *gkv), (y_mla, *ckr) = _run_parts([
            _retention(ret_in, ret_decay_logit, gn_g, None, st, l, n_p, len_p, nb),
            _hyena(hy_in, hy_short_w, short_b, hy_bias, tables, fmat, fmat_t, l, n_p, len_p, nb),
            _gqa_ctx(gqa_in, sink, gkv, l, n_p, len_p, nb),
            _mla(mla_in, q_norm, kv_norm, w_uq_b, w_uk_b, w_uv_b, None, None, ckr, l, n_p, len_p, nb),
        ], (n_p // nb,), "context_mixers")
        xp = _channel_mix(xp, (y_ret, y_hy, y_gqa, y_mla), mod, norm_g, w_out_b, w_up_b, ffn_conv_w, conv_b,
                          w_down_b, l, tm_ffn, len_p, cond_p)
        tables, fmat, fmat_t = groups[len_s]
        ret_in, hy_in, gqa_in, mla_in = _in_proj(xs, mod, norm_g, w_in_t, l, tm_s, cond_s)
        (y_ret,), (y_gqa,), (y_mla,) = _run_parts([
            _retention(ret_in, ret_decay_logit, gn_g, state_ret, None, l, n_s, len_s, 1),
            _gqa_win(gqa_in, sink, cache_k_t, cache_v_t, rope_g[0], rope_g[1], l, n_s, len_s),
            _mla(mla_in, q_norm, kv_norm, w_uq_b, w_uk_b, w_uv_b, (cache_mla_ckv, cache_mla_krope),
                 rope_m, None, l, n_s, len_s, 1),
        ], (n_s,), "latent_mixers")
        (y_hy,), = _run_parts([
            _hyena(hy_in, hy_short_w, short_b, hy_bias, tables, fmat, fmat_t, l, n_s, len_s, 1),
        ], (n_s,), "latent_hyena")
        xs = _channel_mix(xs, (y_ret, y_hy, y_gqa, y_mla), mod, norm_g, w_out_b, w_up_b, ffn_conv_w, conv_b,
                          w_down_b, l, tm_ffn, len_s, cond_s_ffn)

    return (xp.reshape(n_p, len_p, D_MODEL), xs.reshape(n_s, len_s, D_MODEL),
            st[0], jnp.swapaxes(gkv[0], -1, -2), jnp.swapaxes(gkv[1], -1, -2),
            ckr[0], jnp.swapaxes(ckr[1], -1, -2))
```

```python
import functools
import math

import numpy as np
import jax
import jax.numpy as jnp
from jax import lax
from jax.experimental import pallas as pl
from jax.experimental.pallas import tpu as pltpu

F32 = jnp.float32
BF16 = jnp.bfloat16

D_MODEL = 1024
DEPTH = 4
GRID_W = 64
EPS = 1e-6
NEG_INF = -1e30
ROPE_BASE = 10000.0
BLOCK = 128
WINDOW = 128
HEAD_DIM = 64
LANES = 128
RET_HEADS = 4
RET_WIDTH = RET_HEADS * HEAD_DIM
HY_WIDTH = 256
HY_ORDER = 2
HY_BANDS = 8
HY_POS_DIM = 1 + 2 * HY_BANDS
HY_POS_PAD = 32
HY_FILTER_HIDDEN = 64
GQA_Q_HEADS = 4
GQA_KV_HEADS = 2
GQA_GROUPS = GQA_Q_HEADS // GQA_KV_HEADS
MLA_HEADS = 4
MLA_Q_RANK = 256
MLA_KV_RANK = 128
MLA_NOPE = 64
MLA_ROPE = 32
MLA_V = 64
D_FF = 2816
RET_IN = 4 * RET_WIDTH
HY_IN = (HY_ORDER + 1) * HY_WIDTH
GQA_IN = (GQA_Q_HEADS + 2 * GQA_KV_HEADS) * HEAD_DIM
MLA_IN = MLA_Q_RANK + MLA_KV_RANK + MLA_ROPE
MLA_IN_TILED = MLA_Q_RANK + MLA_KV_RANK + MLA_HEADS * MLA_ROPE
IN_WIDTH = RET_IN + HY_IN + GQA_IN + MLA_IN
IN_WIDTH_TILED = RET_IN + HY_IN + GQA_IN + MLA_IN_TILED
GQA_HEAD_ORDER = (0, 2, 1, 3)
MIX_PART = 256
N_COND = 8
FFN_CHUNK = 256
FFN_STAGE = 128
N_FFN_CHUNKS = D_FF // FFN_CHUNK

VMEM_LIMIT = 56 * 1024 * 1024
VMEM_LIMIT_CHANNEL = 60 * 1024 * 1024


def _params(n_axes=1, vmem_limit=VMEM_LIMIT):
    return pltpu.CompilerParams(
        dimension_semantics=("arbitrary",) * n_axes, vmem_limit_bytes=vmem_limit)


def _dot(a, b):
    return jnp.dot(a, b, preferred_element_type=F32)


def _dot_nt(a, b):
    return lax.dot_general(a, b, (((1,), (1,)), ((), ())), preferred_element_type=F32)


def _dot_tn(a, b):
    return lax.dot_general(a, b, (((0,), (0,)), ((), ())), preferred_element_type=F32)


def _rms(x, g):
    return x * lax.rsqrt(jnp.mean(x * x, axis=-1, keepdims=True) + EPS) * g


def _sigmoid(x):
    return 1.0 / (1.0 + jnp.exp(-x))


def _silu(x):
    return x * _sigmoid(x)


def _shift_rows(x, seq_len):
    n = x.shape[0]
    assert seq_len & (seq_len - 1) == 0
    pos = lax.broadcasted_iota(jnp.int32, (n, 1), 0) & (seq_len - 1)
    prev = jnp.where(pos != 0, pltpu.roll(x, 1, axis=0), 0.0)
    nxt = jnp.where(pos != seq_len - 1, pltpu.roll(x, n - 1, axis=0), 0.0)
    return prev, nxt


def _rot_half(x, half):
    lane = lax.broadcasted_iota(jnp.int32, x.shape, 1) & (2 * half - 1)
    return jnp.where(lane < half, pltpu.roll(x, LANES - half, axis=1), pltpu.roll(x, half, axis=1))


def _lane_iota():
    return lax.broadcasted_iota(jnp.int32, (1, LANES), 1)


def _split_heads(t, lo):
    return jnp.concatenate([jnp.where(lo, t, 0.0), jnp.where(lo, 0.0, t)], axis=0)


def _merge_heads(o, lo):
    n = o.shape[0] // 2
    return jnp.where(lo, o[:n], o[n:])


def _per_head(vals, width):
    lane = lax.broadcasted_iota(jnp.int32, (1, width), 1)
    out = vals[-1]
    for h in range(len(vals) - 2, -1, -1):
        out = jnp.where(lane < (h + 1) * HEAD_DIM, vals[h], out)
    return out


def _mod_kernel(cond_ref, w_ref, b_ref, out_ref):
    s = _silu(cond_ref[...]).astype(BF16)
    out_ref[0] = _dot(s, w_ref[0].astype(BF16)) + b_ref[0]


def _modulation(cond, ada_w, ada_b):
    tn = 1536
    return pl.pallas_call(
        _mod_kernel,
        out_shape=jax.ShapeDtypeStruct((DEPTH, N_COND, 6 * D_MODEL), F32),
        grid=(DEPTH, 6 * D_MODEL // tn),
        in_specs=[
            pl.BlockSpec((N_COND, D_MODEL), lambda l, j: (0, 0)),
            pl.BlockSpec((1, D_MODEL, tn), lambda l, j: (l, 0, j)),
            pl.BlockSpec((1, 1, tn), lambda l, j: (l, 0, j)),
        ],
        out_specs=pl.BlockSpec((1, N_COND, tn), lambda l, j: (l, 0, j)),
        compiler_params=_params(2),
        name="modulation",
    )(cond, ada_w, ada_b.reshape(DEPTH, 1, 6 * D_MODEL))


def _filter_kernel(z_ref, w1_ref, b1_ref, w2_ref, b2_ref, w3_ref, dec_ref, f_ref, out_ref, *, seq_len):
    hi = lax.Precision.HIGHEST
    h = jnp.sin(jnp.dot(z_ref[...], w1_ref[0], precision=hi, preferred_element_type=F32) + b1_ref[0])
    h = jnp.sin(jnp.dot(h, w2_ref[0], precision=hi, preferred_element_type=F32) + b2_ref[0])
    h = jnp.dot(h, w3_ref[0], precision=hi, preferred_element_type=F32)
    row = lax.broadcasted_iota(jnp.int32, (seq_len, 1), 0)
    t = row.astype(F32) / seq_len
    dec = jnp.abs(dec_ref[0])
    win_f = jnp.exp(-t * dec[0:1, :])
    win_b = jnp.exp(-t * dec[1:2, :])
    fm = f_ref[...]
    inv = 1.0 / seq_len
    scale = jnp.where(row == 0, 0.5 * inv, inv)
    for o in range(HY_ORDER):
        base = o * 2 * HY_WIDTH
        hf = h[:, base:base + HY_WIDTH] * win_f
        hb = h[:, base + HY_WIDTH:base + 2 * HY_WIDTH] * win_b
        hbs = jnp.where(row != 0, pltpu.roll(hb, 1, axis=0), 0.0)
        sf = _dot(fm, hf.astype(BF16))
        sb = _dot(fm, hbs.astype(BF16))
        g_re = sf[:seq_len] + sb[:seq_len]
        g_im = sf[seq_len:] - sb[seq_len:]
        g_ny = sf[seq_len:] + sb[seq_len:]
        a = g_re * scale
        out_ref[0, 3 * o] = a
        out_ref[0, 3 * o + 1] = jnp.where(row == 0, g_ny * (0.5 * inv), a)
        out_ref[0, 3 * o + 2] = jnp.where(row == 0, 0.0, g_im * inv)


def _hyena_tables(seq_len, z, w1p, b1, w2, b2, w3, decay, fmat):
    hid = HY_FILTER_HIDDEN
    wide = HY_ORDER * 2 * HY_WIDTH
    return pl.pallas_call(
        functools.partial(_filter_kernel, seq_len=seq_len),
        out_shape=jax.ShapeDtypeStruct((DEPTH, 3 * HY_ORDER, seq_len, HY_WIDTH), F32),
        grid=(DEPTH,),
        in_specs=[
            pl.BlockSpec((seq_len, HY_POS_PAD), lambda l: (0, 0)),
            pl.BlockSpec((1, HY_POS_PAD, hid), lambda l: (l, 0, 0)),
            pl.BlockSpec((1, 1, hid), lambda l: (l, 0, 0)),
            pl.BlockSpec((1, hid, hid), lambda l: (l, 0, 0)),
            pl.BlockSpec((1, 1, hid), lambda l: (l, 0, 0)),
            pl.BlockSpec((1, hid, wide), lambda l: (l, 0, 0)),
            pl.BlockSpec((1, 2, HY_WIDTH), lambda l: (l, 0, 0)),
            pl.BlockSpec((2 * seq_len, seq_len), lambda l: (0, 0)),
        ],
        out_specs=pl.BlockSpec((1, 3 * HY_ORDER, seq_len, HY_WIDTH), lambda l: (l, 0, 0, 0)),
        compiler_params=_params(1),
        name=f"hyena_tables_{seq_len}",
    )(z, w1p, b1, w2, b2, w3, decay, fmat)


def _layer_stacked(tail, n_batch, layer, nb):
    zeros = (0,) * len(tail)
    n_layers = DEPTH if layer == 0 else 1
    return (jax.ShapeDtypeStruct((n_batch, DEPTH) + tail, F32),
            pl.BlockSpec((nb, n_layers) + tail, lambda b: (b, layer) + zeros))


def _zero_other_layers(*out_refs):
    for ref in out_refs:
        if ref.shape[1] > 1:
            ref[:, 1:] = jnp.zeros((ref.shape[0], ref.shape[1] - 1) + ref.shape[2:], ref.dtype)


def _carry_specs(in_specs, args, carried, first_out):
    aliases = {}
    if carried is not None:
        for k, arr in enumerate(carried):
            aliases[len(args)] = first_out + k
            in_specs.append(pl.BlockSpec(memory_space=pl.ANY))
            args.append(arr)
    return aliases


def _part(kernel, in_specs, args, out_shape, out_specs, scratch=(), aliases=None):
    as_list = lambda v: list(v) if isinstance(v, (list, tuple)) else [v]
    return dict(kernel=kernel, in_specs=list(in_specs), args=list(args), out_shape=as_list(out_shape),
                out_specs=as_list(out_specs), scratch=list(scratch), aliases=dict(aliases or {}))


def _run_parts(parts, grid, name):
    n_in = sum(len(p["args"]) for p in parts)
    n_out = sum(len(p["out_shape"]) for p in parts)

    def fused(*refs):
        i = o = s = 0
        for p in parts:
            a, b, c = len(p["args"]), len(p["out_shape"]), len(p["scratch"])
            p["kernel"](*refs[i:i + a], *refs[n_in + o:n_in + o + b],
                        *refs[n_in + n_out + s:n_in + n_out + s + c])
            i, o, s = i + a, o + b, s + c

    aliases = {}
    i = o = 0
    for p in parts:
        aliases.update({i + k: o + v for k, v in p["aliases"].items()})
        i, o = i + len(p["args"]), o + len(p["out_shape"])
    outs = pl.pallas_call(
        fused,
        out_shape=[s for p in parts for s in p["out_shape"]],
        grid=grid,
        in_specs=[s for p in parts for s in p["in_specs"]],
        out_specs=[s for p in parts for s in p["out_specs"]],
        input_output_aliases=aliases,
        scratch_shapes=[s for p in parts for s in p["scratch"]],
        compiler_params=_params(len(grid)),
        name=name,
    )(*[a for p in parts for a in p["args"]])
    split, o = [], 0
    for p in parts:
        split.append(list(outs[o:o + len(p["out_shape"])]))
        o += len(p["out_shape"])
    return split


def _in_proj_kernel(x_ref, mod_ref, g_ref, w_ref, ret_ref, hy_ref, gqa_ref, mla_ref, w_scr):
    @pl.when(pl.program_id(0) == 0)
    def _():
        g0 = RET_IN + HY_IN
        m0 = g0 + GQA_IN
        kr0 = m0 + MLA_Q_RANK + MLA_KV_RANK
        step = 2 * LANES
        for r0 in list(range(0, g0, step)) + list(range(g0 + GQA_Q_HEADS * HEAD_DIM, kr0, LANES)):
            n = step if r0 < g0 else LANES
            w_scr[r0:r0 + n, :] = w_ref[0, r0:r0 + n, :].astype(BF16)
        for dst, j in enumerate(GQA_HEAD_ORDER):
            w_scr[g0 + dst * HEAD_DIM:g0 + (dst + 1) * HEAD_DIM, :] = (
                w_ref[0, g0 + j * HEAD_DIM:g0 + (j + 1) * HEAD_DIM, :].astype(BF16))
        kr = w_ref[0, kr0:kr0 + MLA_ROPE, :].astype(BF16)
        for hd in range(MLA_HEADS):
            w_scr[kr0 + hd * MLA_ROPE:kr0 + (hd + 1) * MLA_ROPE, :] = kr

    shift = mod_ref[0, 0, 0:1, :]
    scale = mod_ref[0, 0, 1:2, :]
    h = (_rms(x_ref[...], g_ref[0, 0:1, :]) * (1.0 + scale) + shift).astype(BF16)
    c0 = 0
    for ref in (ret_ref, hy_ref, gqa_ref, mla_ref):
        width = ref.shape[1]
        ref[...] = _dot_nt(h, w_scr[c0:c0 + width, :])
        c0 += width


def _in_proj(x, mod, norm_g, w_in, layer, tm, cond_of_tile):
    rows = x.shape[0]
    widths = (RET_IN, HY_IN, GQA_IN, MLA_IN_TILED)
    return pl.pallas_call(
        _in_proj_kernel,
        out_shape=[jax.ShapeDtypeStruct((rows, w), F32) for w in widths],
        grid=(rows // tm,),
        in_specs=[
            pl.BlockSpec((tm, D_MODEL), lambda i: (i, 0)),
            pl.BlockSpec((1, 1, 6, D_MODEL), lambda i: (layer, cond_of_tile(i), 0, 0)),
            pl.BlockSpec((1, 4, D_MODEL), lambda i: (layer, 0, 0)),
            pl.BlockSpec((1, IN_WIDTH, D_MODEL), lambda i: (layer, 0, 0), pipeline_mode=pl.Buffered(1)),
        ],
        out_specs=[pl.BlockSpec((tm, w), lambda i: (i, 0)) for w in widths],
        scratch_shapes=[pltpu.VMEM((IN_WIDTH_TILED, D_MODEL), BF16)],
        compiler_params=_params(1),
        name="in_proj",
    )(x, mod, norm_g, w_in)


def _channel_kernel(x_ref, m0_ref, m1_ref, m2_ref, m3_ref, mod_ref, g_ref, wo_ref, wu_hbm, cw_ref, cb_ref,
                    wd_hbm, out_ref, act_ref, wu_ref, wd_ref, stage_u, stage_d, sem, *, seq_len, layer):
    body = functools.partial(_channel_body, x_ref, (m0_ref, m1_ref, m2_ref, m3_ref), mod_ref, g_ref, wo_ref,
                             wu_hbm, cw_ref, cb_ref, wd_hbm, out_ref, act_ref, wu_ref, wd_ref, stage_u,
                             stage_d, sem, seq_len, layer)
    first = pl.program_id(0) == 0
    pl.when(first)(functools.partial(body, True))
    pl.when(jnp.logical_not(first))(functools.partial(body, False))


def _channel_body(x_ref, mix_refs, mod_ref, g_ref, wo_ref, wu_hbm, cw_ref, cb_ref, wd_hbm, out_ref, act_ref,
                  wu_ref, wd_ref, stage_u, stage_d, sem, seq_len, layer, stream):
    n_stage = D_FF // FFN_STAGE
    per_chunk = FFN_CHUNK // FFN_STAGE

    def copies(j):
        slot, c0 = j % 2, j * FFN_STAGE
        return (
            pltpu.make_async_copy(wu_hbm.at[layer, :, c0:c0 + FFN_STAGE], stage_u.at[slot, 0], sem.at[slot, 0]),
            pltpu.make_async_copy(wu_hbm.at[layer, :, D_FF + c0:D_FF + c0 + FFN_STAGE], stage_u.at[slot, 1],
                                  sem.at[slot, 1]),
            pltpu.make_async_copy(wd_hbm.at[layer, c0:c0 + FFN_STAGE, :], stage_d.at[slot], sem.at[slot, 2]),
        )

    def land(j):
        slot, c0 = j % 2, j * FFN_STAGE
        for cp in copies(j):
            cp.wait()
        wu_ref[:, c0:c0 + FFN_STAGE] = stage_u[slot, 0].astype(BF16)
        wu_ref[:, D_FF + c0:D_FF + c0 + FFN_STAGE] = stage_u[slot, 1].astype(BF16)
        wd_ref[c0:c0 + FFN_STAGE, :] = stage_d[slot].astype(BF16)
        if j + 2 < n_stage:
            for cp in copies(j + 2):
                cp.start()

    if stream:
        for j in range(2):
            for cp in copies(j):
                cp.start()
    m0_ref, m1_ref, m2_ref, m3_ref = mix_refs
    y = None
    for i, m_ref in enumerate((m0_ref, m1_ref, m2_ref, m3_ref)):
        r0 = i * MIX_PART
        if i == 2:
            w = jnp.concatenate([wo_ref[0, r0 + j * HEAD_DIM:r0 + (j + 1) * HEAD_DIM, :]
                                 for j in GQA_HEAD_ORDER], axis=0)
        else:
            w = wo_ref[0, r0:r0 + MIX_PART, :]
        part = _dot(m_ref[...], w)
        y = part if y is None else y + part
    gate1 = mod_ref[0, 0, 2:3, :]
    shift2 = mod_ref[0, 0, 3:4, :]
    scale2 = mod_ref[0, 0, 4:5, :]
    xm = x_ref[...] + gate1 * _rms(y, g_ref[0, 1:2, :])
    out_ref[...] = xm
    h2 = (_rms(xm, g_ref[0, 2:3, :]) * (1.0 + scale2) + shift2).astype(BF16)
    for c in range(N_FFN_CHUNKS):
        sl = slice(c * FFN_CHUNK, (c + 1) * FFN_CHUNK)
        if stream:
            for j in range(c * per_chunk, (c + 1) * per_chunk):
                land(j)
        gate = _dot(h2, wu_ref[:, sl])
        up = _dot(h2, wu_ref[:, D_FF + c * FFN_CHUNK:D_FF + (c + 1) * FFN_CHUNK])
        prev, nxt = _shift_rows(gate, seq_len)
        gate = (prev * cw_ref[0, 0:1, sl] + gate * cw_ref[0, 1:2, sl] + nxt * cw_ref[0, 2:3, sl]
                + cb_ref[0, :, sl])
        act_ref[:, sl] = (_silu(gate) * up).astype(BF16)
    ffn = _dot(act_ref[...], wd_ref[...])
    gate2 = mod_ref[0, 0, 5:6, :]
    out_ref[...] = out_ref[...] + gate2 * _rms(ffn, g_ref[0, 3:4, :])


def _channel_mix(x, mixes, mod, norm_g, w_out, w_up, conv_w, conv_b, w_down, layer, tm, seq_len, cond_of_tile):
    rows = x.shape[0]
    resident = dict(pipeline_mode=pl.Buffered(1))
    return pl.pallas_call(
        functools.partial(_channel_kernel, seq_len=seq_len, layer=layer),
        out_shape=jax.ShapeDtypeStruct((rows, D_MODEL), F32),
        grid=(rows // tm,),
        in_specs=[pl.BlockSpec((tm, D_MODEL), lambda i: (i, 0))]
        + [pl.BlockSpec((tm, MIX_PART), lambda i: (i, 0))] * 4
        + [
            pl.BlockSpec((1, 1, 6, D_MODEL), lambda i: (layer, cond_of_tile(i), 0, 0)),
            pl.BlockSpec((1, 4, D_MODEL), lambda i: (layer, 0, 0)),
            pl.BlockSpec((1, D_MODEL, D_MODEL), lambda i: (layer, 0, 0), **resident),
            pl.BlockSpec(memory_space=pl.ANY),
            pl.BlockSpec((1, 3, D_FF), lambda i: (layer, 0, 0)),
            pl.BlockSpec((1, 1, D_FF), lambda i: (layer, 0, 0)),
            pl.BlockSpec(memory_space=pl.ANY),
        ],
        out_specs=pl.BlockSpec((tm, D_MODEL), lambda i: (i, 0)),
        scratch_shapes=[
            pltpu.VMEM((tm, D_FF), BF16),
            pltpu.VMEM((D_MODEL, 2 * D_FF), BF16),
            pltpu.VMEM((D_FF, D_MODEL), BF16),
            pltpu.VMEM((2, 2, D_MODEL, FFN_STAGE), F32),
            pltpu.VMEM((2, FFN_STAGE, D_MODEL), F32),
            pltpu.SemaphoreType.DMA((2, 3)),
        ],
        compiler_params=_params(1, VMEM_LIMIT_CHANNEL),
        name="channel_mix",
    )(x, *mixes, mod, norm_g, w_out, w_up, conv_w, conv_b, w_down)


def _log_gamma(dl_ref):
    dl = dl_ref[0]
    return jnp.minimum(dl, 0.0) - jnp.log(1.0 + jnp.exp(-jnp.abs(dl)))


def _pair_decay(lg, t, r0, bq, seq_len):
    top = lax.broadcasted_iota(jnp.int32, (2 * bq, 1), 0) < bq
    lgf = jnp.where(top, lg[0:1, 2 * t:2 * t + 1], lg[0:1, 2 * t + 1:2 * t + 2])
    lgb = jnp.where(top, lg[1:2, 2 * t:2 * t + 1], lg[1:2, 2 * t + 1:2 * t + 2])
    rowf = ((lax.broadcasted_iota(jnp.int32, (2 * bq, 1), 0) & (bq - 1)) + r0).astype(F32)
    colf = lax.broadcasted_iota(jnp.int32, (1, seq_len), 1).astype(F32)
    lag = rowf - colf
    decay = jnp.exp(jnp.where(lag >= 0.0, lag * lgf, -lag * lgb))
    return jnp.where(lag == 0.0, 2.0, decay)


def _head_norm(o, lo):
    inv = 1.0 / HEAD_DIM
    s_all = jnp.sum(o, axis=-1, keepdims=True)
    s_lo = jnp.sum(jnp.where(lo, o, 0.0), axis=-1, keepdims=True)
    d = o - jnp.where(lo, s_lo, s_all - s_lo) * inv
    d2 = d * d
    v_all = jnp.sum(d2, axis=-1, keepdims=True)
    v_lo = jnp.sum(jnp.where(lo, d2, 0.0), axis=-1, keepdims=True)
    return d * lax.rsqrt(jnp.where(lo, v_lo, v_all - v_lo) * inv + EPS)


def _ret_ctx_kernel(*refs, seq_len, nb):
    x_ref, dl_ref, gn_ref, *_, y_ref, st_ref, dec_scr, kdec_scr = refs
    lo = _lane_iota() < HEAD_DIM
    _zero_other_layers(st_ref)

    @pl.when(pl.program_id(0) == 0)
    def _():
        lg = _log_gamma(dl_ref)
        posf = lax.broadcasted_iota(jnp.int32, (seq_len, 1), 0).astype(F32)
        lgf = _per_head([lg[0:1, h:h + 1] for h in range(RET_HEADS)], RET_WIDTH)
        lgb = _per_head([lg[1:2, h:h + 1] for h in range(RET_HEADS)], RET_WIDTH)
        kdec_scr[0] = jnp.exp((seq_len - 1.0 - posf) * lgf)
        kdec_scr[1] = jnp.exp(posf * lgb)
        for t in range(RET_HEADS // 2):
            dec_scr[t] = _pair_decay(lg, t, 0, seq_len, seq_len)

    for i in range(nb):
        rows = slice(i * seq_len, (i + 1) * seq_len)
        for t in range(RET_HEADS // 2):
            cs = slice(t * LANES, (t + 1) * LANES)
            q = x_ref[rows, cs]
            k = x_ref[rows, RET_WIDTH + t * LANES:RET_WIDTH + (t + 1) * LANES] * (HEAD_DIM ** -0.5)
            vb = x_ref[rows, 2 * RET_WIDTH + t * LANES:2 * RET_WIDTH + (t + 1) * LANES].astype(BF16)
            gate = x_ref[rows, 3 * RET_WIDTH + t * LANES:3 * RET_WIDTH + (t + 1) * LANES]
            s = _dot_nt(_split_heads(q, lo).astype(BF16), k.astype(BF16)) * dec_scr[t]
            o = _merge_heads(_dot(s.astype(BF16), vb), lo)
            y = _silu(gate) * (_head_norm(o, lo) * gn_ref[0, :, cs])
            y_ref[rows, cs] = y.astype(BF16)
            for d in range(2):
                st = _dot_tn((k * kdec_scr[d, :, cs]).astype(BF16), vb)
                st_ref[i, 0, d, 2 * t] = st[0:HEAD_DIM, 0:HEAD_DIM]
                st_ref[i, 0, d, 2 * t + 1] = st[HEAD_DIM:LANES, HEAD_DIM:LANES]


def _ret_lat_kernel(x_ref, dl_ref, gn_ref, s0_ref, y_ref, *, seq_len, bq):
    lo = _lane_iota() < HEAD_DIM
    lg = _log_gamma(dl_ref)
    zero = jnp.zeros((HEAD_DIM, HEAD_DIM), F32)
    for t in range(RET_HEADS // 2):
        cs = slice(t * LANES, (t + 1) * LANES)
        kb = (x_ref[:, RET_WIDTH + t * LANES:RET_WIDTH + (t + 1) * LANES] * (HEAD_DIM ** -0.5)).astype(BF16)
        vb = x_ref[:, 2 * RET_WIDTH + t * LANES:2 * RET_WIDTH + (t + 1) * LANES].astype(BF16)
        lgf = _per_head([lg[0:1, 2 * t:2 * t + 1], lg[0:1, 2 * t + 1:2 * t + 2]], LANES)
        lgb = _per_head([lg[1:2, 2 * t:2 * t + 1], lg[1:2, 2 * t + 1:2 * t + 2]], LANES)
        s0 = []
        for d in range(2):
            a = s0_ref[0, 0, d, 2 * t]
            b = s0_ref[0, 0, d, 2 * t + 1]
            s0.append(jnp.concatenate([jnp.concatenate([a, zero], axis=1),
                                       jnp.concatenate([zero, b], axis=1)], axis=0).astype(BF16))
        for r0 in range(0, seq_len, bq):
            q = x_ref[r0:r0 + bq, cs]
            s = _dot_nt(_split_heads(q, lo).astype(BF16), kb) * _pair_decay(lg, t, r0, bq, seq_len)
            o = _merge_heads(_dot(s.astype(BF16), vb), lo)
            rowf = (lax.broadcasted_iota(jnp.int32, (bq, 1), 0) + r0).astype(F32)
            qf = (q * jnp.exp((rowf + 1.0) * lgf)).astype(BF16)
            qr = (q * jnp.exp((seq_len - rowf) * lgb)).astype(BF16)
            o = o + _dot(qf, s0[0]) + _dot(qr, s0[1])
            gate = x_ref[r0:r0 + bq, 3 * RET_WIDTH + t * LANES:3 * RET_WIDTH + (t + 1) * LANES]
            y = _silu(gate) * (_head_norm(o, lo) * gn_ref[0, :, cs])
            y_ref[r0:r0 + bq, cs] = y.astype(BF16)


def _retention(ret_in, decay_logit, gn_g, state0, carried, layer, n_batch, seq_len, nb):
    latent = state0 is not None
    in_specs = [
        pl.BlockSpec((nb * seq_len, RET_IN), lambda b: (b, 0)),
        pl.BlockSpec((1, 2, RET_HEADS), lambda b: (layer, 0, 0)),
        pl.BlockSpec((1, 1, RET_WIDTH), lambda b: (layer, 0, 0)),
    ]
    args = [ret_in, decay_logit, gn_g]
    y_shape = jax.ShapeDtypeStruct((n_batch * seq_len, MIX_PART), BF16)
    y_spec = pl.BlockSpec((nb * seq_len, MIX_PART), lambda b: (b, 0))
    if latent:
        assert nb == 1
        kern = functools.partial(_ret_lat_kernel, seq_len=seq_len, bq=256)
        in_specs.append(pl.BlockSpec((1, 1, 2, RET_HEADS, HEAD_DIM, HEAD_DIM),
                                     lambda b: (b, layer, 0, 0, 0, 0)))
        args.append(state0)
        out_shape, out_specs, aliases, scratch = y_shape, y_spec, {}, []
    else:
        kern = functools.partial(_ret_ctx_kernel, seq_len=seq_len, nb=nb)
        st_shape, st_spec = _layer_stacked((2, RET_HEADS, HEAD_DIM, HEAD_DIM), n_batch, layer, nb)
        out_shape = [y_shape, st_shape]
        out_specs = [y_spec, st_spec]
        aliases = _carry_specs(in_specs, args, carried, 1)
        scratch = [pltpu.VMEM((RET_HEADS // 2, 2 * seq_len, seq_len), F32),
                   pltpu.VMEM((2, seq_len, RET_WIDTH), F32)]
    return _part(kern, in_specs, args, out_shape, out_specs, scratch, aliases)


def _hyena_kernel(x_ref, sw_ref, sb_ref, bias_ref, tab_ref, f_ref, ft_ref, y_ref, *, seq_len, nb):
    for i in range(nb):
        rows = slice(i * seq_len, (i + 1) * seq_len)
        x = x_ref[rows, :]
        prev, nxt = _shift_rows(x, seq_len)
        u = prev * sw_ref[0, 0:1, :] + x * sw_ref[0, 1:2, :] + nxt * sw_ref[0, 2:3, :] + sb_ref[0]
        x1 = u[:, 0:HY_WIDTH]
        x2 = u[:, HY_WIDTH:2 * HY_WIDTH]
        z = u[:, 2 * HY_WIDTH:3 * HY_WIDTH]
        for o, gate in enumerate((x1, x2)):
            spec = _dot(f_ref[...], z.astype(BF16))
            s_re = spec[:seq_len]
            s_im = spec[seq_len:]
            a = tab_ref[0, 3 * o]
            a_ny = tab_ref[0, 3 * o + 1]
            b = tab_ref[0, 3 * o + 2]
            y_re = (s_re * a - s_im * b).astype(BF16)
            y_im = (s_re * b + s_im * a_ny).astype(BF16)
            conv = _dot(ft_ref[:, 0:seq_len], y_re) + _dot(ft_ref[:, seq_len:2 * seq_len], y_im)
            z = gate * (conv + z * bias_ref[0, o:o + 1, :])
        y_ref[rows, :] = z.astype(BF16)


def _hyena(hy_in, short_w, short_b, hy_bias, tables, fmat, fmat_t, layer, n_batch, seq_len, nb):
    const = dict(pipeline_mode=pl.Buffered(1))
    return _part(
        functools.partial(_hyena_kernel, seq_len=seq_len, nb=nb),
        [
            pl.BlockSpec((nb * seq_len, HY_IN), lambda b: (b, 0)),
            pl.BlockSpec((1, 3, HY_IN), lambda b: (layer, 0, 0)),
            pl.BlockSpec((1, 1, HY_IN), lambda b: (layer, 0, 0)),
            pl.BlockSpec((1, HY_ORDER, HY_WIDTH), lambda b: (layer, 0, 0)),
            pl.BlockSpec((1, 3 * HY_ORDER, seq_len, HY_WIDTH), lambda b: (layer, 0, 0, 0), **const),
            pl.BlockSpec((2 * seq_len, seq_len), lambda b: (0, 0), **const),
            pl.BlockSpec((seq_len, 2 * seq_len), lambda b: (0, 0), **const),
        ],
        [hy_in, short_w, short_b, hy_bias, tables, fmat, fmat_t],
        jax.ShapeDtypeStruct((n_batch * seq_len, MIX_PART), BF16),
        pl.BlockSpec((nb * seq_len, MIX_PART), lambda b: (b, 0)))


def _gqa_ctx_kernel(*refs, seq_len, nb):
    x_ref, sink_ref, *_, y_ref, k_out_ref, v_out_ref = refs
    nq = GQA_Q_HEADS * HEAD_DIM
    scale = HEAD_DIM ** -0.5
    head_a = lax.broadcasted_iota(jnp.int32, (LANES, 1), 0) < HEAD_DIM
    first = lax.broadcasted_iota(jnp.int32, (1, 2 * seq_len), 1) < seq_len
    ones = jnp.ones((LANES, seq_len), BF16)
    _zero_other_layers(k_out_ref, v_out_ref)
    for i in range(nb):
        rows = slice(i * seq_len, (i + 1) * seq_len)
        k = x_ref[rows, nq:nq + LANES]
        k_t = k.T
        v_t = x_ref[rows, nq + LANES:nq + 2 * LANES].T
        for kv in range(GQA_KV_HEADS):
            k_out_ref[i, 0, kv] = k_t[kv * HEAD_DIM:(kv + 1) * HEAD_DIM, :]
            v_out_ref[i, 0, kv] = v_t[kv * HEAD_DIM:(kv + 1) * HEAD_DIM, :]
        kb = k.astype(BF16)
        v_ext = jnp.concatenate([v_t.astype(BF16), ones], axis=0)
        for g in range(GQA_GROUPS):
            cs = slice(g * LANES, (g + 1) * LANES)
            q_t = x_ref[rows, cs].T
            q2 = jnp.concatenate([jnp.where(head_a, q_t, 0.0), jnp.where(head_a, 0.0, q_t)], axis=1)
            s = _dot(kb, q2.astype(BF16)) * scale
            sink = jnp.where(first, sink_ref[0, :, g:g + 1], sink_ref[0, :, GQA_GROUPS + g:GQA_GROUPS + g + 1])
            m = jnp.maximum(jnp.max(s, axis=0, keepdims=True), sink)
            pv = _dot(v_ext, jnp.exp(s - m).astype(BF16))
            o = pv[0:LANES, :] / (pv[LANES:LANES + 1, :] + jnp.exp(sink - m))
            o = jnp.where(head_a, o[:, 0:seq_len], o[:, seq_len:2 * seq_len])
            y_ref[rows, cs] = o.T.astype(BF16)


def _gqa_ctx(gqa_in, sink, carried, layer, n_batch, seq_len, nb):
    kv_shape, kv_spec = _layer_stacked((GQA_KV_HEADS, HEAD_DIM, seq_len), n_batch, layer, nb)
    in_specs = [
        pl.BlockSpec((nb * seq_len, GQA_IN), lambda b: (b, 0)),
        pl.BlockSpec((1, 1, GQA_Q_HEADS), lambda b: (layer, 0, 0)),
    ]
    args = [gqa_in, sink]
    aliases = _carry_specs(in_specs, args, carried, 1)
    return _part(
        functools.partial(_gqa_ctx_kernel, seq_len=seq_len, nb=nb), in_specs, args,
        [jax.ShapeDtypeStruct((n_batch * seq_len, MIX_PART), BF16), kv_shape, kv_shape],
        [pl.BlockSpec((nb * seq_len, MIX_PART), lambda b: (b, 0)), kv_spec, kv_spec],
        aliases=aliases)


def _gqa_win_kernel(x_ref, sink_ref, kct_ref, vct_ref, cos_ref, sin_ref, y_ref,
                    q_scr, k_scr, vt_scr, kc_scr, vct_scr, *, seq_len):
    nq = GQA_Q_HEADS * HEAD_DIM
    scale = HEAD_DIM ** -0.5
    head_a = lax.broadcasted_iota(jnp.int32, (LANES, 1), 0) < HEAD_DIM
    first = lax.broadcasted_iota(jnp.int32, (1, 2 * BLOCK), 1) < BLOCK
    cos = cos_ref[...]
    sin = sin_ref[...]

    def rope(t):
        return t * cos + _rot_half(t, HEAD_DIM // 2) * sin

    for g in range(GQA_GROUPS):
        q_t = rope(x_ref[:, g * LANES:(g + 1) * LANES]).T
        q_scr[g, 0] = jnp.where(head_a, q_t, 0.0).astype(BF16)
        q_scr[g, 1] = jnp.where(head_a, 0.0, q_t).astype(BF16)
    zeros = jnp.zeros((BLOCK, LANES), BF16)
    for r0 in (0, BLOCK + seq_len):
        k_scr[r0:r0 + BLOCK, :] = zeros
        vt_scr[:, r0:r0 + BLOCK] = zeros
    k_scr[BLOCK:BLOCK + seq_len, :] = rope(x_ref[:, nq:nq + LANES]).astype(BF16)
    vt_scr[:, BLOCK:BLOCK + seq_len] = x_ref[:, nq + LANES:nq + 2 * LANES].T.astype(BF16)
    kc_scr[...] = jnp.concatenate([kct_ref[0, 0, 0], kct_ref[0, 0, 1]], axis=0).T.astype(BF16)
    vct_scr[...] = jnp.concatenate([vct_ref[0, 0, 0], vct_ref[0, 0, 1]], axis=0).astype(BF16)

    jj = lax.broadcasted_iota(jnp.int32, (3 * BLOCK, 2 * BLOCK), 0)
    ii = lax.broadcasted_iota(jnp.int32, (3 * BLOCK, 2 * BLOCK), 1) & (BLOCK - 1)
    band = (jj >= ii) & (jj <= ii + 2 * WINDOW)
    for n in range(seq_len // BLOCK):
        r0 = n * BLOCK
        kpos = jj + (n - 1) * BLOCK
        valid = band & (kpos >= 0) & (kpos < seq_len)
        kw = k_scr[r0:r0 + 3 * BLOCK, :]
        vw = vt_scr[:, r0:r0 + 3 * BLOCK]
        for g in range(GQA_GROUPS):
            q2 = jnp.concatenate([q_scr[g, 0, :, r0:r0 + BLOCK], q_scr[g, 1, :, r0:r0 + BLOCK]], axis=1)
            s_win = jnp.where(valid, _dot(kw, q2) * scale, NEG_INF)
            s_ctx = _dot(kc_scr[...], q2) * scale
            sink = jnp.where(first, sink_ref[0, :, g:g + 1], sink_ref[0, :, GQA_GROUPS + g:GQA_GROUPS + g + 1])
            m = jnp.maximum(jnp.maximum(jnp.max(s_win, axis=0, keepdims=True),
                                        jnp.max(s_ctx, axis=0, keepdims=True)), sink)
            p_win = jnp.exp(s_win - m)
            p_ctx = jnp.exp(s_ctx - m)
            den = (jnp.sum(p_win, axis=0, keepdims=True) + jnp.sum(p_ctx, axis=0, keepdims=True)
                   + jnp.exp(sink - m))
            o = (_dot(vw, p_win.astype(BF16)) + _dot(vct_scr[...], p_ctx.astype(BF16))) / den
            o = jnp.where(head_a, o[:, 0:BLOCK], o[:, BLOCK:2 * BLOCK])
            y_ref[r0:r0 + BLOCK, g * LANES:(g + 1) * LANES] = o.T.astype(BF16)


def _gqa_win(gqa_in, sink, cache_k_t, cache_v_t, cos, sin, layer, n_batch, seq_len):
    past = cache_k_t.shape[4]
    return _part(
        functools.partial(_gqa_win_kernel, seq_len=seq_len),
        [
            pl.BlockSpec((seq_len, GQA_IN), lambda b: (b, 0)),
            pl.BlockSpec((1, 1, GQA_Q_HEADS), lambda b: (layer, 0, 0)),
            pl.BlockSpec((1, 1, GQA_KV_HEADS, HEAD_DIM, past), lambda b: (b, layer, 0, 0, 0)),
            pl.BlockSpec((1, 1, GQA_KV_HEADS, HEAD_DIM, past), lambda b: (b, layer, 0, 0, 0)),
            pl.BlockSpec((seq_len, LANES), lambda b: (0, 0)),
            pl.BlockSpec((seq_len, LANES), lambda b: (0, 0)),
        ],
        [gqa_in, sink, cache_k_t, cache_v_t, cos, sin],
        jax.ShapeDtypeStruct((n_batch * seq_len, MIX_PART), BF16),
        pl.BlockSpec((seq_len, MIX_PART), lambda b: (b, 0)),
        scratch=[
            pltpu.VMEM((GQA_GROUPS, 2, LANES, seq_len), BF16),
            pltpu.VMEM((seq_len + 2 * BLOCK, LANES), BF16),
            pltpu.VMEM((LANES, seq_len + 2 * BLOCK), BF16),
            pltpu.VMEM((past, LANES), BF16),
            pltpu.VMEM((LANES, past), BF16),
        ])


def _mla_kernel(*refs, seq_len, latent, bq, nb):
    if latent:
        (x_ref, qn_ref, kvn_ref, wq_ref, wk_ref, wv_ref, ckv_c_ref, kr_c_ref, cos_ref, sin_ref,
         y_ref, q_all, k_all, v_all) = refs
        past = ckv_c_ref.shape[2]
    else:
        (x_ref, qn_ref, kvn_ref, wq_ref, wk_ref, wv_ref, *_,
         y_ref, ckv_out_ref, kr_out_ref, q_all, k_all, v_all) = refs
        past = 0
        _zero_other_layers(ckv_out_ref, kr_out_ref)
    keys = past + seq_len
    nope_w = MLA_HEADS * MLA_NOPE
    scale = (MLA_NOPE + MLA_ROPE) ** -0.5
    lane = _lane_iota()
    lo = lane < HEAD_DIM
    rope_lanes = [(lane >= h * MLA_ROPE) & (lane < (h + 1) * MLA_ROPE) for h in range(MLA_HEADS)]
    for i in range(nb):
        rows = slice(i * seq_len, (i + 1) * seq_len)
        q_scr, k_scr, v_scr = q_all.at[i], k_all.at[i], v_all.at[i]
        q_lat = x_ref[rows, 0:MLA_Q_RANK]
        kv_lat = x_ref[rows, MLA_Q_RANK:MLA_Q_RANK + MLA_KV_RANK]
        k_rope = x_ref[rows, MLA_Q_RANK + MLA_KV_RANK:MLA_IN_TILED]
        mq = _dot(_rms(q_lat, qn_ref[0]).astype(BF16), wq_ref[0])
        ckv = _rms(kv_lat, kvn_ref[0])
        q_rope = mq[:, nope_w:nope_w + LANES]
        if latent:
            cos = cos_ref[...]
            sin = sin_ref[...]
            q_rope = q_rope * cos + _rot_half(q_rope, MLA_ROPE // 2) * sin
            k_rope = k_rope * cos + _rot_half(k_rope, MLA_ROPE // 2) * sin
            ckv_c = ckv_c_ref[0, 0].astype(BF16)
            k_scr[0:past, 0:nope_w] = _dot(ckv_c, wk_ref[0]).astype(BF16)
            v_scr[0:past, :] = _dot(ckv_c, wv_ref[0]).astype(BF16)
            kr_c = kr_c_ref[0, 0]
            k_scr[0:past, nope_w:nope_w + LANES] = jnp.concatenate([kr_c] * MLA_HEADS, axis=1).astype(BF16)
        else:
            ckv_out_ref[i, 0] = ckv
            kr_out_ref[i, 0] = k_rope.T[0:MLA_ROPE, :]
        q_scr[:, 0:nope_w] = mq[:, 0:nope_w]
        q_scr[:, nope_w:nope_w + LANES] = q_rope
        ckv_b = ckv.astype(BF16)
        k_scr[past:keys, 0:nope_w] = _dot(ckv_b, wk_ref[0]).astype(BF16)
        v_scr[past:keys, :] = _dot(ckv_b, wv_ref[0]).astype(BF16)
        k_scr[past:keys, nope_w:nope_w + LANES] = k_rope.astype(BF16)

        ones = jnp.ones((keys, LANES), BF16)
        kr_all = k_scr[:, nope_w:nope_w + LANES]
        for t in range(MLA_HEADS // 2):
            cs = slice(t * LANES, (t + 1) * LANES)
            k_cat = jnp.concatenate([k_scr[:, cs], kr_all], axis=1)
            v_ext = jnp.concatenate([v_scr[:, cs], ones], axis=1)
            for r0 in range(0, seq_len, bq):
                qn = q_scr[r0:r0 + bq, cs]
                qr = q_scr[r0:r0 + bq, nope_w:nope_w + LANES]
                q2 = jnp.concatenate([
                    jnp.concatenate([jnp.where(lo, qn, 0.0), jnp.where(rope_lanes[2 * t], qr, 0.0)], axis=1),
                    jnp.concatenate([jnp.where(lo, 0.0, qn), jnp.where(rope_lanes[2 * t + 1], qr, 0.0)], axis=1),
                ], axis=0).astype(BF16)
                s = _dot_nt(q2, k_cat) * scale
                m = jnp.max(s, axis=-1, keepdims=True)
                pv = _dot(jnp.exp(s - m).astype(BF16), v_ext)
                o = pv[:, 0:LANES] / pv[:, LANES:LANES + 1]
                y_ref[i * seq_len + r0:i * seq_len + r0 + bq, cs] = _merge_heads(o, lo).astype(BF16)


def _mla(mla_in, q_norm, kv_norm, w_uq, w_uk, w_uv, cache, rope, carried, layer, n_batch, seq_len, nb):
    latent = cache is not None
    aliases = {}
    past = cache[0].shape[2] if latent else 0
    kern = functools.partial(_mla_kernel, seq_len=seq_len, latent=latent, bq=256, nb=nb)
    q_w = MLA_HEADS * (MLA_NOPE + MLA_ROPE)
    kv_w = MLA_HEADS * MLA_NOPE
    in_specs = [
        pl.BlockSpec((nb * seq_len, MLA_IN_TILED), lambda b: (b, 0)),
        pl.BlockSpec((1, 1, MLA_Q_RANK), lambda b: (layer, 0, 0)),
        pl.BlockSpec((1, 1, MLA_KV_RANK), lambda b: (layer, 0, 0)),
        pl.BlockSpec((1, MLA_Q_RANK, q_w), lambda b: (layer, 0, 0)),
        pl.BlockSpec((1, MLA_KV_RANK, kv_w), lambda b: (layer, 0, 0)),
        pl.BlockSpec((1, MLA_KV_RANK, kv_w), lambda b: (layer, 0, 0)),
    ]
    args = [mla_in, q_norm, kv_norm, w_uq, w_uk, w_uv]
    y_shape = jax.ShapeDtypeStruct((n_batch * seq_len, MIX_PART), BF16)
    y_spec = pl.BlockSpec((nb * seq_len, MIX_PART), lambda b: (b, 0))
    if latent:
        assert nb == 1
        in_specs += [
            pl.BlockSpec((1, 1, past, MLA_KV_RANK), lambda b: (b, layer, 0, 0)),
            pl.BlockSpec((1, 1, past, MLA_ROPE), lambda b: (b, layer, 0, 0)),
            pl.BlockSpec((seq_len, LANES), lambda b: (0, 0)),
            pl.BlockSpec((seq_len, LANES), lambda b: (0, 0)),
        ]
        args += [cache[0], cache[1], rope[0], rope[1]]
        out_shape, out_specs = y_shape, y_spec
    else:
        ckv_shape, ckv_spec = _layer_stacked((seq_len, MLA_KV_RANK), n_batch, layer, nb)
        kr_shape, kr_spec = _layer_stacked((MLA_ROPE, seq_len), n_batch, layer, nb)
        out_shape = [y_shape, ckv_shape, kr_shape]
        out_specs = [y_spec, ckv_spec, kr_spec]
        aliases = _carry_specs(in_specs, args, carried, 1)
    keys = past + seq_len
    return _part(
        kern, in_specs, args, out_shape, out_specs,
        scratch=[
            pltpu.VMEM((nb, seq_len, q_w), F32),
            pltpu.VMEM((nb, keys, kv_w + LANES), BF16),
            pltpu.VMEM((nb, keys, kv_w), BF16),
        ],
        aliases=aliases)


def _dft_matrix(seq_len):
    k = np.arange(seq_len, dtype=np.float64)[:, None]
    j = np.arange(seq_len, dtype=np.float64)[None, :]
    ang = np.pi * k * j / seq_len
    re = np.cos(ang)
    im = -np.sin(ang)
    im[0, :] = np.where(np.arange(seq_len) % 2 == 0, 1.0, -1.0)
    return np.concatenate([re, im], axis=0).astype(np.float32)


def _hyena_positions(seq_len):
    t = np.arange(seq_len, dtype=np.float64) / seq_len
    bands = np.arange(1, HY_BANDS + 1, dtype=np.float64)
    ang = 2.0 * math.pi * t[:, None] * bands
    z = np.concatenate([t[:, None], np.cos(ang), np.sin(ang)], axis=-1)
    return np.pad(z, ((0, 0), (0, HY_POS_PAD - HY_POS_DIM))).astype(np.float32)


def _rope_tables(n_tokens, dim):
    rows = np.repeat(np.arange(n_tokens // GRID_W, dtype=np.float64), GRID_W)
    cols = np.tile(np.arange(GRID_W, dtype=np.float64), n_tokens // GRID_W)
    quarter = dim // 4
    inv = ROPE_BASE ** (-np.arange(quarter, dtype=np.float64) / quarter)
    ang = np.concatenate([rows[:, None] * inv, cols[:, None] * inv], axis=-1)
    cos = np.concatenate([np.cos(ang), np.cos(ang)], axis=-1)
    sin = np.concatenate([-np.sin(ang), np.sin(ang)], axis=-1)
    reps = LANES // dim
    return (np.tile(cos, (1, reps)).astype(np.float32), np.tile(sin, (1, reps)).astype(np.float32))


def kernel(x_prompt, x_sample, state_ret, cache_gqa_k, cache_gqa_v, cache_mla_ckv, cache_mla_krope, c, c_ctx, ada_w, ada_b, norm_g, w_in, ret_decay_logit, ret_gn_g, hy_short_w, hy_short_b, hy_w1, hy_b1, hy_w2, hy_b2, hy_w3, hy_decay, hy_bias, gqa_sink, mla_q_norm, mla_kv_norm, mla_w_uq, mla_w_uk, mla_w_uv, w_out, ffn_w_up, ffn_conv_w, ffn_conv_b, ffn_w_down):
    n_p, len_p, _ = x_prompt.shape
    n_s, len_s, _ = x_sample.shape

    w_in_t = jnp.swapaxes(w_in, 1, 2)
    cache_k_t = jnp.swapaxes(cache_gqa_k, -1, -2)
    cache_v_t = jnp.swapaxes(cache_gqa_v, -1, -2)
    w_out_b = w_out.astype(BF16)
    uq = mla_w_uq.reshape(DEPTH, MLA_Q_RANK, MLA_HEADS, MLA_NOPE + MLA_ROPE)
    w_uq_b = jnp.concatenate(
        [uq[..., :MLA_NOPE].reshape(DEPTH, MLA_Q_RANK, MLA_HEADS * MLA_NOPE),
         uq[..., MLA_NOPE:].reshape(DEPTH, MLA_Q_RANK, MLA_HEADS * MLA_ROPE)], axis=-1).astype(BF16)
    w_uk_b = mla_w_uk.astype(BF16)
    w_uv_b = mla_w_uv.astype(BF16)

    cond = jnp.concatenate([c_ctx[None], c, jnp.zeros((N_COND - 1 - n_s, D_MODEL), F32)], axis=0)
    mod = _modulation(cond, ada_w, ada_b).reshape(DEPTH, N_COND, 6, D_MODEL)

    w1p = jnp.pad(hy_w1, ((0, 0), (0, HY_POS_PAD - HY_POS_DIM), (0, 0)))
    b1 = hy_b1.reshape(DEPTH, 1, HY_FILTER_HIDDEN)
    b2 = hy_b2.reshape(DEPTH, 1, HY_FILTER_HIDDEN)
    groups = {}
    for seq_len in (len_p, len_s):
        f32mat = _dft_matrix(seq_len)
        fmat = jnp.asarray(f32mat).astype(BF16)
        fmat_t = jnp.asarray(np.ascontiguousarray(f32mat.T)).astype(BF16)
        tables = _hyena_tables(seq_len, jnp.asarray(_hyena_positions(seq_len)), w1p, b1, hy_w2, b2,
                               hy_w3, hy_decay, fmat)
        groups[seq_len] = (tables, fmat, fmat_t)

    rope_g = tuple(jnp.asarray(t) for t in _rope_tables(len_s, HEAD_DIM))
    rope_m = tuple(jnp.asarray(t) for t in _rope_tables(len_s, MLA_ROPE))

    gn_g = ret_gn_g.reshape(DEPTH, 1, RET_WIDTH)
    short_b = hy_short_b.reshape(DEPTH, 1, HY_IN)
    sink = gqa_sink.reshape(DEPTH, 1, GQA_Q_HEADS)
    q_norm = mla_q_norm.reshape(DEPTH, 1, MLA_Q_RANK)
    kv_norm = mla_kv_norm.reshape(DEPTH, 1, MLA_KV_RANK)
    conv_b = ffn_conv_b.reshape(DEPTH, 1, D_FF)

    xp = x_prompt.reshape(n_p * len_p, D_MODEL)
    xs = x_sample.reshape(n_s * len_s, D_MODEL)
    tm_p = 1024
    tm_s = 512
    nb_p = 4
    nb_first = 2
    cond_p = lambda i: 0
    cond_s = lambda i: 1 + (i * tm_s) // len_s
    tm_ffn = 1024
    cond_s_ffn = lambda i: 1 + (i * tm_ffn) // len_s

    st = gkv = ckr = None
    for l in range(DEPTH):
        tables, fmat, fmat_t = groups[len_p]
        ret_in, hy_in, gqa_in, mla_in = _in_proj(xp, mod, norm_g, w_in_t, l, tm_p, cond_p)
        nb = nb_first if l == 0 else nb_p
        (y_ret, *st), (y_hy,), (y_gqa, *gkv), (y_mla, *ckr) = _run_parts([
            _retention(ret_in, ret_decay_logit, gn_g, None, st, l, n_p, len_p, nb),
            _hyena(hy_in, hy_short_w, short_b, hy_bias, tables, fmat, fmat_t, l, n_p, len_p, nb),
            _gqa_ctx(gqa_in, sink, gkv, l, n_p, len_p, nb),
            _mla(mla_in, q_norm, kv_norm, w_uq_b, w_uk_b, w_uv_b, None, None, ckr, l, n_p, len_p, nb),
        ], (n_p // nb,), "context_mixers")
        xp = _channel_mix(xp, (y_ret, y_hy, y_gqa, y_mla), mod, norm_g, w_out_b, ffn_w_up, ffn_conv_w, conv_b,
                          ffn_w_down, l, tm_ffn, len_p, cond_p)
        tables, fmat, fmat_t = groups[len_s]
        ret_in, hy_in, gqa_in, mla_in = _in_proj(xs, mod, norm_g, w_in_t, l, tm_s, cond_s)
        (y_ret,), (y_gqa,), (y_mla,) = _run_parts([
            _retention(ret_in, ret_decay_logit, gn_g, state_ret, None, l, n_s, len_s, 1),
            _gqa_win(gqa_in, sink, cache_k_t, cache_v_t, rope_g[0], rope_g[1], l, n_s, len_s),
            _mla(mla_in, q_norm, kv_norm, w_uq_b, w_uk_b, w_uv_b, (cache_mla_ckv, cache_mla_krope),
                 rope_m, None, l, n_s, len_s, 1),
        ], (n_s,), "latent_mixers")
        (y_hy,), = _run_parts([
            _hyena(hy_in, hy_short_w, short_b, hy_bias, tables, fmat, fmat_t, l, n_s, len_s, 1),
        ], (n_s,), "latent_hyena")
        xs = _channel_mix(xs, (y_ret, y_hy, y_gqa, y_mla), mod, norm_g, w_out_b, ffn_w_up, ffn_conv_w, conv_b,
                          ffn_w_down, l, tm_ffn, len_s, cond_s_ffn)

    return (xp.reshape(n_p, len_p, D_MODEL), xs.reshape(n_s, len_s, D_MODEL),
            st[0], jnp.swapaxes(gkv[0], -1, -2), jnp.swapaxes(gkv[1], -1, -2),
            ckr[0], jnp.swapaxes(ckr[1], -1, -2))
```

```python
import functools
import math

import numpy as np
import jax
import jax.numpy as jnp
from jax import lax
from jax.experimental import pallas as pl
from jax.experimental.pallas import tpu as pltpu

F32 = jnp.float32
BF16 = jnp.bfloat16

D_MODEL = 1024
DEPTH = 4
GRID_W = 64
EPS = 1e-6
NEG_INF = -1e30
ROPE_BASE = 10000.0
BLOCK = 128
WINDOW = 128
HEAD_DIM = 64
LANES = 128
RET_HEADS = 4
RET_WIDTH = RET_HEADS * HEAD_DIM
HY_WIDTH = 256
HY_ORDER = 2
HY_BANDS = 8
HY_POS_DIM = 1 + 2 * HY_BANDS
HY_POS_PAD = 32
HY_FILTER_HIDDEN = 64
GQA_Q_HEADS = 4
GQA_KV_HEADS = 2
GQA_GROUPS = GQA_Q_HEADS // GQA_KV_HEADS
MLA_HEADS = 4
MLA_Q_RANK = 256
MLA_KV_RANK = 128
MLA_NOPE = 64
MLA_ROPE = 32
MLA_V = 64
D_FF = 2816
RET_IN = 4 * RET_WIDTH
HY_IN = (HY_ORDER + 1) * HY_WIDTH
GQA_IN = (GQA_Q_HEADS + 2 * GQA_KV_HEADS) * HEAD_DIM
MLA_IN = MLA_Q_RANK + MLA_KV_RANK + MLA_ROPE
MLA_IN_TILED = MLA_Q_RANK + MLA_KV_RANK + MLA_HEADS * MLA_ROPE
IN_WIDTH = RET_IN + HY_IN + GQA_IN + MLA_IN
IN_WIDTH_TILED = RET_IN + HY_IN + GQA_IN + MLA_IN_TILED
GQA_HEAD_ORDER = (0, 2, 1, 3)
MIX_PART = 256
N_COND = 8
FFN_CHUNK = 256
N_FFN_CHUNKS = D_FF // FFN_CHUNK

VMEM_LIMIT = 56 * 1024 * 1024


def _params(n_axes=1, vmem_limit=VMEM_LIMIT):
    return pltpu.CompilerParams(
        dimension_semantics=("arbitrary",) * n_axes, vmem_limit_bytes=vmem_limit)


def _dot(a, b):
    return jnp.dot(a, b, preferred_element_type=F32)


def _dot_nt(a, b):
    return lax.dot_general(a, b, (((1,), (1,)), ((), ())), preferred_element_type=F32)


def _dot_tn(a, b):
    return lax.dot_general(a, b, (((0,), (0,)), ((), ())), preferred_element_type=F32)


def _rms(x, g):
    return x * lax.rsqrt(jnp.mean(x * x, axis=-1, keepdims=True) + EPS) * g


def _sigmoid(x):
    return 1.0 / (1.0 + jnp.exp(-x))


def _silu(x):
    return x * _sigmoid(x)


def _shift_rows(x, seq_len):
    n = x.shape[0]
    assert seq_len & (seq_len - 1) == 0
    pos = lax.broadcasted_iota(jnp.int32, (n, 1), 0) & (seq_len - 1)
    prev = jnp.where(pos != 0, pltpu.roll(x, 1, axis=0), 0.0)
    nxt = jnp.where(pos != seq_len - 1, pltpu.roll(x, n - 1, axis=0), 0.0)
    return prev, nxt


def _rot_half(x, half):
    lane = lax.broadcasted_iota(jnp.int32, x.shape, 1) & (2 * half - 1)
    return jnp.where(lane < half, pltpu.roll(x, LANES - half, axis=1), pltpu.roll(x, half, axis=1))


def _lane_iota():
    return lax.broadcasted_iota(jnp.int32, (1, LANES), 1)


def _split_heads(t, lo):
    return jnp.concatenate([jnp.where(lo, t, 0.0), jnp.where(lo, 0.0, t)], axis=0)


def _merge_heads(o, lo):
    n = o.shape[0] // 2
    return jnp.where(lo, o[:n], o[n:])


def _per_head(vals, width):
    lane = lax.broadcasted_iota(jnp.int32, (1, width), 1)
    out = vals[-1]
    for h in range(len(vals) - 2, -1, -1):
        out = jnp.where(lane < (h + 1) * HEAD_DIM, vals[h], out)
    return out


def _mod_kernel(cond_ref, w_ref, b_ref, out_ref):
    s = _silu(cond_ref[...]).astype(BF16)
    out_ref[0] = _dot(s, w_ref[0].astype(BF16)) + b_ref[0]


def _modulation(cond, ada_w, ada_b):
    tn = 1536
    return pl.pallas_call(
        _mod_kernel,
        out_shape=jax.ShapeDtypeStruct((DEPTH, N_COND, 6 * D_MODEL), F32),
        grid=(DEPTH, 6 * D_MODEL // tn),
        in_specs=[
            pl.BlockSpec((N_COND, D_MODEL), lambda l, j: (0, 0)),
            pl.BlockSpec((1, D_MODEL, tn), lambda l, j: (l, 0, j)),
            pl.BlockSpec((1, 1, tn), lambda l, j: (l, 0, j)),
        ],
        out_specs=pl.BlockSpec((1, N_COND, tn), lambda l, j: (l, 0, j)),
        compiler_params=_params(2),
        name="modulation",
    )(cond, ada_w, ada_b.reshape(DEPTH, 1, 6 * D_MODEL))


def _filter_kernel(z_ref, w1_ref, b1_ref, w2_ref, b2_ref, w3_ref, dec_ref, f_ref, out_ref, *, seq_len):
    hi = lax.Precision.HIGHEST
    h = jnp.sin(jnp.dot(z_ref[...], w1_ref[0], precision=hi, preferred_element_type=F32) + b1_ref[0])
    h = jnp.sin(jnp.dot(h, w2_ref[0], precision=hi, preferred_element_type=F32) + b2_ref[0])
    h = jnp.dot(h, w3_ref[0], precision=hi, preferred_element_type=F32)
    row = lax.broadcasted_iota(jnp.int32, (seq_len, 1), 0)
    t = row.astype(F32) / seq_len
    dec = jnp.abs(dec_ref[0])
    win_f = jnp.exp(-t * dec[0:1, :])
    win_b = jnp.exp(-t * dec[1:2, :])
    fm = f_ref[...]
    inv = 1.0 / seq_len
    scale = jnp.where(row == 0, 0.5 * inv, inv)
    for o in range(HY_ORDER):
        base = o * 2 * HY_WIDTH
        hf = h[:, base:base + HY_WIDTH] * win_f
        hb = h[:, base + HY_WIDTH:base + 2 * HY_WIDTH] * win_b
        hbs = jnp.where(row != 0, pltpu.roll(hb, 1, axis=0), 0.0)
        sf = _dot(fm, hf.astype(BF16))
        sb = _dot(fm, hbs.astype(BF16))
        g_re = sf[:seq_len] + sb[:seq_len]
        g_im = sf[seq_len:] - sb[seq_len:]
        g_ny = sf[seq_len:] + sb[seq_len:]
        a = g_re * scale
        out_ref[0, 3 * o] = a
        out_ref[0, 3 * o + 1] = jnp.where(row == 0, g_ny * (0.5 * inv), a)
        out_ref[0, 3 * o + 2] = jnp.where(row == 0, 0.0, g_im * inv)


def _hyena_tables(seq_len, z, w1p, b1, w2, b2, w3, decay, fmat):
    hid = HY_FILTER_HIDDEN
    wide = HY_ORDER * 2 * HY_WIDTH
    return pl.pallas_call(
        functools.partial(_filter_kernel, seq_len=seq_len),
        out_shape=jax.ShapeDtypeStruct((DEPTH, 3 * HY_ORDER, seq_len, HY_WIDTH), F32),
        grid=(DEPTH,),
        in_specs=[
            pl.BlockSpec((seq_len, HY_POS_PAD), lambda l: (0, 0)),
            pl.BlockSpec((1, HY_POS_PAD, hid), lambda l: (l, 0, 0)),
            pl.BlockSpec((1, 1, hid), lambda l: (l, 0, 0)),
            pl.BlockSpec((1, hid, hid), lambda l: (l, 0, 0)),
            pl.BlockSpec((1, 1, hid), lambda l: (l, 0, 0)),
            pl.BlockSpec((1, hid, wide), lambda l: (l, 0, 0)),
            pl.BlockSpec((1, 2, HY_WIDTH), lambda l: (l, 0, 0)),
            pl.BlockSpec((2 * seq_len, seq_len), lambda l: (0, 0)),
        ],
        out_specs=pl.BlockSpec((1, 3 * HY_ORDER, seq_len, HY_WIDTH), lambda l: (l, 0, 0, 0)),
        compiler_params=_params(1),
        name=f"hyena_tables_{seq_len}",
    )(z, w1p, b1, w2, b2, w3, decay, fmat)


def _layer_stacked(tail, n_batch, layer, nb):
    zeros = (0,) * len(tail)
    n_layers = DEPTH if layer == 0 else 1
    return (jax.ShapeDtypeStruct((n_batch, DEPTH) + tail, F32),
            pl.BlockSpec((nb, n_layers) + tail, lambda b: (b, layer) + zeros))


def _zero_other_layers(*out_refs):
    for ref in out_refs:
        if ref.shape[1] > 1:
            ref[:, 1:] = jnp.zeros((ref.shape[0], ref.shape[1] - 1) + ref.shape[2:], ref.dtype)


def _carry_specs(in_specs, args, carried, first_out):
    aliases = {}
    if carried is not None:
        for k, arr in enumerate(carried):
            aliases[len(args)] = first_out + k
            in_specs.append(pl.BlockSpec(memory_space=pl.ANY))
            args.append(arr)
    return aliases


def _part(kernel, in_specs, args, out_shape, out_specs, scratch=(), aliases=None):
    as_list = lambda v: list(v) if isinstance(v, (list, tuple)) else [v]
    return dict(kernel=kernel, in_specs=list(in_specs), args=list(args), out_shape=as_list(out_shape),
                out_specs=as_list(out_specs), scratch=list(scratch), aliases=dict(aliases or {}))


def _run_parts(parts, grid, name):
    n_in = sum(len(p["args"]) for p in parts)
    n_out = sum(len(p["out_shape"]) for p in parts)

    def fused(*refs):
        i = o = s = 0
        for p in parts:
            a, b, c = len(p["args"]), len(p["out_shape"]), len(p["scratch"])
            p["kernel"](*refs[i:i + a], *refs[n_in + o:n_in + o + b],
                        *refs[n_in + n_out + s:n_in + n_out + s + c])
            i, o, s = i + a, o + b, s + c

    aliases = {}
    i = o = 0
    for p in parts:
        aliases.update({i + k: o + v for k, v in p["aliases"].items()})
        i, o = i + len(p["args"]), o + len(p["out_shape"])
    outs = pl.pallas_call(
        fused,
        out_shape=[s for p in parts for s in p["out_shape"]],
        grid=grid,
        in_specs=[s for p in parts for s in p["in_specs"]],
        out_specs=[s for p in parts for s in p["out_specs"]],
        input_output_aliases=aliases,
        scratch_shapes=[s for p in parts for s in p["scratch"]],
        compiler_params=_params(len(grid)),
        name=name,
    )(*[a for p in parts for a in p["args"]])
    split, o = [], 0
    for p in parts:
        split.append(list(outs[o:o + len(p["out_shape"])]))
        o += len(p["out_shape"])
    return split


def _in_proj_kernel(x_ref, mod_ref, g_ref, w_ref, ret_ref, hy_ref, gqa_ref, mla_ref, w_scr):
    @pl.when(pl.program_id(0) == 0)
    def _():
        g0 = RET_IN + HY_IN
        m0 = g0 + GQA_IN
        kr0 = m0 + MLA_Q_RANK + MLA_KV_RANK
        step = 2 * LANES
        for r0 in list(range(0, g0, step)) + list(range(g0 + GQA_Q_HEADS * HEAD_DIM, kr0, LANES)):
            n = step if r0 < g0 else LANES
            w_scr[r0:r0 + n, :] = w_ref[0, r0:r0 + n, :].astype(BF16)
        for dst, j in enumerate(GQA_HEAD_ORDER):
            w_scr[g0 + dst * HEAD_DIM:g0 + (dst + 1) * HEAD_DIM, :] = (
                w_ref[0, g0 + j * HEAD_DIM:g0 + (j + 1) * HEAD_DIM, :].astype(BF16))
        kr = w_ref[0, kr0:kr0 + MLA_ROPE, :].astype(BF16)
        for hd in range(MLA_HEADS):
            w_scr[kr0 + hd * MLA_ROPE:kr0 + (hd + 1) * MLA_ROPE, :] = kr

    shift = mod_ref[0, 0, 0:1, :]
    scale = mod_ref[0, 0, 1:2, :]
    h = (_rms(x_ref[...], g_ref[0, 0:1, :]) * (1.0 + scale) + shift).astype(BF16)
    c0 = 0
    for ref in (ret_ref, hy_ref, gqa_ref, mla_ref):
        width = ref.shape[1]
        ref[...] = _dot_nt(h, w_scr[c0:c0 + width, :])
        c0 += width


def _in_proj(x, mod, norm_g, w_in, layer, tm, cond_of_tile):
    rows = x.shape[0]
    widths = (RET_IN, HY_IN, GQA_IN, MLA_IN_TILED)
    return pl.pallas_call(
        _in_proj_kernel,
        out_shape=[jax.ShapeDtypeStruct((rows, w), F32) for w in widths],
        grid=(rows // tm,),
        in_specs=[
            pl.BlockSpec((tm, D_MODEL), lambda i: (i, 0)),
            pl.BlockSpec((1, 1, 6, D_MODEL), lambda i: (layer, cond_of_tile(i), 0, 0)),
            pl.BlockSpec((1, 4, D_MODEL), lambda i: (layer, 0, 0)),
            pl.BlockSpec((1, IN_WIDTH, D_MODEL), lambda i: (layer, 0, 0), pipeline_mode=pl.Buffered(1)),
        ],
        out_specs=[pl.BlockSpec((tm, w), lambda i: (i, 0)) for w in widths],
        scratch_shapes=[pltpu.VMEM((IN_WIDTH_TILED, D_MODEL), BF16)],
        compiler_params=_params(1),
        name="in_proj",
    )(x, mod, norm_g, w_in)


def _channel_kernel(x_ref, m0_ref, m1_ref, m2_ref, m3_ref, mod_ref, g_ref, wo_ref, wu_ref, cw_ref, cb_ref,
                    wd_ref, out_ref, h2_ref, acc_ref, *, seq_len):
    y = None
    for i, m_ref in enumerate((m0_ref, m1_ref, m2_ref, m3_ref)):
        r0 = i * MIX_PART
        if i == 2:
            w = jnp.concatenate([wo_ref[0, r0 + j * HEAD_DIM:r0 + (j + 1) * HEAD_DIM, :]
                                 for j in GQA_HEAD_ORDER], axis=0)
        else:
            w = wo_ref[0, r0:r0 + MIX_PART, :]
        part = _dot(m_ref[...], w)
        y = part if y is None else y + part
    gate1 = mod_ref[0, 0, 2:3, :]
    shift2 = mod_ref[0, 0, 3:4, :]
    scale2 = mod_ref[0, 0, 4:5, :]
    xm = x_ref[...] + gate1 * _rms(y, g_ref[0, 1:2, :])
    out_ref[...] = xm
    h2_ref[...] = (_rms(xm, g_ref[0, 2:3, :]) * (1.0 + scale2) + shift2).astype(BF16)
    acc_ref[...] = jnp.zeros(acc_ref.shape, F32)

    def chunk(c, carry):
        off = pl.multiple_of(c * FFN_CHUNK, FFN_CHUNK)
        h2 = h2_ref[...]
        gate = _dot(h2, wu_ref[0, :, pl.ds(off, FFN_CHUNK)])
        up = _dot(h2, wu_ref[0, :, pl.ds(D_FF + off, FFN_CHUNK)])
        cw = cw_ref[0, :, pl.ds(off, FFN_CHUNK)]
        prev, nxt = _shift_rows(gate, seq_len)
        gate = prev * cw[0:1, :] + gate * cw[1:2, :] + nxt * cw[2:3, :] + cb_ref[0, :, pl.ds(off, FFN_CHUNK)]
        act = (_silu(gate) * up).astype(BF16)
        acc_ref[...] += _dot(act, wd_ref[0, pl.ds(off, FFN_CHUNK), :])
        return carry

    lax.fori_loop(0, N_FFN_CHUNKS, chunk, 0)
    gate2 = mod_ref[0, 0, 5:6, :]
    out_ref[...] = out_ref[...] + gate2 * _rms(acc_ref[...], g_ref[0, 3:4, :])


def _channel_mix(x, mixes, mod, norm_g, w_out, w_up, conv_w, conv_b, w_down, layer, tm, seq_len, cond_of_tile):
    rows = x.shape[0]
    resident = dict(pipeline_mode=pl.Buffered(1))
    return pl.pallas_call(
        functools.partial(_channel_kernel, seq_len=seq_len),
        out_shape=jax.ShapeDtypeStruct((rows, D_MODEL), F32),
        grid=(rows // tm,),
        in_specs=[pl.BlockSpec((tm, D_MODEL), lambda i: (i, 0))]
        + [pl.BlockSpec((tm, MIX_PART), lambda i: (i, 0))] * 4
        + [
            pl.BlockSpec((1, 1, 6, D_MODEL), lambda i: (layer, cond_of_tile(i), 0, 0)),
            pl.BlockSpec((1, 4, D_MODEL), lambda i: (layer, 0, 0)),
            pl.BlockSpec((1, D_MODEL, D_MODEL), lambda i: (layer, 0, 0), **resident),
            pl.BlockSpec((1, D_MODEL, 2 * D_FF), lambda i: (layer, 0, 0), **resident),
            pl.BlockSpec((1, 3, D_FF), lambda i: (layer, 0, 0)),
            pl.BlockSpec((1, 1, D_FF), lambda i: (layer, 0, 0)),
            pl.BlockSpec((1, D_FF, D_MODEL), lambda i: (layer, 0, 0), **resident),
        ],
        out_specs=pl.BlockSpec((tm, D_MODEL), lambda i: (i, 0)),
        scratch_shapes=[pltpu.VMEM((tm, D_MODEL), BF16), pltpu.VMEM((tm, D_MODEL), F32)],
        compiler_params=_params(1),
        name="channel_mix",
    )(x, *mixes, mod, norm_g, w_out, w_up, conv_w, conv_b, w_down)


def _log_gamma(dl_ref):
    dl = dl_ref[0]
    return jnp.minimum(dl, 0.0) - jnp.log(1.0 + jnp.exp(-jnp.abs(dl)))


def _pair_decay(lg, t, r0, bq, seq_len):
    top = lax.broadcasted_iota(jnp.int32, (2 * bq, 1), 0) < bq
    lgf = jnp.where(top, lg[0:1, 2 * t:2 * t + 1], lg[0:1, 2 * t + 1:2 * t + 2])
    lgb = jnp.where(top, lg[1:2, 2 * t:2 * t + 1], lg[1:2, 2 * t + 1:2 * t + 2])
    rowf = ((lax.broadcasted_iota(jnp.int32, (2 * bq, 1), 0) & (bq - 1)) + r0).astype(F32)
    colf = lax.broadcasted_iota(jnp.int32, (1, seq_len), 1).astype(F32)
    lag = rowf - colf
    decay = jnp.exp(jnp.where(lag >= 0.0, lag * lgf, -lag * lgb))
    return jnp.where(lag == 0.0, 2.0, decay)


def _head_norm(o, lo):
    inv = 1.0 / HEAD_DIM
    s_all = jnp.sum(o, axis=-1, keepdims=True)
    s_lo = jnp.sum(jnp.where(lo, o, 0.0), axis=-1, keepdims=True)
    d = o - jnp.where(lo, s_lo, s_all - s_lo) * inv
    d2 = d * d
    v_all = jnp.sum(d2, axis=-1, keepdims=True)
    v_lo = jnp.sum(jnp.where(lo, d2, 0.0), axis=-1, keepdims=True)
    return d * lax.rsqrt(jnp.where(lo, v_lo, v_all - v_lo) * inv + EPS)


def _ret_ctx_kernel(*refs, seq_len, nb):
    x_ref, dl_ref, gn_ref, *_, y_ref, st_ref, dec_scr, kdec_scr = refs
    lo = _lane_iota() < HEAD_DIM
    _zero_other_layers(st_ref)

    @pl.when(pl.program_id(0) == 0)
    def _():
        lg = _log_gamma(dl_ref)
        posf = lax.broadcasted_iota(jnp.int32, (seq_len, 1), 0).astype(F32)
        lgf = _per_head([lg[0:1, h:h + 1] for h in range(RET_HEADS)], RET_WIDTH)
        lgb = _per_head([lg[1:2, h:h + 1] for h in range(RET_HEADS)], RET_WIDTH)
        kdec_scr[0] = jnp.exp((seq_len - 1.0 - posf) * lgf)
        kdec_scr[1] = jnp.exp(posf * lgb)
        for t in range(RET_HEADS // 2):
            dec_scr[t] = _pair_decay(lg, t, 0, seq_len, seq_len)

    for i in range(nb):
        rows = slice(i * seq_len, (i + 1) * seq_len)
        for t in range(RET_HEADS // 2):
            cs = slice(t * LANES, (t + 1) * LANES)
            q = x_ref[rows, cs]
            k = x_ref[rows, RET_WIDTH + t * LANES:RET_WIDTH + (t + 1) * LANES] * (HEAD_DIM ** -0.5)
            vb = x_ref[rows, 2 * RET_WIDTH + t * LANES:2 * RET_WIDTH + (t + 1) * LANES].astype(BF16)
            gate = x_ref[rows, 3 * RET_WIDTH + t * LANES:3 * RET_WIDTH + (t + 1) * LANES]
            s = _dot_nt(_split_heads(q, lo).astype(BF16), k.astype(BF16)) * dec_scr[t]
            o = _merge_heads(_dot(s.astype(BF16), vb), lo)
            y = _silu(gate) * (_head_norm(o, lo) * gn_ref[0, :, cs])
            y_ref[rows, cs] = y.astype(BF16)
            for d in range(2):
                st = _dot_tn((k * kdec_scr[d, :, cs]).astype(BF16), vb)
                st_ref[i, 0, d, 2 * t] = st[0:HEAD_DIM, 0:HEAD_DIM]
                st_ref[i, 0, d, 2 * t + 1] = st[HEAD_DIM:LANES, HEAD_DIM:LANES]


def _ret_lat_kernel(x_ref, dl_ref, gn_ref, s0_ref, y_ref, *, seq_len, bq):
    lo = _lane_iota() < HEAD_DIM
    lg = _log_gamma(dl_ref)
    zero = jnp.zeros((HEAD_DIM, HEAD_DIM), F32)
    for t in range(RET_HEADS // 2):
        cs = slice(t * LANES, (t + 1) * LANES)
        kb = (x_ref[:, RET_WIDTH + t * LANES:RET_WIDTH + (t + 1) * LANES] * (HEAD_DIM ** -0.5)).astype(BF16)
        vb = x_ref[:, 2 * RET_WIDTH + t * LANES:2 * RET_WIDTH + (t + 1) * LANES].astype(BF16)
        lgf = _per_head([lg[0:1, 2 * t:2 * t + 1], lg[0:1, 2 * t + 1:2 * t + 2]], LANES)
        lgb = _per_head([lg[1:2, 2 * t:2 * t + 1], lg[1:2, 2 * t + 1:2 * t + 2]], LANES)
        s0 = []
        for d in range(2):
            a = s0_ref[0, 0, d, 2 * t]
            b = s0_ref[0, 0, d, 2 * t + 1]
            s0.append(jnp.concatenate([jnp.concatenate([a, zero], axis=1),
                                       jnp.concatenate([zero, b], axis=1)], axis=0).astype(BF16))
        for r0 in range(0, seq_len, bq):
            q = x_ref[r0:r0 + bq, cs]
            s = _dot_nt(_split_heads(q, lo).astype(BF16), kb) * _pair_decay(lg, t, r0, bq, seq_len)
            o = _merge_heads(_dot(s.astype(BF16), vb), lo)
            rowf = (lax.broadcasted_iota(jnp.int32, (bq, 1), 0) + r0).astype(F32)
            qf = (q * jnp.exp((rowf + 1.0) * lgf)).astype(BF16)
            qr = (q * jnp.exp((seq_len - rowf) * lgb)).astype(BF16)
            o = o + _dot(qf, s0[0]) + _dot(qr, s0[1])
            gate = x_ref[r0:r0 + bq, 3 * RET_WIDTH + t * LANES:3 * RET_WIDTH + (t + 1) * LANES]
            y = _silu(gate) * (_head_norm(o, lo) * gn_ref[0, :, cs])
            y_ref[r0:r0 + bq, cs] = y.astype(BF16)


def _retention(ret_in, decay_logit, gn_g, state0, carried, layer, n_batch, seq_len, nb):
    latent = state0 is not None
    in_specs = [
        pl.BlockSpec((nb * seq_len, RET_IN), lambda b: (b, 0)),
        pl.BlockSpec((1, 2, RET_HEADS), lambda b: (layer, 0, 0)),
        pl.BlockSpec((1, 1, RET_WIDTH), lambda b: (layer, 0, 0)),
    ]
    args = [ret_in, decay_logit, gn_g]
    y_shape = jax.ShapeDtypeStruct((n_batch * seq_len, MIX_PART), BF16)
    y_spec = pl.BlockSpec((nb * seq_len, MIX_PART), lambda b: (b, 0))
    if latent:
        assert nb == 1
        kern = functools.partial(_ret_lat_kernel, seq_len=seq_len, bq=256)
        in_specs.append(pl.BlockSpec((1, 1, 2, RET_HEADS, HEAD_DIM, HEAD_DIM),
                                     lambda b: (b, layer, 0, 0, 0, 0)))
        args.append(state0)
        out_shape, out_specs, aliases, scratch = y_shape, y_spec, {}, []
    else:
        kern = functools.partial(_ret_ctx_kernel, seq_len=seq_len, nb=nb)
        st_shape, st_spec = _layer_stacked((2, RET_HEADS, HEAD_DIM, HEAD_DIM), n_batch, layer, nb)
        out_shape = [y_shape, st_shape]
        out_specs = [y_spec, st_spec]
        aliases = _carry_specs(in_specs, args, carried, 1)
        scratch = [pltpu.VMEM((RET_HEADS // 2, 2 * seq_len, seq_len), F32),
                   pltpu.VMEM((2, seq_len, RET_WIDTH), F32)]
    return _part(kern, in_specs, args, out_shape, out_specs, scratch, aliases)


def _hyena_kernel(x_ref, sw_ref, sb_ref, bias_ref, tab_ref, f_ref, ft_ref, y_ref, *, seq_len, nb):
    for i in range(nb):
        rows = slice(i * seq_len, (i + 1) * seq_len)
        x = x_ref[rows, :]
        prev, nxt = _shift_rows(x, seq_len)
        u = prev * sw_ref[0, 0:1, :] + x * sw_ref[0, 1:2, :] + nxt * sw_ref[0, 2:3, :] + sb_ref[0]
        x1 = u[:, 0:HY_WIDTH]
        x2 = u[:, HY_WIDTH:2 * HY_WIDTH]
        z = u[:, 2 * HY_WIDTH:3 * HY_WIDTH]
        for o, gate in enumerate((x1, x2)):
            spec = _dot(f_ref[...], z.astype(BF16))
            s_re = spec[:seq_len]
            s_im = spec[seq_len:]
            a = tab_ref[0, 3 * o]
            a_ny = tab_ref[0, 3 * o + 1]
            b = tab_ref[0, 3 * o + 2]
            y_re = (s_re * a - s_im * b).astype(BF16)
            y_im = (s_re * b + s_im * a_ny).astype(BF16)
            conv = _dot(ft_ref[:, 0:seq_len], y_re) + _dot(ft_ref[:, seq_len:2 * seq_len], y_im)
            z = gate * (conv + z * bias_ref[0, o:o + 1, :])
        y_ref[rows, :] = z.astype(BF16)


def _hyena(hy_in, short_w, short_b, hy_bias, tables, fmat, fmat_t, layer, n_batch, seq_len, nb):
    const = dict(pipeline_mode=pl.Buffered(1))
    return _part(
        functools.partial(_hyena_kernel, seq_len=seq_len, nb=nb),
        [
            pl.BlockSpec((nb * seq_len, HY_IN), lambda b: (b, 0)),
            pl.BlockSpec((1, 3, HY_IN), lambda b: (layer, 0, 0)),
            pl.BlockSpec((1, 1, HY_IN), lambda b: (layer, 0, 0)),
            pl.BlockSpec((1, HY_ORDER, HY_WIDTH), lambda b: (layer, 0, 0)),
            pl.BlockSpec((1, 3 * HY_ORDER, seq_len, HY_WIDTH), lambda b: (layer, 0, 0, 0), **const),
            pl.BlockSpec((2 * seq_len, seq_len), lambda b: (0, 0), **const),
            pl.BlockSpec((seq_len, 2 * seq_len), lambda b: (0, 0), **const),
        ],
        [hy_in, short_w, short_b, hy_bias, tables, fmat, fmat_t],
        jax.ShapeDtypeStruct((n_batch * seq_len, MIX_PART), BF16),
        pl.BlockSpec((nb * seq_len, MIX_PART), lambda b: (b, 0)))


def _gqa_ctx_kernel(*refs, seq_len, nb):
    x_ref, sink_ref, *_, y_ref, k_out_ref, v_out_ref = refs
    nq = GQA_Q_HEADS * HEAD_DIM
    scale = HEAD_DIM ** -0.5
    head_a = lax.broadcasted_iota(jnp.int32, (LANES, 1), 0) < HEAD_DIM
    first = lax.broadcasted_iota(jnp.int32, (1, 2 * seq_len), 1) < seq_len
    ones = jnp.ones((LANES, seq_len), BF16)
    _zero_other_layers(k_out_ref, v_out_ref)
    for i in range(nb):
        rows = slice(i * seq_len, (i + 1) * seq_len)
        k = x_ref[rows, nq:nq + LANES]
        k_t = k.T
        v_t = x_ref[rows, nq + LANES:nq + 2 * LANES].T
        for kv in range(GQA_KV_HEADS):
            k_out_ref[i, 0, kv] = k_t[kv * HEAD_DIM:(kv + 1) * HEAD_DIM, :]
            v_out_ref[i, 0, kv] = v_t[kv * HEAD_DIM:(kv + 1) * HEAD_DIM, :]
        kb = k.astype(BF16)
        v_ext = jnp.concatenate([v_t.astype(BF16), ones], axis=0)
        for g in range(GQA_GROUPS):
            cs = slice(g * LANES, (g + 1) * LANES)
            q_t = x_ref[rows, cs].T
            q2 = jnp.concatenate([jnp.where(head_a, q_t, 0.0), jnp.where(head_a, 0.0, q_t)], axis=1)
            s = _dot(kb, q2.astype(BF16)) * scale
            sink = jnp.where(first, sink_ref[0, :, g:g + 1], sink_ref[0, :, GQA_GROUPS + g:GQA_GROUPS + g + 1])
            m = jnp.maximum(jnp.max(s, axis=0, keepdims=True), sink)
            pv = _dot(v_ext, jnp.exp(s - m).astype(BF16))
            o = pv[0:LANES, :] / (pv[LANES:LANES + 1, :] + jnp.exp(sink - m))
            o = jnp.where(head_a, o[:, 0:seq_len], o[:, seq_len:2 * seq_len])
            y_ref[rows, cs] = o.T.astype(BF16)


def _gqa_ctx(gqa_in, sink, carried, layer, n_batch, seq_len, nb):
    kv_shape, kv_spec = _layer_stacked((GQA_KV_HEADS, HEAD_DIM, seq_len), n_batch, layer, nb)
    in_specs = [
        pl.BlockSpec((nb * seq_len, GQA_IN), lambda b: (b, 0)),
        pl.BlockSpec((1, 1, GQA_Q_HEADS), lambda b: (layer, 0, 0)),
    ]
    args = [gqa_in, sink]
    aliases = _carry_specs(in_specs, args, carried, 1)
    return _part(
        functools.partial(_gqa_ctx_kernel, seq_len=seq_len, nb=nb), in_specs, args,
        [jax.ShapeDtypeStruct((n_batch * seq_len, MIX_PART), BF16), kv_shape, kv_shape],
        [pl.BlockSpec((nb * seq_len, MIX_PART), lambda b: (b, 0)), kv_spec, kv_spec],
        aliases=aliases)


def _gqa_win_kernel(x_ref, sink_ref, kct_ref, vct_ref, cos_ref, sin_ref, y_ref,
                    q_scr, k_scr, vt_scr, kc_scr, vct_scr, *, seq_len):
    nq = GQA_Q_HEADS * HEAD_DIM
    scale = HEAD_DIM ** -0.5
    head_a = lax.broadcasted_iota(jnp.int32, (LANES, 1), 0) < HEAD_DIM
    first = lax.broadcasted_iota(jnp.int32, (1, 2 * BLOCK), 1) < BLOCK
    cos = cos_ref[...]
    sin = sin_ref[...]

    def rope(t):
        return t * cos + _rot_half(t, HEAD_DIM // 2) * sin

    for g in range(GQA_GROUPS):
        q_t = rope(x_ref[:, g * LANES:(g + 1) * LANES]).T
        q_scr[g, 0] = jnp.where(head_a, q_t, 0.0).astype(BF16)
        q_scr[g, 1] = jnp.where(head_a, 0.0, q_t).astype(BF16)
    zeros = jnp.zeros((BLOCK, LANES), BF16)
    for r0 in (0, BLOCK + seq_len):
        k_scr[r0:r0 + BLOCK, :] = zeros
        vt_scr[:, r0:r0 + BLOCK] = zeros
    k_scr[BLOCK:BLOCK + seq_len, :] = rope(x_ref[:, nq:nq + LANES]).astype(BF16)
    vt_scr[:, BLOCK:BLOCK + seq_len] = x_ref[:, nq + LANES:nq + 2 * LANES].T.astype(BF16)
    kc_scr[...] = jnp.concatenate([kct_ref[0, 0, 0], kct_ref[0, 0, 1]], axis=0).T.astype(BF16)
    vct_scr[...] = jnp.concatenate([vct_ref[0, 0, 0], vct_ref[0, 0, 1]], axis=0).astype(BF16)

    jj = lax.broadcasted_iota(jnp.int32, (3 * BLOCK, 2 * BLOCK), 0)
    ii = lax.broadcasted_iota(jnp.int32, (3 * BLOCK, 2 * BLOCK), 1) & (BLOCK - 1)
    band = (jj >= ii) & (jj <= ii + 2 * WINDOW)
    for n in range(seq_len // BLOCK):
        r0 = n * BLOCK
        kpos = jj + (n - 1) * BLOCK
        valid = band & (kpos >= 0) & (kpos < seq_len)
        kw = k_scr[r0:r0 + 3 * BLOCK, :]
        vw = vt_scr[:, r0:r0 + 3 * BLOCK]
        for g in range(GQA_GROUPS):
            q2 = jnp.concatenate([q_scr[g, 0, :, r0:r0 + BLOCK], q_scr[g, 1, :, r0:r0 + BLOCK]], axis=1)
            s_win = jnp.where(valid, _dot(kw, q2) * scale, NEG_INF)
            s_ctx = _dot(kc_scr[...], q2) * scale
            sink = jnp.where(first, sink_ref[0, :, g:g + 1], sink_ref[0, :, GQA_GROUPS + g:GQA_GROUPS + g + 1])
            m = jnp.maximum(jnp.maximum(jnp.max(s_win, axis=0, keepdims=True),
                                        jnp.max(s_ctx, axis=0, keepdims=True)), sink)
            p_win = jnp.exp(s_win - m)
            p_ctx = jnp.exp(s_ctx - m)
            den = (jnp.sum(p_win, axis=0, keepdims=True) + jnp.sum(p_ctx, axis=0, keepdims=True)
                   + jnp.exp(sink - m))
            o = (_dot(vw, p_win.astype(BF16)) + _dot(vct_scr[...], p_ctx.astype(BF16))) / den
            o = jnp.where(head_a, o[:, 0:BLOCK], o[:, BLOCK:2 * BLOCK])
            y_ref[r0:r0 + BLOCK, g * LANES:(g + 1) * LANES] = o.T.astype(BF16)


def _gqa_win(gqa_in, sink, cache_k_t, cache_v_t, cos, sin, layer, n_batch, seq_len):
    past = cache_k_t.shape[4]
    return _part(
        functools.partial(_gqa_win_kernel, seq_len=seq_len),
        [
            pl.BlockSpec((seq_len, GQA_IN), lambda b: (b, 0)),
            pl.BlockSpec((1, 1, GQA_Q_HEADS), lambda b: (layer, 0, 0)),
            pl.BlockSpec((1, 1, GQA_KV_HEADS, HEAD_DIM, past), lambda b: (b, layer, 0, 0, 0)),
            pl.BlockSpec((1, 1, GQA_KV_HEADS, HEAD_DIM, past), lambda b: (b, layer, 0, 0, 0)),
            pl.BlockSpec((seq_len, LANES), lambda b: (0, 0)),
            pl.BlockSpec((seq_len, LANES), lambda b: (0, 0)),
        ],
        [gqa_in, sink, cache_k_t, cache_v_t, cos, sin],
        jax.ShapeDtypeStruct((n_batch * seq_len, MIX_PART), BF16),
        pl.BlockSpec((seq_len, MIX_PART), lambda b: (b, 0)),
        scratch=[
            pltpu.VMEM((GQA_GROUPS, 2, LANES, seq_len), BF16),
            pltpu.VMEM((seq_len + 2 * BLOCK, LANES), BF16),
            pltpu.VMEM((LANES, seq_len + 2 * BLOCK), BF16),
            pltpu.VMEM((past, LANES), BF16),
            pltpu.VMEM((LANES, past), BF16),
        ])


def _mla_kernel(*refs, seq_len, latent, bq, nb):
    if latent:
        (x_ref, qn_ref, kvn_ref, wq_ref, wk_ref, wv_ref, ckv_c_ref, kr_c_ref, cos_ref, sin_ref,
         y_ref, q_all, k_all, v_all) = refs
        past = ckv_c_ref.shape[2]
    else:
        (x_ref, qn_ref, kvn_ref, wq_ref, wk_ref, wv_ref, *_,
         y_ref, ckv_out_ref, kr_out_ref, q_all, k_all, v_all) = refs
        past = 0
        _zero_other_layers(ckv_out_ref, kr_out_ref)
    keys = past + seq_len
    nope_w = MLA_HEADS * MLA_NOPE
    scale = (MLA_NOPE + MLA_ROPE) ** -0.5
    lane = _lane_iota()
    lo = lane < HEAD_DIM
    rope_lanes = [(lane >= h * MLA_ROPE) & (lane < (h + 1) * MLA_ROPE) for h in range(MLA_HEADS)]
    for i in range(nb):
        rows = slice(i * seq_len, (i + 1) * seq_len)
        q_scr, k_scr, v_scr = q_all.at[i], k_all.at[i], v_all.at[i]
        q_lat = x_ref[rows, 0:MLA_Q_RANK]
        kv_lat = x_ref[rows, MLA_Q_RANK:MLA_Q_RANK + MLA_KV_RANK]
        k_rope = x_ref[rows, MLA_Q_RANK + MLA_KV_RANK:MLA_IN_TILED]
        mq = _dot(_rms(q_lat, qn_ref[0]).astype(BF16), wq_ref[0])
        ckv = _rms(kv_lat, kvn_ref[0])
        q_rope = mq[:, nope_w:nope_w + LANES]
        if latent:
            cos = cos_ref[...]
            sin = sin_ref[...]
            q_rope = q_rope * cos + _rot_half(q_rope, MLA_ROPE // 2) * sin
            k_rope = k_rope * cos + _rot_half(k_rope, MLA_ROPE // 2) * sin
            ckv_c = ckv_c_ref[0, 0].astype(BF16)
            k_scr[0:past, 0:nope_w] = _dot(ckv_c, wk_ref[0]).astype(BF16)
            v_scr[0:past, :] = _dot(ckv_c, wv_ref[0]).astype(BF16)
            kr_c = kr_c_ref[0, 0]
            k_scr[0:past, nope_w:nope_w + LANES] = jnp.concatenate([kr_c] * MLA_HEADS, axis=1).astype(BF16)
        else:
            ckv_out_ref[i, 0] = ckv
            kr_out_ref[i, 0] = k_rope.T[0:MLA_ROPE, :]
        q_scr[:, 0:nope_w] = mq[:, 0:nope_w]
        q_scr[:, nope_w:nope_w + LANES] = q_rope
        ckv_b = ckv.astype(BF16)
        k_scr[past:keys, 0:nope_w] = _dot(ckv_b, wk_ref[0]).astype(BF16)
        v_scr[past:keys, :] = _dot(ckv_b, wv_ref[0]).astype(BF16)
        k_scr[past:keys, nope_w:nope_w + LANES] = k_rope.astype(BF16)

        ones = jnp.ones((keys, LANES), BF16)
        kr_all = k_scr[:, nope_w:nope_w + LANES]
        for t in range(MLA_HEADS // 2):
            cs = slice(t * LANES, (t + 1) * LANES)
            k_cat = jnp.concatenate([k_scr[:, cs], kr_all], axis=1)
            v_ext = jnp.concatenate([v_scr[:, cs], ones], axis=1)
            for r0 in range(0, seq_len, bq):
                qn = q_scr[r0:r0 + bq, cs]
                qr = q_scr[r0:r0 + bq, nope_w:nope_w + LANES]
                q2 = jnp.concatenate([
                    jnp.concatenate([jnp.where(lo, qn, 0.0), jnp.where(rope_lanes[2 * t], qr, 0.0)], axis=1),
                    jnp.concatenate([jnp.where(lo, 0.0, qn), jnp.where(rope_lanes[2 * t + 1], qr, 0.0)], axis=1),
                ], axis=0).astype(BF16)
                s = _dot_nt(q2, k_cat) * scale
                m = jnp.max(s, axis=-1, keepdims=True)
                pv = _dot(jnp.exp(s - m).astype(BF16), v_ext)
                o = pv[:, 0:LANES] / pv[:, LANES:LANES + 1]
                y_ref[i * seq_len + r0:i * seq_len + r0 + bq, cs] = _merge_heads(o, lo).astype(BF16)


def _mla(mla_in, q_norm, kv_norm, w_uq, w_uk, w_uv, cache, rope, carried, layer, n_batch, seq_len, nb):
    latent = cache is not None
    aliases = {}
    past = cache[0].shape[2] if latent else 0
    kern = functools.partial(_mla_kernel, seq_len=seq_len, latent=latent, bq=256, nb=nb)
    q_w = MLA_HEADS * (MLA_NOPE + MLA_ROPE)
    kv_w = MLA_HEADS * MLA_NOPE
    in_specs = [
        pl.BlockSpec((nb * seq_len, MLA_IN_TILED), lambda b: (b, 0)),
        pl.BlockSpec((1, 1, MLA_Q_RANK), lambda b: (layer, 0, 0)),
        pl.BlockSpec((1, 1, MLA_KV_RANK), lambda b: (layer, 0, 0)),
        pl.BlockSpec((1, MLA_Q_RANK, q_w), lambda b: (layer, 0, 0)),
        pl.BlockSpec((1, MLA_KV_RANK, kv_w), lambda b: (layer, 0, 0)),
        pl.BlockSpec((1, MLA_KV_RANK, kv_w), lambda b: (layer, 0, 0)),
    ]
    args = [mla_in, q_norm, kv_norm, w_uq, w_uk, w_uv]
    y_shape = jax.ShapeDtypeStruct((n_batch * seq_len, MIX_PART), BF16)
    y_spec = pl.BlockSpec((nb * seq_len, MIX_PART), lambda b: (b, 0))
    if latent:
        assert nb == 1
        in_specs += [
            pl.BlockSpec((1, 1, past, MLA_KV_RANK), lambda b: (b, layer, 0, 0)),
            pl.BlockSpec((1, 1, past, MLA_ROPE), lambda b: (b, layer, 0, 0)),
            pl.BlockSpec((seq_len, LANES), lambda b: (0, 0)),
            pl.BlockSpec((seq_len, LANES), lambda b: (0, 0)),
        ]
        args += [cache[0], cache[1], rope[0], rope[1]]
        out_shape, out_specs = y_shape, y_spec
    else:
        ckv_shape, ckv_spec = _layer_stacked((seq_len, MLA_KV_RANK), n_batch, layer, nb)
        kr_shape, kr_spec = _layer_stacked((MLA_ROPE, seq_len), n_batch, layer, nb)
        out_shape = [y_shape, ckv_shape, kr_shape]
        out_specs = [y_spec, ckv_spec, kr_spec]
        aliases = _carry_specs(in_specs, args, carried, 1)
    keys = past + seq_len
    return _part(
        kern, in_specs, args, out_shape, out_specs,
        scratch=[
            pltpu.VMEM((nb, seq_len, q_w), F32),
            pltpu.VMEM((nb, keys, kv_w + LANES), BF16),
            pltpu.VMEM((nb, keys, kv_w), BF16),
        ],
        aliases=aliases)


def _dft_matrix(seq_len):
    k = np.arange(seq_len, dtype=np.float64)[:, None]
    j = np.arange(seq_len, dtype=np.float64)[None, :]
    ang = np.pi * k * j / seq_len
    re = np.cos(ang)
    im = -np.sin(ang)
    im[0, :] = np.where(np.arange(seq_len) % 2 == 0, 1.0, -1.0)
    return np.concatenate([re, im], axis=0).astype(np.float32)


def _hyena_positions(seq_len):
    t = np.arange(seq_len, dtype=np.float64) / seq_len
    bands = np.arange(1, HY_BANDS + 1, dtype=np.float64)
    ang = 2.0 * math.pi * t[:, None] * bands
    z = np.concatenate([t[:, None], np.cos(ang), np.sin(ang)], axis=-1)
    return np.pad(z, ((0, 0), (0, HY_POS_PAD - HY_POS_DIM))).astype(np.float32)


def _rope_tables(n_tokens, dim):
    rows = np.repeat(np.arange(n_tokens // GRID_W, dtype=np.float64), GRID_W)
    cols = np.tile(np.arange(GRID_W, dtype=np.float64), n_tokens // GRID_W)
    quarter = dim // 4
    inv = ROPE_BASE ** (-np.arange(quarter, dtype=np.float64) / quarter)
    ang = np.concatenate([rows[:, None] * inv, cols[:, None] * inv], axis=-1)
    cos = np.concatenate([np.cos(ang), np.cos(ang)], axis=-1)
    sin = np.concatenate([-np.sin(ang), np.sin(ang)], axis=-1)
    reps = LANES // dim
    return (np.tile(cos, (1, reps)).astype(np.float32), np.tile(sin, (1, reps)).astype(np.float32))


def kernel(x_prompt, x_sample, state_ret, cache_gqa_k, cache_gqa_v, cache_mla_ckv, cache_mla_krope, c, c_ctx, ada_w, ada_b, norm_g, w_in, ret_decay_logit, ret_gn_g, hy_short_w, hy_short_b, hy_w1, hy_b1, hy_w2, hy_b2, hy_w3, hy_decay, hy_bias, gqa_sink, mla_q_norm, mla_kv_norm, mla_w_uq, mla_w_uk, mla_w_uv, w_out, ffn_w_up, ffn_conv_w, ffn_conv_b, ffn_w_down):
    n_p, len_p, _ = x_prompt.shape
    n_s, len_s, _ = x_sample.shape

    w_in_t = jnp.swapaxes(w_in, 1, 2)
    cache_k_t = jnp.swapaxes(cache_gqa_k, -1, -2)
    cache_v_t = jnp.swapaxes(cache_gqa_v, -1, -2)
    w_out_b = w_out.astype(BF16)
    w_up_b = ffn_w_up.astype(BF16)
    w_down_b = ffn_w_down.astype(BF16)
    uq = mla_w_uq.reshape(DEPTH, MLA_Q_RANK, MLA_HEADS, MLA_NOPE + MLA_ROPE)
    w_uq_b = jnp.concatenate(
        [uq[..., :MLA_NOPE].reshape(DEPTH, MLA_Q_RANK, MLA_HEADS * MLA_NOPE),
         uq[..., MLA_NOPE:].reshape(DEPTH, MLA_Q_RANK, MLA_HEADS * MLA_ROPE)], axis=-1).astype(BF16)
    w_uk_b = mla_w_uk.astype(BF16)
    w_uv_b = mla_w_uv.astype(BF16)

    cond = jnp.concatenate([c_ctx[None], c, jnp.zeros((N_COND - 1 - n_s, D_MODEL), F32)], axis=0)
    mod = _modulation(cond, ada_w, ada_b).reshape(DEPTH, N_COND, 6, D_MODEL)

    w1p = jnp.pad(hy_w1, ((0, 0), (0, HY_POS_PAD - HY_POS_DIM), (0, 0)))
    b1 = hy_b1.reshape(DEPTH, 1, HY_FILTER_HIDDEN)
    b2 = hy_b2.reshape(DEPTH, 1, HY_FILTER_HIDDEN)
    groups = {}
    for seq_len in (len_p, len_s):
        f32mat = _dft_matrix(seq_len)
        fmat = jnp.asarray(f32mat).astype(BF16)
        fmat_t = jnp.asarray(np.ascontiguousarray(f32mat.T)).astype(BF16)
        tables = _hyena_tables(seq_len, jnp.asarray(_hyena_positions(seq_len)), w1p, b1, hy_w2, b2,
                               hy_w3, hy_decay, fmat)
        groups[seq_len] = (tables, fmat, fmat_t)

    rope_g = tuple(jnp.asarray(t) for t in _rope_tables(len_s, HEAD_DIM))
    rope_m = tuple(jnp.asarray(t) for t in _rope_tables(len_s, MLA_ROPE))

    gn_g = ret_gn_g.reshape(DEPTH, 1, RET_WIDTH)
    short_b = hy_short_b.reshape(DEPTH, 1, HY_IN)
    sink = gqa_sink.reshape(DEPTH, 1, GQA_Q_HEADS)
    q_norm = mla_q_norm.reshape(DEPTH, 1, MLA_Q_RANK)
    kv_norm = mla_kv_norm.reshape(DEPTH, 1, MLA_KV_RANK)
    conv_b = ffn_conv_b.reshape(DEPTH, 1, D_FF)

    xp = x_prompt.reshape(n_p * len_p, D_MODEL)
    xs = x_sample.reshape(n_s * len_s, D_MODEL)
    tm_p = 1024
    tm_s = 512
    nb_p = 4
    nb_first = 2
    cond_p = lambda i: 0
    cond_s = lambda i: 1 + (i * tm_s) // len_s
    tm_ffn = 1024
    cond_s_ffn = lambda i: 1 + (i * tm_ffn) // len_s

    st = gkv = ckr = None
    for l in range(DEPTH):
        tables, fmat, fmat_t = groups[len_p]
        ret_in, hy_in, gqa_in, mla_in = _in_proj(xp, mod, norm_g, w_in_t, l, tm_p, cond_p)
        nb = nb_first if l == 0 else nb_p
        (y_ret, *st), (y_hy,), (y_gqa, *gkv), (y_mla, *ckr) = _run_parts([
            _retention(ret_in, ret_decay_logit, gn_g, None, st, l, n_p, len_p, nb),
            _hyena(hy_in, hy_short_w, short_b, hy_bias, tables, fmat, fmat_t, l, n_p, len_p, nb),
            _gqa_ctx(gqa_in, sink, gkv, l, n_p, len_p, nb),
            _mla(mla_in, q_norm, kv_norm, w_uq_b, w_uk_b, w_uv_b, None, None, ckr, l, n_p, len_p, nb),
        ], (n_p // nb,), "context_mixers")
        xp = _channel_mix(xp, (y_ret, y_hy, y_gqa, y_mla), mod, norm_g, w_out_b, w_up_b, ffn_conv_w, conv_b,
                          w_down_b, l, tm_ffn, len_p, cond_p)
        tables, fmat, fmat_t = groups[len_s]
        ret_in, hy_in, gqa_in, mla_in = _in_proj(xs, mod, norm_g, w_in_t, l, tm_s, cond_s)
        (y_ret,), (y_gqa,), (y_mla,) = _run_parts([
            _retention(ret_in, ret_decay_logit, gn_g, state_ret, None, l, n_s, len_s, 1),
            _gqa_win(gqa_in, sink, cache_k_t, cache_v_t, rope_g[0], rope_g[1], l, n_s, len_s),
            _mla(mla_in, q_norm, kv_norm, w_uq_b, w_uk_b, w_uv_b, (cache_mla_ckv, cache_mla_krope),
                 rope_m, None, l, n_s, len_s, 1),
        ], (n_s,), "latent_mixers")
        (y_hy,), = _run_parts([
            _hyena(hy_in, hy_short_w, short_b, hy_bias, tables, fmat, fmat_t, l, n_s, len_s, 1),
        ], (n_s,), "latent_hyena")
        xs = _channel_mix(xs, (y_ret, y_hy, y_gqa, y_mla), mod, norm_g, w_out_b, w_up_b, ffn_conv_w, conv_b,
                          w_down_b, l, tm_ffn, len_s, cond_s_ffn)

    return (xp.reshape(n_p, len_p, D_MODEL), xs.reshape(n_s, len_s, D_MODEL),
            st[0], jnp.swapaxes(gkv[0], -1, -2), jnp.swapaxes(gkv[1], -1, -2),
            ckr[0], jnp.swapaxes(ckr[1], -1, -2))
```

```python
import functools
import math

import numpy as np
import jax
import jax.numpy as jnp
from jax import lax
from jax.experimental import pallas as pl
from jax.experimental.pallas import tpu as pltpu

F32 = jnp.float32
BF16 = jnp.bfloat16

D_MODEL = 1024
DEPTH = 4
GRID_W = 64
EPS = 1e-6
NEG_INF = -1e30
ROPE_BASE = 10000.0
BLOCK = 128
WINDOW = 128
HEAD_DIM = 64
LANES = 128
RET_HEADS = 4
RET_WIDTH = RET_HEADS * HEAD_DIM
HY_WIDTH = 256
HY_ORDER = 2
HY_BANDS = 8
HY_POS_DIM = 1 + 2 * HY_BANDS
HY_POS_PAD = 32
HY_FILTER_HIDDEN = 64
GQA_Q_HEADS = 4
GQA_KV_HEADS = 2
GQA_GROUPS = GQA_Q_HEADS // GQA_KV_HEADS
MLA_HEADS = 4
MLA_Q_RANK = 256
MLA_KV_RANK = 128
MLA_NOPE = 64
MLA_ROPE = 32
MLA_V = 64
D_FF = 2816
RET_IN = 4 * RET_WIDTH
HY_IN = (HY_ORDER + 1) * HY_WIDTH
GQA_IN = (GQA_Q_HEADS + 2 * GQA_KV_HEADS) * HEAD_DIM
MLA_IN = MLA_Q_RANK + MLA_KV_RANK + MLA_ROPE
MLA_IN_TILED = MLA_Q_RANK + MLA_KV_RANK + MLA_HEADS * MLA_ROPE
IN_WIDTH = RET_IN + HY_IN + GQA_IN + MLA_IN
IN_WIDTH_TILED = RET_IN + HY_IN + GQA_IN + MLA_IN_TILED
GQA_HEAD_ORDER = (0, 2, 1, 3)
MIX_PART = 256
N_COND = 8
FFN_CHUNK = 256
N_FFN_CHUNKS = D_FF // FFN_CHUNK

VMEM_LIMIT = 56 * 1024 * 1024
VMEM_MIB = dict(modulation=16, hyena_tables=32, in_proj=52, context_mixers=44, latent_mixers=40,
                latent_hyena=34, channel_mix=56)


def _params(n_axes, call):
    limit = min(VMEM_MIB[call] * 1024 * 1024, VMEM_LIMIT)
    return pltpu.CompilerParams(dimension_semantics=("arbitrary",) * n_axes, vmem_limit_bytes=limit)


def _dot(a, b):
    return jnp.dot(a, b, preferred_element_type=F32)


def _dot_nt(a, b):
    return lax.dot_general(a, b, (((1,), (1,)), ((), ())), preferred_element_type=F32)


def _dot_tn(a, b):
    return lax.dot_general(a, b, (((0,), (0,)), ((), ())), preferred_element_type=F32)


def _rms(x, g):
    return x * lax.rsqrt(jnp.mean(x * x, axis=-1, keepdims=True) + EPS) * g


def _sigmoid(x):
    return 1.0 / (1.0 + jnp.exp(-x))


def _silu(x):
    return x * _sigmoid(x)


def _shift_rows(x, seq_len):
    n = x.shape[0]
    assert seq_len & (seq_len - 1) == 0
    pos = lax.broadcasted_iota(jnp.int32, (n, 1), 0) & (seq_len - 1)
    prev = jnp.where(pos != 0, pltpu.roll(x, 1, axis=0), 0.0)
    nxt = jnp.where(pos != seq_len - 1, pltpu.roll(x, n - 1, axis=0), 0.0)
    return prev, nxt


def _rot_half(x, half):
    lane = lax.broadcasted_iota(jnp.int32, x.shape, 1) & (2 * half - 1)
    return jnp.where(lane < half, pltpu.roll(x, LANES - half, axis=1), pltpu.roll(x, half, axis=1))


def _lane_iota():
    return lax.broadcasted_iota(jnp.int32, (1, LANES), 1)


def _split_heads(t, lo):
    return jnp.concatenate([jnp.where(lo, t, 0.0), jnp.where(lo, 0.0, t)], axis=0)


def _merge_heads(o, lo):
    n = o.shape[0] // 2
    return jnp.where(lo, o[:n], o[n:])


def _per_head(vals, width):
    lane = lax.broadcasted_iota(jnp.int32, (1, width), 1)
    out = vals[-1]
    for h in range(len(vals) - 2, -1, -1):
        out = jnp.where(lane < (h + 1) * HEAD_DIM, vals[h], out)
    return out


def _mod_kernel(cond_ref, w_ref, b_ref, out_ref):
    s = _silu(cond_ref[...]).astype(BF16)
    out_ref[0] = _dot(s, w_ref[0].astype(BF16)) + b_ref[0]


def _modulation(cond, ada_w, ada_b):
    tn = 1536
    return pl.pallas_call(
        _mod_kernel,
        out_shape=jax.ShapeDtypeStruct((DEPTH, N_COND, 6 * D_MODEL), F32),
        grid=(DEPTH, 6 * D_MODEL // tn),
        in_specs=[
            pl.BlockSpec((N_COND, D_MODEL), lambda l, j: (0, 0)),
            pl.BlockSpec((1, D_MODEL, tn), lambda l, j: (l, 0, j)),
            pl.BlockSpec((1, 1, tn), lambda l, j: (l, 0, j)),
        ],
        out_specs=pl.BlockSpec((1, N_COND, tn), lambda l, j: (l, 0, j)),
        compiler_params=_params(2, "modulation"),
        name="modulation",
    )(cond, ada_w, ada_b.reshape(DEPTH, 1, 6 * D_MODEL))


def _filter_kernel(z_ref, w1_ref, b1_ref, w2_ref, b2_ref, w3_ref, dec_ref, f_ref, out_ref, *, seq_len):
    hi = lax.Precision.HIGHEST
    h = jnp.sin(jnp.dot(z_ref[...], w1_ref[0], precision=hi, preferred_element_type=F32) + b1_ref[0])
    h = jnp.sin(jnp.dot(h, w2_ref[0], precision=hi, preferred_element_type=F32) + b2_ref[0])
    h = jnp.dot(h, w3_ref[0], precision=hi, preferred_element_type=F32)
    row = lax.broadcasted_iota(jnp.int32, (seq_len, 1), 0)
    t = row.astype(F32) / seq_len
    dec = jnp.abs(dec_ref[0])
    win_f = jnp.exp(-t * dec[0:1, :])
    win_b = jnp.exp(-t * dec[1:2, :])
    fm = f_ref[...]
    inv = 1.0 / seq_len
    scale = jnp.where(row == 0, 0.5 * inv, inv)
    for o in range(HY_ORDER):
        base = o * 2 * HY_WIDTH
        hf = h[:, base:base + HY_WIDTH] * win_f
        hb = h[:, base + HY_WIDTH:base + 2 * HY_WIDTH] * win_b
        hbs = jnp.where(row != 0, pltpu.roll(hb, 1, axis=0), 0.0)
        sf = _dot(fm, hf.astype(BF16))
        sb = _dot(fm, hbs.astype(BF16))
        g_re = sf[:seq_len] + sb[:seq_len]
        g_im = sf[seq_len:] - sb[seq_len:]
        g_ny = sf[seq_len:] + sb[seq_len:]
        a = g_re * scale
        out_ref[0, 3 * o] = a
        out_ref[0, 3 * o + 1] = jnp.where(row == 0, g_ny * (0.5 * inv), a)
        out_ref[0, 3 * o + 2] = jnp.where(row == 0, 0.0, g_im * inv)


def _hyena_tables(seq_len, z, w1p, b1, w2, b2, w3, decay, fmat):
    hid = HY_FILTER_HIDDEN
    wide = HY_ORDER * 2 * HY_WIDTH
    return pl.pallas_call(
        functools.partial(_filter_kernel, seq_len=seq_len),
        out_shape=jax.ShapeDtypeStruct((DEPTH, 3 * HY_ORDER, seq_len, HY_WIDTH), F32),
        grid=(DEPTH,),
        in_specs=[
            pl.BlockSpec((seq_len, HY_POS_PAD), lambda l: (0, 0)),
            pl.BlockSpec((1, HY_POS_PAD, hid), lambda l: (l, 0, 0)),
            pl.BlockSpec((1, 1, hid), lambda l: (l, 0, 0)),
            pl.BlockSpec((1, hid, hid), lambda l: (l, 0, 0)),
            pl.BlockSpec((1, 1, hid), lambda l: (l, 0, 0)),
            pl.BlockSpec((1, hid, wide), lambda l: (l, 0, 0)),
            pl.BlockSpec((1, 2, HY_WIDTH), lambda l: (l, 0, 0)),
            pl.BlockSpec((2 * seq_len, seq_len), lambda l: (0, 0)),
        ],
        out_specs=pl.BlockSpec((1, 3 * HY_ORDER, seq_len, HY_WIDTH), lambda l: (l, 0, 0, 0)),
        compiler_params=_params(1, "hyena_tables"),
        name=f"hyena_tables_{seq_len}",
    )(z, w1p, b1, w2, b2, w3, decay, fmat)


def _layer_stacked(tail, n_batch, layer, nb):
    zeros = (0,) * len(tail)
    n_layers = DEPTH if layer == 0 else 1
    return (jax.ShapeDtypeStruct((n_batch, DEPTH) + tail, F32),
            pl.BlockSpec((nb, n_layers) + tail, lambda b: (b, layer) + zeros))


def _zero_other_layers(*out_refs):
    for ref in out_refs:
        if ref.shape[1] > 1:
            ref[:, 1:] = jnp.zeros((ref.shape[0], ref.shape[1] - 1) + ref.shape[2:], ref.dtype)


def _carry_specs(in_specs, args, carried, first_out):
    aliases = {}
    if carried is not None:
        for k, arr in enumerate(carried):
            aliases[len(args)] = first_out + k
            in_specs.append(pl.BlockSpec(memory_space=pl.ANY))
            args.append(arr)
    return aliases


def _part(kernel, in_specs, args, out_shape, out_specs, scratch=(), aliases=None):
    as_list = lambda v: list(v) if isinstance(v, (list, tuple)) else [v]
    return dict(kernel=kernel, in_specs=list(in_specs), args=list(args), out_shape=as_list(out_shape),
                out_specs=as_list(out_specs), scratch=list(scratch), aliases=dict(aliases or {}))


def _run_parts(parts, grid, name):
    n_in = sum(len(p["args"]) for p in parts)
    n_out = sum(len(p["out_shape"]) for p in parts)

    def fused(*refs):
        i = o = s = 0
        for p in parts:
            a, b, c = len(p["args"]), len(p["out_shape"]), len(p["scratch"])
            p["kernel"](*refs[i:i + a], *refs[n_in + o:n_in + o + b],
                        *refs[n_in + n_out + s:n_in + n_out + s + c])
            i, o, s = i + a, o + b, s + c

    aliases = {}
    i = o = 0
    for p in parts:
        aliases.update({i + k: o + v for k, v in p["aliases"].items()})
        i, o = i + len(p["args"]), o + len(p["out_shape"])
    outs = pl.pallas_call(
        fused,
        out_shape=[s for p in parts for s in p["out_shape"]],
        grid=grid,
        in_specs=[s for p in parts for s in p["in_specs"]],
        out_specs=[s for p in parts for s in p["out_specs"]],
        input_output_aliases=aliases,
        scratch_shapes=[s for p in parts for s in p["scratch"]],
        compiler_params=_params(len(grid), name),
        name=name,
    )(*[a for p in parts for a in p["args"]])
    split, o = [], 0
    for p in parts:
        split.append(list(outs[o:o + len(p["out_shape"])]))
        o += len(p["out_shape"])
    return split


def _in_proj_kernel(x_ref, mod_ref, g_ref, w_ref, ret_ref, hy_ref, gqa_ref, mla_ref, w_scr):
    @pl.when(pl.program_id(0) == 0)
    def _():
        g0 = RET_IN + HY_IN
        m0 = g0 + GQA_IN
        kr0 = m0 + MLA_Q_RANK + MLA_KV_RANK
        step = 2 * LANES
        for r0 in list(range(0, g0, step)) + list(range(g0 + GQA_Q_HEADS * HEAD_DIM, kr0, LANES)):
            n = step if r0 < g0 else LANES
            w_scr[r0:r0 + n, :] = w_ref[0, r0:r0 + n, :].astype(BF16)
        for dst, j in enumerate(GQA_HEAD_ORDER):
            w_scr[g0 + dst * HEAD_DIM:g0 + (dst + 1) * HEAD_DIM, :] = (
                w_ref[0, g0 + j * HEAD_DIM:g0 + (j + 1) * HEAD_DIM, :].astype(BF16))
        kr = w_ref[0, kr0:kr0 + MLA_ROPE, :].astype(BF16)
        for hd in range(MLA_HEADS):
            w_scr[kr0 + hd * MLA_ROPE:kr0 + (hd + 1) * MLA_ROPE, :] = kr

    shift = mod_ref[0, 0, 0:1, :]
    scale = mod_ref[0, 0, 1:2, :]
    h = (_rms(x_ref[...], g_ref[0, 0:1, :]) * (1.0 + scale) + shift).astype(BF16)
    c0 = 0
    for ref in (ret_ref, hy_ref, gqa_ref, mla_ref):
        width = ref.shape[1]
        ref[...] = _dot_nt(h, w_scr[c0:c0 + width, :])
        c0 += width


def _in_proj(x, mod, norm_g, w_in, layer, tm, cond_of_tile):
    rows = x.shape[0]
    widths = (RET_IN, HY_IN, GQA_IN, MLA_IN_TILED)
    return pl.pallas_call(
        _in_proj_kernel,
        out_shape=[jax.ShapeDtypeStruct((rows, w), F32) for w in widths],
        grid=(rows // tm,),
        in_specs=[
            pl.BlockSpec((tm, D_MODEL), lambda i: (i, 0)),
            pl.BlockSpec((1, 1, 6, D_MODEL), lambda i: (layer, cond_of_tile(i), 0, 0)),
            pl.BlockSpec((1, 4, D_MODEL), lambda i: (layer, 0, 0)),
            pl.BlockSpec((1, IN_WIDTH, D_MODEL), lambda i: (layer, 0, 0), pipeline_mode=pl.Buffered(1)),
        ],
        out_specs=[pl.BlockSpec((tm, w), lambda i: (i, 0)) for w in widths],
        scratch_shapes=[pltpu.VMEM((IN_WIDTH_TILED, D_MODEL), BF16)],
        compiler_params=_params(1, "in_proj"),
        name="in_proj",
    )(x, mod, norm_g, w_in)


def _channel_kernel(x_ref, m0_ref, m1_ref, m2_ref, m3_ref, mod_ref, g_ref, wo_ref, wu_ref, cw_ref, cb_ref,
                    wd_ref, out_ref, act_ref, *, seq_len):
    y = None
    for i, m_ref in enumerate((m0_ref, m1_ref, m2_ref, m3_ref)):
        r0 = i * MIX_PART
        if i == 2:
            w = jnp.concatenate([wo_ref[0, r0 + j * HEAD_DIM:r0 + (j + 1) * HEAD_DIM, :]
                                 for j in GQA_HEAD_ORDER], axis=0)
        else:
            w = wo_ref[0, r0:r0 + MIX_PART, :]
        part = _dot(m_ref[...], w)
        y = part if y is None else y + part
    gate1 = mod_ref[0, 0, 2:3, :]
    shift2 = mod_ref[0, 0, 3:4, :]
    scale2 = mod_ref[0, 0, 4:5, :]
    xm = x_ref[...] + gate1 * _rms(y, g_ref[0, 1:2, :])
    out_ref[...] = xm
    h2 = (_rms(xm, g_ref[0, 2:3, :]) * (1.0 + scale2) + shift2).astype(BF16)
    for c in range(N_FFN_CHUNKS):
        sl = slice(c * FFN_CHUNK, (c + 1) * FFN_CHUNK)
        gate = _dot(h2, wu_ref[0, :, sl])
        up = _dot(h2, wu_ref[0, :, D_FF + c * FFN_CHUNK:D_FF + (c + 1) * FFN_CHUNK])
        prev, nxt = _shift_rows(gate, seq_len)
        gate = (prev * cw_ref[0, 0:1, sl] + gate * cw_ref[0, 1:2, sl] + nxt * cw_ref[0, 2:3, sl]
                + cb_ref[0, :, sl])
        act_ref[:, sl] = (_silu(gate) * up).astype(BF16)
    ffn = _dot(act_ref[...], wd_ref[0])
    gate2 = mod_ref[0, 0, 5:6, :]
    out_ref[...] = out_ref[...] + gate2 * _rms(ffn, g_ref[0, 3:4, :])


def _channel_mix(x, mixes, mod, norm_g, w_out, w_up, conv_w, conv_b, w_down, layer, tm, seq_len, cond_of_tile):
    rows = x.shape[0]
    resident = dict(pipeline_mode=pl.Buffered(1))
    return pl.pallas_call(
        functools.partial(_channel_kernel, seq_len=seq_len),
        out_shape=jax.ShapeDtypeStruct((rows, D_MODEL), F32),
        grid=(rows // tm,),
        in_specs=[pl.BlockSpec((tm, D_MODEL), lambda i: (i, 0))]
        + [pl.BlockSpec((tm, MIX_PART), lambda i: (i, 0))] * 4
        + [
            pl.BlockSpec((1, 1, 6, D_MODEL), lambda i: (layer, cond_of_tile(i), 0, 0)),
            pl.BlockSpec((1, 4, D_MODEL), lambda i: (layer, 0, 0)),
            pl.BlockSpec((1, D_MODEL, D_MODEL), lambda i: (layer, 0, 0), **resident),
            pl.BlockSpec((1, D_MODEL, 2 * D_FF), lambda i: (layer, 0, 0), **resident),
            pl.BlockSpec((1, 3, D_FF), lambda i: (layer, 0, 0)),
            pl.BlockSpec((1, 1, D_FF), lambda i: (layer, 0, 0)),
            pl.BlockSpec((1, D_FF, D_MODEL), lambda i: (layer, 0, 0), **resident),
        ],
        out_specs=pl.BlockSpec((tm, D_MODEL), lambda i: (i, 0)),
        scratch_shapes=[pltpu.VMEM((tm, D_FF), BF16)],
        compiler_params=_params(1, "channel_mix"),
        name="channel_mix",
    )(x, *mixes, mod, norm_g, w_out, w_up, conv_w, conv_b, w_down)


def _log_gamma(dl_ref):
    dl = dl_ref[0]
    return jnp.minimum(dl, 0.0) - jnp.log(1.0 + jnp.exp(-jnp.abs(dl)))


def _pair_decay(lg, t, r0, bq, seq_len):
    top = lax.broadcasted_iota(jnp.int32, (2 * bq, 1), 0) < bq
    lgf = jnp.where(top, lg[0:1, 2 * t:2 * t + 1], lg[0:1, 2 * t + 1:2 * t + 2])
    lgb = jnp.where(top, lg[1:2, 2 * t:2 * t + 1], lg[1:2, 2 * t + 1:2 * t + 2])
    rowf = ((lax.broadcasted_iota(jnp.int32, (2 * bq, 1), 0) & (bq - 1)) + r0).astype(F32)
    colf = lax.broadcasted_iota(jnp.int32, (1, seq_len), 1).astype(F32)
    lag = rowf - colf
    decay = jnp.exp(jnp.where(lag >= 0.0, lag * lgf, -lag * lgb))
    return jnp.where(lag == 0.0, 2.0, decay)


def _head_norm(o, lo):
    inv = 1.0 / HEAD_DIM
    s_all = jnp.sum(o, axis=-1, keepdims=True)
    s_lo = jnp.sum(jnp.where(lo, o, 0.0), axis=-1, keepdims=True)
    d = o - jnp.where(lo, s_lo, s_all - s_lo) * inv
    d2 = d * d
    v_all = jnp.sum(d2, axis=-1, keepdims=True)
    v_lo = jnp.sum(jnp.where(lo, d2, 0.0), axis=-1, keepdims=True)
    return d * lax.rsqrt(jnp.where(lo, v_lo, v_all - v_lo) * inv + EPS)


def _ret_ctx_kernel(*refs, seq_len, nb):
    x_ref, dl_ref, gn_ref, *_, y_ref, st_ref, dec_scr, kdec_scr = refs
    lo = _lane_iota() < HEAD_DIM
    _zero_other_layers(st_ref)

    @pl.when(pl.program_id(0) == 0)
    def _():
        lg = _log_gamma(dl_ref)
        posf = lax.broadcasted_iota(jnp.int32, (seq_len, 1), 0).astype(F32)
        lgf = _per_head([lg[0:1, h:h + 1] for h in range(RET_HEADS)], RET_WIDTH)
        lgb = _per_head([lg[1:2, h:h + 1] for h in range(RET_HEADS)], RET_WIDTH)
        kdec_scr[0] = jnp.exp((seq_len - 1.0 - posf) * lgf)
        kdec_scr[1] = jnp.exp(posf * lgb)
        for t in range(RET_HEADS // 2):
            dec_scr[t] = _pair_decay(lg, t, 0, seq_len, seq_len)

    for i in range(nb):
        rows = slice(i * seq_len, (i + 1) * seq_len)
        for t in range(RET_HEADS // 2):
            cs = slice(t * LANES, (t + 1) * LANES)
            q = x_ref[rows, cs]
            k = x_ref[rows, RET_WIDTH + t * LANES:RET_WIDTH + (t + 1) * LANES] * (HEAD_DIM ** -0.5)
            vb = x_ref[rows, 2 * RET_WIDTH + t * LANES:2 * RET_WIDTH + (t + 1) * LANES].astype(BF16)
            gate = x_ref[rows, 3 * RET_WIDTH + t * LANES:3 * RET_WIDTH + (t + 1) * LANES]
            s = _dot_nt(_split_heads(q, lo).astype(BF16), k.astype(BF16)) * dec_scr[t]
            o = _merge_heads(_dot(s.astype(BF16), vb), lo)
            y = _silu(gate) * (_head_norm(o, lo) * gn_ref[0, :, cs])
            y_ref[rows, cs] = y.astype(BF16)
            for d in range(2):
                st = _dot_tn((k * kdec_scr[d, :, cs]).astype(BF16), vb)
                st_ref[i, 0, d, 2 * t] = st[0:HEAD_DIM, 0:HEAD_DIM]
                st_ref[i, 0, d, 2 * t + 1] = st[HEAD_DIM:LANES, HEAD_DIM:LANES]


def _ret_lat_kernel(x_ref, dl_ref, gn_ref, s0_ref, y_ref, *, seq_len, bq):
    lo = _lane_iota() < HEAD_DIM
    lg = _log_gamma(dl_ref)
    zero = jnp.zeros((HEAD_DIM, HEAD_DIM), F32)
    for t in range(RET_HEADS // 2):
        cs = slice(t * LANES, (t + 1) * LANES)
        kb = (x_ref[:, RET_WIDTH + t * LANES:RET_WIDTH + (t + 1) * LANES] * (HEAD_DIM ** -0.5)).astype(BF16)
        vb = x_ref[:, 2 * RET_WIDTH + t * LANES:2 * RET_WIDTH + (t + 1) * LANES].astype(BF16)
        lgf = _per_head([lg[0:1, 2 * t:2 * t + 1], lg[0:1, 2 * t + 1:2 * t + 2]], LANES)
        lgb = _per_head([lg[1:2, 2 * t:2 * t + 1], lg[1:2, 2 * t + 1:2 * t + 2]], LANES)
        s0 = []
        for d in range(2):
            a = s0_ref[0, 0, d, 2 * t]
            b = s0_ref[0, 0, d, 2 * t + 1]
            s0.append(jnp.concatenate([jnp.concatenate([a, zero], axis=1),
                                       jnp.concatenate([zero, b], axis=1)], axis=0).astype(BF16))
        for r0 in range(0, seq_len, bq):
            q = x_ref[r0:r0 + bq, cs]
            s = _dot_nt(_split_heads(q, lo).astype(BF16), kb) * _pair_decay(lg, t, r0, bq, seq_len)
            o = _merge_heads(_dot(s.astype(BF16), vb), lo)
            rowf = (lax.broadcasted_iota(jnp.int32, (bq, 1), 0) + r0).astype(F32)
            qf = (q * jnp.exp((rowf + 1.0) * lgf)).astype(BF16)
            qr = (q * jnp.exp((seq_len - rowf) * lgb)).astype(BF16)
            o = o + _dot(qf, s0[0]) + _dot(qr, s0[1])
            gate = x_ref[r0:r0 + bq, 3 * RET_WIDTH + t * LANES:3 * RET_WIDTH + (t + 1) * LANES]
            y = _silu(gate) * (_head_norm(o, lo) * gn_ref[0, :, cs])
            y_ref[r0:r0 + bq, cs] = y.astype(BF16)


def _retention(ret_in, decay_logit, gn_g, state0, carried, layer, n_batch, seq_len, nb):
    latent = state0 is not None
    in_specs = [
        pl.BlockSpec((nb * seq_len, RET_IN), lambda b: (b, 0)),
        pl.BlockSpec((1, 2, RET_HEADS), lambda b: (layer, 0, 0)),
        pl.BlockSpec((1, 1, RET_WIDTH), lambda b: (layer, 0, 0)),
    ]
    args = [ret_in, decay_logit, gn_g]
    y_shape = jax.ShapeDtypeStruct((n_batch * seq_len, MIX_PART), BF16)
    y_spec = pl.BlockSpec((nb * seq_len, MIX_PART), lambda b: (b, 0))
    if latent:
        assert nb == 1
        kern = functools.partial(_ret_lat_kernel, seq_len=seq_len, bq=256)
        in_specs.append(pl.BlockSpec((1, 1, 2, RET_HEADS, HEAD_DIM, HEAD_DIM),
                                     lambda b: (b, layer, 0, 0, 0, 0)))
        args.append(state0)
        out_shape, out_specs, aliases, scratch = y_shape, y_spec, {}, []
    else:
        kern = functools.partial(_ret_ctx_kernel, seq_len=seq_len, nb=nb)
        st_shape, st_spec = _layer_stacked((2, RET_HEADS, HEAD_DIM, HEAD_DIM), n_batch, layer, nb)
        out_shape = [y_shape, st_shape]
        out_specs = [y_spec, st_spec]
        aliases = _carry_specs(in_specs, args, carried, 1)
        scratch = [pltpu.VMEM((RET_HEADS // 2, 2 * seq_len, seq_len), F32),
                   pltpu.VMEM((2, seq_len, RET_WIDTH), F32)]
    return _part(kern, in_specs, args, out_shape, out_specs, scratch, aliases)


def _hyena_kernel(x_ref, sw_ref, sb_ref, bias_ref, tab_ref, f_ref, ft_ref, y_ref, *, seq_len, nb):
    for i in range(nb):
        rows = slice(i * seq_len, (i + 1) * seq_len)
        x = x_ref[rows, :]
        prev, nxt = _shift_rows(x, seq_len)
        u = prev * sw_ref[0, 0:1, :] + x * sw_ref[0, 1:2, :] + nxt * sw_ref[0, 2:3, :] + sb_ref[0]
        x1 = u[:, 0:HY_WIDTH]
        x2 = u[:, HY_WIDTH:2 * HY_WIDTH]
        z = u[:, 2 * HY_WIDTH:3 * HY_WIDTH]
        for o, gate in enumerate((x1, x2)):
            spec = _dot(f_ref[...], z.astype(BF16))
            s_re = spec[:seq_len]
            s_im = spec[seq_len:]
            a = tab_ref[0, 3 * o]
            a_ny = tab_ref[0, 3 * o + 1]
            b = tab_ref[0, 3 * o + 2]
            y_re = (s_re * a - s_im * b).astype(BF16)
            y_im = (s_re * b + s_im * a_ny).astype(BF16)
            conv = _dot(ft_ref[:, 0:seq_len], y_re) + _dot(ft_ref[:, seq_len:2 * seq_len], y_im)
            z = gate * (conv + z * bias_ref[0, o:o + 1, :])
        y_ref[rows, :] = z.astype(BF16)


def _hyena(hy_in, short_w, short_b, hy_bias, tables, fmat, fmat_t, layer, n_batch, seq_len, nb):
    const = dict(pipeline_mode=pl.Buffered(1))
    return _part(
        functools.partial(_hyena_kernel, seq_len=seq_len, nb=nb),
        [
            pl.BlockSpec((nb * seq_len, HY_IN), lambda b: (b, 0)),
            pl.BlockSpec((1, 3, HY_IN), lambda b: (layer, 0, 0)),
            pl.BlockSpec((1, 1, HY_IN), lambda b: (layer, 0, 0)),
            pl.BlockSpec((1, HY_ORDER, HY_WIDTH), lambda b: (layer, 0, 0)),
            pl.BlockSpec((1, 3 * HY_ORDER, seq_len, HY_WIDTH), lambda b: (layer, 0, 0, 0), **const),
            pl.BlockSpec((2 * seq_len, seq_len), lambda b: (0, 0), **const),
            pl.BlockSpec((seq_len, 2 * seq_len), lambda b: (0, 0), **const),
        ],
        [hy_in, short_w, short_b, hy_bias, tables, fmat, fmat_t],
        jax.ShapeDtypeStruct((n_batch * seq_len, MIX_PART), BF16),
        pl.BlockSpec((nb * seq_len, MIX_PART), lambda b: (b, 0)))


def _gqa_ctx_kernel(*refs, seq_len, nb):
    x_ref, sink_ref, *_, y_ref, k_out_ref, v_out_ref = refs
    nq = GQA_Q_HEADS * HEAD_DIM
    scale = HEAD_DIM ** -0.5
    head_a = lax.broadcasted_iota(jnp.int32, (LANES, 1), 0) < HEAD_DIM
    first = lax.broadcasted_iota(jnp.int32, (1, 2 * seq_len), 1) < seq_len
    ones = jnp.ones((LANES, seq_len), BF16)
    _zero_other_layers(k_out_ref, v_out_ref)
    for i in range(nb):
        rows = slice(i * seq_len, (i + 1) * seq_len)
        k = x_ref[rows, nq:nq + LANES]
        k_t = k.T
        v_t = x_ref[rows, nq + LANES:nq + 2 * LANES].T
        for kv in range(GQA_KV_HEADS):
            k_out_ref[i, 0, kv] = k_t[kv * HEAD_DIM:(kv + 1) * HEAD_DIM, :]
            v_out_ref[i, 0, kv] = v_t[kv * HEAD_DIM:(kv + 1) * HEAD_DIM, :]
        kb = k.astype(BF16)
        v_ext = jnp.concatenate([v_t.astype(BF16), ones], axis=0)
        for g in range(GQA_GROUPS):
            cs = slice(g * LANES, (g + 1) * LANES)
            q_t = x_ref[rows, cs].T
            q2 = jnp.concatenate([jnp.where(head_a, q_t, 0.0), jnp.where(head_a, 0.0, q_t)], axis=1)
            s = _dot(kb, q2.astype(BF16)) * scale
            sink = jnp.where(first, sink_ref[0, :, g:g + 1], sink_ref[0, :, GQA_GROUPS + g:GQA_GROUPS + g + 1])
            m = jnp.maximum(jnp.max(s, axis=0, keepdims=True), sink)
            pv = _dot(v_ext, jnp.exp(s - m).astype(BF16))
            o = pv[0:LANES, :] / (pv[LANES:LANES + 1, :] + jnp.exp(sink - m))
            o = jnp.where(head_a, o[:, 0:seq_len], o[:, seq_len:2 * seq_len])
            y_ref[rows, cs] = o.T.astype(BF16)


def _gqa_ctx(gqa_in, sink, carried, layer, n_batch, seq_len, nb):
    kv_shape, kv_spec = _layer_stacked((GQA_KV_HEADS, HEAD_DIM, seq_len), n_batch, layer, nb)
    in_specs = [
        pl.BlockSpec((nb * seq_len, GQA_IN), lambda b: (b, 0)),
        pl.BlockSpec((1, 1, GQA_Q_HEADS), lambda b: (layer, 0, 0)),
    ]
    args = [gqa_in, sink]
    aliases = _carry_specs(in_specs, args, carried, 1)
    return _part(
        functools.partial(_gqa_ctx_kernel, seq_len=seq_len, nb=nb), in_specs, args,
        [jax.ShapeDtypeStruct((n_batch * seq_len, MIX_PART), BF16), kv_shape, kv_shape],
        [pl.BlockSpec((nb * seq_len, MIX_PART), lambda b: (b, 0)), kv_spec, kv_spec],
        aliases=aliases)


def _gqa_win_kernel(x_ref, sink_ref, kct_ref, vct_ref, cos_ref, sin_ref, y_ref,
                    q_scr, k_scr, vt_scr, kc_scr, vct_scr, *, seq_len):
    nq = GQA_Q_HEADS * HEAD_DIM
    scale = HEAD_DIM ** -0.5
    head_a = lax.broadcasted_iota(jnp.int32, (LANES, 1), 0) < HEAD_DIM
    first = lax.broadcasted_iota(jnp.int32, (1, 2 * BLOCK), 1) < BLOCK
    cos = cos_ref[...]
    sin = sin_ref[...]

    def rope(t):
        return t * cos + _rot_half(t, HEAD_DIM // 2) * sin

    for g in range(GQA_GROUPS):
        q_t = rope(x_ref[:, g * LANES:(g + 1) * LANES]).T
        q_scr[g, 0] = jnp.where(head_a, q_t, 0.0).astype(BF16)
        q_scr[g, 1] = jnp.where(head_a, 0.0, q_t).astype(BF16)
    zeros = jnp.zeros((BLOCK, LANES), BF16)
    for r0 in (0, BLOCK + seq_len):
        k_scr[r0:r0 + BLOCK, :] = zeros
        vt_scr[:, r0:r0 + BLOCK] = zeros
    k_scr[BLOCK:BLOCK + seq_len, :] = rope(x_ref[:, nq:nq + LANES]).astype(BF16)
    vt_scr[:, BLOCK:BLOCK + seq_len] = x_ref[:, nq + LANES:nq + 2 * LANES].T.astype(BF16)
    kc_scr[...] = jnp.concatenate([kct_ref[0, 0, 0], kct_ref[0, 0, 1]], axis=0).T.astype(BF16)
    vct_scr[...] = jnp.concatenate([vct_ref[0, 0, 0], vct_ref[0, 0, 1]], axis=0).astype(BF16)

    jj = lax.broadcasted_iota(jnp.int32, (3 * BLOCK, 2 * BLOCK), 0)
    ii = lax.broadcasted_iota(jnp.int32, (3 * BLOCK, 2 * BLOCK), 1) & (BLOCK - 1)
    band = (jj >= ii) & (jj <= ii + 2 * WINDOW)
    for n in range(seq_len // BLOCK):
        r0 = n * BLOCK
        kpos = jj + (n - 1) * BLOCK
        valid = band & (kpos >= 0) & (kpos < seq_len)
        kw = k_scr[r0:r0 + 3 * BLOCK, :]
        vw = vt_scr[:, r0:r0 + 3 * BLOCK]
        for g in range(GQA_GROUPS):
            q2 = jnp.concatenate([q_scr[g, 0, :, r0:r0 + BLOCK], q_scr[g, 1, :, r0:r0 + BLOCK]], axis=1)
            s_win = jnp.where(valid, _dot(kw, q2) * scale, NEG_INF)
            s_ctx = _dot(kc_scr[...], q2) * scale
            sink = jnp.where(first, sink_ref[0, :, g:g + 1], sink_ref[0, :, GQA_GROUPS + g:GQA_GROUPS + g + 1])
            m = jnp.maximum(jnp.maximum(jnp.max(s_win, axis=0, keepdims=True),
                                        jnp.max(s_ctx, axis=0, keepdims=True)), sink)
            p_win = jnp.exp(s_win - m)
            p_ctx = jnp.exp(s_ctx - m)
            den = (jnp.sum(p_win, axis=0, keepdims=True) + jnp.sum(p_ctx, axis=0, keepdims=True)
                   + jnp.exp(sink - m))
            o = (_dot(vw, p_win.astype(BF16)) + _dot(vct_scr[...], p_ctx.astype(BF16))) / den
            o = jnp.where(head_a, o[:, 0:BLOCK], o[:, BLOCK:2 * BLOCK])
            y_ref[r0:r0 + BLOCK, g * LANES:(g + 1) * LANES] = o.T.astype(BF16)


def _gqa_win(gqa_in, sink, cache_k_t, cache_v_t, cos, sin, layer, n_batch, seq_len):
    past = cache_k_t.shape[4]
    return _part(
        functools.partial(_gqa_win_kernel, seq_len=seq_len),
        [
            pl.BlockSpec((seq_len, GQA_IN), lambda b: (b, 0)),
            pl.BlockSpec((1, 1, GQA_Q_HEADS), lambda b: (layer, 0, 0)),
            pl.BlockSpec((1, 1, GQA_KV_HEADS, HEAD_DIM, past), lambda b: (b, layer, 0, 0, 0)),
            pl.BlockSpec((1, 1, GQA_KV_HEADS, HEAD_DIM, past), lambda b: (b, layer, 0, 0, 0)),
            pl.BlockSpec((seq_len, LANES), lambda b: (0, 0)),
            pl.BlockSpec((seq_len, LANES), lambda b: (0, 0)),
        ],
        [gqa_in, sink, cache_k_t, cache_v_t, cos, sin],
        jax.ShapeDtypeStruct((n_batch * seq_len, MIX_PART), BF16),
        pl.BlockSpec((seq_len, MIX_PART), lambda b: (b, 0)),
        scratch=[
            pltpu.VMEM((GQA_GROUPS, 2, LANES, seq_len), BF16),
            pltpu.VMEM((seq_len + 2 * BLOCK, LANES), BF16),
            pltpu.VMEM((LANES, seq_len + 2 * BLOCK), BF16),
            pltpu.VMEM((past, LANES), BF16),
            pltpu.VMEM((LANES, past), BF16),
        ])


def _mla_kernel(*refs, seq_len, latent, bq, nb):
    if latent:
        (x_ref, qn_ref, kvn_ref, wq_ref, wk_ref, wv_ref, ckv_c_ref, kr_c_ref, cos_ref, sin_ref,
         y_ref, q_all, k_all, v_all) = refs
        past = ckv_c_ref.shape[2]
    else:
        (x_ref, qn_ref, kvn_ref, wq_ref, wk_ref, wv_ref, *_,
         y_ref, ckv_out_ref, kr_out_ref, q_all, k_all, v_all) = refs
        past = 0
        _zero_other_layers(ckv_out_ref, kr_out_ref)
    keys = past + seq_len
    nope_w = MLA_HEADS * MLA_NOPE
    scale = (MLA_NOPE + MLA_ROPE) ** -0.5
    lane = _lane_iota()
    lo = lane < HEAD_DIM
    rope_lanes = [(lane >= h * MLA_ROPE) & (lane < (h + 1) * MLA_ROPE) for h in range(MLA_HEADS)]
    for i in range(nb):
        rows = slice(i * seq_len, (i + 1) * seq_len)
        q_scr, k_scr, v_scr = q_all.at[i], k_all.at[i], v_all.at[i]
        q_lat = x_ref[rows, 0:MLA_Q_RANK]
        kv_lat = x_ref[rows, MLA_Q_RANK:MLA_Q_RANK + MLA_KV_RANK]
        k_rope = x_ref[rows, MLA_Q_RANK + MLA_KV_RANK:MLA_IN_TILED]
        mq = _dot(_rms(q_lat, qn_ref[0]).astype(BF16), wq_ref[0])
        ckv = _rms(kv_lat, kvn_ref[0])
        q_rope = mq[:, nope_w:nope_w + LANES]
        if latent:
            cos = cos_ref[...]
            sin = sin_ref[...]
            q_rope = q_rope * cos + _rot_half(q_rope, MLA_ROPE // 2) * sin
            k_rope = k_rope * cos + _rot_half(k_rope, MLA_ROPE // 2) * sin
            ckv_c = ckv_c_ref[0, 0].astype(BF16)
            k_scr[0:past, 0:nope_w] = _dot(ckv_c, wk_ref[0]).astype(BF16)
            v_scr[0:past, :] = _dot(ckv_c, wv_ref[0]).astype(BF16)
            kr_c = kr_c_ref[0, 0]
            k_scr[0:past, nope_w:nope_w + LANES] = jnp.concatenate([kr_c] * MLA_HEADS, axis=1).astype(BF16)
        else:
            ckv_out_ref[i, 0] = ckv
            kr_out_ref[i, 0] = k_rope.T[0:MLA_ROPE, :]
        q_scr[:, 0:nope_w] = mq[:, 0:nope_w]
        q_scr[:, nope_w:nope_w + LANES] = q_rope
        ckv_b = ckv.astype(BF16)
        k_scr[past:keys, 0:nope_w] = _dot(ckv_b, wk_ref[0]).astype(BF16)
        v_scr[past:keys, :] = _dot(ckv_b, wv_ref[0]).astype(BF16)
        k_scr[past:keys, nope_w:nope_w + LANES] = k_rope.astype(BF16)

        ones = jnp.ones((keys, LANES), BF16)
        kr_all = k_scr[:, nope_w:nope_w + LANES]
        for t in range(MLA_HEADS // 2):
            cs = slice(t * LANES, (t + 1) * LANES)
            k_cat = jnp.concatenate([k_scr[:, cs], kr_all], axis=1)
            v_ext = jnp.concatenate([v_scr[:, cs], ones], axis=1)
            for r0 in range(0, seq_len, bq):
                qn = q_scr[r0:r0 + bq, cs]
                qr = q_scr[r0:r0 + bq, nope_w:nope_w + LANES]
                q2 = jnp.concatenate([
                    jnp.concatenate([jnp.where(lo, qn, 0.0), jnp.where(rope_lanes[2 * t], qr, 0.0)], axis=1),
                    jnp.concatenate([jnp.where(lo, 0.0, qn), jnp.where(rope_lanes[2 * t + 1], qr, 0.0)], axis=1),
                ], axis=0).astype(BF16)
                s = _dot_nt(q2, k_cat) * scale
                m = jnp.max(s, axis=-1, keepdims=True)
                pv = _dot(jnp.exp(s - m).astype(BF16), v_ext)
                o = pv[:, 0:LANES] / pv[:, LANES:LANES + 1]
                y_ref[i * seq_len + r0:i * seq_len + r0 + bq, cs] = _merge_heads(o, lo).astype(BF16)


def _mla(mla_in, q_norm, kv_norm, w_uq, w_uk, w_uv, cache, rope, carried, layer, n_batch, seq_len, nb):
    latent = cache is not None
    aliases = {}
    past = cache[0].shape[2] if latent else 0
    kern = functools.partial(_mla_kernel, seq_len=seq_len, latent=latent, bq=256, nb=nb)
    q_w = MLA_HEADS * (MLA_NOPE + MLA_ROPE)
    kv_w = MLA_HEADS * MLA_NOPE
    in_specs = [
        pl.BlockSpec((nb * seq_len, MLA_IN_TILED), lambda b: (b, 0)),
        pl.BlockSpec((1, 1, MLA_Q_RANK), lambda b: (layer, 0, 0)),
        pl.BlockSpec((1, 1, MLA_KV_RANK), lambda b: (layer, 0, 0)),
        pl.BlockSpec((1, MLA_Q_RANK, q_w), lambda b: (layer, 0, 0)),
        pl.BlockSpec((1, MLA_KV_RANK, kv_w), lambda b: (layer, 0, 0)),
        pl.BlockSpec((1, MLA_KV_RANK, kv_w), lambda b: (layer, 0, 0)),
    ]
    args = [mla_in, q_norm, kv_norm, w_uq, w_uk, w_uv]
    y_shape = jax.ShapeDtypeStruct((n_batch * seq_len, MIX_PART), BF16)
    y_spec = pl.BlockSpec((nb * seq_len, MIX_PART), lambda b: (b, 0))
    if latent:
        assert nb == 1
        in_specs += [
            pl.BlockSpec((1, 1, past, MLA_KV_RANK), lambda b: (b, layer, 0, 0)),
            pl.BlockSpec((1, 1, past, MLA_ROPE), lambda b: (b, layer, 0, 0)),
            pl.BlockSpec((seq_len, LANES), lambda b: (0, 0)),
            pl.BlockSpec((seq_len, LANES), lambda b: (0, 0)),
        ]
        args += [cache[0], cache[1], rope[0], rope[1]]
        out_shape, out_specs = y_shape, y_spec
    else:
        ckv_shape, ckv_spec = _layer_stacked((seq_len, MLA_KV_RANK), n_batch, layer, nb)
        kr_shape, kr_spec = _layer_stacked((MLA_ROPE, seq_len), n_batch, layer, nb)
        out_shape = [y_shape, ckv_shape, kr_shape]
        out_specs = [y_spec, ckv_spec, kr_spec]
        aliases = _carry_specs(in_specs, args, carried, 1)
    keys = past + seq_len
    return _part(
        kern, in_specs, args, out_shape, out_specs,
        scratch=[
            pltpu.VMEM((nb, seq_len, q_w), F32),
            pltpu.VMEM((nb, keys, kv_w + LANES), BF16),
            pltpu.VMEM((nb, keys, kv_w), BF16),
        ],
        aliases=aliases)


def _dft_matrix(seq_len):
    k = np.arange(seq_len, dtype=np.float64)[:, None]
    j = np.arange(seq_len, dtype=np.float64)[None, :]
    ang = np.pi * k * j / seq_len
    re = np.cos(ang)
    im = -np.sin(ang)
    im[0, :] = np.where(np.arange(seq_len) % 2 == 0, 1.0, -1.0)
    return np.concatenate([re, im], axis=0).astype(np.float32)


def _hyena_positions(seq_len):
    t = np.arange(seq_len, dtype=np.float64) / seq_len
    bands = np.arange(1, HY_BANDS + 1, dtype=np.float64)
    ang = 2.0 * math.pi * t[:, None] * bands
    z = np.concatenate([t[:, None], np.cos(ang), np.sin(ang)], axis=-1)
    return np.pad(z, ((0, 0), (0, HY_POS_PAD - HY_POS_DIM))).astype(np.float32)


def _rope_tables(n_tokens, dim):
    rows = np.repeat(np.arange(n_tokens // GRID_W, dtype=np.float64), GRID_W)
    cols = np.tile(np.arange(GRID_W, dtype=np.float64), n_tokens // GRID_W)
    quarter = dim // 4
    inv = ROPE_BASE ** (-np.arange(quarter, dtype=np.float64) / quarter)
    ang = np.concatenate([rows[:, None] * inv, cols[:, None] * inv], axis=-1)
    cos = np.concatenate([np.cos(ang), np.cos(ang)], axis=-1)
    sin = np.concatenate([-np.sin(ang), np.sin(ang)], axis=-1)
    reps = LANES // dim
    return (np.tile(cos, (1, reps)).astype(np.float32), np.tile(sin, (1, reps)).astype(np.float32))


def kernel(x_prompt, x_sample, state_ret, cache_gqa_k, cache_gqa_v, cache_mla_ckv, cache_mla_krope, c, c_ctx, ada_w, ada_b, norm_g, w_in, ret_decay_logit, ret_gn_g, hy_short_w, hy_short_b, hy_w1, hy_b1, hy_w2, hy_b2, hy_w3, hy_decay, hy_bias, gqa_sink, mla_q_norm, mla_kv_norm, mla_w_uq, mla_w_uk, mla_w_uv, w_out, ffn_w_up, ffn_conv_w, ffn_conv_b, ffn_w_down):
    n_p, len_p, _ = x_prompt.shape
    n_s, len_s, _ = x_sample.shape

    w_in_t = jnp.swapaxes(w_in, 1, 2)
    cache_k_t = jnp.swapaxes(cache_gqa_k, -1, -2)
    cache_v_t = jnp.swapaxes(cache_gqa_v, -1, -2)
    w_out_b = w_out.astype(BF16)
    w_up_b = ffn_w_up.astype(BF16)
    w_down_b = ffn_w_down.astype(BF16)
    uq = mla_w_uq.reshape(DEPTH, MLA_Q_RANK, MLA_HEADS, MLA_NOPE + MLA_ROPE)
    w_uq_b = jnp.concatenate(
        [uq[..., :MLA_NOPE].reshape(DEPTH, MLA_Q_RANK, MLA_HEADS * MLA_NOPE),
         uq[..., MLA_NOPE:].reshape(DEPTH, MLA_Q_RANK, MLA_HEADS * MLA_ROPE)], axis=-1).astype(BF16)
    w_uk_b = mla_w_uk.astype(BF16)
    w_uv_b = mla_w_uv.astype(BF16)

    cond = jnp.concatenate([c_ctx[None], c, jnp.zeros((N_COND - 1 - n_s, D_MODEL), F32)], axis=0)
    mod = _modulation(cond, ada_w, ada_b).reshape(DEPTH, N_COND, 6, D_MODEL)

    w1p = jnp.pad(hy_w1, ((0, 0), (0, HY_POS_PAD - HY_POS_DIM), (0, 0)))
    b1 = hy_b1.reshape(DEPTH, 1, HY_FILTER_HIDDEN)
    b2 = hy_b2.reshape(DEPTH, 1, HY_FILTER_HIDDEN)
    groups = {}
    for seq_len in (len_p, len_s):
        f32mat = _dft_matrix(seq_len)
        fmat = jnp.asarray(f32mat).astype(BF16)
        fmat_t = jnp.asarray(np.ascontiguousarray(f32mat.T)).astype(BF16)
        tables = _hyena_tables(seq_len, jnp.asarray(_hyena_positions(seq_len)), w1p, b1, hy_w2, b2,
                               hy_w3, hy_decay, fmat)
        groups[seq_len] = (tables, fmat, fmat_t)

    rope_g = tuple(jnp.asarray(t) for t in _rope_tables(len_s, HEAD_DIM))
    rope_m = tuple(jnp.asarray(t) for t in _rope_tables(len_s, MLA_ROPE))

    gn_g = ret_gn_g.reshape(DEPTH, 1, RET_WIDTH)
    short_b = hy_short_b.reshape(DEPTH, 1, HY_IN)
    sink = gqa_sink.reshape(DEPTH, 1, GQA_Q_HEADS)
    q_norm = mla_q_norm.reshape(DEPTH, 1, MLA_Q_RANK)
    kv_norm = mla_kv_norm.reshape(DEPTH, 1, MLA_KV_RANK)
    conv_b = ffn_conv_b.reshape(DEPTH, 1, D_FF)

    xp = x_prompt.reshape(n_p * len_p, D_MODEL)
    xs = x_sample.reshape(n_s * len_s, D_MODEL)
    tm_p = 1024
    tm_s = 512
    nb_p = 4
    nb_first = 2
    cond_p = lambda i: 0
    cond_s = lambda i: 1 + (i * tm_s) // len_s
    tm_ffn = 1024
    cond_s_ffn = lambda i: 1 + (i * tm_ffn) // len_s

    st = gkv = ckr = None
    for l in range(DEPTH):
        tables, fmat, fmat_t = groups[len_p]
        ret_in, hy_in, gqa_in, mla_in = _in_proj(xp, mod, norm_g, w_in_t, l, tm_p, cond_p)
        nb = nb_first if l == 0 else nb_p
        (y_ret, *st), (y_hy,), (y_gqa, *gkv), (y_mla, *ckr) = _run_parts([
            _retention(ret_in, ret_decay_logit, gn_g, None, st, l, n_p, len_p, nb),
            _hyena(hy_in, hy_short_w, short_b, hy_bias, tables, fmat, fmat_t, l, n_p, len_p, nb),
            _gqa_ctx(gqa_in, sink, gkv, l, n_p, len_p, nb),
            _mla(mla_in, q_norm, kv_norm, w_uq_b, w_uk_b, w_uv_b, None, None, ckr, l, n_p, len_p, nb),
        ], (n_p // nb,), "context_mixers")
        xp = _channel_mix(xp, (y_ret, y_hy, y_gqa, y_mla), mod, norm_g, w_out_b, w_up_b, ffn_conv_w, conv_b,
                          w_down_b, l, tm_ffn, len_p, cond_p)
        tables, fmat, fmat_t = groups[len_s]
        ret_in, hy_in, gqa_in, mla_in = _in_proj(xs, mod, norm_g, w_in_t, l, tm_s, cond_s)
        (y_ret,), (y_gqa,), (y_mla,) = _run_parts([
            _retention(ret_in, ret_decay_logit, gn_g, state_ret, None, l, n_s, len_s, 1),
            _gqa_win(gqa_in, sink, cache_k_t, cache_v_t, rope_g[0], rope_g[1], l, n_s, len_s),
            _mla(mla_in, q_norm, kv_norm, w_uq_b, w_uk_b, w_uv_b, (cache_mla_ckv, cache_mla_krope),
                 rope_m, None, l, n_s, len_s, 1),
        ], (n_s,), "latent_mixers")
        (y_hy,), = _run_parts([
            _hyena(hy_in, hy_short_w, short_b, hy_bias, tables, fmat, fmat_t, l, n_s, len_s, 1),
        ], (n_s,), "latent_hyena")
        xs = _channel_mix(xs, (y_ret, y_hy, y_gqa, y_mla), mod, norm_g, w_out_b, w_up_b, ffn_conv_w, conv_b,
                          w_down_b, l, tm_ffn, len_s, cond_s_ffn)

    return (xp.reshape(n_p, len_p, D_MODEL), xs.reshape(n_s, len_s, D_MODEL),
            st[0], jnp.swapaxes(gkv[0], -1, -2), jnp.swapaxes(gkv[1], -1, -2),
            ckr[0], jnp.swapaxes(ckr[1], -1, -2))
```

```python
import functools
import math

import numpy as np
import jax
import jax.numpy as jnp
from jax import lax
from jax.experimental import pallas as pl
from jax.experimental.pallas import tpu as pltpu

F32 = jnp.float32
BF16 = jnp.bfloat16

D_MODEL = 1024
DEPTH = 4
GRID_W = 64
EPS = 1e-6
NEG_INF = -1e30
ROPE_BASE = 10000.0
BLOCK = 128
WINDOW = 128
HEAD_DIM = 64
LANES = 128
RET_HEADS = 4
RET_WIDTH = RET_HEADS * HEAD_DIM
HY_WIDTH = 256
HY_ORDER = 2
HY_BANDS = 8
HY_POS_DIM = 1 + 2 * HY_BANDS
HY_POS_PAD = 32
HY_FILTER_HIDDEN = 64
GQA_Q_HEADS = 4
GQA_KV_HEADS = 2
GQA_GROUPS = GQA_Q_HEADS // GQA_KV_HEADS
MLA_HEADS = 4
MLA_Q_RANK = 256
MLA_KV_RANK = 128
MLA_NOPE = 64
MLA_ROPE = 32
MLA_V = 64
D_FF = 2816
RET_IN = 4 * RET_WIDTH
HY_IN = (HY_ORDER + 1) * HY_WIDTH
GQA_IN = (GQA_Q_HEADS + 2 * GQA_KV_HEADS) * HEAD_DIM
MLA_IN = MLA_Q_RANK + MLA_KV_RANK + MLA_ROPE
MLA_IN_TILED = MLA_Q_RANK + MLA_KV_RANK + MLA_HEADS * MLA_ROPE
IN_WIDTH = RET_IN + HY_IN + GQA_IN + MLA_IN
IN_WIDTH_TILED = RET_IN + HY_IN + GQA_IN + MLA_IN_TILED
GQA_HEAD_ORDER = (0, 2, 1, 3)
MIX_PART = 256
N_COND = 8
FFN_CHUNK = 512

VMEM_LIMIT = 56 * 1024 * 1024
VMEM_MIB = dict(modulation=56, hyena_tables=56, in_proj=56, context_mixers=56, latent_mixers=56,
                latent_hyena=56, channel_mix=56)


def _params(n_axes, call):
    limit = min(VMEM_MIB[call] * 1024 * 1024, VMEM_LIMIT)
    return pltpu.CompilerParams(dimension_semantics=("arbitrary",) * n_axes, vmem_limit_bytes=limit)


def _dot(a, b):
    return jnp.dot(a, b, preferred_element_type=F32)


def _dot_nt(a, b):
    return lax.dot_general(a, b, (((1,), (1,)), ((), ())), preferred_element_type=F32)


def _dot_tn(a, b):
    return lax.dot_general(a, b, (((0,), (0,)), ((), ())), preferred_element_type=F32)


def _rms(x, g):
    return x * lax.rsqrt(jnp.mean(x * x, axis=-1, keepdims=True) + EPS) * g


def _sigmoid(x):
    return 1.0 / (1.0 + jnp.exp(-x))


def _silu(x):
    return x * _sigmoid(x)


def _shift_rows(x, seq_len):
    n = x.shape[0]
    assert seq_len & (seq_len - 1) == 0
    pos = lax.broadcasted_iota(jnp.int32, (n, 1), 0) & (seq_len - 1)
    prev = jnp.where(pos != 0, pltpu.roll(x, 1, axis=0), 0.0)
    nxt = jnp.where(pos != seq_len - 1, pltpu.roll(x, n - 1, axis=0), 0.0)
    return prev, nxt


def _rot_half(x, half):
    lane = lax.broadcasted_iota(jnp.int32, x.shape, 1) & (2 * half - 1)
    return jnp.where(lane < half, pltpu.roll(x, LANES - half, axis=1), pltpu.roll(x, half, axis=1))


def _lane_iota():
    return lax.broadcasted_iota(jnp.int32, (1, LANES), 1)


def _split_heads(t, lo):
    return jnp.concatenate([jnp.where(lo, t, 0.0), jnp.where(lo, 0.0, t)], axis=0)


def _merge_heads(o, lo):
    n = o.shape[0] // 2
    return jnp.where(lo, o[:n], o[n:])


def _per_head(vals, width):
    lane = lax.broadcasted_iota(jnp.int32, (1, width), 1)
    out = vals[-1]
    for h in range(len(vals) - 2, -1, -1):
        out = jnp.where(lane < (h + 1) * HEAD_DIM, vals[h], out)
    return out


def _mod_kernel(cond_ref, w_ref, b_ref, out_ref):
    s = _silu(cond_ref[...]).astype(BF16)
    out_ref[0] = _dot(s, w_ref[0].astype(BF16)) + b_ref[0]


def _modulation(cond, ada_w, ada_b):
    tn = 1536
    return pl.pallas_call(
        _mod_kernel,
        out_shape=jax.ShapeDtypeStruct((DEPTH, N_COND, 6 * D_MODEL), F32),
        grid=(DEPTH, 6 * D_MODEL // tn),
        in_specs=[
            pl.BlockSpec((N_COND, D_MODEL), lambda l, j: (0, 0)),
            pl.BlockSpec((1, D_MODEL, tn), lambda l, j: (l, 0, j)),
            pl.BlockSpec((1, 1, tn), lambda l, j: (l, 0, j)),
        ],
        out_specs=pl.BlockSpec((1, N_COND, tn), lambda l, j: (l, 0, j)),
        compiler_params=_params(2, "modulation"),
        name="modulation",
    )(cond, ada_w, ada_b.reshape(DEPTH, 1, 6 * D_MODEL))


def _filter_kernel(z_ref, w1_ref, b1_ref, w2_ref, b2_ref, w3_ref, dec_ref, f_ref, out_ref, *, seq_len):
    hi = lax.Precision.HIGHEST
    h = jnp.sin(jnp.dot(z_ref[...], w1_ref[0], precision=hi, preferred_element_type=F32) + b1_ref[0])
    h = jnp.sin(jnp.dot(h, w2_ref[0], precision=hi, preferred_element_type=F32) + b2_ref[0])
    h = jnp.dot(h, w3_ref[0], precision=hi, preferred_element_type=F32)
    row = lax.broadcasted_iota(jnp.int32, (seq_len, 1), 0)
    t = row.astype(F32) / seq_len
    dec = jnp.abs(dec_ref[0])
    win_f = jnp.exp(-t * dec[0:1, :])
    win_b = jnp.exp(-t * dec[1:2, :])
    fm = f_ref[...]
    inv = 1.0 / seq_len
    scale = jnp.where(row == 0, 0.5 * inv, inv)
    for o in range(HY_ORDER):
        base = o * 2 * HY_WIDTH
        hf = h[:, base:base + HY_WIDTH] * win_f
        hb = h[:, base + HY_WIDTH:base + 2 * HY_WIDTH] * win_b
        hbs = jnp.where(row != 0, pltpu.roll(hb, 1, axis=0), 0.0)
        sf = _dot(fm, hf.astype(BF16))
        sb = _dot(fm, hbs.astype(BF16))
        g_re = sf[:seq_len] + sb[:seq_len]
        g_im = sf[seq_len:] - sb[seq_len:]
        g_ny = sf[seq_len:] + sb[seq_len:]
        a = g_re * scale
        out_ref[0, 3 * o] = a
        out_ref[0, 3 * o + 1] = jnp.where(row == 0, g_ny * (0.5 * inv), a)
        out_ref[0, 3 * o + 2] = jnp.where(row == 0, 0.0, g_im * inv)


def _hyena_tables(seq_len, z, w1p, b1, w2, b2, w3, decay, fmat):
    hid = HY_FILTER_HIDDEN
    wide = HY_ORDER * 2 * HY_WIDTH
    return pl.pallas_call(
        functools.partial(_filter_kernel, seq_len=seq_len),
        out_shape=jax.ShapeDtypeStruct((DEPTH, 3 * HY_ORDER, seq_len, HY_WIDTH), F32),
        grid=(DEPTH,),
        in_specs=[
            pl.BlockSpec((seq_len, HY_POS_PAD), lambda l: (0, 0)),
            pl.BlockSpec((1, HY_POS_PAD, hid), lambda l: (l, 0, 0)),
            pl.BlockSpec((1, 1, hid), lambda l: (l, 0, 0)),
            pl.BlockSpec((1, hid, hid), lambda l: (l, 0, 0)),
            pl.BlockSpec((1, 1, hid), lambda l: (l, 0, 0)),
            pl.BlockSpec((1, hid, wide), lambda l: (l, 0, 0)),
            pl.BlockSpec((1, 2, HY_WIDTH), lambda l: (l, 0, 0)),
            pl.BlockSpec((2 * seq_len, seq_len), lambda l: (0, 0)),
        ],
        out_specs=pl.BlockSpec((1, 3 * HY_ORDER, seq_len, HY_WIDTH), lambda l: (l, 0, 0, 0)),
        compiler_params=_params(1, "hyena_tables"),
        name=f"hyena_tables_{seq_len}",
    )(z, w1p, b1, w2, b2, w3, decay, fmat)


def _layer_stacked(tail, n_batch, layer, nb):
    zeros = (0,) * len(tail)
    n_layers = DEPTH if layer == 0 else 1
    return (jax.ShapeDtypeStruct((n_batch, DEPTH) + tail, F32),
            pl.BlockSpec((nb, n_layers) + tail, lambda b: (b, layer) + zeros))


def _zero_other_layers(*out_refs):
    for ref in out_refs:
        if ref.shape[1] > 1:
            ref[:, 1:] = jnp.zeros((ref.shape[0], ref.shape[1] - 1) + ref.shape[2:], ref.dtype)


def _carry_specs(in_specs, args, carried, first_out):
    aliases = {}
    if carried is not None:
        for k, arr in enumerate(carried):
            aliases[len(args)] = first_out + k
            in_specs.append(pl.BlockSpec(memory_space=pl.ANY))
            args.append(arr)
    return aliases


def _part(kernel, in_specs, args, out_shape, out_specs, scratch=(), aliases=None):
    as_list = lambda v: list(v) if isinstance(v, (list, tuple)) else [v]
    return dict(kernel=kernel, in_specs=list(in_specs), args=list(args), out_shape=as_list(out_shape),
                out_specs=as_list(out_specs), scratch=list(scratch), aliases=dict(aliases or {}))


def _run_parts(parts, grid, name):
    n_in = sum(len(p["args"]) for p in parts)
    n_out = sum(len(p["out_shape"]) for p in parts)

    def fused(*refs):
        i = o = s = 0
        for p in parts:
            a, b, c = len(p["args"]), len(p["out_shape"]), len(p["scratch"])
            p["kernel"](*refs[i:i + a], *refs[n_in + o:n_in + o + b],
                        *refs[n_in + n_out + s:n_in + n_out + s + c])
            i, o, s = i + a, o + b, s + c

    aliases = {}
    i = o = 0
    for p in parts:
        aliases.update({i + k: o + v for k, v in p["aliases"].items()})
        i, o = i + len(p["args"]), o + len(p["out_shape"])
    outs = pl.pallas_call(
        fused,
        out_shape=[s for p in parts for s in p["out_shape"]],
        grid=grid,
        in_specs=[s for p in parts for s in p["in_specs"]],
        out_specs=[s for p in parts for s in p["out_specs"]],
        input_output_aliases=aliases,
        scratch_shapes=[s for p in parts for s in p["scratch"]],
        compiler_params=_params(len(grid), name),
        name=name,
    )(*[a for p in parts for a in p["args"]])
    split, o = [], 0
    for p in parts:
        split.append(list(outs[o:o + len(p["out_shape"])]))
        o += len(p["out_shape"])
    return split


def _in_proj_kernel(x_ref, mod_ref, g_ref, w_ref, ret_ref, hy_ref, gqa_ref, mla_ref, w_scr):
    @pl.when(pl.program_id(0) == 0)
    def _():
        g0 = RET_IN + HY_IN
        m0 = g0 + GQA_IN
        kr0 = m0 + MLA_Q_RANK + MLA_KV_RANK
        step = 2 * LANES
        for r0 in list(range(0, g0, step)) + list(range(g0 + GQA_Q_HEADS * HEAD_DIM, kr0, LANES)):
            n = step if r0 < g0 else LANES
            w_scr[r0:r0 + n, :] = w_ref[0, r0:r0 + n, :].astype(BF16)
        for dst, j in enumerate(GQA_HEAD_ORDER):
            w_scr[g0 + dst * HEAD_DIM:g0 + (dst + 1) * HEAD_DIM, :] = (
                w_ref[0, g0 + j * HEAD_DIM:g0 + (j + 1) * HEAD_DIM, :].astype(BF16))
        kr = w_ref[0, kr0:kr0 + MLA_ROPE, :].astype(BF16)
        for hd in range(MLA_HEADS):
            w_scr[kr0 + hd * MLA_ROPE:kr0 + (hd + 1) * MLA_ROPE, :] = kr

    shift = mod_ref[0, 0, 0:1, :]
    scale = mod_ref[0, 0, 1:2, :]
    h = (_rms(x_ref[...], g_ref[0, 0:1, :]) * (1.0 + scale) + shift).astype(BF16)
    c0 = 0
    for ref in (ret_ref, hy_ref, gqa_ref, mla_ref):
        width = ref.shape[1]
        ref[...] = _dot_nt(h, w_scr[c0:c0 + width, :])
        c0 += width


def _in_proj(x, mod, norm_g, w_in, layer, tm, cond_of_tile):
    rows = x.shape[0]
    widths = (RET_IN, HY_IN, GQA_IN, MLA_IN_TILED)
    return pl.pallas_call(
        _in_proj_kernel,
        out_shape=[jax.ShapeDtypeStruct((rows, w), F32) for w in widths],
        grid=(rows // tm,),
        in_specs=[
            pl.BlockSpec((tm, D_MODEL), lambda i: (i, 0)),
            pl.BlockSpec((1, 1, 6, D_MODEL), lambda i: (layer, cond_of_tile(i), 0, 0)),
            pl.BlockSpec((1, 4, D_MODEL), lambda i: (layer, 0, 0)),
            pl.BlockSpec((1, IN_WIDTH, D_MODEL), lambda i: (layer, 0, 0), pipeline_mode=pl.Buffered(1)),
        ],
        out_specs=[pl.BlockSpec((tm, w), lambda i: (i, 0)) for w in widths],
        scratch_shapes=[pltpu.VMEM((IN_WIDTH_TILED, D_MODEL), BF16)],
        compiler_params=_params(1, "in_proj"),
        name="in_proj",
    )(x, mod, norm_g, w_in)


def _channel_kernel(x_ref, m0_ref, m1_ref, m2_ref, m3_ref, mod_ref, g_ref, wo_ref, wu_ref, cw_ref, cb_ref,
                    wd_ref, out_ref, act_ref, *, seq_len):
    y = None
    for i, m_ref in enumerate((m0_ref, m1_ref, m2_ref, m3_ref)):
        r0 = i * MIX_PART
        if i == 2:
            w = jnp.concatenate([wo_ref[0, r0 + j * HEAD_DIM:r0 + (j + 1) * HEAD_DIM, :]
                                 for j in GQA_HEAD_ORDER], axis=0)
        else:
            w = wo_ref[0, r0:r0 + MIX_PART, :]
        part = _dot(m_ref[...], w)
        y = part if y is None else y + part
    gate1 = mod_ref[0, 0, 2:3, :]
    shift2 = mod_ref[0, 0, 3:4, :]
    scale2 = mod_ref[0, 0, 4:5, :]
    xm = x_ref[...] + gate1 * _rms(y, g_ref[0, 1:2, :])
    out_ref[...] = xm
    h2 = (_rms(xm, g_ref[0, 2:3, :]) * (1.0 + scale2) + shift2).astype(BF16)
    for c0 in range(0, D_FF, FFN_CHUNK):
        sl = slice(c0, min(c0 + FFN_CHUNK, D_FF))
        gate = _dot(h2, wu_ref[0, :, sl])
        up = _dot(h2, wu_ref[0, :, D_FF + sl.start:D_FF + sl.stop])
        prev, nxt = _shift_rows(gate, seq_len)
        gate = (prev * cw_ref[0, 0:1, sl] + gate * cw_ref[0, 1:2, sl] + nxt * cw_ref[0, 2:3, sl]
                + cb_ref[0, :, sl])
        act_ref[:, sl] = (_silu(gate) * up).astype(BF16)
    ffn = _dot(act_ref[...], wd_ref[0])
    gate2 = mod_ref[0, 0, 5:6, :]
    out_ref[...] = out_ref[...] + gate2 * _rms(ffn, g_ref[0, 3:4, :])


def _channel_mix(x, mixes, mod, norm_g, w_out, w_up, conv_w, conv_b, w_down, layer, tm, seq_len, cond_of_tile):
    rows = x.shape[0]
    resident = dict(pipeline_mode=pl.Buffered(1))
    return pl.pallas_call(
        functools.partial(_channel_kernel, seq_len=seq_len),
        out_shape=jax.ShapeDtypeStruct((rows, D_MODEL), F32),
        grid=(rows // tm,),
        in_specs=[pl.BlockSpec((tm, D_MODEL), lambda i: (i, 0))]
        + [pl.BlockSpec((tm, MIX_PART), lambda i: (i, 0))] * 4
        + [
            pl.BlockSpec((1, 1, 6, D_MODEL), lambda i: (layer, cond_of_tile(i), 0, 0)),
            pl.BlockSpec((1, 4, D_MODEL), lambda i: (layer, 0, 0)),
            pl.BlockSpec((1, D_MODEL, D_MODEL), lambda i: (layer, 0, 0), **resident),
            pl.BlockSpec((1, D_MODEL, 2 * D_FF), lambda i: (layer, 0, 0), **resident),
            pl.BlockSpec((1, 3, D_FF), lambda i: (layer, 0, 0)),
            pl.BlockSpec((1, 1, D_FF), lambda i: (layer, 0, 0)),
            pl.BlockSpec((1, D_FF, D_MODEL), lambda i: (layer, 0, 0), **resident),
        ],
        out_specs=pl.BlockSpec((tm, D_MODEL), lambda i: (i, 0)),
        scratch_shapes=[pltpu.VMEM((tm, D_FF), BF16)],
        compiler_params=_params(1, "channel_mix"),
        name="channel_mix",
    )(x, *mixes, mod, norm_g, w_out, w_up, conv_w, conv_b, w_down)


def _log_gamma(dl_ref):
    dl = dl_ref[0]
    return jnp.minimum(dl, 0.0) - jnp.log(1.0 + jnp.exp(-jnp.abs(dl)))


def _pair_decay(lg, t, r0, bq, seq_len):
    top = lax.broadcasted_iota(jnp.int32, (2 * bq, 1), 0) < bq
    lgf = jnp.where(top, lg[0:1, 2 * t:2 * t + 1], lg[0:1, 2 * t + 1:2 * t + 2])
    lgb = jnp.where(top, lg[1:2, 2 * t:2 * t + 1], lg[1:2, 2 * t + 1:2 * t + 2])
    rowf = ((lax.broadcasted_iota(jnp.int32, (2 * bq, 1), 0) & (bq - 1)) + r0).astype(F32)
    colf = lax.broadcasted_iota(jnp.int32, (1, seq_len), 1).astype(F32)
    lag = rowf - colf
    decay = jnp.exp(jnp.where(lag >= 0.0, lag * lgf, -lag * lgb))
    return jnp.where(lag == 0.0, 2.0, decay)


def _head_norm(o, lo):
    inv = 1.0 / HEAD_DIM
    s_all = jnp.sum(o, axis=-1, keepdims=True)
    s_lo = jnp.sum(jnp.where(lo, o, 0.0), axis=-1, keepdims=True)
    d = o - jnp.where(lo, s_lo, s_all - s_lo) * inv
    d2 = d * d
    v_all = jnp.sum(d2, axis=-1, keepdims=True)
    v_lo = jnp.sum(jnp.where(lo, d2, 0.0), axis=-1, keepdims=True)
    return d * lax.rsqrt(jnp.where(lo, v_lo, v_all - v_lo) * inv + EPS)


def _ret_ctx_kernel(*refs, seq_len, nb):
    x_ref, dl_ref, gn_ref, *_, y_ref, st_ref, dec_scr, kdec_scr = refs
    lo = _lane_iota() < HEAD_DIM
    _zero_other_layers(st_ref)

    @pl.when(pl.program_id(0) == 0)
    def _():
        lg = _log_gamma(dl_ref)
        posf = lax.broadcasted_iota(jnp.int32, (seq_len, 1), 0).astype(F32)
        lgf = _per_head([lg[0:1, h:h + 1] for h in range(RET_HEADS)], RET_WIDTH)
        lgb = _per_head([lg[1:2, h:h + 1] for h in range(RET_HEADS)], RET_WIDTH)
        kdec_scr[0] = jnp.exp((seq_len - 1.0 - posf) * lgf)
        kdec_scr[1] = jnp.exp(posf * lgb)
        for t in range(RET_HEADS // 2):
            dec_scr[t] = _pair_decay(lg, t, 0, seq_len, seq_len)

    for i in range(nb):
        rows = slice(i * seq_len, (i + 1) * seq_len)
        for t in range(RET_HEADS // 2):
            cs = slice(t * LANES, (t + 1) * LANES)
            q = x_ref[rows, cs]
            k = x_ref[rows, RET_WIDTH + t * LANES:RET_WIDTH + (t + 1) * LANES] * (HEAD_DIM ** -0.5)
            vb = x_ref[rows, 2 * RET_WIDTH + t * LANES:2 * RET_WIDTH + (t + 1) * LANES].astype(BF16)
            gate = x_ref[rows, 3 * RET_WIDTH + t * LANES:3 * RET_WIDTH + (t + 1) * LANES]
            s = _dot_nt(_split_heads(q, lo).astype(BF16), k.astype(BF16)) * dec_scr[t]
            o = _merge_heads(_dot(s.astype(BF16), vb), lo)
            y = _silu(gate) * (_head_norm(o, lo) * gn_ref[0, :, cs])
            y_ref[rows, cs] = y.astype(BF16)
            for d in range(2):
                st = _dot_tn((k * kdec_scr[d, :, cs]).astype(BF16), vb)
                st_ref[i, 0, d, 2 * t] = st[0:HEAD_DIM, 0:HEAD_DIM]
                st_ref[i, 0, d, 2 * t + 1] = st[HEAD_DIM:LANES, HEAD_DIM:LANES]


def _ret_lat_kernel(x_ref, dl_ref, gn_ref, s0_ref, y_ref, *, seq_len, bq):
    lo = _lane_iota() < HEAD_DIM
    lg = _log_gamma(dl_ref)
    zero = jnp.zeros((HEAD_DIM, HEAD_DIM), F32)
    for t in range(RET_HEADS // 2):
        cs = slice(t * LANES, (t + 1) * LANES)
        kb = (x_ref[:, RET_WIDTH + t * LANES:RET_WIDTH + (t + 1) * LANES] * (HEAD_DIM ** -0.5)).astype(BF16)
        vb = x_ref[:, 2 * RET_WIDTH + t * LANES:2 * RET_WIDTH + (t + 1) * LANES].astype(BF16)
        lgf = _per_head([lg[0:1, 2 * t:2 * t + 1], lg[0:1, 2 * t + 1:2 * t + 2]], LANES)
        lgb = _per_head([lg[1:2, 2 * t:2 * t + 1], lg[1:2, 2 * t + 1:2 * t + 2]], LANES)
        s0 = []
        for d in range(2):
            a = s0_ref[0, 0, d, 2 * t]
            b = s0_ref[0, 0, d, 2 * t + 1]
            s0.append(jnp.concatenate([jnp.concatenate([a, zero], axis=1),
                                       jnp.concatenate([zero, b], axis=1)], axis=0).astype(BF16))
        for r0 in range(0, seq_len, bq):
            q = x_ref[r0:r0 + bq, cs]
            s = _dot_nt(_split_heads(q, lo).astype(BF16), kb) * _pair_decay(lg, t, r0, bq, seq_len)
            o = _merge_heads(_dot(s.astype(BF16), vb), lo)
            rowf = (lax.broadcasted_iota(jnp.int32, (bq, 1), 0) + r0).astype(F32)
            qf = (q * jnp.exp((rowf + 1.0) * lgf)).astype(BF16)
            qr = (q * jnp.exp((seq_len - rowf) * lgb)).astype(BF16)
            o = o + _dot(qf, s0[0]) + _dot(qr, s0[1])
            gate = x_ref[r0:r0 + bq, 3 * RET_WIDTH + t * LANES:3 * RET_WIDTH + (t + 1) * LANES]
            y = _silu(gate) * (_head_norm(o, lo) * gn_ref[0, :, cs])
            y_ref[r0:r0 + bq, cs] = y.astype(BF16)


def _retention(ret_in, decay_logit, gn_g, state0, carried, layer, n_batch, seq_len, nb):
    latent = state0 is not None
    in_specs = [
        pl.BlockSpec((nb * seq_len, RET_IN), lambda b: (b, 0)),
        pl.BlockSpec((1, 2, RET_HEADS), lambda b: (layer, 0, 0)),
        pl.BlockSpec((1, 1, RET_WIDTH), lambda b: (layer, 0, 0)),
    ]
    args = [ret_in, decay_logit, gn_g]
    y_shape = jax.ShapeDtypeStruct((n_batch * seq_len, MIX_PART), BF16)
    y_spec = pl.BlockSpec((nb * seq_len, MIX_PART), lambda b: (b, 0))
    if latent:
        assert nb == 1
        kern = functools.partial(_ret_lat_kernel, seq_len=seq_len, bq=256)
        in_specs.append(pl.BlockSpec((1, 1, 2, RET_HEADS, HEAD_DIM, HEAD_DIM),
                                     lambda b: (b, layer, 0, 0, 0, 0)))
        args.append(state0)
        out_shape, out_specs, aliases, scratch = y_shape, y_spec, {}, []
    else:
        kern = functools.partial(_ret_ctx_kernel, seq_len=seq_len, nb=nb)
        st_shape, st_spec = _layer_stacked((2, RET_HEADS, HEAD_DIM, HEAD_DIM), n_batch, layer, nb)
        out_shape = [y_shape, st_shape]
        out_specs = [y_spec, st_spec]
        aliases = _carry_specs(in_specs, args, carried, 1)
        scratch = [pltpu.VMEM((RET_HEADS // 2, 2 * seq_len, seq_len), F32),
                   pltpu.VMEM((2, seq_len, RET_WIDTH), F32)]
    return _part(kern, in_specs, args, out_shape, out_specs, scratch, aliases)


def _hyena_kernel(x_ref, sw_ref, sb_ref, bias_ref, tab_ref, f_ref, ft_ref, y_ref, *, seq_len, nb):
    for i in range(nb):
        rows = slice(i * seq_len, (i + 1) * seq_len)
        x = x_ref[rows, :]
        prev, nxt = _shift_rows(x, seq_len)
        u = prev * sw_ref[0, 0:1, :] + x * sw_ref[0, 1:2, :] + nxt * sw_ref[0, 2:3, :] + sb_ref[0]
        x1 = u[:, 0:HY_WIDTH]
        x2 = u[:, HY_WIDTH:2 * HY_WIDTH]
        z = u[:, 2 * HY_WIDTH:3 * HY_WIDTH]
        for o, gate in enumerate((x1, x2)):
            spec = _dot(f_ref[...], z.astype(BF16))
            s_re = spec[:seq_len]
            s_im = spec[seq_len:]
            a = tab_ref[0, 3 * o]
            a_ny = tab_ref[0, 3 * o + 1]
            b = tab_ref[0, 3 * o + 2]
            y_re = (s_re * a - s_im * b).astype(BF16)
            y_im = (s_re * b + s_im * a_ny).astype(BF16)
            conv = _dot(ft_ref[:, 0:seq_len], y_re) + _dot(ft_ref[:, seq_len:2 * seq_len], y_im)
            z = gate * (conv + z * bias_ref[0, o:o + 1, :])
        y_ref[rows, :] = z.astype(BF16)


def _hyena(hy_in, short_w, short_b, hy_bias, tables, fmat, fmat_t, layer, n_batch, seq_len, nb):
    const = dict(pipeline_mode=pl.Buffered(1))
    return _part(
        functools.partial(_hyena_kernel, seq_len=seq_len, nb=nb),
        [
            pl.BlockSpec((nb * seq_len, HY_IN), lambda b: (b, 0)),
            pl.BlockSpec((1, 3, HY_IN), lambda b: (layer, 0, 0)),
            pl.BlockSpec((1, 1, HY_IN), lambda b: (layer, 0, 0)),
            pl.BlockSpec((1, HY_ORDER, HY_WIDTH), lambda b: (layer, 0, 0)),
            pl.BlockSpec((1, 3 * HY_ORDER, seq_len, HY_WIDTH), lambda b: (layer, 0, 0, 0), **const),
            pl.BlockSpec((2 * seq_len, seq_len), lambda b: (0, 0), **const),
            pl.BlockSpec((seq_len, 2 * seq_len), lambda b: (0, 0), **const),
        ],
        [hy_in, short_w, short_b, hy_bias, tables, fmat, fmat_t],
        jax.ShapeDtypeStruct((n_batch * seq_len, MIX_PART), BF16),
        pl.BlockSpec((nb * seq_len, MIX_PART), lambda b: (b, 0)))


def _gqa_ctx_kernel(*refs, seq_len, nb):
    x_ref, sink_ref, *_, y_ref, k_out_ref, v_out_ref = refs
    nq = GQA_Q_HEADS * HEAD_DIM
    scale = HEAD_DIM ** -0.5
    head_a = lax.broadcasted_iota(jnp.int32, (LANES, 1), 0) < HEAD_DIM
    first = lax.broadcasted_iota(jnp.int32, (1, 2 * seq_len), 1) < seq_len
    ones = jnp.ones((LANES, seq_len), BF16)
    _zero_other_layers(k_out_ref, v_out_ref)
    for i in range(nb):
        rows = slice(i * seq_len, (i + 1) * seq_len)
        k = x_ref[rows, nq:nq + LANES]
        k_t = k.T
        v_t = x_ref[rows, nq + LANES:nq + 2 * LANES].T
        for kv in range(GQA_KV_HEADS):
            k_out_ref[i, 0, kv] = k_t[kv * HEAD_DIM:(kv + 1) * HEAD_DIM, :]
            v_out_ref[i, 0, kv] = v_t[kv * HEAD_DIM:(kv + 1) * HEAD_DIM, :]
        kb = k.astype(BF16)
        v_ext = jnp.concatenate([v_t.astype(BF16), ones], axis=0)
        for g in range(GQA_GROUPS):
            cs = slice(g * LANES, (g + 1) * LANES)
            q_t = x_ref[rows, cs].T
            q2 = jnp.concatenate([jnp.where(head_a, q_t, 0.0), jnp.where(head_a, 0.0, q_t)], axis=1)
            s = _dot(kb, q2.astype(BF16)) * scale
            sink = jnp.where(first, sink_ref[0, :, g:g + 1], sink_ref[0, :, GQA_GROUPS + g:GQA_GROUPS + g + 1])
            m = jnp.maximum(jnp.max(s, axis=0, keepdims=True), sink)
            pv = _dot(v_ext, jnp.exp(s - m).astype(BF16))
            o = pv[0:LANES, :] / (pv[LANES:LANES + 1, :] + jnp.exp(sink - m))
            o = jnp.where(head_a, o[:, 0:seq_len], o[:, seq_len:2 * seq_len])
            y_ref[rows, cs] = o.T.astype(BF16)


def _gqa_ctx(gqa_in, sink, carried, layer, n_batch, seq_len, nb):
    kv_shape, kv_spec = _layer_stacked((GQA_KV_HEADS, HEAD_DIM, seq_len), n_batch, layer, nb)
    in_specs = [
        pl.BlockSpec((nb * seq_len, GQA_IN), lambda b: (b, 0)),
        pl.BlockSpec((1, 1, GQA_Q_HEADS), lambda b: (layer, 0, 0)),
    ]
    args = [gqa_in, sink]
    aliases = _carry_specs(in_specs, args, carried, 1)
    return _part(
        functools.partial(_gqa_ctx_kernel, seq_len=seq_len, nb=nb), in_specs, args,
        [jax.ShapeDtypeStruct((n_batch * seq_len, MIX_PART), BF16), kv_shape, kv_shape],
        [pl.BlockSpec((nb * seq_len, MIX_PART), lambda b: (b, 0)), kv_spec, kv_spec],
        aliases=aliases)


def _gqa_win_kernel(x_ref, sink_ref, kct_ref, vct_ref, cos_ref, sin_ref, y_ref,
                    q_scr, k_scr, vt_scr, kc_scr, vct_scr, *, seq_len):
    nq = GQA_Q_HEADS * HEAD_DIM
    scale = HEAD_DIM ** -0.5
    head_a = lax.broadcasted_iota(jnp.int32, (LANES, 1), 0) < HEAD_DIM
    first = lax.broadcasted_iota(jnp.int32, (1, 2 * BLOCK), 1) < BLOCK
    cos = cos_ref[...]
    sin = sin_ref[...]

    def rope(t):
        return t * cos + _rot_half(t, HEAD_DIM // 2) * sin

    for g in range(GQA_GROUPS):
        q_t = rope(x_ref[:, g * LANES:(g + 1) * LANES]).T
        q_scr[g, 0] = jnp.where(head_a, q_t, 0.0).astype(BF16)
        q_scr[g, 1] = jnp.where(head_a, 0.0, q_t).astype(BF16)
    zeros = jnp.zeros((BLOCK, LANES), BF16)
    for r0 in (0, BLOCK + seq_len):
        k_scr[r0:r0 + BLOCK, :] = zeros
        vt_scr[:, r0:r0 + BLOCK] = zeros
    k_scr[BLOCK:BLOCK + seq_len, :] = rope(x_ref[:, nq:nq + LANES]).astype(BF16)
    vt_scr[:, BLOCK:BLOCK + seq_len] = x_ref[:, nq + LANES:nq + 2 * LANES].T.astype(BF16)
    kc_scr[...] = jnp.concatenate([kct_ref[0, 0, 0], kct_ref[0, 0, 1]], axis=0).T.astype(BF16)
    vct_scr[...] = jnp.concatenate([vct_ref[0, 0, 0], vct_ref[0, 0, 1]], axis=0).astype(BF16)

    jj = lax.broadcasted_iota(jnp.int32, (3 * BLOCK, 2 * BLOCK), 0)
    ii = lax.broadcasted_iota(jnp.int32, (3 * BLOCK, 2 * BLOCK), 1) & (BLOCK - 1)
    band = (jj >= ii) & (jj <= ii + 2 * WINDOW)
    for n in range(seq_len // BLOCK):
        r0 = n * BLOCK
        kpos = jj + (n - 1) * BLOCK
        valid = band & (kpos >= 0) & (kpos < seq_len)
        kw = k_scr[r0:r0 + 3 * BLOCK, :]
        vw = vt_scr[:, r0:r0 + 3 * BLOCK]
        for g in range(GQA_GROUPS):
            q2 = jnp.concatenate([q_scr[g, 0, :, r0:r0 + BLOCK], q_scr[g, 1, :, r0:r0 + BLOCK]], axis=1)
            s_win = jnp.where(valid, _dot(kw, q2) * scale, NEG_INF)
            s_ctx = _dot(kc_scr[...], q2) * scale
            sink = jnp.where(first, sink_ref[0, :, g:g + 1], sink_ref[0, :, GQA_GROUPS + g:GQA_GROUPS + g + 1])
            m = jnp.maximum(jnp.maximum(jnp.max(s_win, axis=0, keepdims=True),
                                        jnp.max(s_ctx, axis=0, keepdims=True)), sink)
            p_win = jnp.exp(s_win - m)
            p_ctx = jnp.exp(s_ctx - m)
            den = (jnp.sum(p_win, axis=0, keepdims=True) + jnp.sum(p_ctx, axis=0, keepdims=True)
                   + jnp.exp(sink - m))
            o = (_dot(vw, p_win.astype(BF16)) + _dot(vct_scr[...], p_ctx.astype(BF16))) / den
            o = jnp.where(head_a, o[:, 0:BLOCK], o[:, BLOCK:2 * BLOCK])
            y_ref[r0:r0 + BLOCK, g * LANES:(g + 1) * LANES] = o.T.astype(BF16)


def _gqa_win(gqa_in, sink, cache_k_t, cache_v_t, cos, sin, layer, n_batch, seq_len):
    past = cache_k_t.shape[4]
    return _part(
        functools.partial(_gqa_win_kernel, seq_len=seq_len),
        [
            pl.BlockSpec((seq_len, GQA_IN), lambda b: (b, 0)),
            pl.BlockSpec((1, 1, GQA_Q_HEADS), lambda b: (layer, 0, 0)),
            pl.BlockSpec((1, 1, GQA_KV_HEADS, HEAD_DIM, past), lambda b: (b, layer, 0, 0, 0)),
            pl.BlockSpec((1, 1, GQA_KV_HEADS, HEAD_DIM, past), lambda b: (b, layer, 0, 0, 0)),
            pl.BlockSpec((seq_len, LANES), lambda b: (0, 0)),
            pl.BlockSpec((seq_len, LANES), lambda b: (0, 0)),
        ],
        [gqa_in, sink, cache_k_t, cache_v_t, cos, sin],
        jax.ShapeDtypeStruct((n_batch * seq_len, MIX_PART), BF16),
        pl.BlockSpec((seq_len, MIX_PART), lambda b: (b, 0)),
        scratch=[
            pltpu.VMEM((GQA_GROUPS, 2, LANES, seq_len), BF16),
            pltpu.VMEM((seq_len + 2 * BLOCK, LANES), BF16),
            pltpu.VMEM((LANES, seq_len + 2 * BLOCK), BF16),
            pltpu.VMEM((past, LANES), BF16),
            pltpu.VMEM((LANES, past), BF16),
        ])


def _mla_kernel(*refs, seq_len, latent, bq, nb):
    if latent:
        (x_ref, qn_ref, kvn_ref, wq_ref, wk_ref, wv_ref, ckv_c_ref, kr_c_ref, cos_ref, sin_ref,
         y_ref, q_all, k_all, v_all) = refs
        past = ckv_c_ref.shape[2]
    else:
        (x_ref, qn_ref, kvn_ref, wq_ref, wk_ref, wv_ref, *_,
         y_ref, ckv_out_ref, kr_out_ref, q_all, k_all, v_all) = refs
        past = 0
        _zero_other_layers(ckv_out_ref, kr_out_ref)
    keys = past + seq_len
    nope_w = MLA_HEADS * MLA_NOPE
    scale = (MLA_NOPE + MLA_ROPE) ** -0.5
    lane = _lane_iota()
    lo = lane < HEAD_DIM
    rope_lanes = [(lane >= h * MLA_ROPE) & (lane < (h + 1) * MLA_ROPE) for h in range(MLA_HEADS)]
    for i in range(nb):
        rows = slice(i * seq_len, (i + 1) * seq_len)
        q_scr, k_scr, v_scr = q_all.at[i], k_all.at[i], v_all.at[i]
        q_lat = x_ref[rows, 0:MLA_Q_RANK]
        kv_lat = x_ref[rows, MLA_Q_RANK:MLA_Q_RANK + MLA_KV_RANK]
        k_rope = x_ref[rows, MLA_Q_RANK + MLA_KV_RANK:MLA_IN_TILED]
        mq = _dot(_rms(q_lat, qn_ref[0]).astype(BF16), wq_ref[0])
        ckv = _rms(kv_lat, kvn_ref[0])
        q_rope = mq[:, nope_w:nope_w + LANES]
        if latent:
            cos = cos_ref[...]
            sin = sin_ref[...]
            q_rope = q_rope * cos + _rot_half(q_rope, MLA_ROPE // 2) * sin
            k_rope = k_rope * cos + _rot_half(k_rope, MLA_ROPE // 2) * sin
            ckv_c = ckv_c_ref[0, 0].astype(BF16)
            k_scr[0:past, 0:nope_w] = _dot(ckv_c, wk_ref[0]).astype(BF16)
            v_scr[0:past, :] = _dot(ckv_c, wv_ref[0]).astype(BF16)
            kr_c = kr_c_ref[0, 0]
            k_scr[0:past, nope_w:nope_w + LANES] = jnp.concatenate([kr_c] * MLA_HEADS, axis=1).astype(BF16)
        else:
            ckv_out_ref[i, 0] = ckv
            kr_out_ref[i, 0] = k_rope.T[0:MLA_ROPE, :]
        q_scr[:, 0:nope_w] = mq[:, 0:nope_w]
        q_scr[:, nope_w:nope_w + LANES] = q_rope
        ckv_b = ckv.astype(BF16)
        k_scr[past:keys, 0:nope_w] = _dot(ckv_b, wk_ref[0]).astype(BF16)
        v_scr[past:keys, :] = _dot(ckv_b, wv_ref[0]).astype(BF16)
        k_scr[past:keys, nope_w:nope_w + LANES] = k_rope.astype(BF16)

        ones = jnp.ones((keys, LANES), BF16)
        kr_all = k_scr[:, nope_w:nope_w + LANES]
        for t in range(MLA_HEADS // 2):
            cs = slice(t * LANES, (t + 1) * LANES)
            k_cat = jnp.concatenate([k_scr[:, cs], kr_all], axis=1)
            v_ext = jnp.concatenate([v_scr[:, cs], ones], axis=1)
            for r0 in range(0, seq_len, bq):
                qn = q_scr[r0:r0 + bq, cs]
                qr = q_scr[r0:r0 + bq, nope_w:nope_w + LANES]
                q2 = jnp.concatenate([
                    jnp.concatenate([jnp.where(lo, qn, 0.0), jnp.where(rope_lanes[2 * t], qr, 0.0)], axis=1),
                    jnp.concatenate([jnp.where(lo, 0.0, qn), jnp.where(rope_lanes[2 * t + 1], qr, 0.0)], axis=1),
                ], axis=0).astype(BF16)
                s = _dot_nt(q2, k_cat) * scale
                m = jnp.max(s, axis=-1, keepdims=True)
                pv = _dot(jnp.exp(s - m).astype(BF16), v_ext)
                o = pv[:, 0:LANES] / pv[:, LANES:LANES + 1]
                y_ref[i * seq_len + r0:i * seq_len + r0 + bq, cs] = _merge_heads(o, lo).astype(BF16)


def _mla(mla_in, q_norm, kv_norm, w_uq, w_uk, w_uv, cache, rope, carried, layer, n_batch, seq_len, nb):
    latent = cache is not None
    aliases = {}
    past = cache[0].shape[2] if latent else 0
    kern = functools.partial(_mla_kernel, seq_len=seq_len, latent=latent, bq=256, nb=nb)
    q_w = MLA_HEADS * (MLA_NOPE + MLA_ROPE)
    kv_w = MLA_HEADS * MLA_NOPE
    in_specs = [
        pl.BlockSpec((nb * seq_len, MLA_IN_TILED), lambda b: (b, 0)),
        pl.BlockSpec((1, 1, MLA_Q_RANK), lambda b: (layer, 0, 0)),
        pl.BlockSpec((1, 1, MLA_KV_RANK), lambda b: (layer, 0, 0)),
        pl.BlockSpec((1, MLA_Q_RANK, q_w), lambda b: (layer, 0, 0)),
        pl.BlockSpec((1, MLA_KV_RANK, kv_w), lambda b: (layer, 0, 0)),
        pl.BlockSpec((1, MLA_KV_RANK, kv_w), lambda b: (layer, 0, 0)),
    ]
    args = [mla_in, q_norm, kv_norm, w_uq, w_uk, w_uv]
    y_shape = jax.ShapeDtypeStruct((n_batch * seq_len, MIX_PART), BF16)
    y_spec = pl.BlockSpec((nb * seq_len, MIX_PART), lambda b: (b, 0))
    if latent:
        assert nb == 1
        in_specs += [
            pl.BlockSpec((1, 1, past, MLA_KV_RANK), lambda b: (b, layer, 0, 0)),
            pl.BlockSpec((1, 1, past, MLA_ROPE), lambda b: (b, layer, 0, 0)),
            pl.BlockSpec((seq_len, LANES), lambda b: (0, 0)),
            pl.BlockSpec((seq_len, LANES), lambda b: (0, 0)),
        ]
        args += [cache[0], cache[1], rope[0], rope[1]]
        out_shape, out_specs = y_shape, y_spec
    else:
        ckv_shape, ckv_spec = _layer_stacked((seq_len, MLA_KV_RANK), n_batch, layer, nb)
        kr_shape, kr_spec = _layer_stacked((MLA_ROPE, seq_len), n_batch, layer, nb)
        out_shape = [y_shape, ckv_shape, kr_shape]
        out_specs = [y_spec, ckv_spec, kr_spec]
        aliases = _carry_specs(in_specs, args, carried, 1)
    keys = past + seq_len
    return _part(
        kern, in_specs, args, out_shape, out_specs,
        scratch=[
            pltpu.VMEM((nb, seq_len, q_w), F32),
            pltpu.VMEM((nb, keys, kv_w + LANES), BF16),
            pltpu.VMEM((nb, keys, kv_w), BF16),
        ],
        aliases=aliases)


def _dft_matrix(seq_len):
    k = np.arange(seq_len, dtype=np.float64)[:, None]
    j = np.arange(seq_len, dtype=np.float64)[None, :]
    ang = np.pi * k * j / seq_len
    re = np.cos(ang)
    im = -np.sin(ang)
    im[0, :] = np.where(np.arange(seq_len) % 2 == 0, 1.0, -1.0)
    return np.concatenate([re, im], axis=0).astype(np.float32)


def _hyena_positions(seq_len):
    t = np.arange(seq_len, dtype=np.float64) / seq_len
    bands = np.arange(1, HY_BANDS + 1, dtype=np.float64)
    ang = 2.0 * math.pi * t[:, None] * bands
    z = np.concatenate([t[:, None], np.cos(ang), np.sin(ang)], axis=-1)
    return np.pad(z, ((0, 0), (0, HY_POS_PAD - HY_POS_DIM))).astype(np.float32)


def _rope_tables(n_tokens, dim):
    rows = np.repeat(np.arange(n_tokens // GRID_W, dtype=np.float64), GRID_W)
    cols = np.tile(np.arange(GRID_W, dtype=np.float64), n_tokens // GRID_W)
    quarter = dim // 4
    inv = ROPE_BASE ** (-np.arange(quarter, dtype=np.float64) / quarter)
    ang = np.concatenate([rows[:, None] * inv, cols[:, None] * inv], axis=-1)
    cos = np.concatenate([np.cos(ang), np.cos(ang)], axis=-1)
    sin = np.concatenate([-np.sin(ang), np.sin(ang)], axis=-1)
    reps = LANES // dim
    return (np.tile(cos, (1, reps)).astype(np.float32), np.tile(sin, (1, reps)).astype(np.float32))


def kernel(x_prompt, x_sample, state_ret, cache_gqa_k, cache_gqa_v, cache_mla_ckv, cache_mla_krope, c, c_ctx, ada_w, ada_b, norm_g, w_in, ret_decay_logit, ret_gn_g, hy_short_w, hy_short_b, hy_w1, hy_b1, hy_w2, hy_b2, hy_w3, hy_decay, hy_bias, gqa_sink, mla_q_norm, mla_kv_norm, mla_w_uq, mla_w_uk, mla_w_uv, w_out, ffn_w_up, ffn_conv_w, ffn_conv_b, ffn_w_down):
    n_p, len_p, _ = x_prompt.shape
    n_s, len_s, _ = x_sample.shape

    w_in_t = jnp.swapaxes(w_in, 1, 2)
    cache_k_t = jnp.swapaxes(cache_gqa_k, -1, -2)
    cache_v_t = jnp.swapaxes(cache_gqa_v, -1, -2)
    w_out_b = w_out.astype(BF16)
    w_up_b = ffn_w_up.astype(BF16)
    w_down_b = ffn_w_down.astype(BF16)
    uq = mla_w_uq.reshape(DEPTH, MLA_Q_RANK, MLA_HEADS, MLA_NOPE + MLA_ROPE)
    w_uq_b = jnp.concatenate(
        [uq[..., :MLA_NOPE].reshape(DEPTH, MLA_Q_RANK, MLA_HEADS * MLA_NOPE),
         uq[..., MLA_NOPE:].reshape(DEPTH, MLA_Q_RANK, MLA_HEADS * MLA_ROPE)], axis=-1).astype(BF16)
    w_uk_b = mla_w_uk.astype(BF16)
    w_uv_b = mla_w_uv.astype(BF16)

    cond = jnp.concatenate([c_ctx[None], c, jnp.zeros((N_COND - 1 - n_s, D_MODEL), F32)], axis=0)
    mod = _modulation(cond, ada_w, ada_b).reshape(DEPTH, N_COND, 6, D_MODEL)

    w1p = jnp.pad(hy_w1, ((0, 0), (0, HY_POS_PAD - HY_POS_DIM), (0, 0)))
    b1 = hy_b1.reshape(DEPTH, 1, HY_FILTER_HIDDEN)
    b2 = hy_b2.reshape(DEPTH, 1, HY_FILTER_HIDDEN)
    groups = {}
    for seq_len in (len_p, len_s):
        f32mat = _dft_matrix(seq_len)
        fmat = jnp.asarray(f32mat).astype(BF16)
        fmat_t = jnp.asarray(np.ascontiguousarray(f32mat.T)).astype(BF16)
        tables = _hyena_tables(seq_len, jnp.asarray(_hyena_positions(seq_len)), w1p, b1, hy_w2, b2,
                               hy_w3, hy_decay, fmat)
        groups[seq_len] = (tables, fmat, fmat_t)

    rope_g = tuple(jnp.asarray(t) for t in _rope_tables(len_s, HEAD_DIM))
    rope_m = tuple(jnp.asarray(t) for t in _rope_tables(len_s, MLA_ROPE))

    gn_g = ret_gn_g.reshape(DEPTH, 1, RET_WIDTH)
    short_b = hy_short_b.reshape(DEPTH, 1, HY_IN)
    sink = gqa_sink.reshape(DEPTH, 1, GQA_Q_HEADS)
    q_norm = mla_q_norm.reshape(DEPTH, 1, MLA_Q_RANK)
    kv_norm = mla_kv_norm.reshape(DEPTH, 1, MLA_KV_RANK)
    conv_b = ffn_conv_b.reshape(DEPTH, 1, D_FF)

    xp = x_prompt.reshape(n_p * len_p, D_MODEL)
    xs = x_sample.reshape(n_s * len_s, D_MODEL)
    tm_p = 1024
    tm_s = 1024
    nb_p = 4
    nb_first = 2
    cond_p = lambda i: 0
    cond_s = lambda i: 1 + (i * tm_s) // len_s
    tm_ffn = 1024
    cond_s_ffn = lambda i: 1 + (i * tm_ffn) // len_s

    st = gkv = ckr = None
    for l in range(DEPTH):
        tables, fmat, fmat_t = groups[len_p]
        ret_in, hy_in, gqa_in, mla_in = _in_proj(xp, mod, norm_g, w_in_t, l, tm_p, cond_p)
        nb = nb_first if l == 0 else nb_p
        (y_ret, *st), (y_hy,), (y_gqa, *gkv), (y_mla, *ckr) = _run_parts([
            _retention(ret_in, ret_decay_logit, gn_g, None, st, l, n_p, len_p, nb),
            _hyena(hy_in, hy_short_w, short_b, hy_bias, tables, fmat, fmat_t, l, n_p, len_p, nb),
            _gqa_ctx(gqa_in, sink, gkv, l, n_p, len_p, nb),
            _mla(mla_in, q_norm, kv_norm, w_uq_b, w_uk_b, w_uv_b, None, None, ckr, l, n_p, len_p, nb),
        ], (n_p // nb,), "context_mixers")
        xp = _channel_mix(xp, (y_ret, y_hy, y_gqa, y_mla), mod, norm_g, w_out_b, w_up_b, ffn_conv_w, conv_b,
                          w_down_b, l, tm_ffn, len_p, cond_p)
        tables, fmat, fmat_t = groups[len_s]
        ret_in, hy_in, gqa_in, mla_in = _in_proj(xs, mod, norm_g, w_in_t, l, tm_s, cond_s)
        (y_ret,), (y_gqa,), (y_mla,) = _run_parts([
            _retention(ret_in, ret_decay_logit, gn_g, state_ret, None, l, n_s, len_s, 1),
            _gqa_win(gqa_in, sink, cache_k_t, cache_v_t, rope_g[0], rope_g[1], l, n_s, len_s),
            _mla(mla_in, q_norm, kv_norm, w_uq_b, w_uk_b, w_uv_b, (cache_mla_ckv, cache_mla_krope),
                 rope_m, None, l, n_s, len_s, 1),
        ], (n_s,), "latent_mixers")
        (y_hy,), = _run_parts([
            _hyena(hy_in, hy_short_w, short_b, hy_bias, tables, fmat, fmat_t, l, n_s, len_s, n_s),
        ], (1,), "latent_hyena")
        xs = _channel_mix(xs, (y_ret, y_hy, y_gqa, y_mla), mod, norm_g, w_out_b, w_up_b, ffn_conv_w, conv_b,
                          w_down_b, l, tm_ffn, len_s, cond_s_ffn)

    return (xp.reshape(n_p, len_p, D_MODEL), xs.reshape(n_s, len_s, D_MODEL),
            st[0], jnp.swapaxes(gkv[0], -1, -2), jnp.swapaxes(gkv[1], -1, -2),
            ckr[0], jnp.swapaxes(ckr[1], -1, -2))
```

```python
import functools
import math

import numpy as np
import jax
import jax.numpy as jnp
from jax import lax
from jax.experimental import pallas as pl
from jax.experimental.pallas import tpu as pltpu

F32 = jnp.float32
BF16 = jnp.bfloat16

D_MODEL = 1024
DEPTH = 4
GRID_W = 64
EPS = 1e-6
NEG_INF = -1e30
ROPE_BASE = 10000.0
BLOCK = 128
WINDOW = 128
HEAD_DIM = 64
LANES = 128
RET_HEADS = 4
RET_WIDTH = RET_HEADS * HEAD_DIM
HY_WIDTH = 256
HY_ORDER = 2
HY_BANDS = 8
HY_POS_DIM = 1 + 2 * HY_BANDS
HY_POS_PAD = 32
HY_FILTER_HIDDEN = 64
GQA_Q_HEADS = 4
GQA_KV_HEADS = 2
GQA_GROUPS = GQA_Q_HEADS // GQA_KV_HEADS
MLA_HEADS = 4
MLA_Q_RANK = 256
MLA_KV_RANK = 128
MLA_NOPE = 64
MLA_ROPE = 32
MLA_V = 64
D_FF = 2816
RET_IN = 4 * RET_WIDTH
HY_IN = (HY_ORDER + 1) * HY_WIDTH
GQA_IN = (GQA_Q_HEADS + 2 * GQA_KV_HEADS) * HEAD_DIM
MLA_IN = MLA_Q_RANK + MLA_KV_RANK + MLA_ROPE
MLA_IN_TILED = MLA_Q_RANK + MLA_KV_RANK + MLA_HEADS * MLA_ROPE
IN_WIDTH = RET_IN + HY_IN + GQA_IN + MLA_IN
IN_WIDTH_TILED = RET_IN + HY_IN + GQA_IN + MLA_IN_TILED
GQA_HEAD_ORDER = (0, 2, 1, 3)
MIX_PART = 256
N_COND = 8
FFN_CHUNK = 512

VMEM_LIMIT = 56 * 1024 * 1024
VMEM_MIB = dict(modulation=56, hyena_tables=56, in_proj=56, context_mixers=56, latent_mixers=40,
                latent_hyena=44, channel_mix=56)


def _params(n_axes, call):
    limit = min(VMEM_MIB[call] * 1024 * 1024, VMEM_LIMIT)
    return pltpu.CompilerParams(dimension_semantics=("arbitrary",) * n_axes, vmem_limit_bytes=limit)


def _dot(a, b):
    return jnp.dot(a, b, preferred_element_type=F32)


def _dot_nt(a, b):
    return lax.dot_general(a, b, (((1,), (1,)), ((), ())), preferred_element_type=F32)


def _dot_tn(a, b):
    return lax.dot_general(a, b, (((0,), (0,)), ((), ())), preferred_element_type=F32)


def _rms(x, g):
    return x * lax.rsqrt(jnp.mean(x * x, axis=-1, keepdims=True) + EPS) * g


def _sigmoid(x):
    return 1.0 / (1.0 + jnp.exp(-x))


def _silu(x):
    return x * _sigmoid(x)


def _shift_rows(x, seq_len):
    n = x.shape[0]
    assert seq_len & (seq_len - 1) == 0
    pos = lax.broadcasted_iota(jnp.int32, (n, 1), 0) & (seq_len - 1)
    prev = jnp.where(pos != 0, pltpu.roll(x, 1, axis=0), 0.0)
    nxt = jnp.where(pos != seq_len - 1, pltpu.roll(x, n - 1, axis=0), 0.0)
    return prev, nxt


def _rot_half(x, half):
    lane = lax.broadcasted_iota(jnp.int32, x.shape, 1) & (2 * half - 1)
    return jnp.where(lane < half, pltpu.roll(x, LANES - half, axis=1), pltpu.roll(x, half, axis=1))


def _lane_iota():
    return lax.broadcasted_iota(jnp.int32, (1, LANES), 1)


def _split_heads(t, lo):
    return jnp.concatenate([jnp.where(lo, t, 0.0), jnp.where(lo, 0.0, t)], axis=0)


def _merge_heads(o, lo):
    n = o.shape[0] // 2
    return jnp.where(lo, o[:n], o[n:])


def _per_head(vals, width):
    lane = lax.broadcasted_iota(jnp.int32, (1, width), 1)
    out = vals[-1]
    for h in range(len(vals) - 2, -1, -1):
        out = jnp.where(lane < (h + 1) * HEAD_DIM, vals[h], out)
    return out


def _mod_kernel(cond_ref, w_ref, b_ref, out_ref):
    s = _silu(cond_ref[...]).astype(BF16)
    out_ref[0] = _dot(s, w_ref[0].astype(BF16)) + b_ref[0]


def _modulation(cond, ada_w, ada_b):
    tn = 1536
    return pl.pallas_call(
        _mod_kernel,
        out_shape=jax.ShapeDtypeStruct((DEPTH, N_COND, 6 * D_MODEL), F32),
        grid=(DEPTH, 6 * D_MODEL // tn),
        in_specs=[
            pl.BlockSpec((N_COND, D_MODEL), lambda l, j: (0, 0)),
            pl.BlockSpec((1, D_MODEL, tn), lambda l, j: (l, 0, j)),
            pl.BlockSpec((1, 1, tn), lambda l, j: (l, 0, j)),
        ],
        out_specs=pl.BlockSpec((1, N_COND, tn), lambda l, j: (l, 0, j)),
        compiler_params=_params(2, "modulation"),
        name="modulation",
    )(cond, ada_w, ada_b.reshape(DEPTH, 1, 6 * D_MODEL))


def _filter_kernel(z_ref, w1_ref, b1_ref, w2_ref, b2_ref, w3_ref, dec_ref, f_ref, out_ref, *, seq_len):
    hi = lax.Precision.HIGHEST
    h = jnp.sin(jnp.dot(z_ref[...], w1_ref[0], precision=hi, preferred_element_type=F32) + b1_ref[0])
    h = jnp.sin(jnp.dot(h, w2_ref[0], precision=hi, preferred_element_type=F32) + b2_ref[0])
    h = jnp.dot(h, w3_ref[0], precision=hi, preferred_element_type=F32)
    row = lax.broadcasted_iota(jnp.int32, (seq_len, 1), 0)
    t = row.astype(F32) / seq_len
    dec = jnp.abs(dec_ref[0])
    win_f = jnp.exp(-t * dec[0:1, :])
    win_b = jnp.exp(-t * dec[1:2, :])
    fm = f_ref[...]
    inv = 1.0 / seq_len
    scale = jnp.where(row == 0, 0.5 * inv, inv)
    for o in range(HY_ORDER):
        base = o * 2 * HY_WIDTH
        hf = h[:, base:base + HY_WIDTH] * win_f
        hb = h[:, base + HY_WIDTH:base + 2 * HY_WIDTH] * win_b
        hbs = jnp.where(row != 0, pltpu.roll(hb, 1, axis=0), 0.0)
        sf = _dot(fm, hf.astype(BF16))
        sb = _dot(fm, hbs.astype(BF16))
        g_re = sf[:seq_len] + sb[:seq_len]
        g_im = sf[seq_len:] - sb[seq_len:]
        g_ny = sf[seq_len:] + sb[seq_len:]
        a = g_re * scale
        out_ref[0, 3 * o] = a
        out_ref[0, 3 * o + 1] = jnp.where(row == 0, g_ny * (0.5 * inv), a)
        out_ref[0, 3 * o + 2] = jnp.where(row == 0, 0.0, g_im * inv)


def _hyena_tables(seq_len, z, w1p, b1, w2, b2, w3, decay, fmat):
    hid = HY_FILTER_HIDDEN
    wide = HY_ORDER * 2 * HY_WIDTH
    return pl.pallas_call(
        functools.partial(_filter_kernel, seq_len=seq_len),
        out_shape=jax.ShapeDtypeStruct((DEPTH, 3 * HY_ORDER, seq_len, HY_WIDTH), F32),
        grid=(DEPTH,),
        in_specs=[
            pl.BlockSpec((seq_len, HY_POS_PAD), lambda l: (0, 0)),
            pl.BlockSpec((1, HY_POS_PAD, hid), lambda l: (l, 0, 0)),
            pl.BlockSpec((1, 1, hid), lambda l: (l, 0, 0)),
            pl.BlockSpec((1, hid, hid), lambda l: (l, 0, 0)),
            pl.BlockSpec((1, 1, hid), lambda l: (l, 0, 0)),
            pl.BlockSpec((1, hid, wide), lambda l: (l, 0, 0)),
            pl.BlockSpec((1, 2, HY_WIDTH), lambda l: (l, 0, 0)),
            pl.BlockSpec((2 * seq_len, seq_len), lambda l: (0, 0)),
        ],
        out_specs=pl.BlockSpec((1, 3 * HY_ORDER, seq_len, HY_WIDTH), lambda l: (l, 0, 0, 0)),
        compiler_params=_params(1, "hyena_tables"),
        name=f"hyena_tables_{seq_len}",
    )(z, w1p, b1, w2, b2, w3, decay, fmat)


def _layer_stacked(tail, n_batch, layer, nb):
    zeros = (0,) * len(tail)
    n_layers = DEPTH if layer == 0 else 1
    return (jax.ShapeDtypeStruct((n_batch, DEPTH) + tail, F32),
            pl.BlockSpec((nb, n_layers) + tail, lambda b: (b, layer) + zeros))


def _zero_other_layers(*out_refs):
    for ref in out_refs:
        if ref.shape[1] > 1:
            ref[:, 1:] = jnp.zeros((ref.shape[0], ref.shape[1] - 1) + ref.shape[2:], ref.dtype)


def _carry_specs(in_specs, args, carried, first_out):
    aliases = {}
    if carried is not None:
        for k, arr in enumerate(carried):
            aliases[len(args)] = first_out + k
            in_specs.append(pl.BlockSpec(memory_space=pl.ANY))
            args.append(arr)
    return aliases


def _part(kernel, in_specs, args, out_shape, out_specs, scratch=(), aliases=None):
    as_list = lambda v: list(v) if isinstance(v, (list, tuple)) else [v]
    return dict(kernel=kernel, in_specs=list(in_specs), args=list(args), out_shape=as_list(out_shape),
                out_specs=as_list(out_specs), scratch=list(scratch), aliases=dict(aliases or {}))


def _run_parts(parts, grid, name):
    n_in = sum(len(p["args"]) for p in parts)
    n_out = sum(len(p["out_shape"]) for p in parts)

    def fused(*refs):
        i = o = s = 0
        for p in parts:
            a, b, c = len(p["args"]), len(p["out_shape"]), len(p["scratch"])
            p["kernel"](*refs[i:i + a], *refs[n_in + o:n_in + o + b],
                        *refs[n_in + n_out + s:n_in + n_out + s + c])
            i, o, s = i + a, o + b, s + c

    aliases = {}
    i = o = 0
    for p in parts:
        aliases.update({i + k: o + v for k, v in p["aliases"].items()})
        i, o = i + len(p["args"]), o + len(p["out_shape"])
    outs = pl.pallas_call(
        fused,
        out_shape=[s for p in parts for s in p["out_shape"]],
        grid=grid,
        in_specs=[s for p in parts for s in p["in_specs"]],
        out_specs=[s for p in parts for s in p["out_specs"]],
        input_output_aliases=aliases,
        scratch_shapes=[s for p in parts for s in p["scratch"]],
        compiler_params=_params(len(grid), name),
        name=name,
    )(*[a for p in parts for a in p["args"]])
    split, o = [], 0
    for p in parts:
        split.append(list(outs[o:o + len(p["out_shape"])]))
        o += len(p["out_shape"])
    return split


def _in_proj_kernel(x_ref, mod_ref, g_ref, w_ref, ret_ref, hy_ref, gqa_ref, mla_ref, w_scr):
    @pl.when(pl.program_id(0) == 0)
    def _():
        g0 = RET_IN + HY_IN
        m0 = g0 + GQA_IN
        kr0 = m0 + MLA_Q_RANK + MLA_KV_RANK
        step = 2 * LANES
        for r0 in list(range(0, g0, step)) + list(range(g0 + GQA_Q_HEADS * HEAD_DIM, kr0, LANES)):
            n = step if r0 < g0 else LANES
            w_scr[r0:r0 + n, :] = w_ref[0, r0:r0 + n, :].astype(BF16)
        for dst, j in enumerate(GQA_HEAD_ORDER):
            w_scr[g0 + dst * HEAD_DIM:g0 + (dst + 1) * HEAD_DIM, :] = (
                w_ref[0, g0 + j * HEAD_DIM:g0 + (j + 1) * HEAD_DIM, :].astype(BF16))
        kr = w_ref[0, kr0:kr0 + MLA_ROPE, :].astype(BF16)
        for hd in range(MLA_HEADS):
            w_scr[kr0 + hd * MLA_ROPE:kr0 + (hd + 1) * MLA_ROPE, :] = kr

    shift = mod_ref[0, 0, 0:1, :]
    scale = mod_ref[0, 0, 1:2, :]
    h = (_rms(x_ref[...], g_ref[0, 0:1, :]) * (1.0 + scale) + shift).astype(BF16)
    c0 = 0
    for ref in (ret_ref, hy_ref, gqa_ref, mla_ref):
        width = ref.shape[1]
        ref[...] = _dot_nt(h, w_scr[c0:c0 + width, :])
        c0 += width


def _in_proj(x, mod, norm_g, w_in, layer, tm, cond_of_tile):
    rows = x.shape[0]
    widths = (RET_IN, HY_IN, GQA_IN, MLA_IN_TILED)
    return pl.pallas_call(
        _in_proj_kernel,
        out_shape=[jax.ShapeDtypeStruct((rows, w), F32) for w in widths],
        grid=(rows // tm,),
        in_specs=[
            pl.BlockSpec((tm, D_MODEL), lambda i: (i, 0)),
            pl.BlockSpec((1, 1, 6, D_MODEL), lambda i: (layer, cond_of_tile(i), 0, 0)),
            pl.BlockSpec((1, 4, D_MODEL), lambda i: (layer, 0, 0)),
            pl.BlockSpec((1, IN_WIDTH, D_MODEL), lambda i: (layer, 0, 0), pipeline_mode=pl.Buffered(1)),
        ],
        out_specs=[pl.BlockSpec((tm, w), lambda i: (i, 0)) for w in widths],
        scratch_shapes=[pltpu.VMEM((IN_WIDTH_TILED, D_MODEL), BF16)],
        compiler_params=_params(1, "in_proj"),
        name="in_proj",
    )(x, mod, norm_g, w_in)


def _channel_kernel(x_ref, m0_ref, m1_ref, m2_ref, m3_ref, mod_ref, g_ref, wo_ref, wu_ref, cw_ref, cb_ref,
                    wd_ref, out_ref, act_ref, *, seq_len):
    y = None
    for i, m_ref in enumerate((m0_ref, m1_ref, m2_ref, m3_ref)):
        r0 = i * MIX_PART
        if i == 2:
            w = jnp.concatenate([wo_ref[0, r0 + j * HEAD_DIM:r0 + (j + 1) * HEAD_DIM, :]
                                 for j in GQA_HEAD_ORDER], axis=0)
        else:
            w = wo_ref[0, r0:r0 + MIX_PART, :]
        part = _dot(m_ref[...], w)
        y = part if y is None else y + part
    gate1 = mod_ref[0, 0, 2:3, :]
    shift2 = mod_ref[0, 0, 3:4, :]
    scale2 = mod_ref[0, 0, 4:5, :]
    xm = x_ref[...] + gate1 * _rms(y, g_ref[0, 1:2, :])
    out_ref[...] = xm
    h2 = (_rms(xm, g_ref[0, 2:3, :]) * (1.0 + scale2) + shift2).astype(BF16)
    for c0 in range(0, D_FF, FFN_CHUNK):
        sl = slice(c0, min(c0 + FFN_CHUNK, D_FF))
        gate = _dot(h2, wu_ref[0, :, sl])
        up = _dot(h2, wu_ref[0, :, D_FF + sl.start:D_FF + sl.stop])
        prev, nxt = _shift_rows(gate, seq_len)
        gate = (prev * cw_ref[0, 0:1, sl] + gate * cw_ref[0, 1:2, sl] + nxt * cw_ref[0, 2:3, sl]
                + cb_ref[0, :, sl])
        act_ref[:, sl] = (_silu(gate) * up).astype(BF16)
    ffn = _dot(act_ref[...], wd_ref[0])
    gate2 = mod_ref[0, 0, 5:6, :]
    out_ref[...] = out_ref[...] + gate2 * _rms(ffn, g_ref[0, 3:4, :])


def _channel_mix(x, mixes, mod, norm_g, w_out, w_up, conv_w, conv_b, w_down, layer, tm, seq_len, cond_of_tile):
    rows = x.shape[0]
    resident = dict(pipeline_mode=pl.Buffered(1))
    return pl.pallas_call(
        functools.partial(_channel_kernel, seq_len=seq_len),
        out_shape=jax.ShapeDtypeStruct((rows, D_MODEL), F32),
        grid=(rows // tm,),
        in_specs=[pl.BlockSpec((tm, D_MODEL), lambda i: (i, 0))]
        + [pl.BlockSpec((tm, MIX_PART), lambda i: (i, 0))] * 4
        + [
            pl.BlockSpec((1, 1, 6, D_MODEL), lambda i: (layer, cond_of_tile(i), 0, 0)),
            pl.BlockSpec((1, 4, D_MODEL), lambda i: (layer, 0, 0)),
            pl.BlockSpec((1, D_MODEL, D_MODEL), lambda i: (layer, 0, 0), **resident),
            pl.BlockSpec((1, D_MODEL, 2 * D_FF), lambda i: (layer, 0, 0), **resident),
            pl.BlockSpec((1, 3, D_FF), lambda i: (layer, 0, 0)),
            pl.BlockSpec((1, 1, D_FF), lambda i: (layer, 0, 0)),
            pl.BlockSpec((1, D_FF, D_MODEL), lambda i: (layer, 0, 0), **resident),
        ],
        out_specs=pl.BlockSpec((tm, D_MODEL), lambda i: (i, 0)),
        scratch_shapes=[pltpu.VMEM((tm, D_FF), BF16)],
        compiler_params=_params(1, "channel_mix"),
        name="channel_mix",
    )(x, *mixes, mod, norm_g, w_out, w_up, conv_w, conv_b, w_down)


def _log_gamma(dl_ref):
    dl = dl_ref[0]
    return jnp.minimum(dl, 0.0) - jnp.log(1.0 + jnp.exp(-jnp.abs(dl)))


def _pair_decay(lg, t, r0, bq, seq_len):
    top = lax.broadcasted_iota(jnp.int32, (2 * bq, 1), 0) < bq
    lgf = jnp.where(top, lg[0:1, 2 * t:2 * t + 1], lg[0:1, 2 * t + 1:2 * t + 2])
    lgb = jnp.where(top, lg[1:2, 2 * t:2 * t + 1], lg[1:2, 2 * t + 1:2 * t + 2])
    rowf = ((lax.broadcasted_iota(jnp.int32, (2 * bq, 1), 0) & (bq - 1)) + r0).astype(F32)
    colf = lax.broadcasted_iota(jnp.int32, (1, seq_len), 1).astype(F32)
    lag = rowf - colf
    decay = jnp.exp(jnp.where(lag >= 0.0, lag * lgf, -lag * lgb))
    return jnp.where(lag == 0.0, 2.0, decay)


def _head_norm(o, lo):
    inv = 1.0 / HEAD_DIM
    s_all = jnp.sum(o, axis=-1, keepdims=True)
    s_lo = jnp.sum(jnp.where(lo, o, 0.0), axis=-1, keepdims=True)
    d = o - jnp.where(lo, s_lo, s_all - s_lo) * inv
    d2 = d * d
    v_all = jnp.sum(d2, axis=-1, keepdims=True)
    v_lo = jnp.sum(jnp.where(lo, d2, 0.0), axis=-1, keepdims=True)
    return d * lax.rsqrt(jnp.where(lo, v_lo, v_all - v_lo) * inv + EPS)


def _ret_ctx_kernel(*refs, seq_len, nb):
    x_ref, dl_ref, gn_ref, *_, y_ref, st_ref, dec_scr, kdec_scr = refs
    lo = _lane_iota() < HEAD_DIM
    _zero_other_layers(st_ref)

    @pl.when(pl.program_id(0) == 0)
    def _():
        lg = _log_gamma(dl_ref)
        posf = lax.broadcasted_iota(jnp.int32, (seq_len, 1), 0).astype(F32)
        lgf = _per_head([lg[0:1, h:h + 1] for h in range(RET_HEADS)], RET_WIDTH)
        lgb = _per_head([lg[1:2, h:h + 1] for h in range(RET_HEADS)], RET_WIDTH)
        kdec_scr[0] = jnp.exp((seq_len - 1.0 - posf) * lgf)
        kdec_scr[1] = jnp.exp(posf * lgb)
        for t in range(RET_HEADS // 2):
            dec_scr[t] = _pair_decay(lg, t, 0, seq_len, seq_len)

    for i in range(nb):
        rows = slice(i * seq_len, (i + 1) * seq_len)
        for t in range(RET_HEADS // 2):
            cs = slice(t * LANES, (t + 1) * LANES)
            q = x_ref[rows, cs]
            k = x_ref[rows, RET_WIDTH + t * LANES:RET_WIDTH + (t + 1) * LANES] * (HEAD_DIM ** -0.5)
            vb = x_ref[rows, 2 * RET_WIDTH + t * LANES:2 * RET_WIDTH + (t + 1) * LANES].astype(BF16)
            gate = x_ref[rows, 3 * RET_WIDTH + t * LANES:3 * RET_WIDTH + (t + 1) * LANES]
            s = _dot_nt(_split_heads(q, lo).astype(BF16), k.astype(BF16)) * dec_scr[t]
            o = _merge_heads(_dot(s.astype(BF16), vb), lo)
            y = _silu(gate) * (_head_norm(o, lo) * gn_ref[0, :, cs])
            y_ref[rows, cs] = y.astype(BF16)
            for d in range(2):
                st = _dot_tn((k * kdec_scr[d, :, cs]).astype(BF16), vb)
                st_ref[i, 0, d, 2 * t] = st[0:HEAD_DIM, 0:HEAD_DIM]
                st_ref[i, 0, d, 2 * t + 1] = st[HEAD_DIM:LANES, HEAD_DIM:LANES]


def _ret_lat_kernel(x_ref, dl_ref, gn_ref, s0_ref, y_ref, *, seq_len, bq):
    lo = _lane_iota() < HEAD_DIM
    lg = _log_gamma(dl_ref)
    zero = jnp.zeros((HEAD_DIM, HEAD_DIM), F32)
    for t in range(RET_HEADS // 2):
        cs = slice(t * LANES, (t + 1) * LANES)
        kb = (x_ref[:, RET_WIDTH + t * LANES:RET_WIDTH + (t + 1) * LANES] * (HEAD_DIM ** -0.5)).astype(BF16)
        vb = x_ref[:, 2 * RET_WIDTH + t * LANES:2 * RET_WIDTH + (t + 1) * LANES].astype(BF16)
        lgf = _per_head([lg[0:1, 2 * t:2 * t + 1], lg[0:1, 2 * t + 1:2 * t + 2]], LANES)
        lgb = _per_head([lg[1:2, 2 * t:2 * t + 1], lg[1:2, 2 * t + 1:2 * t + 2]], LANES)
        s0 = []
        for d in range(2):
            a = s0_ref[0, 0, d, 2 * t]
            b = s0_ref[0, 0, d, 2 * t + 1]
            s0.append(jnp.concatenate([jnp.concatenate([a, zero], axis=1),
                                       jnp.concatenate([zero, b], axis=1)], axis=0).astype(BF16))
        for r0 in range(0, seq_len, bq):
            q = x_ref[r0:r0 + bq, cs]
            s = _dot_nt(_split_heads(q, lo).astype(BF16), kb) * _pair_decay(lg, t, r0, bq, seq_len)
            o = _merge_heads(_dot(s.astype(BF16), vb), lo)
            rowf = (lax.broadcasted_iota(jnp.int32, (bq, 1), 0) + r0).astype(F32)
            qf = (q * jnp.exp((rowf + 1.0) * lgf)).astype(BF16)
            qr = (q * jnp.exp((seq_len - rowf) * lgb)).astype(BF16)
            o = o + _dot(qf, s0[0]) + _dot(qr, s0[1])
            gate = x_ref[r0:r0 + bq, 3 * RET_WIDTH + t * LANES:3 * RET_WIDTH + (t + 1) * LANES]
            y = _silu(gate) * (_head_norm(o, lo) * gn_ref[0, :, cs])
            y_ref[r0:r0 + bq, cs] = y.astype(BF16)


def _retention(ret_in, decay_logit, gn_g, state0, carried, layer, n_batch, seq_len, nb):
    latent = state0 is not None
    in_specs = [
        pl.BlockSpec((nb * seq_len, RET_IN), lambda b: (b, 0)),
        pl.BlockSpec((1, 2, RET_HEADS), lambda b: (layer, 0, 0)),
        pl.BlockSpec((1, 1, RET_WIDTH), lambda b: (layer, 0, 0)),
    ]
    args = [ret_in, decay_logit, gn_g]
    y_shape = jax.ShapeDtypeStruct((n_batch * seq_len, MIX_PART), BF16)
    y_spec = pl.BlockSpec((nb * seq_len, MIX_PART), lambda b: (b, 0))
    if latent:
        assert nb == 1
        kern = functools.partial(_ret_lat_kernel, seq_len=seq_len, bq=256)
        in_specs.append(pl.BlockSpec((1, 1, 2, RET_HEADS, HEAD_DIM, HEAD_DIM),
                                     lambda b: (b, layer, 0, 0, 0, 0)))
        args.append(state0)
        out_shape, out_specs, aliases, scratch = y_shape, y_spec, {}, []
    else:
        kern = functools.partial(_ret_ctx_kernel, seq_len=seq_len, nb=nb)
        st_shape, st_spec = _layer_stacked((2, RET_HEADS, HEAD_DIM, HEAD_DIM), n_batch, layer, nb)
        out_shape = [y_shape, st_shape]
        out_specs = [y_spec, st_spec]
        aliases = _carry_specs(in_specs, args, carried, 1)
        scratch = [pltpu.VMEM((RET_HEADS // 2, 2 * seq_len, seq_len), F32),
                   pltpu.VMEM((2, seq_len, RET_WIDTH), F32)]
    return _part(kern, in_specs, args, out_shape, out_specs, scratch, aliases)


def _hyena_kernel(x_ref, sw_ref, sb_ref, bias_ref, tab_ref, f_ref, ft_ref, y_ref, *, seq_len, nb):
    for i in range(nb):
        rows = slice(i * seq_len, (i + 1) * seq_len)
        x = x_ref[rows, :]
        prev, nxt = _shift_rows(x, seq_len)
        u = prev * sw_ref[0, 0:1, :] + x * sw_ref[0, 1:2, :] + nxt * sw_ref[0, 2:3, :] + sb_ref[0]
        x1 = u[:, 0:HY_WIDTH]
        x2 = u[:, HY_WIDTH:2 * HY_WIDTH]
        z = u[:, 2 * HY_WIDTH:3 * HY_WIDTH]
        for o, gate in enumerate((x1, x2)):
            spec = _dot(f_ref[...], z.astype(BF16))
            s_re = spec[:seq_len]
            s_im = spec[seq_len:]
            a = tab_ref[0, 3 * o]
            a_ny = tab_ref[0, 3 * o + 1]
            b = tab_ref[0, 3 * o + 2]
            y_re = (s_re * a - s_im * b).astype(BF16)
            y_im = (s_re * b + s_im * a_ny).astype(BF16)
            conv = _dot(ft_ref[:, 0:seq_len], y_re) + _dot(ft_ref[:, seq_len:2 * seq_len], y_im)
            z = gate * (conv + z * bias_ref[0, o:o + 1, :])
        y_ref[rows, :] = z.astype(BF16)


def _hyena(hy_in, short_w, short_b, hy_bias, tables, fmat, fmat_t, layer, n_batch, seq_len, nb):
    const = dict(pipeline_mode=pl.Buffered(1))
    return _part(
        functools.partial(_hyena_kernel, seq_len=seq_len, nb=nb),
        [
            pl.BlockSpec((nb * seq_len, HY_IN), lambda b: (b, 0)),
            pl.BlockSpec((1, 3, HY_IN), lambda b: (layer, 0, 0)),
            pl.BlockSpec((1, 1, HY_IN), lambda b: (layer, 0, 0)),
            pl.BlockSpec((1, HY_ORDER, HY_WIDTH), lambda b: (layer, 0, 0)),
            pl.BlockSpec((1, 3 * HY_ORDER, seq_len, HY_WIDTH), lambda b: (layer, 0, 0, 0), **const),
            pl.BlockSpec((2 * seq_len, seq_len), lambda b: (0, 0), **const),
            pl.BlockSpec((seq_len, 2 * seq_len), lambda b: (0, 0), **const),
        ],
        [hy_in, short_w, short_b, hy_bias, tables, fmat, fmat_t],
        jax.ShapeDtypeStruct((n_batch * seq_len, MIX_PART), BF16),
        pl.BlockSpec((nb * seq_len, MIX_PART), lambda b: (b, 0)))


def _gqa_ctx_kernel(*refs, seq_len, nb):
    x_ref, sink_ref, *_, y_ref, k_out_ref, v_out_ref = refs
    nq = GQA_Q_HEADS * HEAD_DIM
    scale = HEAD_DIM ** -0.5
    head_a = lax.broadcasted_iota(jnp.int32, (LANES, 1), 0) < HEAD_DIM
    first = lax.broadcasted_iota(jnp.int32, (1, 2 * seq_len), 1) < seq_len
    ones = jnp.ones((LANES, seq_len), BF16)
    _zero_other_layers(k_out_ref, v_out_ref)
    for i in range(nb):
        rows = slice(i * seq_len, (i + 1) * seq_len)
        k = x_ref[rows, nq:nq + LANES]
        k_t = k.T
        v_t = x_ref[rows, nq + LANES:nq + 2 * LANES].T
        for kv in range(GQA_KV_HEADS):
            k_out_ref[i, 0, kv] = k_t[kv * HEAD_DIM:(kv + 1) * HEAD_DIM, :]
            v_out_ref[i, 0, kv] = v_t[kv * HEAD_DIM:(kv + 1) * HEAD_DIM, :]
        kb = k.astype(BF16)
        v_ext = jnp.concatenate([v_t.astype(BF16), ones], axis=0)
        for g in range(GQA_GROUPS):
            cs = slice(g * LANES, (g + 1) * LANES)
            q_t = x_ref[rows, cs].T
            q2 = jnp.concatenate([jnp.where(head_a, q_t, 0.0), jnp.where(head_a, 0.0, q_t)], axis=1)
            s = _dot(kb, q2.astype(BF16)) * scale
            sink = jnp.where(first, sink_ref[0, :, g:g + 1], sink_ref[0, :, GQA_GROUPS + g:GQA_GROUPS + g + 1])
            m = jnp.maximum(jnp.max(s, axis=0, keepdims=True), sink)
            pv = _dot(v_ext, jnp.exp(s - m).astype(BF16))
            o = pv[0:LANES, :] / (pv[LANES:LANES + 1, :] + jnp.exp(sink - m))
            o = jnp.where(head_a, o[:, 0:seq_len], o[:, seq_len:2 * seq_len])
            y_ref[rows, cs] = o.T.astype(BF16)


def _gqa_ctx(gqa_in, sink, carried, layer, n_batch, seq_len, nb):
    kv_shape, kv_spec = _layer_stacked((GQA_KV_HEADS, HEAD_DIM, seq_len), n_batch, layer, nb)
    in_specs = [
        pl.BlockSpec((nb * seq_len, GQA_IN), lambda b: (b, 0)),
        pl.BlockSpec((1, 1, GQA_Q_HEADS), lambda b: (layer, 0, 0)),
    ]
    args = [gqa_in, sink]
    aliases = _carry_specs(in_specs, args, carried, 1)
    return _part(
        functools.partial(_gqa_ctx_kernel, seq_len=seq_len, nb=nb), in_specs, args,
        [jax.ShapeDtypeStruct((n_batch * seq_len, MIX_PART), BF16), kv_shape, kv_shape],
        [pl.BlockSpec((nb * seq_len, MIX_PART), lambda b: (b, 0)), kv_spec, kv_spec],
        aliases=aliases)


def _gqa_win_kernel(x_ref, sink_ref, kct_ref, vct_ref, cos_ref, sin_ref, y_ref,
                    q_scr, k_scr, vt_scr, kc_scr, vct_scr, *, seq_len):
    nq = GQA_Q_HEADS * HEAD_DIM
    scale = HEAD_DIM ** -0.5
    head_a = lax.broadcasted_iota(jnp.int32, (LANES, 1), 0) < HEAD_DIM
    first = lax.broadcasted_iota(jnp.int32, (1, 2 * BLOCK), 1) < BLOCK
    cos = cos_ref[...]
    sin = sin_ref[...]

    def rope(t):
        return t * cos + _rot_half(t, HEAD_DIM // 2) * sin

    for g in range(GQA_GROUPS):
        q_t = rope(x_ref[:, g * LANES:(g + 1) * LANES]).T
        q_scr[g, 0] = jnp.where(head_a, q_t, 0.0).astype(BF16)
        q_scr[g, 1] = jnp.where(head_a, 0.0, q_t).astype(BF16)
    zeros = jnp.zeros((BLOCK, LANES), BF16)
    for r0 in (0, BLOCK + seq_len):
        k_scr[r0:r0 + BLOCK, :] = zeros
        vt_scr[:, r0:r0 + BLOCK] = zeros
    k_scr[BLOCK:BLOCK + seq_len, :] = rope(x_ref[:, nq:nq + LANES]).astype(BF16)
    vt_scr[:, BLOCK:BLOCK + seq_len] = x_ref[:, nq + LANES:nq + 2 * LANES].T.astype(BF16)
    kc_scr[...] = jnp.concatenate([kct_ref[0, 0, 0], kct_ref[0, 0, 1]], axis=0).T.astype(BF16)
    vct_scr[...] = jnp.concatenate([vct_ref[0, 0, 0], vct_ref[0, 0, 1]], axis=0).astype(BF16)

    jj = lax.broadcasted_iota(jnp.int32, (3 * BLOCK, 2 * BLOCK), 0)
    ii = lax.broadcasted_iota(jnp.int32, (3 * BLOCK, 2 * BLOCK), 1) & (BLOCK - 1)
    band = (jj >= ii) & (jj <= ii + 2 * WINDOW)
    for n in range(seq_len // BLOCK):
        r0 = n * BLOCK
        kpos = jj + (n - 1) * BLOCK
        valid = band & (kpos >= 0) & (kpos < seq_len)
        kw = k_scr[r0:r0 + 3 * BLOCK, :]
        vw = vt_scr[:, r0:r0 + 3 * BLOCK]
        for g in range(GQA_GROUPS):
            q2 = jnp.concatenate([q_scr[g, 0, :, r0:r0 + BLOCK], q_scr[g, 1, :, r0:r0 + BLOCK]], axis=1)
            s_win = jnp.where(valid, _dot(kw, q2) * scale, NEG_INF)
            s_ctx = _dot(kc_scr[...], q2) * scale
            sink = jnp.where(first, sink_ref[0, :, g:g + 1], sink_ref[0, :, GQA_GROUPS + g:GQA_GROUPS + g + 1])
            m = jnp.maximum(jnp.maximum(jnp.max(s_win, axis=0, keepdims=True),
                                        jnp.max(s_ctx, axis=0, keepdims=True)), sink)
            p_win = jnp.exp(s_win - m)
            p_ctx = jnp.exp(s_ctx - m)
            den = (jnp.sum(p_win, axis=0, keepdims=True) + jnp.sum(p_ctx, axis=0, keepdims=True)
                   + jnp.exp(sink - m))
            o = (_dot(vw, p_win.astype(BF16)) + _dot(vct_scr[...], p_ctx.astype(BF16))) / den
            o = jnp.where(head_a, o[:, 0:BLOCK], o[:, BLOCK:2 * BLOCK])
            y_ref[r0:r0 + BLOCK, g * LANES:(g + 1) * LANES] = o.T.astype(BF16)


def _gqa_win(gqa_in, sink, cache_k_t, cache_v_t, cos, sin, layer, n_batch, seq_len):
    past = cache_k_t.shape[4]
    return _part(
        functools.partial(_gqa_win_kernel, seq_len=seq_len),
        [
            pl.BlockSpec((seq_len, GQA_IN), lambda b: (b, 0)),
            pl.BlockSpec((1, 1, GQA_Q_HEADS), lambda b: (layer, 0, 0)),
            pl.BlockSpec((1, 1, GQA_KV_HEADS, HEAD_DIM, past), lambda b: (b, layer, 0, 0, 0)),
            pl.BlockSpec((1, 1, GQA_KV_HEADS, HEAD_DIM, past), lambda b: (b, layer, 0, 0, 0)),
            pl.BlockSpec((seq_len, LANES), lambda b: (0, 0)),
            pl.BlockSpec((seq_len, LANES), lambda b: (0, 0)),
        ],
        [gqa_in, sink, cache_k_t, cache_v_t, cos, sin],
        jax.ShapeDtypeStruct((n_batch * seq_len, MIX_PART), BF16),
        pl.BlockSpec((seq_len, MIX_PART), lambda b: (b, 0)),
        scratch=[
            pltpu.VMEM((GQA_GROUPS, 2, LANES, seq_len), BF16),
            pltpu.VMEM((seq_len + 2 * BLOCK, LANES), BF16),
            pltpu.VMEM((LANES, seq_len + 2 * BLOCK), BF16),
            pltpu.VMEM((past, LANES), BF16),
            pltpu.VMEM((LANES, past), BF16),
        ])


def _mla_kernel(*refs, seq_len, latent, bq, nb):
    if latent:
        (x_ref, qn_ref, kvn_ref, wq_ref, wk_ref, wv_ref, ckv_c_ref, kr_c_ref, cos_ref, sin_ref,
         y_ref, q_all, k_all, v_all) = refs
        past = ckv_c_ref.shape[2]
    else:
        (x_ref, qn_ref, kvn_ref, wq_ref, wk_ref, wv_ref, *_,
         y_ref, ckv_out_ref, kr_out_ref, q_all, k_all, v_all) = refs
        past = 0
        _zero_other_layers(ckv_out_ref, kr_out_ref)
    keys = past + seq_len
    nope_w = MLA_HEADS * MLA_NOPE
    scale = (MLA_NOPE + MLA_ROPE) ** -0.5
    lane = _lane_iota()
    lo = lane < HEAD_DIM
    rope_lanes = [(lane >= h * MLA_ROPE) & (lane < (h + 1) * MLA_ROPE) for h in range(MLA_HEADS)]
    for i in range(nb):
        rows = slice(i * seq_len, (i + 1) * seq_len)
        q_scr, k_scr, v_scr = q_all.at[i], k_all.at[i], v_all.at[i]
        q_lat = x_ref[rows, 0:MLA_Q_RANK]
        kv_lat = x_ref[rows, MLA_Q_RANK:MLA_Q_RANK + MLA_KV_RANK]
        k_rope = x_ref[rows, MLA_Q_RANK + MLA_KV_RANK:MLA_IN_TILED]
        mq = _dot(_rms(q_lat, qn_ref[0]).astype(BF16), wq_ref[0])
        ckv = _rms(kv_lat, kvn_ref[0])
        q_rope = mq[:, nope_w:nope_w + LANES]
        if latent:
            cos = cos_ref[...]
            sin = sin_ref[...]
            q_rope = q_rope * cos + _rot_half(q_rope, MLA_ROPE // 2) * sin
            k_rope = k_rope * cos + _rot_half(k_rope, MLA_ROPE // 2) * sin
            ckv_c = ckv_c_ref[0, 0].astype(BF16)
            k_scr[0:past, 0:nope_w] = _dot(ckv_c, wk_ref[0]).astype(BF16)
            v_scr[0:past, :] = _dot(ckv_c, wv_ref[0]).astype(BF16)
            kr_c = kr_c_ref[0, 0]
            k_scr[0:past, nope_w:nope_w + LANES] = jnp.concatenate([kr_c] * MLA_HEADS, axis=1).astype(BF16)
        else:
            ckv_out_ref[i, 0] = ckv
            kr_out_ref[i, 0] = k_rope.T[0:MLA_ROPE, :]
        q_scr[:, 0:nope_w] = mq[:, 0:nope_w]
        q_scr[:, nope_w:nope_w + LANES] = q_rope
        ckv_b = ckv.astype(BF16)
        k_scr[past:keys, 0:nope_w] = _dot(ckv_b, wk_ref[0]).astype(BF16)
        v_scr[past:keys, :] = _dot(ckv_b, wv_ref[0]).astype(BF16)
        k_scr[past:keys, nope_w:nope_w + LANES] = k_rope.astype(BF16)

        ones = jnp.ones((keys, LANES), BF16)
        kr_all = k_scr[:, nope_w:nope_w + LANES]
        for t in range(MLA_HEADS // 2):
            cs = slice(t * LANES, (t + 1) * LANES)
            k_cat = jnp.concatenate([k_scr[:, cs], kr_all], axis=1)
            v_ext = jnp.concatenate([v_scr[:, cs], ones], axis=1)
            for r0 in range(0, seq_len, bq):
                qn = q_scr[r0:r0 + bq, cs]
                qr = q_scr[r0:r0 + bq, nope_w:nope_w + LANES]
                q2 = jnp.concatenate([
                    jnp.concatenate([jnp.where(lo, qn, 0.0), jnp.where(rope_lanes[2 * t], qr, 0.0)], axis=1),
                    jnp.concatenate([jnp.where(lo, 0.0, qn), jnp.where(rope_lanes[2 * t + 1], qr, 0.0)], axis=1),
                ], axis=0).astype(BF16)
                s = _dot_nt(q2, k_cat) * scale
                m = jnp.max(s, axis=-1, keepdims=True)
                pv = _dot(jnp.exp(s - m).astype(BF16), v_ext)
                o = pv[:, 0:LANES] / pv[:, LANES:LANES + 1]
                y_ref[i * seq_len + r0:i * seq_len + r0 + bq, cs] = _merge_heads(o, lo).astype(BF16)


def _mla(mla_in, q_norm, kv_norm, w_uq, w_uk, w_uv, cache, rope, carried, layer, n_batch, seq_len, nb):
    latent = cache is not None
    aliases = {}
    past = cache[0].shape[2] if latent else 0
    kern = functools.partial(_mla_kernel, seq_len=seq_len, latent=latent, bq=256, nb=nb)
    q_w = MLA_HEADS * (MLA_NOPE + MLA_ROPE)
    kv_w = MLA_HEADS * MLA_NOPE
    in_specs = [
        pl.BlockSpec((nb * seq_len, MLA_IN_TILED), lambda b: (b, 0)),
        pl.BlockSpec((1, 1, MLA_Q_RANK), lambda b: (layer, 0, 0)),
        pl.BlockSpec((1, 1, MLA_KV_RANK), lambda b: (layer, 0, 0)),
        pl.BlockSpec((1, MLA_Q_RANK, q_w), lambda b: (layer, 0, 0)),
        pl.BlockSpec((1, MLA_KV_RANK, kv_w), lambda b: (layer, 0, 0)),
        pl.BlockSpec((1, MLA_KV_RANK, kv_w), lambda b: (layer, 0, 0)),
    ]
    args = [mla_in, q_norm, kv_norm, w_uq, w_uk, w_uv]
    y_shape = jax.ShapeDtypeStruct((n_batch * seq_len, MIX_PART), BF16)
    y_spec = pl.BlockSpec((nb * seq_len, MIX_PART), lambda b: (b, 0))
    if latent:
        assert nb == 1
        in_specs += [
            pl.BlockSpec((1, 1, past, MLA_KV_RANK), lambda b: (b, layer, 0, 0)),
            pl.BlockSpec((1, 1, past, MLA_ROPE), lambda b: (b, layer, 0, 0)),
            pl.BlockSpec((seq_len, LANES), lambda b: (0, 0)),
            pl.BlockSpec((seq_len, LANES), lambda b: (0, 0)),
        ]
        args += [cache[0], cache[1], rope[0], rope[1]]
        out_shape, out_specs = y_shape, y_spec
    else:
        ckv_shape, ckv_spec = _layer_stacked((seq_len, MLA_KV_RANK), n_batch, layer, nb)
        kr_shape, kr_spec = _layer_stacked((MLA_ROPE, seq_len), n_batch, layer, nb)
        out_shape = [y_shape, ckv_shape, kr_shape]
        out_specs = [y_spec, ckv_spec, kr_spec]
        aliases = _carry_specs(in_specs, args, carried, 1)
    keys = past + seq_len
    return _part(
        kern, in_specs, args, out_shape, out_specs,
        scratch=[
            pltpu.VMEM((nb, seq_len, q_w), F32),
            pltpu.VMEM((nb, keys, kv_w + LANES), BF16),
            pltpu.VMEM((nb, keys, kv_w), BF16),
        ],
        aliases=aliases)


def _dft_matrix(seq_len):
    k = np.arange(seq_len, dtype=np.float64)[:, None]
    j = np.arange(seq_len, dtype=np.float64)[None, :]
    ang = np.pi * k * j / seq_len
    re = np.cos(ang)
    im = -np.sin(ang)
    im[0, :] = np.where(np.arange(seq_len) % 2 == 0, 1.0, -1.0)
    return np.concatenate([re, im], axis=0).astype(np.float32)


def _hyena_positions(seq_len):
    t = np.arange(seq_len, dtype=np.float64) / seq_len
    bands = np.arange(1, HY_BANDS + 1, dtype=np.float64)
    ang = 2.0 * math.pi * t[:, None] * bands
    z = np.concatenate([t[:, None], np.cos(ang), np.sin(ang)], axis=-1)
    return np.pad(z, ((0, 0), (0, HY_POS_PAD - HY_POS_DIM))).astype(np.float32)


def _rope_tables(n_tokens, dim):
    rows = np.repeat(np.arange(n_tokens // GRID_W, dtype=np.float64), GRID_W)
    cols = np.tile(np.arange(GRID_W, dtype=np.float64), n_tokens // GRID_W)
    quarter = dim // 4
    inv = ROPE_BASE ** (-np.arange(quarter, dtype=np.float64) / quarter)
    ang = np.concatenate([rows[:, None] * inv, cols[:, None] * inv], axis=-1)
    cos = np.concatenate([np.cos(ang), np.cos(ang)], axis=-1)
    sin = np.concatenate([-np.sin(ang), np.sin(ang)], axis=-1)
    reps = LANES // dim
    return (np.tile(cos, (1, reps)).astype(np.float32), np.tile(sin, (1, reps)).astype(np.float32))


def kernel(x_prompt, x_sample, state_ret, cache_gqa_k, cache_gqa_v, cache_mla_ckv, cache_mla_krope, c, c_ctx, ada_w, ada_b, norm_g, w_in, ret_decay_logit, ret_gn_g, hy_short_w, hy_short_b, hy_w1, hy_b1, hy_w2, hy_b2, hy_w3, hy_decay, hy_bias, gqa_sink, mla_q_norm, mla_kv_norm, mla_w_uq, mla_w_uk, mla_w_uv, w_out, ffn_w_up, ffn_conv_w, ffn_conv_b, ffn_w_down):
    n_p, len_p, _ = x_prompt.shape
    n_s, len_s, _ = x_sample.shape

    w_in_t = jnp.swapaxes(w_in, 1, 2)
    cache_k_t = jnp.swapaxes(cache_gqa_k, -1, -2)
    cache_v_t = jnp.swapaxes(cache_gqa_v, -1, -2)
    w_out_b = w_out.astype(BF16)
    w_up_b = ffn_w_up.astype(BF16)
    w_down_b = ffn_w_down.astype(BF16)
    uq = mla_w_uq.reshape(DEPTH, MLA_Q_RANK, MLA_HEADS, MLA_NOPE + MLA_ROPE)
    w_uq_b = jnp.concatenate(
        [uq[..., :MLA_NOPE].reshape(DEPTH, MLA_Q_RANK, MLA_HEADS * MLA_NOPE),
         uq[..., MLA_NOPE:].reshape(DEPTH, MLA_Q_RANK, MLA_HEADS * MLA_ROPE)], axis=-1).astype(BF16)
    w_uk_b = mla_w_uk.astype(BF16)
    w_uv_b = mla_w_uv.astype(BF16)

    cond = jnp.concatenate([c_ctx[None], c, jnp.zeros((N_COND - 1 - n_s, D_MODEL), F32)], axis=0)
    mod = _modulation(cond, ada_w, ada_b).reshape(DEPTH, N_COND, 6, D_MODEL)

    w1p = jnp.pad(hy_w1, ((0, 0), (0, HY_POS_PAD - HY_POS_DIM), (0, 0)))
    b1 = hy_b1.reshape(DEPTH, 1, HY_FILTER_HIDDEN)
    b2 = hy_b2.reshape(DEPTH, 1, HY_FILTER_HIDDEN)
    groups = {}
    for seq_len in (len_p, len_s):
        f32mat = _dft_matrix(seq_len)
        fmat = jnp.asarray(f32mat).astype(BF16)
        fmat_t = jnp.asarray(np.ascontiguousarray(f32mat.T)).astype(BF16)
        tables = _hyena_tables(seq_len, jnp.asarray(_hyena_positions(seq_len)), w1p, b1, hy_w2, b2,
                               hy_w3, hy_decay, fmat)
        groups[seq_len] = (tables, fmat, fmat_t)

    rope_g = tuple(jnp.asarray(t) for t in _rope_tables(len_s, HEAD_DIM))
    rope_m = tuple(jnp.asarray(t) for t in _rope_tables(len_s, MLA_ROPE))

    gn_g = ret_gn_g.reshape(DEPTH, 1, RET_WIDTH)
    short_b = hy_short_b.reshape(DEPTH, 1, HY_IN)
    sink = gqa_sink.reshape(DEPTH, 1, GQA_Q_HEADS)
    q_norm = mla_q_norm.reshape(DEPTH, 1, MLA_Q_RANK)
    kv_norm = mla_kv_norm.reshape(DEPTH, 1, MLA_KV_RANK)
    conv_b = ffn_conv_b.reshape(DEPTH, 1, D_FF)

    xp = x_prompt.reshape(n_p * len_p, D_MODEL)
    xs = x_sample.reshape(n_s * len_s, D_MODEL)
    tm_p = 1024
    tm_s = 1024
    nb_p = 4
    nb_first = 2
    cond_p = lambda i: 0
    cond_s = lambda i: 1 + (i * tm_s) // len_s
    tm_ffn = 1024
    cond_s_ffn = lambda i: 1 + (i * tm_ffn) // len_s

    st = gkv = ckr = None
    for l in range(DEPTH):
        tables, fmat, fmat_t = groups[len_p]
        ret_in, hy_in, gqa_in, mla_in = _in_proj(xp, mod, norm_g, w_in_t, l, tm_p, cond_p)
        nb = nb_first if l == 0 else nb_p
        (y_ret, *st), (y_hy,), (y_gqa, *gkv), (y_mla, *ckr) = _run_parts([
            _retention(ret_in, ret_decay_logit, gn_g, None, st, l, n_p, len_p, nb),
            _hyena(hy_in, hy_short_w, short_b, hy_bias, tables, fmat, fmat_t, l, n_p, len_p, nb),
            _gqa_ctx(gqa_in, sink, gkv, l, n_p, len_p, nb),
            _mla(mla_in, q_norm, kv_norm, w_uq_b, w_uk_b, w_uv_b, None, None, ckr, l, n_p, len_p, nb),
        ], (n_p // nb,), "context_mixers")
        xp = _channel_mix(xp, (y_ret, y_hy, y_gqa, y_mla), mod, norm_g, w_out_b, w_up_b, ffn_conv_w, conv_b,
                          w_down_b, l, tm_ffn, len_p, cond_p)
        tables, fmat, fmat_t = groups[len_s]
        ret_in, hy_in, gqa_in, mla_in = _in_proj(xs, mod, norm_g, w_in_t, l, tm_s, cond_s)
        (y_ret,), (y_gqa,), (y_mla,) = _run_parts([
            _retention(ret_in, ret_decay_logit, gn_g, state_ret, None, l, n_s, len_s, 1),
            _gqa_win(gqa_in, sink, cache_k_t, cache_v_t, rope_g[0], rope_g[1], l, n_s, len_s),
            _mla(mla_in, q_norm, kv_norm, w_uq_b, w_uk_b, w_uv_b, (cache_mla_ckv, cache_mla_krope),
                 rope_m, None, l, n_s, len_s, 1),
        ], (n_s,), "latent_mixers")
        (y_hy,), = _run_parts([
            _hyena(hy_in, hy_short_w, short_b, hy_bias, tables, fmat, fmat_t, l, n_s, len_s, n_s),
        ], (1,), "latent_hyena")
        xs = _channel_mix(xs, (y_ret, y_hy, y_gqa, y_mla), mod, norm_g, w_out_b, w_up_b, ffn_conv_w, conv_b,
                          w_down_b, l, tm_ffn, len_s, cond_s_ffn)

    return (xp.reshape(n_p, len_p, D_MODEL), xs.reshape(n_s, len_s, D_MODEL),
            st[0], jnp.swapaxes(gkv[0], -1, -2), jnp.swapaxes(gkv[1], -1, -2),
            ckr[0], jnp.swapaxes(ckr[1], -1, -2))
```

```python
import functools
import math

import numpy as np
import jax
import jax.numpy as jnp
from jax import lax
from jax.experimental import pallas as pl
from jax.experimental.pallas import tpu as pltpu

F32 = jnp.float32
BF16 = jnp.bfloat16

D_MODEL = 1024
DEPTH = 4
GRID_W = 64
EPS = 1e-6
NEG_INF = -1e30
ROPE_BASE = 10000.0
BLOCK = 128
WINDOW = 128
HEAD_DIM = 64
LANES = 128
RET_HEADS = 4
RET_WIDTH = RET_HEADS * HEAD_DIM
HY_WIDTH = 256
HY_ORDER = 2
HY_BANDS = 8
HY_POS_DIM = 1 + 2 * HY_BANDS
HY_POS_PAD = 32
HY_FILTER_HIDDEN = 64
GQA_Q_HEADS = 4
GQA_KV_HEADS = 2
GQA_GROUPS = GQA_Q_HEADS // GQA_KV_HEADS
MLA_HEADS = 4
MLA_Q_RANK = 256
MLA_KV_RANK = 128
MLA_NOPE = 64
MLA_ROPE = 32
MLA_V = 64
D_FF = 2816
RET_IN = 4 * RET_WIDTH
HY_IN = (HY_ORDER + 1) * HY_WIDTH
GQA_IN = (GQA_Q_HEADS + 2 * GQA_KV_HEADS) * HEAD_DIM
MLA_IN = MLA_Q_RANK + MLA_KV_RANK + MLA_ROPE
MLA_IN_TILED = MLA_Q_RANK + MLA_KV_RANK + MLA_HEADS * MLA_ROPE
IN_WIDTH = RET_IN + HY_IN + GQA_IN + MLA_IN
IN_WIDTH_TILED = RET_IN + HY_IN + GQA_IN + MLA_IN_TILED
GQA_HEAD_ORDER = (0, 2, 1, 3)
MIX_PART = 256
N_COND = 8
FFN_CHUNK = 512

VMEM_LIMIT = 56 * 1024 * 1024


def _params(n_axes=1):
    return pltpu.CompilerParams(
        dimension_semantics=("arbitrary",) * n_axes, vmem_limit_bytes=VMEM_LIMIT)


def _dot(a, b):
    return jnp.dot(a, b, preferred_element_type=F32)


def _dot_nt(a, b):
    return lax.dot_general(a, b, (((1,), (1,)), ((), ())), preferred_element_type=F32)


def _dot_tn(a, b):
    return lax.dot_general(a, b, (((0,), (0,)), ((), ())), preferred_element_type=F32)


def _rms(x, g):
    return x * lax.rsqrt(jnp.mean(x * x, axis=-1, keepdims=True) + EPS) * g


def _sigmoid(x):
    return 1.0 / (1.0 + jnp.exp(-x))


def _silu(x):
    return x * _sigmoid(x)


def _shift_rows(x, seq_len):
    n = x.shape[0]
    assert seq_len & (seq_len - 1) == 0
    pos = lax.broadcasted_iota(jnp.int32, (n, 1), 0) & (seq_len - 1)
    prev = jnp.where(pos != 0, pltpu.roll(x, 1, axis=0), 0.0)
    nxt = jnp.where(pos != seq_len - 1, pltpu.roll(x, n - 1, axis=0), 0.0)
    return prev, nxt


def _rot_half(x, half):
    lane = lax.broadcasted_iota(jnp.int32, x.shape, 1) & (2 * half - 1)
    return jnp.where(lane < half, pltpu.roll(x, LANES - half, axis=1), pltpu.roll(x, half, axis=1))


def _lane_iota():
    return lax.broadcasted_iota(jnp.int32, (1, LANES), 1)


def _split_heads(t, lo):
    return jnp.concatenate([jnp.where(lo, t, 0.0), jnp.where(lo, 0.0, t)], axis=0)


def _merge_heads(o, lo):
    n = o.shape[0] // 2
    return jnp.where(lo, o[:n], o[n:])


def _per_head(vals, width):
    lane = lax.broadcasted_iota(jnp.int32, (1, width), 1)
    out = vals[-1]
    for h in range(len(vals) - 2, -1, -1):
        out = jnp.where(lane < (h + 1) * HEAD_DIM, vals[h], out)
    return out


def _mod_kernel(cond_ref, w_ref, b_ref, out_ref):
    s = _silu(cond_ref[...]).astype(BF16)
    out_ref[0] = _dot(s, w_ref[0].astype(BF16)) + b_ref[0]


def _modulation(cond, ada_w, ada_b):
    tn = 1536
    return pl.pallas_call(
        _mod_kernel,
        out_shape=jax.ShapeDtypeStruct((DEPTH, N_COND, 6 * D_MODEL), F32),
        grid=(DEPTH, 6 * D_MODEL // tn),
        in_specs=[
            pl.BlockSpec((N_COND, D_MODEL), lambda l, j: (0, 0)),
            pl.BlockSpec((1, D_MODEL, tn), lambda l, j: (l, 0, j)),
            pl.BlockSpec((1, 1, tn), lambda l, j: (l, 0, j)),
        ],
        out_specs=pl.BlockSpec((1, N_COND, tn), lambda l, j: (l, 0, j)),
        compiler_params=_params(2),
        name="modulation",
    )(cond, ada_w, ada_b.reshape(DEPTH, 1, 6 * D_MODEL))


def _filter_kernel(z_ref, w1_ref, b1_ref, w2_ref, b2_ref, w3_ref, dec_ref, f_ref, out_ref, *, seq_len):
    hi = lax.Precision.HIGHEST
    h = jnp.sin(jnp.dot(z_ref[...], w1_ref[0], precision=hi, preferred_element_type=F32) + b1_ref[0])
    h = jnp.sin(jnp.dot(h, w2_ref[0], precision=hi, preferred_element_type=F32) + b2_ref[0])
    h = jnp.dot(h, w3_ref[0], precision=hi, preferred_element_type=F32)
    row = lax.broadcasted_iota(jnp.int32, (seq_len, 1), 0)
    t = row.astype(F32) / seq_len
    dec = jnp.abs(dec_ref[0])
    win_f = jnp.exp(-t * dec[0:1, :])
    win_b = jnp.exp(-t * dec[1:2, :])
    fm = f_ref[...]
    inv = 1.0 / seq_len
    scale = jnp.where(row == 0, 0.5 * inv, inv)
    for o in range(HY_ORDER):
        base = o * 2 * HY_WIDTH
        hf = h[:, base:base + HY_WIDTH] * win_f
        hb = h[:, base + HY_WIDTH:base + 2 * HY_WIDTH] * win_b
        hbs = jnp.where(row != 0, pltpu.roll(hb, 1, axis=0), 0.0)
        sf = _dot(fm, hf.astype(BF16))
        sb = _dot(fm, hbs.astype(BF16))
        g_re = sf[:seq_len] + sb[:seq_len]
        g_im = sf[seq_len:] - sb[seq_len:]
        g_ny = sf[seq_len:] + sb[seq_len:]
        a = g_re * scale
        out_ref[0, 3 * o] = a
        out_ref[0, 3 * o + 1] = jnp.where(row == 0, g_ny * (0.5 * inv), a)
        out_ref[0, 3 * o + 2] = jnp.where(row == 0, 0.0, g_im * inv)


def _hyena_tables(seq_len, z, w1p, b1, w2, b2, w3, decay, fmat):
    hid = HY_FILTER_HIDDEN
    wide = HY_ORDER * 2 * HY_WIDTH
    return pl.pallas_call(
        functools.partial(_filter_kernel, seq_len=seq_len),
        out_shape=jax.ShapeDtypeStruct((DEPTH, 3 * HY_ORDER, seq_len, HY_WIDTH), F32),
        grid=(DEPTH,),
        in_specs=[
            pl.BlockSpec((seq_len, HY_POS_PAD), lambda l: (0, 0)),
            pl.BlockSpec((1, HY_POS_PAD, hid), lambda l: (l, 0, 0)),
            pl.BlockSpec((1, 1, hid), lambda l: (l, 0, 0)),
            pl.BlockSpec((1, hid, hid), lambda l: (l, 0, 0)),
            pl.BlockSpec((1, 1, hid), lambda l: (l, 0, 0)),
            pl.BlockSpec((1, hid, wide), lambda l: (l, 0, 0)),
            pl.BlockSpec((1, 2, HY_WIDTH), lambda l: (l, 0, 0)),
            pl.BlockSpec((2 * seq_len, seq_len), lambda l: (0, 0)),
        ],
        out_specs=pl.BlockSpec((1, 3 * HY_ORDER, seq_len, HY_WIDTH), lambda l: (l, 0, 0, 0)),
        compiler_params=_params(1),
        name=f"hyena_tables_{seq_len}",
    )(z, w1p, b1, w2, b2, w3, decay, fmat)


def _layer_stacked(tail, n_batch, layer, nb):
    zeros = (0,) * len(tail)
    n_layers = DEPTH if layer == 0 else 1
    return (jax.ShapeDtypeStruct((n_batch, DEPTH) + tail, F32),
            pl.BlockSpec((nb, n_layers) + tail, lambda b: (b, layer) + zeros))


def _zero_other_layers(*out_refs):
    for ref in out_refs:
        if ref.shape[1] > 1:
            ref[:, 1:] = jnp.zeros((ref.shape[0], ref.shape[1] - 1) + ref.shape[2:], ref.dtype)


def _carry_specs(in_specs, args, carried, first_out):
    aliases = {}
    if carried is not None:
        for k, arr in enumerate(carried):
            aliases[len(args)] = first_out + k
            in_specs.append(pl.BlockSpec(memory_space=pl.ANY))
            args.append(arr)
    return aliases


def _part(kernel, in_specs, args, out_shape, out_specs, scratch=(), aliases=None):
    as_list = lambda v: list(v) if isinstance(v, (list, tuple)) else [v]
    return dict(kernel=kernel, in_specs=list(in_specs), args=list(args), out_shape=as_list(out_shape),
                out_specs=as_list(out_specs), scratch=list(scratch), aliases=dict(aliases or {}))


def _run_parts(parts, grid, name):
    n_in = sum(len(p["args"]) for p in parts)
    n_out = sum(len(p["out_shape"]) for p in parts)

    def fused(*refs):
        i = o = s = 0
        for p in parts:
            a, b, c = len(p["args"]), len(p["out_shape"]), len(p["scratch"])
            p["kernel"](*refs[i:i + a], *refs[n_in + o:n_in + o + b],
                        *refs[n_in + n_out + s:n_in + n_out + s + c])
            i, o, s = i + a, o + b, s + c

    aliases = {}
    i = o = 0
    for p in parts:
        aliases.update({i + k: o + v for k, v in p["aliases"].items()})
        i, o = i + len(p["args"]), o + len(p["out_shape"])
    outs = pl.pallas_call(
        fused,
        out_shape=[s for p in parts for s in p["out_shape"]],
        grid=grid,
        in_specs=[s for p in parts for s in p["in_specs"]],
        out_specs=[s for p in parts for s in p["out_specs"]],
        input_output_aliases=aliases,
        scratch_shapes=[s for p in parts for s in p["scratch"]],
        compiler_params=_params(len(grid)),
        name=name,
    )(*[a for p in parts for a in p["args"]])
    split, o = [], 0
    for p in parts:
        split.append(list(outs[o:o + len(p["out_shape"])]))
        o += len(p["out_shape"])
    return split


def _in_proj_kernel(x_ref, mod_ref, g_ref, w_ref, ret_ref, hy_ref, gqa_ref, mla_ref, w_scr):
    @pl.when(pl.program_id(0) == 0)
    def _():
        g0 = RET_IN + HY_IN
        m0 = g0 + GQA_IN
        kr0 = m0 + MLA_Q_RANK + MLA_KV_RANK
        step = 2 * LANES
        for r0 in list(range(0, g0, step)) + list(range(g0 + GQA_Q_HEADS * HEAD_DIM, kr0, LANES)):
            n = step if r0 < g0 else LANES
            w_scr[r0:r0 + n, :] = w_ref[0, r0:r0 + n, :].astype(BF16)
        for dst, j in enumerate(GQA_HEAD_ORDER):
            w_scr[g0 + dst * HEAD_DIM:g0 + (dst + 1) * HEAD_DIM, :] = (
                w_ref[0, g0 + j * HEAD_DIM:g0 + (j + 1) * HEAD_DIM, :].astype(BF16))
        kr = w_ref[0, kr0:kr0 + MLA_ROPE, :].astype(BF16)
        for hd in range(MLA_HEADS):
            w_scr[kr0 + hd * MLA_ROPE:kr0 + (hd + 1) * MLA_ROPE, :] = kr

    shift = mod_ref[0, 0, 0:1, :]
    scale = mod_ref[0, 0, 1:2, :]
    h = (_rms(x_ref[...], g_ref[0, 0:1, :]) * (1.0 + scale) + shift).astype(BF16)
    c0 = 0
    for ref in (ret_ref, hy_ref, gqa_ref, mla_ref):
        width = ref.shape[1]
        ref[...] = _dot_nt(h, w_scr[c0:c0 + width, :])
        c0 += width


def _in_proj(x, mod, norm_g, w_in, layer, tm, cond_of_tile):
    rows = x.shape[0]
    widths = (RET_IN, HY_IN, GQA_IN, MLA_IN_TILED)
    return pl.pallas_call(
        _in_proj_kernel,
        out_shape=[jax.ShapeDtypeStruct((rows, w), F32) for w in widths],
        grid=(rows // tm,),
        in_specs=[
            pl.BlockSpec((tm, D_MODEL), lambda i: (i, 0)),
            pl.BlockSpec((1, 1, 6, D_MODEL), lambda i: (layer, cond_of_tile(i), 0, 0)),
            pl.BlockSpec((1, 4, D_MODEL), lambda i: (layer, 0, 0)),
            pl.BlockSpec((1, IN_WIDTH, D_MODEL), lambda i: (layer, 0, 0), pipeline_mode=pl.Buffered(1)),
        ],
        out_specs=[pl.BlockSpec((tm, w), lambda i: (i, 0)) for w in widths],
        scratch_shapes=[pltpu.VMEM((IN_WIDTH_TILED, D_MODEL), BF16)],
        compiler_params=_params(1),
        name="in_proj",
    )(x, mod, norm_g, w_in)


def _channel_kernel(x_ref, m0_ref, m1_ref, m2_ref, m3_ref, mod_ref, g_ref, wo_ref, wu_ref, cw_ref, cb_ref,
                    wd_ref, out_ref, act_ref, *, seq_len):
    y = None
    for i, m_ref in enumerate((m0_ref, m1_ref, m2_ref, m3_ref)):
        r0 = i * MIX_PART
        if i == 2:
            w = jnp.concatenate([wo_ref[0, r0 + j * HEAD_DIM:r0 + (j + 1) * HEAD_DIM, :]
                                 for j in GQA_HEAD_ORDER], axis=0)
        else:
            w = wo_ref[0, r0:r0 + MIX_PART, :]
        part = _dot(m_ref[...], w)
        y = part if y is None else y + part
    gate1 = mod_ref[0, 0, 2:3, :]
    shift2 = mod_ref[0, 0, 3:4, :]
    scale2 = mod_ref[0, 0, 4:5, :]
    xm = x_ref[...] + gate1 * _rms(y, g_ref[0, 1:2, :])
    out_ref[...] = xm
    h2 = (_rms(xm, g_ref[0, 2:3, :]) * (1.0 + scale2) + shift2).astype(BF16)
    for c0 in range(0, D_FF, FFN_CHUNK):
        sl = slice(c0, min(c0 + FFN_CHUNK, D_FF))
        gate = _dot(h2, wu_ref[0, :, sl])
        up = _dot(h2, wu_ref[0, :, D_FF + sl.start:D_FF + sl.stop])
        prev, nxt = _shift_rows(gate, seq_len)
        gate = (prev * cw_ref[0, 0:1, sl] + gate * cw_ref[0, 1:2, sl] + nxt * cw_ref[0, 2:3, sl]
                + cb_ref[0, :, sl])
        act_ref[:, sl] = (_silu(gate) * up).astype(BF16)
    ffn = _dot(act_ref[...], wd_ref[0])
    gate2 = mod_ref[0, 0, 5:6, :]
    out_ref[...] = out_ref[...] + gate2 * _rms(ffn, g_ref[0, 3:4, :])


def _channel_mix(x, mixes, mod, norm_g, w_out, w_up, conv_w, conv_b, w_down, layer, tm, seq_len, cond_of_tile):
    rows = x.shape[0]
    resident = dict(pipeline_mode=pl.Buffered(1))
    return pl.pallas_call(
        functools.partial(_channel_kernel, seq_len=seq_len),
        out_shape=jax.ShapeDtypeStruct((rows, D_MODEL), F32),
        grid=(rows // tm,),
        in_specs=[pl.BlockSpec((tm, D_MODEL), lambda i: (i, 0))]
        + [pl.BlockSpec((tm, MIX_PART), lambda i: (i, 0))] * 4
        + [
            pl.BlockSpec((1, 1, 6, D_MODEL), lambda i: (layer, cond_of_tile(i), 0, 0)),
            pl.BlockSpec((1, 4, D_MODEL), lambda i: (layer, 0, 0)),
            pl.BlockSpec((1, D_MODEL, D_MODEL), lambda i: (layer, 0, 0), **resident),
            pl.BlockSpec((1, D_MODEL, 2 * D_FF), lambda i: (layer, 0, 0), **resident),
            pl.BlockSpec((1, 3, D_FF), lambda i: (layer, 0, 0)),
            pl.BlockSpec((1, 1, D_FF), lambda i: (layer, 0, 0)),
            pl.BlockSpec((1, D_FF, D_MODEL), lambda i: (layer, 0, 0), **resident),
        ],
        out_specs=pl.BlockSpec((tm, D_MODEL), lambda i: (i, 0)),
        scratch_shapes=[pltpu.VMEM((tm, D_FF), BF16)],
        compiler_params=_params(1),
        name="channel_mix",
    )(x, *mixes, mod, norm_g, w_out, w_up, conv_w, conv_b, w_down)


def _log_gamma(dl_ref):
    dl = dl_ref[0]
    return jnp.minimum(dl, 0.0) - jnp.log(1.0 + jnp.exp(-jnp.abs(dl)))


def _pair_decay(lg, t, r0, bq, seq_len):
    top = lax.broadcasted_iota(jnp.int32, (2 * bq, 1), 0) < bq
    lgf = jnp.where(top, lg[0:1, 2 * t:2 * t + 1], lg[0:1, 2 * t + 1:2 * t + 2])
    lgb = jnp.where(top, lg[1:2, 2 * t:2 * t + 1], lg[1:2, 2 * t + 1:2 * t + 2])
    rowf = ((lax.broadcasted_iota(jnp.int32, (2 * bq, 1), 0) & (bq - 1)) + r0).astype(F32)
    colf = lax.broadcasted_iota(jnp.int32, (1, seq_len), 1).astype(F32)
    lag = rowf - colf
    decay = jnp.exp(jnp.where(lag >= 0.0, lag * lgf, -lag * lgb))
    return jnp.where(lag == 0.0, 2.0, decay)


def _head_norm(o, lo):
    inv = 1.0 / HEAD_DIM
    s_all = jnp.sum(o, axis=-1, keepdims=True)
    s_lo = jnp.sum(jnp.where(lo, o, 0.0), axis=-1, keepdims=True)
    d = o - jnp.where(lo, s_lo, s_all - s_lo) * inv
    d2 = d * d
    v_all = jnp.sum(d2, axis=-1, keepdims=True)
    v_lo = jnp.sum(jnp.where(lo, d2, 0.0), axis=-1, keepdims=True)
    return d * lax.rsqrt(jnp.where(lo, v_lo, v_all - v_lo) * inv + EPS)


def _ret_ctx_kernel(*refs, seq_len, nb):
    x_ref, dl_ref, gn_ref, *_, y_ref, st_ref, dec_scr, kdec_scr = refs
    lo = _lane_iota() < HEAD_DIM
    _zero_other_layers(st_ref)

    @pl.when(pl.program_id(0) == 0)
    def _():
        lg = _log_gamma(dl_ref)
        posf = lax.broadcasted_iota(jnp.int32, (seq_len, 1), 0).astype(F32)
        lgf = _per_head([lg[0:1, h:h + 1] for h in range(RET_HEADS)], RET_WIDTH)
        lgb = _per_head([lg[1:2, h:h + 1] for h in range(RET_HEADS)], RET_WIDTH)
        kdec_scr[0] = jnp.exp((seq_len - 1.0 - posf) * lgf)
        kdec_scr[1] = jnp.exp(posf * lgb)
        for t in range(RET_HEADS // 2):
            dec_scr[t] = _pair_decay(lg, t, 0, seq_len, seq_len)

    for i in range(nb):
        rows = slice(i * seq_len, (i + 1) * seq_len)
        for t in range(RET_HEADS // 2):
            cs = slice(t * LANES, (t + 1) * LANES)
            q = x_ref[rows, cs]
            k = x_ref[rows, RET_WIDTH + t * LANES:RET_WIDTH + (t + 1) * LANES] * (HEAD_DIM ** -0.5)
            vb = x_ref[rows, 2 * RET_WIDTH + t * LANES:2 * RET_WIDTH + (t + 1) * LANES].astype(BF16)
            gate = x_ref[rows, 3 * RET_WIDTH + t * LANES:3 * RET_WIDTH + (t + 1) * LANES]
            s = _dot_nt(_split_heads(q, lo).astype(BF16), k.astype(BF16)) * dec_scr[t]
            o = _merge_heads(_dot(s.astype(BF16), vb), lo)
            y = _silu(gate) * (_head_norm(o, lo) * gn_ref[0, :, cs])
            y_ref[rows, cs] = y.astype(BF16)
            for d in range(2):
                st = _dot_tn((k * kdec_scr[d, :, cs]).astype(BF16), vb)
                st_ref[i, 0, d, 2 * t] = st[0:HEAD_DIM, 0:HEAD_DIM]
                st_ref[i, 0, d, 2 * t + 1] = st[HEAD_DIM:LANES, HEAD_DIM:LANES]


def _ret_lat_kernel(x_ref, dl_ref, gn_ref, s0_ref, y_ref, *, seq_len, bq):
    lo = _lane_iota() < HEAD_DIM
    lg = _log_gamma(dl_ref)
    zero = jnp.zeros((HEAD_DIM, HEAD_DIM), F32)
    for t in range(RET_HEADS // 2):
        cs = slice(t * LANES, (t + 1) * LANES)
        kb = (x_ref[:, RET_WIDTH + t * LANES:RET_WIDTH + (t + 1) * LANES] * (HEAD_DIM ** -0.5)).astype(BF16)
        vb = x_ref[:, 2 * RET_WIDTH + t * LANES:2 * RET_WIDTH + (t + 1) * LANES].astype(BF16)
        lgf = _per_head([lg[0:1, 2 * t:2 * t + 1], lg[0:1, 2 * t + 1:2 * t + 2]], LANES)
        lgb = _per_head([lg[1:2, 2 * t:2 * t + 1], lg[1:2, 2 * t + 1:2 * t + 2]], LANES)
        s0 = []
        for d in range(2):
            a = s0_ref[0, 0, d, 2 * t]
            b = s0_ref[0, 0, d, 2 * t + 1]
            s0.append(jnp.concatenate([jnp.concatenate([a, zero], axis=1),
                                       jnp.concatenate([zero, b], axis=1)], axis=0).astype(BF16))
        for r0 in range(0, seq_len, bq):
            q = x_ref[r0:r0 + bq, cs]
            s = _dot_nt(_split_heads(q, lo).astype(BF16), kb) * _pair_decay(lg, t, r0, bq, seq_len)
            o = _merge_heads(_dot(s.astype(BF16), vb), lo)
            rowf = (lax.broadcasted_iota(jnp.int32, (bq, 1), 0) + r0).astype(F32)
            qf = (q * jnp.exp((rowf + 1.0) * lgf)).astype(BF16)
            qr = (q * jnp.exp((seq_len - rowf) * lgb)).astype(BF16)
            o = o + _dot(qf, s0[0]) + _dot(qr, s0[1])
            gate = x_ref[r0:r0 + bq, 3 * RET_WIDTH + t * LANES:3 * RET_WIDTH + (t + 1) * LANES]
            y = _silu(gate) * (_head_norm(o, lo) * gn_ref[0, :, cs])
            y_ref[r0:r0 + bq, cs] = y.astype(BF16)


def _retention(ret_in, decay_logit, gn_g, state0, carried, layer, n_batch, seq_len, nb):
    latent = state0 is not None
    in_specs = [
        pl.BlockSpec((nb * seq_len, RET_IN), lambda b: (b, 0)),
        pl.BlockSpec((1, 2, RET_HEADS), lambda b: (layer, 0, 0)),
        pl.BlockSpec((1, 1, RET_WIDTH), lambda b: (layer, 0, 0)),
    ]
    args = [ret_in, decay_logit, gn_g]
    y_shape = jax.ShapeDtypeStruct((n_batch * seq_len, MIX_PART), BF16)
    y_spec = pl.BlockSpec((nb * seq_len, MIX_PART), lambda b: (b, 0))
    if latent:
        assert nb == 1
        kern = functools.partial(_ret_lat_kernel, seq_len=seq_len, bq=256)
        in_specs.append(pl.BlockSpec((1, 1, 2, RET_HEADS, HEAD_DIM, HEAD_DIM),
                                     lambda b: (b, layer, 0, 0, 0, 0)))
        args.append(state0)
        out_shape, out_specs, aliases, scratch = y_shape, y_spec, {}, []
    else:
        kern = functools.partial(_ret_ctx_kernel, seq_len=seq_len, nb=nb)
        st_shape, st_spec = _layer_stacked((2, RET_HEADS, HEAD_DIM, HEAD_DIM), n_batch, layer, nb)
        out_shape = [y_shape, st_shape]
        out_specs = [y_spec, st_spec]
        aliases = _carry_specs(in_specs, args, carried, 1)
        scratch = [pltpu.VMEM((RET_HEADS // 2, 2 * seq_len, seq_len), F32),
                   pltpu.VMEM((2, seq_len, RET_WIDTH), F32)]
    return _part(kern, in_specs, args, out_shape, out_specs, scratch, aliases)


def _hyena_kernel(x_ref, sw_ref, sb_ref, bias_ref, tab_ref, f_ref, ft_ref, y_ref, *, seq_len, nb):
    for i in range(nb):
        rows = slice(i * seq_len, (i + 1) * seq_len)
        x = x_ref[rows, :]
        prev, nxt = _shift_rows(x, seq_len)
        u = prev * sw_ref[0, 0:1, :] + x * sw_ref[0, 1:2, :] + nxt * sw_ref[0, 2:3, :] + sb_ref[0]
        x1 = u[:, 0:HY_WIDTH]
        x2 = u[:, HY_WIDTH:2 * HY_WIDTH]
        z = u[:, 2 * HY_WIDTH:3 * HY_WIDTH]
        for o, gate in enumerate((x1, x2)):
            spec = _dot(f_ref[...], z.astype(BF16))
            s_re = spec[:seq_len]
            s_im = spec[seq_len:]
            a = tab_ref[0, 3 * o]
            a_ny = tab_ref[0, 3 * o + 1]
            b = tab_ref[0, 3 * o + 2]
            y_re = (s_re * a - s_im * b).astype(BF16)
            y_im = (s_re * b + s_im * a_ny).astype(BF16)
            conv = _dot(ft_ref[:, 0:seq_len], y_re) + _dot(ft_ref[:, seq_len:2 * seq_len], y_im)
            z = gate * (conv + z * bias_ref[0, o:o + 1, :])
        y_ref[rows, :] = z.astype(BF16)


def _hyena(hy_in, short_w, short_b, hy_bias, tables, fmat, fmat_t, layer, n_batch, seq_len, nb):
    const = dict(pipeline_mode=pl.Buffered(1))
    return _part(
        functools.partial(_hyena_kernel, seq_len=seq_len, nb=nb),
        [
            pl.BlockSpec((nb * seq_len, HY_IN), lambda b: (b, 0)),
            pl.BlockSpec((1, 3, HY_IN), lambda b: (layer, 0, 0)),
            pl.BlockSpec((1, 1, HY_IN), lambda b: (layer, 0, 0)),
            pl.BlockSpec((1, HY_ORDER, HY_WIDTH), lambda b: (layer, 0, 0)),
            pl.BlockSpec((1, 3 * HY_ORDER, seq_len, HY_WIDTH), lambda b: (layer, 0, 0, 0), **const),
            pl.BlockSpec((2 * seq_len, seq_len), lambda b: (0, 0), **const),
            pl.BlockSpec((seq_len, 2 * seq_len), lambda b: (0, 0), **const),
        ],
        [hy_in, short_w, short_b, hy_bias, tables, fmat, fmat_t],
        jax.ShapeDtypeStruct((n_batch * seq_len, MIX_PART), BF16),
        pl.BlockSpec((nb * seq_len, MIX_PART), lambda b: (b, 0)))


def _gqa_ctx_kernel(*refs, seq_len, nb):
    x_ref, sink_ref, *_, y_ref, k_out_ref, v_out_ref = refs
    nq = GQA_Q_HEADS * HEAD_DIM
    scale = HEAD_DIM ** -0.5
    head_a = lax.broadcasted_iota(jnp.int32, (LANES, 1), 0) < HEAD_DIM
    first = lax.broadcasted_iota(jnp.int32, (1, 2 * seq_len), 1) < seq_len
    ones = jnp.ones((LANES, seq_len), BF16)
    _zero_other_layers(k_out_ref, v_out_ref)
    for i in range(nb):
        rows = slice(i * seq_len, (i + 1) * seq_len)
        k = x_ref[rows, nq:nq + LANES]
        k_t = k.T
        v_t = x_ref[rows, nq + LANES:nq + 2 * LANES].T
        for kv in range(GQA_KV_HEADS):
            k_out_ref[i, 0, kv] = k_t[kv * HEAD_DIM:(kv + 1) * HEAD_DIM, :]
            v_out_ref[i, 0, kv] = v_t[kv * HEAD_DIM:(kv + 1) * HEAD_DIM, :]
        kb = k.astype(BF16)
        v_ext = jnp.concatenate([v_t.astype(BF16), ones], axis=0)
        for g in range(GQA_GROUPS):
            cs = slice(g * LANES, (g + 1) * LANES)
            q_t = x_ref[rows, cs].T
            q2 = jnp.concatenate([jnp.where(head_a, q_t, 0.0), jnp.where(head_a, 0.0, q_t)], axis=1)
            s = _dot(kb, q2.astype(BF16)) * scale
            sink = jnp.where(first, sink_ref[0, :, g:g + 1], sink_ref[0, :, GQA_GROUPS + g:GQA_GROUPS + g + 1])
            m = jnp.maximum(jnp.max(s, axis=0, keepdims=True), sink)
            pv = _dot(v_ext, jnp.exp(s - m).astype(BF16))
            o = pv[0:LANES, :] / (pv[LANES:LANES + 1, :] + jnp.exp(sink - m))
            o = jnp.where(head_a, o[:, 0:seq_len], o[:, seq_len:2 * seq_len])
            y_ref[rows, cs] = o.T.astype(BF16)


def _gqa_ctx(gqa_in, sink, carried, layer, n_batch, seq_len, nb):
    kv_shape, kv_spec = _layer_stacked((GQA_KV_HEADS, HEAD_DIM, seq_len), n_batch, layer, nb)
    in_specs = [
        pl.BlockSpec((nb * seq_len, GQA_IN), lambda b: (b, 0)),
        pl.BlockSpec((1, 1, GQA_Q_HEADS), lambda b: (layer, 0, 0)),
    ]
    args = [gqa_in, sink]
    aliases = _carry_specs(in_specs, args, carried, 1)
    return _part(
        functools.partial(_gqa_ctx_kernel, seq_len=seq_len, nb=nb), in_specs, args,
        [jax.ShapeDtypeStruct((n_batch * seq_len, MIX_PART), BF16), kv_shape, kv_shape],
        [pl.BlockSpec((nb * seq_len, MIX_PART), lambda b: (b, 0)), kv_spec, kv_spec],
        aliases=aliases)


def _gqa_win_kernel(x_ref, sink_ref, kct_ref, vct_ref, cos_ref, sin_ref, y_ref,
                    q_scr, k_scr, vt_scr, kc_scr, vct_scr, *, seq_len):
    nq = GQA_Q_HEADS * HEAD_DIM
    scale = HEAD_DIM ** -0.5
    head_a = lax.broadcasted_iota(jnp.int32, (LANES, 1), 0) < HEAD_DIM
    first = lax.broadcasted_iota(jnp.int32, (1, 2 * BLOCK), 1) < BLOCK
    cos = cos_ref[...]
    sin = sin_ref[...]

    def rope(t):
        return t * cos + _rot_half(t, HEAD_DIM // 2) * sin

    for g in range(GQA_GROUPS):
        q_t = rope(x_ref[:, g * LANES:(g + 1) * LANES]).T
        q_scr[g, 0] = jnp.where(head_a, q_t, 0.0).astype(BF16)
        q_scr[g, 1] = jnp.where(head_a, 0.0, q_t).astype(BF16)
    zeros = jnp.zeros((BLOCK, LANES), BF16)
    for r0 in (0, BLOCK + seq_len):
        k_scr[r0:r0 + BLOCK, :] = zeros
        vt_scr[:, r0:r0 + BLOCK] = zeros
    k_scr[BLOCK:BLOCK + seq_len, :] = rope(x_ref[:, nq:nq + LANES]).astype(BF16)
    vt_scr[:, BLOCK:BLOCK + seq_len] = x_ref[:, nq + LANES:nq + 2 * LANES].T.astype(BF16)
    kc_scr[...] = jnp.concatenate([kct_ref[0, 0, 0], kct_ref[0, 0, 1]], axis=0).T.astype(BF16)
    vct_scr[...] = jnp.concatenate([vct_ref[0, 0, 0], vct_ref[0, 0, 1]], axis=0).astype(BF16)

    jj = lax.broadcasted_iota(jnp.int32, (3 * BLOCK, 2 * BLOCK), 0)
    ii = lax.broadcasted_iota(jnp.int32, (3 * BLOCK, 2 * BLOCK), 1) & (BLOCK - 1)
    band = (jj >= ii) & (jj <= ii + 2 * WINDOW)
    for n in range(seq_len // BLOCK):
        r0 = n * BLOCK
        kpos = jj + (n - 1) * BLOCK
        valid = band & (kpos >= 0) & (kpos < seq_len)
        kw = k_scr[r0:r0 + 3 * BLOCK, :]
        vw = vt_scr[:, r0:r0 + 3 * BLOCK]
        for g in range(GQA_GROUPS):
            q2 = jnp.concatenate([q_scr[g, 0, :, r0:r0 + BLOCK], q_scr[g, 1, :, r0:r0 + BLOCK]], axis=1)
            s_win = jnp.where(valid, _dot(kw, q2) * scale, NEG_INF)
            s_ctx = _dot(kc_scr[...], q2) * scale
            sink = jnp.where(first, sink_ref[0, :, g:g + 1], sink_ref[0, :, GQA_GROUPS + g:GQA_GROUPS + g + 1])
            m = jnp.maximum(jnp.maximum(jnp.max(s_win, axis=0, keepdims=True),
                                        jnp.max(s_ctx, axis=0, keepdims=True)), sink)
            p_win = jnp.exp(s_win - m)
            p_ctx = jnp.exp(s_ctx - m)
            den = (jnp.sum(p_win, axis=0, keepdims=True) + jnp.sum(p_ctx, axis=0, keepdims=True)
                   + jnp.exp(sink - m))
            o = (_dot(vw, p_win.astype(BF16)) + _dot(vct_scr[...], p_ctx.astype(BF16))) / den
            o = jnp.where(head_a, o[:, 0:BLOCK], o[:, BLOCK:2 * BLOCK])
            y_ref[r0:r0 + BLOCK, g * LANES:(g + 1) * LANES] = o.T.astype(BF16)


def _gqa_win(gqa_in, sink, cache_k_t, cache_v_t, cos, sin, layer, n_batch, seq_len):
    past = cache_k_t.shape[4]
    return _part(
        functools.partial(_gqa_win_kernel, seq_len=seq_len),
        [
            pl.BlockSpec((seq_len, GQA_IN), lambda b: (b, 0)),
            pl.BlockSpec((1, 1, GQA_Q_HEADS), lambda b: (layer, 0, 0)),
            pl.BlockSpec((1, 1, GQA_KV_HEADS, HEAD_DIM, past), lambda b: (b, layer, 0, 0, 0)),
            pl.BlockSpec((1, 1, GQA_KV_HEADS, HEAD_DIM, past), lambda b: (b, layer, 0, 0, 0)),
            pl.BlockSpec((seq_len, LANES), lambda b: (0, 0)),
            pl.BlockSpec((seq_len, LANES), lambda b: (0, 0)),
        ],
        [gqa_in, sink, cache_k_t, cache_v_t, cos, sin],
        jax.ShapeDtypeStruct((n_batch * seq_len, MIX_PART), BF16),
        pl.BlockSpec((seq_len, MIX_PART), lambda b: (b, 0)),
        scratch=[
            pltpu.VMEM((GQA_GROUPS, 2, LANES, seq_len), BF16),
            pltpu.VMEM((seq_len + 2 * BLOCK, LANES), BF16),
            pltpu.VMEM((LANES, seq_len + 2 * BLOCK), BF16),
            pltpu.VMEM((past, LANES), BF16),
            pltpu.VMEM((LANES, past), BF16),
        ])


def _mla_kernel(*refs, seq_len, latent, bq, nb):
    if latent:
        (x_ref, qn_ref, kvn_ref, wq_ref, wk_ref, wv_ref, ckv_c_ref, kr_c_ref, cos_ref, sin_ref,
         y_ref, q_all, k_all, v_all) = refs
        past = ckv_c_ref.shape[2]
    else:
        (x_ref, qn_ref, kvn_ref, wq_ref, wk_ref, wv_ref, *_,
         y_ref, ckv_out_ref, kr_out_ref, q_all, k_all, v_all) = refs
        past = 0
        _zero_other_layers(ckv_out_ref, kr_out_ref)
    keys = past + seq_len
    nope_w = MLA_HEADS * MLA_NOPE
    scale = (MLA_NOPE + MLA_ROPE) ** -0.5
    lane = _lane_iota()
    lo = lane < HEAD_DIM
    rope_lanes = [(lane >= h * MLA_ROPE) & (lane < (h + 1) * MLA_ROPE) for h in range(MLA_HEADS)]
    for i in range(nb):
        rows = slice(i * seq_len, (i + 1) * seq_len)
        q_scr, k_scr, v_scr = q_all.at[i], k_all.at[i], v_all.at[i]
        q_lat = x_ref[rows, 0:MLA_Q_RANK]
        kv_lat = x_ref[rows, MLA_Q_RANK:MLA_Q_RANK + MLA_KV_RANK]
        k_rope = x_ref[rows, MLA_Q_RANK + MLA_KV_RANK:MLA_IN_TILED]
        mq = _dot(_rms(q_lat, qn_ref[0]).astype(BF16), wq_ref[0])
        ckv = _rms(kv_lat, kvn_ref[0])
        q_rope = mq[:, nope_w:nope_w + LANES]
        if latent:
            cos = cos_ref[...]
            sin = sin_ref[...]
            q_rope = q_rope * cos + _rot_half(q_rope, MLA_ROPE // 2) * sin
            k_rope = k_rope * cos + _rot_half(k_rope, MLA_ROPE // 2) * sin
            ckv_c = ckv_c_ref[0, 0].astype(BF16)
            k_scr[0:past, 0:nope_w] = _dot(ckv_c, wk_ref[0]).astype(BF16)
            v_scr[0:past, :] = _dot(ckv_c, wv_ref[0]).astype(BF16)
            kr_c = kr_c_ref[0, 0]
            k_scr[0:past, nope_w:nope_w + LANES] = jnp.concatenate([kr_c] * MLA_HEADS, axis=1).astype(BF16)
        else:
            ckv_out_ref[i, 0] = ckv
            kr_out_ref[i, 0] = k_rope.T[0:MLA_ROPE, :]
        q_scr[:, 0:nope_w] = mq[:, 0:nope_w]
        q_scr[:, nope_w:nope_w + LANES] = q_rope
        ckv_b = ckv.astype(BF16)
        k_scr[past:keys, 0:nope_w] = _dot(ckv_b, wk_ref[0]).astype(BF16)
        v_scr[past:keys, :] = _dot(ckv_b, wv_ref[0]).astype(BF16)
        k_scr[past:keys, nope_w:nope_w + LANES] = k_rope.astype(BF16)

        ones = jnp.ones((keys, LANES), BF16)
        kr_all = k_scr[:, nope_w:nope_w + LANES]
        for t in range(MLA_HEADS // 2):
            cs = slice(t * LANES, (t + 1) * LANES)
            k_cat = jnp.concatenate([k_scr[:, cs], kr_all], axis=1)
            v_ext = jnp.concatenate([v_scr[:, cs], ones], axis=1)
            for r0 in range(0, seq_len, bq):
                qn = q_scr[r0:r0 + bq, cs]
                qr = q_scr[r0:r0 + bq, nope_w:nope_w + LANES]
                q2 = jnp.concatenate([
                    jnp.concatenate([jnp.where(lo, qn, 0.0), jnp.where(rope_lanes[2 * t], qr, 0.0)], axis=1),
                    jnp.concatenate([jnp.where(lo, 0.0, qn), jnp.where(rope_lanes[2 * t + 1], qr, 0.0)], axis=1),
                ], axis=0).astype(BF16)
                s = _dot_nt(q2, k_cat) * scale
                m = jnp.max(s, axis=-1, keepdims=True)
                pv = _dot(jnp.exp(s - m).astype(BF16), v_ext)
                o = pv[:, 0:LANES] / pv[:, LANES:LANES + 1]
                y_ref[i * seq_len + r0:i * seq_len + r0 + bq, cs] = _merge_heads(o, lo).astype(BF16)


def _mla(mla_in, q_norm, kv_norm, w_uq, w_uk, w_uv, cache, rope, carried, layer, n_batch, seq_len, nb):
    latent = cache is not None
    aliases = {}
    past = cache[0].shape[2] if latent else 0
    kern = functools.partial(_mla_kernel, seq_len=seq_len, latent=latent, bq=256, nb=nb)
    q_w = MLA_HEADS * (MLA_NOPE + MLA_ROPE)
    kv_w = MLA_HEADS * MLA_NOPE
    in_specs = [
        pl.BlockSpec((nb * seq_len, MLA_IN_TILED), lambda b: (b, 0)),
        pl.BlockSpec((1, 1, MLA_Q_RANK), lambda b: (layer, 0, 0)),
        pl.BlockSpec((1, 1, MLA_KV_RANK), lambda b: (layer, 0, 0)),
        pl.BlockSpec((1, MLA_Q_RANK, q_w), lambda b: (layer, 0, 0)),
        pl.BlockSpec((1, MLA_KV_RANK, kv_w), lambda b: (layer, 0, 0)),
        pl.BlockSpec((1, MLA_KV_RANK, kv_w), lambda b: (layer, 0, 0)),
    ]
    args = [mla_in, q_norm, kv_norm, w_uq, w_uk, w_uv]
    y_shape = jax.ShapeDtypeStruct((n_batch * seq_len, MIX_PART), BF16)
    y_spec = pl.BlockSpec((nb * seq_len, MIX_PART), lambda b: (b, 0))
    if latent:
        assert nb == 1
        in_specs += [
            pl.BlockSpec((1, 1, past, MLA_KV_RANK), lambda b: (b, layer, 0, 0)),
            pl.BlockSpec((1, 1, past, MLA_ROPE), lambda b: (b, layer, 0, 0)),
            pl.BlockSpec((seq_len, LANES), lambda b: (0, 0)),
            pl.BlockSpec((seq_len, LANES), lambda b: (0, 0)),
        ]
        args += [cache[0], cache[1], rope[0], rope[1]]
        out_shape, out_specs = y_shape, y_spec
    else:
        ckv_shape, ckv_spec = _layer_stacked((seq_len, MLA_KV_RANK), n_batch, layer, nb)
        kr_shape, kr_spec = _layer_stacked((MLA_ROPE, seq_len), n_batch, layer, nb)
        out_shape = [y_shape, ckv_shape, kr_shape]
        out_specs = [y_spec, ckv_spec, kr_spec]
        aliases = _carry_specs(in_specs, args, carried, 1)
    keys = past + seq_len
    return _part(
        kern, in_specs, args, out_shape, out_specs,
        scratch=[
            pltpu.VMEM((nb, seq_len, q_w), F32),
            pltpu.VMEM((nb, keys, kv_w + LANES), BF16),
            pltpu.VMEM((nb, keys, kv_w), BF16),
        ],
        aliases=aliases)


def _dft_matrix(seq_len):
    k = np.arange(seq_len, dtype=np.float64)[:, None]
    j = np.arange(seq_len, dtype=np.float64)[None, :]
    ang = np.pi * k * j / seq_len
    re = np.cos(ang)
    im = -np.sin(ang)
    im[0, :] = np.where(np.arange(seq_len) % 2 == 0, 1.0, -1.0)
    return np.concatenate([re, im], axis=0).astype(np.float32)


def _hyena_positions(seq_len):
    t = np.arange(seq_len, dtype=np.float64) / seq_len
    bands = np.arange(1, HY_BANDS + 1, dtype=np.float64)
    ang = 2.0 * math.pi * t[:, None] * bands
    z = np.concatenate([t[:, None], np.cos(ang), np.sin(ang)], axis=-1)
    return np.pad(z, ((0, 0), (0, HY_POS_PAD - HY_POS_DIM))).astype(np.float32)


def _rope_tables(n_tokens, dim):
    rows = np.repeat(np.arange(n_tokens // GRID_W, dtype=np.float64), GRID_W)
    cols = np.tile(np.arange(GRID_W, dtype=np.float64), n_tokens // GRID_W)
    quarter = dim // 4
    inv = ROPE_BASE ** (-np.arange(quarter, dtype=np.float64) / quarter)
    ang = np.concatenate([rows[:, None] * inv, cols[:, None] * inv], axis=-1)
    cos = np.concatenate([np.cos(ang), np.cos(ang)], axis=-1)
    sin = np.concatenate([-np.sin(ang), np.sin(ang)], axis=-1)
    reps = LANES // dim
    return (np.tile(cos, (1, reps)).astype(np.float32), np.tile(sin, (1, reps)).astype(np.float32))


def kernel(x_prompt, x_sample, state_ret, cache_gqa_k, cache_gqa_v, cache_mla_ckv, cache_mla_krope, c, c_ctx, ada_w, ada_b, norm_g, w_in, ret_decay_logit, ret_gn_g, hy_short_w, hy_short_b, hy_w1, hy_b1, hy_w2, hy_b2, hy_w3, hy_decay, hy_bias, gqa_sink, mla_q_norm, mla_kv_norm, mla_w_uq, mla_w_uk, mla_w_uv, w_out, ffn_w_up, ffn_conv_w, ffn_conv_b, ffn_w_down):
    n_p, len_p, _ = x_prompt.shape
    n_s, len_s, _ = x_sample.shape

    w_in_t = jnp.swapaxes(w_in, 1, 2)
    cache_k_t = jnp.swapaxes(cache_gqa_k, -1, -2)
    cache_v_t = jnp.swapaxes(cache_gqa_v, -1, -2)
    w_out_b = w_out.astype(BF16)
    w_up_b = ffn_w_up.astype(BF16)
    w_down_b = ffn_w_down.astype(BF16)
    uq = mla_w_uq.reshape(DEPTH, MLA_Q_RANK, MLA_HEADS, MLA_NOPE + MLA_ROPE)
    w_uq_b = jnp.concatenate(
        [uq[..., :MLA_NOPE].reshape(DEPTH, MLA_Q_RANK, MLA_HEADS * MLA_NOPE),
         uq[..., MLA_NOPE:].reshape(DEPTH, MLA_Q_RANK, MLA_HEADS * MLA_ROPE)], axis=-1).astype(BF16)
    w_uk_b = mla_w_uk.astype(BF16)
    w_uv_b = mla_w_uv.astype(BF16)

    cond = jnp.concatenate([c_ctx[None], c, jnp.zeros((N_COND - 1 - n_s, D_MODEL), F32)], axis=0)
    mod = _modulation(cond, ada_w, ada_b).reshape(DEPTH, N_COND, 6, D_MODEL)

    w1p = jnp.pad(hy_w1, ((0, 0), (0, HY_POS_PAD - HY_POS_DIM), (0, 0)))
    b1 = hy_b1.reshape(DEPTH, 1, HY_FILTER_HIDDEN)
    b2 = hy_b2.reshape(DEPTH, 1, HY_FILTER_HIDDEN)
    groups = {}
    for seq_len in (len_p, len_s):
        f32mat = _dft_matrix(seq_len)
        fmat = jnp.asarray(f32mat).astype(BF16)
        fmat_t = jnp.asarray(np.ascontiguousarray(f32mat.T)).astype(BF16)
        tables = _hyena_tables(seq_len, jnp.asarray(_hyena_positions(seq_len)), w1p, b1, hy_w2, b2,
                               hy_w3, hy_decay, fmat)
        groups[seq_len] = (tables, fmat, fmat_t)

    rope_g = tuple(jnp.asarray(t) for t in _rope_tables(len_s, HEAD_DIM))
    rope_m = tuple(jnp.asarray(t) for t in _rope_tables(len_s, MLA_ROPE))

    gn_g = ret_gn_g.reshape(DEPTH, 1, RET_WIDTH)
    short_b = hy_short_b.reshape(DEPTH, 1, HY_IN)
    sink = gqa_sink.reshape(DEPTH, 1, GQA_Q_HEADS)
    q_norm = mla_q_norm.reshape(DEPTH, 1, MLA_Q_RANK)
    kv_norm = mla_kv_norm.reshape(DEPTH, 1, MLA_KV_RANK)
    conv_b = ffn_conv_b.reshape(DEPTH, 1, D_FF)

    xp = x_prompt.reshape(n_p * len_p, D_MODEL)
    xs = x_sample.reshape(n_s * len_s, D_MODEL)
    tm_p = 1024
    tm_s = 512
    nb_p = 4
    nb_first = 2
    cond_p = lambda i: 0
    cond_s = lambda i: 1 + (i * tm_s) // len_s
    tm_ffn = 1024
    cond_s_ffn = lambda i: 1 + (i * tm_ffn) // len_s

    st = gkv = ckr = None
    for l in range(DEPTH):
        tables, fmat, fmat_t = groups[len_p]
        ret_in, hy_in, gqa_in, mla_in = _in_proj(xp, mod, norm_g, w_in_t, l, tm_p, cond_p)
        nb = nb_first if l == 0 else nb_p
        (y_ret, *st), (y_hy,), (y_gqa, *gkv), (y_mla, *ckr) = _run_parts([
            _retention(ret_in, ret_decay_logit, gn_g, None, st, l, n_p, len_p, nb),
            _hyena(hy_in, hy_short_w, short_b, hy_bias, tables, fmat, fmat_t, l, n_p, len_p, nb),
            _gqa_ctx(gqa_in, sink, gkv, l, n_p, len_p, nb),
            _mla(mla_in, q_norm, kv_norm, w_uq_b, w_uk_b, w_uv_b, None, None, ckr, l, n_p, len_p, nb),
        ], (n_p // nb,), "context_mixers")
        xp = _channel_mix(xp, (y_ret, y_hy, y_gqa, y_mla), mod, norm_g, w_out_b, w_up_b, ffn_conv_w, conv_b,
                          w_down_b, l, tm_ffn, len_p, cond_p)
        tables, fmat, fmat_t = groups[len_s]
        ret_in, hy_in, gqa_in, mla_in = _in_proj(xs, mod, norm_g, w_in_t, l, tm_s, cond_s)
        (y_ret,), (y_gqa,), (y_mla,) = _run_parts([
            _retention(ret_in, ret_decay_logit, gn_g, state_ret, None, l, n_s, len_s, 1),
            _gqa_win(gqa_in, sink, cache_k_t, cache_v_t, rope_g[0], rope_g[1], l, n_s, len_s),
            _mla(mla_in, q_norm, kv_norm, w_uq_b, w_uk_b, w_uv_b, (cache_mla_ckv, cache_mla_krope),
                 rope_m, None, l, n_s, len_s, 1),
        ], (n_s,), "latent_mixers")
        (y_hy,), = _run_parts([
            _hyena(hy_in, hy_short_w, short_b, hy_bias, tables, fmat, fmat_t, l, n_s, len_s, n_s),
        ], (1,), "latent_hyena")
        xs = _channel_mix(xs, (y_ret, y_hy, y_gqa, y_mla), mod, norm_g, w_out_b, w_up_b, ffn_conv_w, conv_b,
                          w_down_b, l, tm_ffn, len_s, cond_s_ffn)

    return (xp.reshape(n_p, len_p, D_MODEL), xs.reshape(n_s, len_s, D_MODEL),
            st[0], jnp.swapaxes(gkv[0], -1, -2), jnp.swapaxes(gkv[1], -1, -2),
            ckr[0], jnp.swapaxes(ckr[1], -1, -2))
```
